```python
import jax, jax.numpy as jnp
from jax import lax
import numpy as np

D_MODEL = 1024
BATCH = 8
SEQ = 4096
DEPTH = 4

N_MIXERS = 3
SB_HEADS = 16
SB_HEAD_DIM = D_MODEL // SB_HEADS
SB_BLOCK = 128
SSD_EXPAND = 2
SSD_D_INNER = SSD_EXPAND * D_MODEL
SSD_HEAD_DIM = 64
SSD_HEADS = SSD_D_INNER // SSD_HEAD_DIM
SSD_GROUPS = 8
SSD_HPG = SSD_HEADS // SSD_GROUPS
SSD_STATE = 128
SSD_CONV = 4
SSD_CHUNK = 128
SSD_CONV_DIM = SSD_D_INNER + 2 * SSD_GROUPS * SSD_STATE
SSD_IN_DIM = SSD_D_INNER + SSD_CONV_DIM + SSD_HEADS
SSD_NORM_GROUP = SSD_D_INNER // SSD_GROUPS
SC_WIDTH = 3
D_FF = 2816
RMS_EPS = 1e-6
N_SB_LAYERS = (DEPTH + 2) // 3
N_SSD_LAYERS = (DEPTH + 1) // 3
N_SC_LAYERS = DEPTH // 3

kernel_name = "hybrid_sb_ssd_shortconv_macaron"


def rmsnorm(x, g):
    xf = x.astype(jnp.float32)
    y = xf * lax.rsqrt(jnp.mean(xf * xf, axis=-1, keepdims=True) + RMS_EPS)
    return (y * g.astype(jnp.float32)).astype(x.dtype)


def swiglu(x, w_gu, w_down):
    gate, up = jnp.split(x @ w_gu, 2, axis=-1)
    return (jax.nn.silu(gate) * up) @ w_down


def causal_depthwise_conv(x, w):
    k, c = w.shape
    return lax.conv_general_dilated(
        x, w[:, None, :].astype(x.dtype), window_strides=(1,), padding=[(k - 1, 0)],
        dimension_numbers=('NWC', 'WIO', 'NWC'), feature_group_count=c)


def stick_breaking_attention(x, w_qkv, w_o):
    b, l, _ = x.shape
    q, k, v = jnp.split(x @ w_qkv, 3, axis=-1)
    q = q.reshape(b, l, SB_HEADS, SB_HEAD_DIM)
    k = k.reshape(b, l, SB_HEADS, SB_HEAD_DIM)
    v = v.reshape(b, l, SB_HEADS, SB_HEAD_DIM)
    n_blk = l // SB_BLOCK
    q_blocks = jnp.moveaxis(q.reshape(b, n_blk, SB_BLOCK, SB_HEADS, SB_HEAD_DIM), 1, 0)
    key_pos = jnp.arange(l)
    scale = SB_HEAD_DIM ** -0.5

    def one_block(args):
        q_blk, blk = args
        z = jnp.einsum('bqhd,bkhd->bhqk', q_blk, k).astype(jnp.float32) * scale
        q_pos = blk * SB_BLOCK + jnp.arange(SB_BLOCK)
        mask = key_pos[None, :] < q_pos[:, None]
        log_beta = jax.nn.log_sigmoid(z)
        log_keep = jnp.where(mask, jax.nn.log_sigmoid(-z), 0.0)
        tail = lax.cumsum(log_keep, axis=3, reverse=True) - log_keep
        att = jnp.where(mask, jnp.exp(log_beta + tail), 0.0)
        return jnp.einsum('bhqk,bkhd->bqhd', att.astype(v.dtype), v)

    o = lax.map(one_block, (q_blocks, jnp.arange(n_blk)))
    o = jnp.moveaxis(o, 0, 1).reshape(b, l, D_MODEL)
    return o @ w_o


def ssd_chunked(xs, dt, a, bm, cm):
    b, l, g, r, p = xs.shape
    n = bm.shape[-1]
    nc, cl = l // SSD_CHUNK, SSD_CHUNK
    x_c = xs.reshape(b, nc, cl, g, r, p)
    dt_c = dt.reshape(b, nc, cl, g, r)
    b_c = bm.reshape(b, nc, cl, g, n)
    c_c = cm.reshape(b, nc, cl, g, n)
    a_cum = jnp.cumsum(dt_c * a, axis=2)
    seg = a_cum[:, :, :, None] - a_cum[:, :, None, :]
    causal = jnp.tril(jnp.ones((cl, cl), dtype=bool))[:, :, None, None]
    decay = jnp.exp(jnp.where(causal, seg, -jnp.inf))
    cb = jnp.einsum('bclgn,bcsgn->bclsg', c_c, b_c)
    w = cb[..., None] * decay * dt_c[:, :, None]
    y_diag = jnp.einsum('bclsgr,bcsgrp->bclgrp', w, x_c)
    a_last = a_cum[:, :, -1]
    to_end = jnp.exp(a_last[:, :, None] - a_cum) * dt_c
    states = jnp.einsum('bcsgn,bcsgr,bcsgrp->bcgrpn', b_c, to_end, x_c)

    def step(h, inp):
        st, al = inp
        return h * jnp.exp(al)[..., None, None] + st, h

    h0 = jnp.zeros((b, g, r, p, n), dtype=xs.dtype)
    _, h_prev = lax.scan(step, h0, (jnp.moveaxis(states, 1, 0), jnp.moveaxis(a_last, 1, 0)))
    h_prev = jnp.moveaxis(h_prev, 0, 1)
    y_off = jnp.einsum('bclgn,bcgrpn,bclgr->bclgrp', c_c, h_prev, jnp.exp(a_cum))
    return (y_diag + y_off).reshape(b, l, g, r, p)


def ssd_mixer(x, w_in, conv_w, conv_b, dt_bias, a_log, d_skip, norm_g, w_out):
    b, l, _ = x.shape
    z, xbc, dt = jnp.split(x @ w_in, [SSD_D_INNER, SSD_D_INNER + SSD_CONV_DIM], axis=-1)
    xbc = jax.nn.silu(causal_depthwise_conv(xbc, conv_w) + conv_b)
    xs, bm, cm = jnp.split(xbc, [SSD_D_INNER, SSD_D_INNER + SSD_GROUPS * SSD_STATE], axis=-1)
    xs = xs.astype(jnp.float32).reshape(b, l, SSD_GROUPS, SSD_HPG, SSD_HEAD_DIM)
    bm = bm.astype(jnp.float32).reshape(b, l, SSD_GROUPS, SSD_STATE)
    cm = cm.astype(jnp.float32).reshape(b, l, SSD_GROUPS, SSD_STATE)
    dt = jax.nn.softplus(dt.astype(jnp.float32) + dt_bias.astype(jnp.float32))
    dt = dt.reshape(b, l, SSD_GROUPS, SSD_HPG)
    a = -jnp.exp(a_log.astype(jnp.float32)).reshape(SSD_GROUPS, SSD_HPG)
    y = ssd_chunked(xs, dt, a, bm, cm)
    y = y + d_skip.astype(jnp.float32).reshape(SSD_GROUPS, SSD_HPG, 1) * xs
    y = y.reshape(b, l, SSD_D_INNER).astype(x.dtype) * jax.nn.silu(z)
    y = rmsnorm(y.reshape(b, l, SSD_GROUPS, SSD_NORM_GROUP),
                norm_g.reshape(SSD_GROUPS, SSD_NORM_GROUP)).reshape(b, l, SSD_D_INNER)
    return y @ w_out


def short_conv_mixer(x, w_in, conv_w, w_out):
    b_gate, c_gate, h = jnp.split(x @ w_in, 3, axis=-1)
    u = causal_depthwise_conv(c_gate * h, conv_w)
    return (b_gate * u) @ w_out


def _fwd_setup_inputs(seed: int = 0) -> dict:
    key = jax.random.key(seed)
    ks = jax.random.split(key, 24)
    f32 = jnp.float32

    def wn(k, shape, fan_in):
        return jax.random.normal(k, shape, f32) * (fan_in ** -0.5)

    def gain(k, shape):
        return 1.0 + 0.01 * jax.random.normal(k, shape, f32)

    dt0 = jnp.exp(jax.random.uniform(ks[14], (N_SSD_LAYERS, SSD_HEADS), f32)
                  * (np.log(0.1) - np.log(0.001)) + np.log(0.001))
    dt_bias = dt0 + jnp.log(-jnp.expm1(-dt0))
    return {
        "x": jax.random.normal(ks[0], (BATCH, SEQ, D_MODEL), f32),
        "ffn1_norm": gain(ks[1], (DEPTH, D_MODEL)),
        "ffn1_w_gu": wn(ks[2], (DEPTH, D_MODEL, 2 * D_FF), D_MODEL),
        "ffn1_w_down": wn(ks[3], (DEPTH, D_FF, D_MODEL), D_FF),
        "mix_norm": gain(ks[4], (DEPTH, D_MODEL)),
        "ffn2_norm": gain(ks[5], (DEPTH, D_MODEL)),
        "ffn2_w_gu": wn(ks[6], (DEPTH, D_MODEL, 2 * D_FF), D_MODEL),
        "ffn2_w_down": wn(ks[7], (DEPTH, D_FF, D_MODEL), D_FF),
        "sb_w_qkv": wn(ks[8], (N_SB_LAYERS, D_MODEL, 3 * D_MODEL), D_MODEL),
        "sb_w_o": wn(ks[9], (N_SB_LAYERS, D_MODEL, D_MODEL), D_MODEL),
        "ssd_w_in": wn(ks[10], (N_SSD_LAYERS, D_MODEL, SSD_IN_DIM), D_MODEL),
        "ssd_conv_w": wn(ks[11], (N_SSD_LAYERS, SSD_CONV, SSD_CONV_DIM), SSD_CONV),
        "ssd_conv_b": 0.01 * jax.random.normal(ks[12], (N_SSD_LAYERS, SSD_CONV_DIM), f32),
        "ssd_dt_bias": dt_bias,
        "ssd_a_log": jnp.log(jax.random.uniform(ks[15], (N_SSD_LAYERS, SSD_HEADS), f32, 1.0, 16.0)),
        "ssd_d": gain(ks[16], (N_SSD_LAYERS, SSD_HEADS)),
        "ssd_norm": gain(ks[17], (N_SSD_LAYERS, SSD_D_INNER)),
        "ssd_w_out": wn(ks[18], (N_SSD_LAYERS, SSD_D_INNER, D_MODEL), SSD_D_INNER),
        "sc_w_in": wn(ks[19], (N_SC_LAYERS, D_MODEL, 3 * D_MODEL), D_MODEL),
        "sc_conv_w": wn(ks[20], (N_SC_LAYERS, SC_WIDTH, D_MODEL), SC_WIDTH),
        "sc_w_out": wn(ks[21], (N_SC_LAYERS, D_MODEL, D_MODEL), D_MODEL),
        "final_norm": gain(ks[22], (D_MODEL,)),
    }


def _fwd_reference(x, ffn1_norm, ffn1_w_gu, ffn1_w_down, mix_norm, ffn2_norm, ffn2_w_gu, ffn2_w_down,
              sb_w_qkv, sb_w_o, ssd_w_in, ssd_conv_w, ssd_conv_b, ssd_dt_bias, ssd_a_log, ssd_d,
              ssd_norm, ssd_w_out, sc_w_in, sc_conv_w, sc_w_out, final_norm):
    for i in range(DEPTH):
        x = x + 0.5 * swiglu(rmsnorm(x, ffn1_norm[i]), ffn1_w_gu[i], ffn1_w_down[i])
        h = rmsnorm(x, mix_norm[i])
        kind, j = i % N_MIXERS, i // N_MIXERS
        if kind == 0:
            m = stick_breaking_attention(h, sb_w_qkv[j], sb_w_o[j])
        elif kind == 1:
            m = ssd_mixer(h, ssd_w_in[j], ssd_conv_w[j], ssd_conv_b[j], ssd_dt_bias[j],
                          ssd_a_log[j], ssd_d[j], ssd_norm[j], ssd_w_out[j])
        else:
            m = short_conv_mixer(h, sc_w_in[j], sc_conv_w[j], sc_w_out[j])
        x = x + m
        x = x + 0.5 * swiglu(rmsnorm(x, ffn2_norm[i]), ffn2_w_gu[i], ffn2_w_down[i])
    return rmsnorm(x, final_norm)


import jax as _jax
import jax.numpy as _jnp

TWIN_FORMAT = 'train_step'
FWD_PARAMS = ['x', 'ffn1_norm', 'ffn1_w_gu', 'ffn1_w_down', 'mix_norm', 'ffn2_norm', 'ffn2_w_gu', 'ffn2_w_down', 'sb_w_qkv', 'sb_w_o', 'ssd_w_in', 'ssd_conv_w', 'ssd_conv_b', 'ssd_dt_bias', 'ssd_a_log', 'ssd_d', 'ssd_norm', 'ssd_w_out', 'sc_w_in', 'sc_conv_w', 'sc_w_out', 'final_norm']
TWIN_WEIGHTS = ['ffn1_norm', 'ffn1_w_gu', 'ffn1_w_down', 'mix_norm', 'ffn2_norm', 'ffn2_w_gu', 'ffn2_w_down', 'sb_w_qkv', 'sb_w_o', 'ssd_w_in', 'ssd_conv_w', 'ssd_conv_b', 'ssd_dt_bias', 'ssd_a_log', 'ssd_d', 'ssd_norm', 'ssd_w_out', 'sc_w_in', 'sc_conv_w', 'sc_w_out', 'final_norm']
TWIN_DIFF_INPUT = 'x'
TWIN_INPUTS = ['x', 'ffn1_norm', 'ffn1_w_gu', 'ffn1_w_down', 'mix_norm', 'ffn2_norm', 'ffn2_w_gu', 'ffn2_w_down', 'sb_w_qkv', 'sb_w_o', 'ssd_w_in', 'ssd_conv_w', 'ssd_conv_b', 'ssd_dt_bias', 'ssd_a_log', 'ssd_d', 'ssd_norm', 'ssd_w_out', 'sc_w_in', 'sc_conv_w', 'sc_w_out', 'final_norm', 'loss_target', 'm_ffn1_norm', 'm_ffn1_w_gu', 'm_ffn1_w_down', 'm_mix_norm', 'm_ffn2_norm', 'm_ffn2_w_gu', 'm_ffn2_w_down', 'm_sb_w_qkv', 'm_sb_w_o', 'm_ssd_w_in', 'm_ssd_conv_w', 'm_ssd_conv_b', 'm_ssd_dt_bias', 'm_ssd_a_log', 'm_ssd_d', 'm_ssd_norm', 'm_ssd_w_out', 'm_sc_w_in', 'm_sc_conv_w', 'm_sc_w_out', 'm_final_norm', 'v_ffn1_norm', 'v_ffn1_w_gu', 'v_ffn1_w_down', 'v_mix_norm', 'v_ffn2_norm', 'v_ffn2_w_gu', 'v_ffn2_w_down', 'v_sb_w_qkv', 'v_sb_w_o', 'v_ssd_w_in', 'v_ssd_conv_w', 'v_ssd_conv_b', 'v_ssd_dt_bias', 'v_ssd_a_log', 'v_ssd_d', 'v_ssd_norm', 'v_ssd_w_out', 'v_sc_w_in', 'v_sc_conv_w', 'v_sc_w_out', 'v_final_norm']
TWIN_OUTPUTS = ['loss', 'grad_x', 'grad_ffn1_norm', 'grad_ffn1_w_gu', 'grad_ffn1_w_down', 'grad_mix_norm', 'grad_ffn2_norm', 'grad_ffn2_w_gu', 'grad_ffn2_w_down', 'grad_sb_w_qkv', 'grad_sb_w_o', 'grad_ssd_w_in', 'grad_ssd_conv_w', 'grad_ssd_conv_b', 'grad_ssd_dt_bias', 'grad_ssd_a_log', 'grad_ssd_d', 'grad_ssd_norm', 'grad_ssd_w_out', 'grad_sc_w_in', 'grad_sc_conv_w', 'grad_sc_w_out', 'grad_final_norm', 'delta_ffn1_norm', 'delta_ffn1_w_gu', 'delta_ffn1_w_down', 'delta_mix_norm', 'delta_ffn2_norm', 'delta_ffn2_w_gu', 'delta_ffn2_w_down', 'delta_sb_w_qkv', 'delta_sb_w_o', 'delta_ssd_w_in', 'delta_ssd_conv_w', 'delta_ssd_conv_b', 'delta_ssd_dt_bias', 'delta_ssd_a_log', 'delta_ssd_d', 'delta_ssd_norm', 'delta_ssd_w_out', 'delta_sc_w_in', 'delta_sc_conv_w', 'delta_sc_w_out', 'delta_final_norm', 'new_m_ffn1_norm', 'new_m_ffn1_w_gu', 'new_m_ffn1_w_down', 'new_m_mix_norm', 'new_m_ffn2_norm', 'new_m_ffn2_w_gu', 'new_m_ffn2_w_down', 'new_m_sb_w_qkv', 'new_m_sb_w_o', 'new_m_ssd_w_in', 'new_m_ssd_conv_w', 'new_m_ssd_conv_b', 'new_m_ssd_dt_bias', 'new_m_ssd_a_log', 'new_m_ssd_d', 'new_m_ssd_norm', 'new_m_ssd_w_out', 'new_m_sc_w_in', 'new_m_sc_conv_w', 'new_m_sc_w_out', 'new_m_final_norm', 'new_v_ffn1_norm', 'new_v_ffn1_w_gu', 'new_v_ffn1_w_down', 'new_v_mix_norm', 'new_v_ffn2_norm', 'new_v_ffn2_w_gu', 'new_v_ffn2_w_down', 'new_v_sb_w_qkv', 'new_v_sb_w_o', 'new_v_ssd_w_in', 'new_v_ssd_conv_w', 'new_v_ssd_conv_b', 'new_v_ssd_dt_bias', 'new_v_ssd_a_log', 'new_v_ssd_d', 'new_v_ssd_norm', 'new_v_ssd_w_out', 'new_v_sc_w_in', 'new_v_sc_conv_w', 'new_v_sc_w_out', 'new_v_final_norm']
TWIN_LEAF_KINDS = {'loss': 'loss', 'grad_x': 'grad_x', 'grad_ffn1_norm': 'grad_w', 'grad_ffn1_w_gu': 'grad_w', 'grad_ffn1_w_down': 'grad_w', 'grad_mix_norm': 'grad_w', 'grad_ffn2_norm': 'grad_w', 'grad_ffn2_w_gu': 'grad_w', 'grad_ffn2_w_down': 'grad_w', 'grad_sb_w_qkv': 'grad_w', 'grad_sb_w_o': 'grad_w', 'grad_ssd_w_in': 'grad_w', 'grad_ssd_conv_w': 'grad_w', 'grad_ssd_conv_b': 'grad_w', 'grad_ssd_dt_bias': 'grad_w', 'grad_ssd_a_log': 'grad_w', 'grad_ssd_d': 'grad_w', 'grad_ssd_norm': 'grad_w', 'grad_ssd_w_out': 'grad_w', 'grad_sc_w_in': 'grad_w', 'grad_sc_conv_w': 'grad_w', 'grad_sc_w_out': 'grad_w', 'grad_final_norm': 'grad_w', 'delta_ffn1_norm': 'delta_w', 'delta_ffn1_w_gu': 'delta_w', 'delta_ffn1_w_down': 'delta_w', 'delta_mix_norm': 'delta_w', 'delta_ffn2_norm': 'delta_w', 'delta_ffn2_w_gu': 'delta_w', 'delta_ffn2_w_down': 'delta_w', 'delta_sb_w_qkv': 'delta_w', 'delta_sb_w_o': 'delta_w', 'delta_ssd_w_in': 'delta_w', 'delta_ssd_conv_w': 'delta_w', 'delta_ssd_conv_b': 'delta_w', 'delta_ssd_dt_bias': 'delta_w', 'delta_ssd_a_log': 'delta_w', 'delta_ssd_d': 'delta_w', 'delta_ssd_norm': 'delta_w', 'delta_ssd_w_out': 'delta_w', 'delta_sc_w_in': 'delta_w', 'delta_sc_conv_w': 'delta_w', 'delta_sc_w_out': 'delta_w', 'delta_final_norm': 'delta_w', 'new_m_ffn1_norm': 'new_m', 'new_m_ffn1_w_gu': 'new_m', 'new_m_ffn1_w_down': 'new_m', 'new_m_mix_norm': 'new_m', 'new_m_ffn2_norm': 'new_m', 'new_m_ffn2_w_gu': 'new_m', 'new_m_ffn2_w_down': 'new_m', 'new_m_sb_w_qkv': 'new_m', 'new_m_sb_w_o': 'new_m', 'new_m_ssd_w_in': 'new_m', 'new_m_ssd_conv_w': 'new_m', 'new_m_ssd_conv_b': 'new_m', 'new_m_ssd_dt_bias': 'new_m', 'new_m_ssd_a_log': 'new_m', 'new_m_ssd_d': 'new_m', 'new_m_ssd_norm': 'new_m', 'new_m_ssd_w_out': 'new_m', 'new_m_sc_w_in': 'new_m', 'new_m_sc_conv_w': 'new_m', 'new_m_sc_w_out': 'new_m', 'new_m_final_norm': 'new_m', 'new_v_ffn1_norm': 'new_v', 'new_v_ffn1_w_gu': 'new_v', 'new_v_ffn1_w_down': 'new_v', 'new_v_mix_norm': 'new_v', 'new_v_ffn2_norm': 'new_v', 'new_v_ffn2_w_gu': 'new_v', 'new_v_ffn2_w_down': 'new_v', 'new_v_sb_w_qkv': 'new_v', 'new_v_sb_w_o': 'new_v', 'new_v_ssd_w_in': 'new_v', 'new_v_ssd_conv_w': 'new_v', 'new_v_ssd_conv_b': 'new_v', 'new_v_ssd_dt_bias': 'new_v', 'new_v_ssd_a_log': 'new_v', 'new_v_ssd_d': 'new_v', 'new_v_ssd_norm': 'new_v', 'new_v_ssd_w_out': 'new_v', 'new_v_sc_w_in': 'new_v', 'new_v_sc_conv_w': 'new_v', 'new_v_sc_w_out': 'new_v', 'new_v_final_norm': 'new_v'}


def _forward(args):
    return _fwd_reference(*[args[k] for k in FWD_PARAMS])


def _output_shape():
    def fwd():
        inp = _fwd_setup_inputs(0)
        return _fwd_reference(*[inp[k] for k in FWD_PARAMS])
    out = _jax.eval_shape(fwd)
    return out.shape, out.dtype

N_MICROBATCH = 1
ADAM_LR = 0.001
ADAM_B1 = 0.9
ADAM_B2 = 0.999
ADAM_EPS = 1e-08
ADAM_WD = 0.01
ADAM_STEP = 10
PER_EXAMPLE_BATCH_AXIS = {'x': 0, 'loss_target': 0}
SHARED_INPUTS = []
_WEIGHT_DTYPES = {'ffn1_norm': _jnp.float32, 'ffn1_w_gu': _jnp.float32, 'ffn1_w_down': _jnp.float32, 'mix_norm': _jnp.float32, 'ffn2_norm': _jnp.float32, 'ffn2_w_gu': _jnp.float32, 'ffn2_w_down': _jnp.float32, 'sb_w_qkv': _jnp.float32, 'sb_w_o': _jnp.float32, 'ssd_w_in': _jnp.float32, 'ssd_conv_w': _jnp.float32, 'ssd_conv_b': _jnp.float32, 'ssd_dt_bias': _jnp.float32, 'ssd_a_log': _jnp.float32, 'ssd_d': _jnp.float32, 'ssd_norm': _jnp.float32, 'ssd_w_out': _jnp.float32, 'sc_w_in': _jnp.float32, 'sc_conv_w': _jnp.float32, 'sc_w_out': _jnp.float32, 'final_norm': _jnp.float32}
MOMENT_SCALE = {'ffn1_norm': 8.941626e-02, 'ffn1_w_gu': 3.790463e-02, 'ffn1_w_down': 6.174497e-02, 'mix_norm': 1.773798e-01, 'ffn2_norm': 6.585967e-02, 'ffn2_w_gu': 2.746441e-02, 'ffn2_w_down': 4.474837e-02, 'sb_w_qkv': 8.291856e-02, 'sb_w_o': 1.206149e-01, 'ssd_w_in': 8.934234e-02, 'ssd_conv_w': 7.767621e-02, 'ssd_conv_b': 1.032815e-01, 'ssd_dt_bias': 2.180117e-01, 'ssd_a_log': 2.441993e-01, 'ssd_d': 6.684978e-01, 'ssd_norm': 1.107501e-01, 'ssd_w_out': 1.451771e-01, 'sc_w_in': 9.425488e-02, 'sc_conv_w': 9.854848e-02, 'sc_w_out': 9.485501e-02, 'final_norm': 3.200814e+01}


def _to_microbatches(a, axis):
    t = _jnp.moveaxis(a, axis, 0)
    t = t.reshape((N_MICROBATCH, t.shape[0] // N_MICROBATCH) + t.shape[1:])
    return _jnp.moveaxis(t, 1, axis + 1)


def setup_inputs(seed: int = 0) -> dict:
    inp = _fwd_setup_inputs(seed)
    key = _jax.random.fold_in(_jax.random.key(seed), 7919)
    shape, _ = _output_shape()
    out = dict(inp)
    out["loss_target"] = _jax.random.normal(_jax.random.fold_in(key, 0), shape, _jnp.float32)
    for i, name in enumerate(TWIN_WEIGHTS):
        w = inp[name].astype(_jnp.float32)
        if MOMENT_SCALE is None:
            s = _jnp.sqrt(_jnp.mean(_jnp.square(w)) + 1e-30)
        else:
            s = MOMENT_SCALE[name]
        km, kv = _jax.random.split(_jax.random.fold_in(key, i + 1))
        out[name] = w
        out["m_" + name] = s * _jax.random.normal(km, w.shape, _jnp.float32)
        out["v_" + name] = (s * s) * _jax.random.uniform(kv, w.shape, _jnp.float32, 0.5, 1.5)
    if N_MICROBATCH > 1:
        for name, axis in PER_EXAMPLE_BATCH_AXIS.items():
            out[name] = _to_microbatches(out[name], axis)
    return {'x': out['x'], 'ffn1_norm': out['ffn1_norm'], 'ffn1_w_gu': out['ffn1_w_gu'], 'ffn1_w_down': out['ffn1_w_down'], 'mix_norm': out['mix_norm'], 'ffn2_norm': out['ffn2_norm'], 'ffn2_w_gu': out['ffn2_w_gu'], 'ffn2_w_down': out['ffn2_w_down'], 'sb_w_qkv': out['sb_w_qkv'], 'sb_w_o': out['sb_w_o'], 'ssd_w_in': out['ssd_w_in'], 'ssd_conv_w': out['ssd_conv_w'], 'ssd_conv_b': out['ssd_conv_b'], 'ssd_dt_bias': out['ssd_dt_bias'], 'ssd_a_log': out['ssd_a_log'], 'ssd_d': out['ssd_d'], 'ssd_norm': out['ssd_norm'], 'ssd_w_out': out['ssd_w_out'], 'sc_w_in': out['sc_w_in'], 'sc_conv_w': out['sc_conv_w'], 'sc_w_out': out['sc_w_out'], 'final_norm': out['final_norm'], 'loss_target': out['loss_target'], 'm_ffn1_norm': out['m_ffn1_norm'], 'm_ffn1_w_gu': out['m_ffn1_w_gu'], 'm_ffn1_w_down': out['m_ffn1_w_down'], 'm_mix_norm': out['m_mix_norm'], 'm_ffn2_norm': out['m_ffn2_norm'], 'm_ffn2_w_gu': out['m_ffn2_w_gu'], 'm_ffn2_w_down': out['m_ffn2_w_down'], 'm_sb_w_qkv': out['m_sb_w_qkv'], 'm_sb_w_o': out['m_sb_w_o'], 'm_ssd_w_in': out['m_ssd_w_in'], 'm_ssd_conv_w': out['m_ssd_conv_w'], 'm_ssd_conv_b': out['m_ssd_conv_b'], 'm_ssd_dt_bias': out['m_ssd_dt_bias'], 'm_ssd_a_log': out['m_ssd_a_log'], 'm_ssd_d': out['m_ssd_d'], 'm_ssd_norm': out['m_ssd_norm'], 'm_ssd_w_out': out['m_ssd_w_out'], 'm_sc_w_in': out['m_sc_w_in'], 'm_sc_conv_w': out['m_sc_conv_w'], 'm_sc_w_out': out['m_sc_w_out'], 'm_final_norm': out['m_final_norm'], 'v_ffn1_norm': out['v_ffn1_norm'], 'v_ffn1_w_gu': out['v_ffn1_w_gu'], 'v_ffn1_w_down': out['v_ffn1_w_down'], 'v_mix_norm': out['v_mix_norm'], 'v_ffn2_norm': out['v_ffn2_norm'], 'v_ffn2_w_gu': out['v_ffn2_w_gu'], 'v_ffn2_w_down': out['v_ffn2_w_down'], 'v_sb_w_qkv': out['v_sb_w_qkv'], 'v_sb_w_o': out['v_sb_w_o'], 'v_ssd_w_in': out['v_ssd_w_in'], 'v_ssd_conv_w': out['v_ssd_conv_w'], 'v_ssd_conv_b': out['v_ssd_conv_b'], 'v_ssd_dt_bias': out['v_ssd_dt_bias'], 'v_ssd_a_log': out['v_ssd_a_log'], 'v_ssd_d': out['v_ssd_d'], 'v_ssd_norm': out['v_ssd_norm'], 'v_ssd_w_out': out['v_ssd_w_out'], 'v_sc_w_in': out['v_sc_w_in'], 'v_sc_conv_w': out['v_sc_conv_w'], 'v_sc_w_out': out['v_sc_w_out'], 'v_final_norm': out['v_final_norm']}


def _loss(weights, diff, rest, loss_target):
    with _jax.named_scope("forward"):
        args = {**rest, TWIN_DIFF_INPUT: diff, **{k: w.astype(_WEIGHT_DTYPES[k]) for k, w in weights.items()}}
        y = _forward(args)
    with _jax.named_scope("loss_head"):
        err = _jnp.square(y.astype(_jnp.float32) - loss_target)
        return 0.5 * _jnp.sum(_jnp.mean(err, axis=-1)) if err.ndim else 0.5 * err


def _adamw(w, g, m, v):
    m = ADAM_B1 * m + (1.0 - ADAM_B1) * g
    v = ADAM_B2 * v + (1.0 - ADAM_B2) * _jnp.square(g)
    m_hat = m / (1.0 - ADAM_B1 ** ADAM_STEP)
    v_hat = v / (1.0 - ADAM_B2 ** ADAM_STEP)
    delta = -ADAM_LR * (m_hat / (_jnp.sqrt(v_hat) + ADAM_EPS) + ADAM_WD * w)
    return delta, m, v


def reference(x, ffn1_norm, ffn1_w_gu, ffn1_w_down, mix_norm, ffn2_norm, ffn2_w_gu, ffn2_w_down, sb_w_qkv, sb_w_o, ssd_w_in, ssd_conv_w, ssd_conv_b, ssd_dt_bias, ssd_a_log, ssd_d, ssd_norm, ssd_w_out, sc_w_in, sc_conv_w, sc_w_out, final_norm, loss_target, m_ffn1_norm, m_ffn1_w_gu, m_ffn1_w_down, m_mix_norm, m_ffn2_norm, m_ffn2_w_gu, m_ffn2_w_down, m_sb_w_qkv, m_sb_w_o, m_ssd_w_in, m_ssd_conv_w, m_ssd_conv_b, m_ssd_dt_bias, m_ssd_a_log, m_ssd_d, m_ssd_norm, m_ssd_w_out, m_sc_w_in, m_sc_conv_w, m_sc_w_out, m_final_norm, v_ffn1_norm, v_ffn1_w_gu, v_ffn1_w_down, v_mix_norm, v_ffn2_norm, v_ffn2_w_gu, v_ffn2_w_down, v_sb_w_qkv, v_sb_w_o, v_ssd_w_in, v_ssd_conv_w, v_ssd_conv_b, v_ssd_dt_bias, v_ssd_a_log, v_ssd_d, v_ssd_norm, v_ssd_w_out, v_sc_w_in, v_sc_conv_w, v_sc_w_out, v_final_norm):
    given = dict(x=x, ffn1_norm=ffn1_norm, ffn1_w_gu=ffn1_w_gu, ffn1_w_down=ffn1_w_down, mix_norm=mix_norm, ffn2_norm=ffn2_norm, ffn2_w_gu=ffn2_w_gu, ffn2_w_down=ffn2_w_down, sb_w_qkv=sb_w_qkv, sb_w_o=sb_w_o, ssd_w_in=ssd_w_in, ssd_conv_w=ssd_conv_w, ssd_conv_b=ssd_conv_b, ssd_dt_bias=ssd_dt_bias, ssd_a_log=ssd_a_log, ssd_d=ssd_d, ssd_norm=ssd_norm, ssd_w_out=ssd_w_out, sc_w_in=sc_w_in, sc_conv_w=sc_conv_w, sc_w_out=sc_w_out, final_norm=final_norm, loss_target=loss_target, m_ffn1_norm=m_ffn1_norm, m_ffn1_w_gu=m_ffn1_w_gu, m_ffn1_w_down=m_ffn1_w_down, m_mix_norm=m_mix_norm, m_ffn2_norm=m_ffn2_norm, m_ffn2_w_gu=m_ffn2_w_gu, m_ffn2_w_down=m_ffn2_w_down, m_sb_w_qkv=m_sb_w_qkv, m_sb_w_o=m_sb_w_o, m_ssd_w_in=m_ssd_w_in, m_ssd_conv_w=m_ssd_conv_w, m_ssd_conv_b=m_ssd_conv_b, m_ssd_dt_bias=m_ssd_dt_bias, m_ssd_a_log=m_ssd_a_log, m_ssd_d=m_ssd_d, m_ssd_norm=m_ssd_norm, m_ssd_w_out=m_ssd_w_out, m_sc_w_in=m_sc_w_in, m_sc_conv_w=m_sc_conv_w, m_sc_w_out=m_sc_w_out, m_final_norm=m_final_norm, v_ffn1_norm=v_ffn1_norm, v_ffn1_w_gu=v_ffn1_w_gu, v_ffn1_w_down=v_ffn1_w_down, v_mix_norm=v_mix_norm, v_ffn2_norm=v_ffn2_norm, v_ffn2_w_gu=v_ffn2_w_gu, v_ffn2_w_down=v_ffn2_w_down, v_sb_w_qkv=v_sb_w_qkv, v_sb_w_o=v_sb_w_o, v_ssd_w_in=v_ssd_w_in, v_ssd_conv_w=v_ssd_conv_w, v_ssd_conv_b=v_ssd_conv_b, v_ssd_dt_bias=v_ssd_dt_bias, v_ssd_a_log=v_ssd_a_log, v_ssd_d=v_ssd_d, v_ssd_norm=v_ssd_norm, v_ssd_w_out=v_ssd_w_out, v_sc_w_in=v_sc_w_in, v_sc_conv_w=v_sc_conv_w, v_sc_w_out=v_sc_w_out, v_final_norm=v_final_norm)
    weights = {n: given[n] for n in TWIN_WEIGHTS}
    shared = {n: given[n] for n in SHARED_INPUTS}
    per_example = {n: given[n] for n in ['x']}
    grad_fn = _jax.value_and_grad(_loss, argnums=(0, 1))

    def one_microbatch(ex, loss_target):
        ex = dict(ex)
        diff = ex.pop(TWIN_DIFF_INPUT)
        return grad_fn(weights, diff, {**shared, **ex}, loss_target)

    if N_MICROBATCH == 1:
        loss, (grad_w, grad_x) = one_microbatch(per_example, given["loss_target"])
    else:
        def body(carry, xs):
            loss_sum, grad_sum = carry
            l_k, (gw_k, gx_k) = one_microbatch(xs[0], xs[1])
            with _jax.named_scope("update"):
                return (loss_sum + l_k, _jax.tree.map(_jnp.add, grad_sum, gw_k)), gx_k

        init = (_jnp.zeros((), _jnp.float32), _jax.tree.map(_jnp.zeros_like, weights))
        (loss, grad_w), grad_x = _jax.lax.scan(body, init, (per_example, given["loss_target"]))
    with _jax.named_scope("update"):
        delta_w, new_m, new_v = {}, {}, {}
        for n in TWIN_WEIGHTS:
            delta_w[n], new_m[n], new_v[n] = _adamw(weights[n], grad_w[n], given["m_" + n], given["v_" + n])
    return (loss, grad_x, *[grad_w[n] for n in TWIN_WEIGHTS], *[delta_w[n] for n in TWIN_WEIGHTS],
            *[new_m[n] for n in TWIN_WEIGHTS], *[new_v[n] for n in TWIN_WEIGHTS])
```

```python
import functools
import math

import jax
import jax.numpy as jnp
from jax import lax
from jax.experimental import pallas as pl
from jax.experimental.pallas import tpu as pltpu

F32 = jnp.float32
CDT = jnp.bfloat16
HI = lax.Precision.HIGHEST
BS = pl.BlockSpec
SDS = jax.ShapeDtypeStruct
MESH = pl.DeviceIdType.MESH

RMS_EPS = 1e-6
ADAM_LR, ADAM_B1, ADAM_B2, ADAM_EPS, ADAM_WD, ADAM_STEP = 0.001, 0.9, 0.999, 1e-08, 0.01, 10
V7X_VMEM_LIMIT_BYTES = 56 * 1024 * 1024
LANES = 128
CHUNK = 128
HEAD_DIM = 64
SSD_HPG = 4
N_CHIPS = 4
N_DEV = 8
PACK_W = 1024


class Cfg:
    def __init__(self, d_model=1024, d_ff=2816, depth=4, sb_heads=16, ssd_groups=8, ssd_state=128,
                 ssd_conv=4, sc_width=3):
        self.D, self.F, self.depth = d_model, d_ff, depth
        self.sb_heads = sb_heads
        self.DI = 2 * d_model
        self.G = ssd_groups
        self.NH = self.DI // HEAD_DIM
        assert self.NH == SSD_HPG * self.G and ssd_state == LANES and sb_heads * HEAD_DIM == d_model
        self.conv_dim = self.DI + 2 * self.G * ssd_state
        self.in_dim = self.DI + self.conv_dim + self.NH
        self.ssd_conv, self.sc_width = ssd_conv, sc_width
        self.n_sb, self.n_ssd, self.n_sc = (depth + 2) // 3, (depth + 1) // 3, depth // 3


def _pallas(body, **kw):
    return pl.pallas_call(body, **kw)


def _params(*sem):
    return pltpu.CompilerParams(dimension_semantics=sem, vmem_limit_bytes=V7X_VMEM_LIMIT_BYTES)


def _tile(n, cap):
    if n <= cap:
        return n
    best = None
    for m in range(1, n // LANES + 1):
        t = m * LANES
        if n % t == 0 and t <= cap:
            best = t
    assert best is not None, (n, cap)
    return best


def _sigmoid(x):
    return 1.0 / (1.0 + jnp.exp(-x))


def _softplus_neg_abs(x):
    return jnp.log(1.0 + jnp.exp(-jnp.abs(x)))


def _colsel(blk, idx):
    lane = lax.broadcasted_iota(jnp.int32, blk.shape, 1)
    return jnp.sum(jnp.where(lane == idx, blk, 0.0), axis=1, keepdims=True)


def _rowsel(blk, idx):
    sub = lax.broadcasted_iota(jnp.int32, blk.shape, 0)
    return jnp.sum(jnp.where(sub == idx, blk, 0.0), axis=0, keepdims=True)


def _nt(a, b):
    return lax.dot_general(a, b, (((1,), (1,)), ((), ())), preferred_element_type=F32)


def _tn(a, b):
    return lax.dot_general(a, b, (((0,), (0,)), ((), ())), preferred_element_type=F32)


def _nn(a, b):
    return jnp.dot(a, b, preferred_element_type=F32)


def matmul(a, b, *, name, ta=False, tb=False, out_dtype=F32, res=None, scale=1.0, tm=512, tn=1408, tk=1408):
    M, K = (a.shape[1], a.shape[0]) if ta else a.shape
    N, K2 = b.shape if tb else (b.shape[1], b.shape[0])
    assert K == K2, (a.shape, b.shape, ta, tb)
    tm, tn, tk = _tile(M, tm), _tile(N, tn), _tile(K, tk)
    nk = K // tk
    dn = (((0 if ta else 1,), (1 if tb else 0,)), ((), ()))

    def body(*refs):
        if res is None:
            a_ref, b_ref, o_ref, acc = refs
        else:
            a_ref, b_ref, r_ref, o_ref, acc = refs
        k = pl.program_id(2)

        @pl.when(k == 0)
        def _():
            acc[...] = jnp.zeros_like(acc)

        acc[...] += lax.dot_general(a_ref[...].astype(CDT), b_ref[...].astype(CDT), dn,
                                    preferred_element_type=F32)

        @pl.when(k == nk - 1)
        def _():
            o = acc[...] * scale
            if res is not None:
                o = o + r_ref[...].astype(F32)
            o_ref[...] = o.astype(o_ref.dtype)

    a_spec = BS((tk, tm), lambda i, j, k: (k, i)) if ta else BS((tm, tk), lambda i, j, k: (i, k))
    b_spec = BS((tn, tk), lambda i, j, k: (j, k)) if tb else BS((tk, tn), lambda i, j, k: (k, j))
    o_spec = BS((tm, tn), lambda i, j, k: (i, j))
    ins, specs = [a, b], [a_spec, b_spec]
    if res is not None:
        ins.append(res)
        specs.append(o_spec)
    return _pallas(body, grid=(M // tm, N // tn, nk), in_specs=specs, out_specs=o_spec,
                   out_shape=SDS((M, N), out_dtype), scratch_shapes=[pltpu.VMEM((tm, tn), F32)],
                   compiler_params=_params("parallel", "parallel", "arbitrary"), name=name)(*ins)


def rms_fwd(x, g, *, name):
    T, D = x.shape
    tm = _tile(T, 512)

    def body(x_ref, g_ref, o_ref):
        xv = x_ref[...]
        r = lax.rsqrt(jnp.mean(xv * xv, axis=-1, keepdims=True) + RMS_EPS)
        o_ref[...] = (xv * r * g_ref[...]).astype(o_ref.dtype)

    return _pallas(body, grid=(T // tm,), in_specs=[BS((tm, D), lambda i: (i, 0)), BS((1, D), lambda i: (0, 0))],
                   out_specs=BS((tm, D), lambda i: (i, 0)), out_shape=SDS((T, D), CDT),
                   compiler_params=_params("parallel"), name=name)(x, g)


def rms_bwd(dh, x, g, dres, *, name):
    T, D = x.shape
    tm = _tile(T, 512)

    def body(dh_ref, x_ref, g_ref, r_ref, dx_ref, dg_ref):
        i = pl.program_id(0)
        xv = x_ref[...]
        r = lax.rsqrt(jnp.mean(xv * xv, axis=-1, keepdims=True) + RMS_EPS)
        xhat = xv * r
        dhv = dh_ref[...].astype(F32)
        dxh = dhv * g_ref[...]
        m = jnp.mean(dxh * xhat, axis=-1, keepdims=True)
        dx_ref[...] = r * (dxh - xhat * m) + r_ref[...]

        @pl.when(i == 0)
        def _():
            dg_ref[...] = jnp.zeros_like(dg_ref)

        dg_ref[...] += jnp.sum(dhv * xhat, axis=0, keepdims=True)

    row = BS((tm, D), lambda i: (i, 0))
    vec = BS((1, D), lambda i: (0, 0))
    return _pallas(body, grid=(T // tm,), in_specs=[row, row, vec, row], out_specs=[row, vec],
                   out_shape=[SDS((T, D), F32), SDS((1, D), F32)],
                   compiler_params=_params("arbitrary"), name=name)(dh, x, g, dres)


def swiglu_fwd(gu, *, name):
    T, F2 = gu.shape
    F = F2 // 2
    tm = _tile(T, 512)

    def body(gu_ref, o_ref):
        g = gu_ref[:, :F].astype(F32)
        u = gu_ref[:, F:].astype(F32)
        o_ref[...] = (g * _sigmoid(g) * u).astype(o_ref.dtype)

    return _pallas(body, grid=(T // tm,), in_specs=[BS((tm, F2), lambda i: (i, 0))],
                   out_specs=BS((tm, F), lambda i: (i, 0)), out_shape=SDS((T, F), CDT),
                   compiler_params=_params("parallel"), name=name)(gu)


def swiglu_bwd(da, gu, *, name):
    T, F2 = gu.shape
    F = F2 // 2
    tm = _tile(T, 512)

    def body(da_ref, gu_ref, o_ref):
        g = gu_ref[:, :F].astype(F32)
        u = gu_ref[:, F:].astype(F32)
        dav = da_ref[...].astype(F32)
        s = _sigmoid(g)
        o_ref[:, :F] = (dav * u * (s * (1.0 + g * (1.0 - s)))).astype(o_ref.dtype)
        o_ref[:, F:] = (dav * (g * s)).astype(o_ref.dtype)

    return _pallas(body, grid=(T // tm,), in_specs=[BS((tm, F), lambda i: (i, 0)), BS((tm, F2), lambda i: (i, 0))],
                   out_specs=BS((tm, F2), lambda i: (i, 0)), out_shape=SDS((T, F2), CDT),
                   compiler_params=_params("parallel"), name=name)(da, gu)


def loss_head(x, g, tgt, *, name):
    T, D = x.shape
    tm = _tile(T, 512)

    def body(x_ref, g_ref, t_ref, l_ref, dx_ref, dg_ref):
        i = pl.program_id(0)
        xv = x_ref[...]
        gv = g_ref[...]
        r = lax.rsqrt(jnp.mean(xv * xv, axis=-1, keepdims=True) + RMS_EPS)
        xhat = xv * r
        err = xhat * gv - t_ref[...]
        part = 0.5 * jnp.sum(jnp.mean(err * err, axis=-1, keepdims=True), axis=0, keepdims=True)
        dy = err * (1.0 / D)
        dxh = dy * gv
        m = jnp.mean(dxh * xhat, axis=-1, keepdims=True)
        dx_ref[...] = r * (dxh - xhat * m)

        @pl.when(i == 0)
        def _():
            dg_ref[...] = jnp.zeros_like(dg_ref)
            l_ref[...] = jnp.zeros_like(l_ref)

        dg_ref[...] += jnp.sum(dy * xhat, axis=0, keepdims=True)
        l_ref[...] += jnp.broadcast_to(part, l_ref.shape)

    row = BS((tm, D), lambda i: (i, 0))
    vec = BS((1, D), lambda i: (0, 0))
    return _pallas(body, grid=(T // tm,), in_specs=[row, vec, row],
                   out_specs=[BS((1, LANES), lambda i: (0, 0)), row, vec],
                   out_shape=[SDS((1, LANES), F32), SDS((T, D), F32), SDS((1, D), F32)],
                   compiler_params=_params("arbitrary"), name=name)(x, g, tgt)


def adamw(w, g, m, v, *, name):
    R, C = w.shape
    tr = R
    if R * C * 4 > (1 << 20):
        for cand in range(8, R + 1, 8):
            if R % cand == 0 and cand * C * 4 <= (1 << 20):
                tr = cand

    def body(w_ref, g_ref, m_ref, v_ref, d_ref, nm_ref, nv_ref):
        gv = g_ref[...]
        nm = ADAM_B1 * m_ref[...] + (1.0 - ADAM_B1) * gv
        nv = ADAM_B2 * v_ref[...] + (1.0 - ADAM_B2) * (gv * gv)
        m_hat = nm / (1.0 - ADAM_B1 ** ADAM_STEP)
        v_hat = nv / (1.0 - ADAM_B2 ** ADAM_STEP)
        d_ref[...] = -ADAM_LR * (m_hat / (jnp.sqrt(v_hat) + ADAM_EPS) + ADAM_WD * w_ref[...])
        nm_ref[...] = nm
        nv_ref[...] = nv

    blk = BS((tr, C), lambda i: (i, 0))
    return _pallas(body, grid=(R // tr,), in_specs=[blk] * 4, out_specs=[blk] * 3,
                   out_shape=[SDS((R, C), F32)] * 3, compiler_params=_params("parallel"), name=name)(w, g, m, v)


def _sb_tri():
    row = lax.broadcasted_iota(jnp.int32, (CHUNK, CHUNK), 0)
    col = lax.broadcasted_iota(jnp.int32, (CHUNK, CHUNK), 1)
    return row, col


def sb_fwd(q, k, v, *, name):
    H, T, dh = q.shape
    nq = T // CHUNK
    assert nq <= LANES
    scale = dh ** -0.5

    def body(q_ref, k_ref, v_ref, o_ref, tab_ref, acc_scr, cs_scr):
        qi = pl.program_id(1)
        qv = q_ref[0]
        row, col = _sb_tri()
        upper = (row > col).astype(F32)
        acc_scr[...] = jnp.zeros_like(acc_scr)
        cs_scr[...] = jnp.zeros_like(cs_scr)
        tab_ref[0, 0] = jnp.zeros((CHUNK, CHUNK), F32)

        def step(jj, carry):
            cs = cs_scr[...]
            j = qi - jj
            sl = pl.ds(pl.multiple_of(j * CHUNK, CHUNK), CHUNK)
            kj = k_ref[0, sl, :]
            vj = v_ref[0, sl, :]
            z = _nt(qv, kj) * scale
            lsz = jnp.minimum(z, 0.0) - _softplus_neg_abs(z)
            mask = jnp.logical_or(jj > 0, col < row)
            lk = jnp.where(mask, lsz - z, 0.0)
            tail = jnp.dot(lk, upper, precision=HI, preferred_element_type=F32) + cs
            att = jnp.where(mask, jnp.exp(lsz + tail), 0.0)
            acc_scr[...] += _nn(att.astype(CDT), vj)
            tab_ref[0, 0] = jnp.where(col == j, cs, tab_ref[0, 0])
            cs_scr[...] = cs + jnp.sum(lk, axis=1, keepdims=True)
            return carry

        lax.fori_loop(0, qi + 1, step, 0)
        o_ref[0] = acc_scr[...].astype(o_ref.dtype)

    qs = BS((1, CHUNK, dh), lambda h, i: (h, i, 0))
    ks = BS((1, T, dh), lambda h, i: (h, 0, 0))
    return _pallas(body, grid=(H, nq), in_specs=[qs, ks, ks],
                   out_specs=[qs, BS((1, 1, CHUNK, CHUNK), lambda h, i: (h, i, 0, 0))],
                   out_shape=[SDS((H, T, dh), CDT), SDS((H, nq, CHUNK, CHUNK), F32)],
                   scratch_shapes=[pltpu.VMEM((CHUNK, dh), F32), pltpu.VMEM((CHUNK, 1), F32)],
                   compiler_params=_params("parallel", "parallel"), name=name)(q, k, v)


def sb_bwd(q, k, v, do, tab, *, name):
    H, T, dh = q.shape
    nq = T // CHUNK
    scale = dh ** -0.5

    def body(q_ref, k_ref, v_ref, do_ref, tab_ref, dq_ref, dk_ref, dv_ref, dk_acc, dv_acc):
        qi = pl.program_id(1)

        @pl.when(qi == 0)
        def _():
            dk_acc[...] = jnp.zeros_like(dk_acc)
            dv_acc[...] = jnp.zeros_like(dv_acc)

        qv = q_ref[0]
        dov = do_ref[0]
        tab = tab_ref[0, 0]
        row, col = _sb_tri()
        upper = (row > col).astype(F32)
        before = (row < col).astype(F32)

        def step(j, carry):
            dq, pfx = carry
            sl = pl.ds(pl.multiple_of(j * CHUNK, CHUNK), CHUNK)
            kj = k_ref[0, sl, :]
            vj = v_ref[0, sl, :]
            z = _nt(qv, kj) * scale
            lsz = jnp.minimum(z, 0.0) - _softplus_neg_abs(z)
            mask = jnp.logical_or(j < qi, col < row)
            lk = jnp.where(mask, lsz - z, 0.0)
            cs = jnp.sum(jnp.where(col == j, tab, 0.0), axis=1, keepdims=True)
            tail = jnp.dot(lk, upper, precision=HI, preferred_element_type=F32) + cs
            att = jnp.where(mask, jnp.exp(lsz + tail), 0.0)
            dl = att * _nt(dov, vj)
            pre = jnp.dot(dl, before, precision=HI, preferred_element_type=F32) + pfx
            sig = jnp.exp(lsz)
            dz = ((dl * (1.0 - sig) - jnp.where(mask, sig * pre, 0.0)) * scale).astype(CDT)
            dq = dq + _nn(dz, kj)
            dk_acc[sl, :] += _tn(dz, qv)
            dv_acc[sl, :] += _tn(att.astype(CDT), dov)
            pfx = pfx + jnp.sum(dl, axis=1, keepdims=True)
            return dq, pfx

        dq, _ = lax.fori_loop(0, qi + 1, step, (jnp.zeros((CHUNK, dh), F32), jnp.zeros((CHUNK, 1), F32)))
        dq_ref[0] = dq.astype(dq_ref.dtype)

        @pl.when(qi == nq - 1)
        def _():
            dk_ref[0] = dk_acc[...].astype(dk_ref.dtype)
            dv_ref[0] = dv_acc[...].astype(dv_ref.dtype)

    qs = BS((1, CHUNK, dh), lambda h, i: (h, i, 0))
    ks = BS((1, T, dh), lambda h, i: (h, 0, 0))
    ts = BS((1, 1, CHUNK, CHUNK), lambda h, i: (h, i, 0, 0))
    return _pallas(body, grid=(H, nq), in_specs=[qs, ks, ks, qs, ts], out_specs=[qs, ks, ks],
                   out_shape=[SDS((H, T, dh), CDT)] * 3,
                   scratch_shapes=[pltpu.VMEM((T, dh), F32), pltpu.VMEM((T, dh), F32)],
                   compiler_params=_params("parallel", "arbitrary"), name=name)(q, k, v, do, tab)


HALO = 8


def _shift_down(cur, prev8, s):
    if s == 0:
        return cur
    tt = cur.shape[0]
    row = lax.broadcasted_iota(jnp.int32, cur.shape, 0)
    fix = jnp.concatenate([pltpu.roll(prev8, s, 0), jnp.zeros((tt - HALO, cur.shape[1]), cur.dtype)], axis=0)
    return jnp.where(row < s, fix, pltpu.roll(cur, s, 0))


def _shift_up(cur, next8, s):
    if s == 0:
        return cur
    tt = cur.shape[0]
    row = lax.broadcasted_iota(jnp.int32, cur.shape, 0)
    fix = jnp.concatenate([jnp.zeros((tt - HALO, cur.shape[1]), cur.dtype), pltpu.roll(next8, HALO - s, 0)], axis=0)
    return jnp.where(row >= tt - s, fix, pltpu.roll(cur, tt - s, 0))


def _halo_specs(tt, width, T):
    per = tt // HALO
    last = T // HALO - 1
    cur = BS((tt, width), lambda i: (i, 0))
    prev = BS((HALO, width), lambda i: (jnp.maximum(i * per - 1, 0), 0))
    nxt = BS((HALO, width), lambda i: (jnp.minimum((i + 1) * per, last), 0))
    return cur, prev, nxt


def sc_fwd(proj, w, *, name):
    T, D3 = proj.shape
    D = D3 // 3
    K = w.shape[0]
    tt = _tile(T, 256)
    cur, prev, _ = _halo_specs(tt, D3, T)

    def body(p_ref, pp_ref, w_ref, o_ref):
        i = pl.program_id(0)
        pc = p_ref[:, D:2 * D] * p_ref[:, 2 * D:]
        pp = jnp.where(i > 0, pp_ref[:, D:2 * D] * pp_ref[:, 2 * D:], 0.0)
        u = jnp.zeros((tt, D), F32)
        for kk in range(K):
            u = u + w_ref[kk:kk + 1, :] * _shift_down(pc, pp, K - 1 - kk)
        o_ref[...] = (p_ref[:, :D] * u).astype(o_ref.dtype)

    return _pallas(body, grid=(T // tt,), in_specs=[cur, prev, BS((K, D), lambda i: (0, 0))],
                   out_specs=BS((tt, D), lambda i: (i, 0)), out_shape=SDS((T, D), CDT),
                   compiler_params=_params("parallel"), name=name)(proj, proj, w)


def sc_bwd(dm, proj, w, *, name):
    T, D3 = proj.shape
    D = D3 // 3
    K = w.shape[0]
    tt = _tile(T, 256)
    cur, prev, nxt = _halo_specs(tt, D3, T)
    dcur, _, dnxt = _halo_specs(tt, D, T)
    nsteps = T // tt

    def body(dm_ref, dmn_ref, p_ref, pp_ref, pn_ref, w_ref, o_ref, dw_ref):
        i = pl.program_id(0)
        b = p_ref[:, :D]
        c = p_ref[:, D:2 * D]
        h = p_ref[:, 2 * D:]
        pc = c * h
        pp = jnp.where(i > 0, pp_ref[:, D:2 * D] * pp_ref[:, 2 * D:], 0.0)
        dmv = dm_ref[...]
        du = dmv * b
        dun = jnp.where(i < nsteps - 1, dmn_ref[...] * pn_ref[:, :D], 0.0)
        u = jnp.zeros((tt, D), F32)
        dp = jnp.zeros((tt, D), F32)
        rows = []
        for kk in range(K):
            sh = _shift_down(pc, pp, K - 1 - kk)
            u = u + w_ref[kk:kk + 1, :] * sh
            dp = dp + w_ref[kk:kk + 1, :] * _shift_up(du, dun, K - 1 - kk)
            rows.append(jnp.sum(du * sh, axis=0, keepdims=True))
        rows.append(jnp.zeros((HALO - K, D), F32))
        o_ref[:, :D] = (dmv * u).astype(o_ref.dtype)
        o_ref[:, D:2 * D] = (dp * h).astype(o_ref.dtype)
        o_ref[:, 2 * D:] = (dp * c).astype(o_ref.dtype)

        @pl.when(i == 0)
        def _():
            dw_ref[...] = jnp.zeros_like(dw_ref)

        dw_ref[...] += jnp.concatenate(rows, axis=0)

    return _pallas(body, grid=(nsteps,), in_specs=[dcur, dnxt, cur, prev, nxt, BS((K, D), lambda i: (0, 0))],
                   out_specs=[BS((tt, D3), lambda i: (i, 0)), BS((HALO, D), lambda i: (0, 0))],
                   out_shape=[SDS((T, D3), CDT), SDS((HALO, D), F32)],
                   compiler_params=_params("arbitrary"), name=name)(dm, dm, proj, proj, proj, w)


def ssd_conv_fwd(xpre, w, b, *, name):
    T, C = xpre.shape
    K = w.shape[0]
    tt = _tile(T, 256)
    cur, prev, _ = _halo_specs(tt, C, T)

    def body(x_ref, xp_ref, w_ref, b_ref, pre_ref, o_ref):
        i = pl.program_id(0)
        xv = x_ref[...]
        xp = jnp.where(i > 0, xp_ref[...], 0.0)
        acc = jnp.broadcast_to(b_ref[...], (tt, C))
        for kk in range(K):
            acc = acc + w_ref[kk:kk + 1, :] * _shift_down(xv, xp, K - 1 - kk)
        pre_ref[...] = acc
        o_ref[...] = acc * _sigmoid(acc)

    row = BS((tt, C), lambda i: (i, 0))
    return _pallas(body, grid=(T // tt,), in_specs=[cur, prev, BS((K, C), lambda i: (0, 0)), BS((1, C), lambda i: (0, 0))],
                   out_specs=[row, row], out_shape=[SDS((T, C), F32)] * 2,
                   compiler_params=_params("parallel"), name=name)(xpre, xpre, w, b)


def ssd_conv_bwd(dxs, db_, dc_, pre, xpre, w, *, name):
    T, C = xpre.shape
    K = w.shape[0]
    tt = _tile(T, 256)
    cur, prev, nxt = _halo_specs(tt, C, T)
    widths = (dxs.shape[1], db_.shape[1], dc_.shape[1])
    gspecs = []
    for wd in widths:
        c_, _, n_ = _halo_specs(tt, wd, T)
        gspecs += [c_, n_]
    nsteps = T // tt

    def dsilu(p):
        s = _sigmoid(p)
        return s * (1.0 + p * (1.0 - s))

    def body(g0, g0n, g1, g1n, g2, g2n, pre_ref, pren_ref, x_ref, xp_ref, w_ref, dx_ref, dw_ref, dbias_ref):
        i = pl.program_id(0)
        gcur = jnp.concatenate([g0[...], g1[...], g2[...]], axis=1)
        gnxt = jnp.concatenate([g0n[...], g1n[...], g2n[...]], axis=1)
        dpre = gcur * dsilu(pre_ref[...])
        dpren = jnp.where(i < nsteps - 1, gnxt * dsilu(pren_ref[...]), 0.0)
        xv = x_ref[...]
        xp = jnp.where(i > 0, xp_ref[...], 0.0)
        dx = jnp.zeros((tt, C), F32)
        rows = []
        for kk in range(K):
            dx = dx + w_ref[kk:kk + 1, :] * _shift_up(dpre, dpren, K - 1 - kk)
            rows.append(jnp.sum(dpre * _shift_down(xv, xp, K - 1 - kk), axis=0, keepdims=True))
        rows.append(jnp.zeros((HALO - K, C), F32))
        dx_ref[...] = dx.astype(dx_ref.dtype)

        @pl.when(i == 0)
        def _():
            dw_ref[...] = jnp.zeros_like(dw_ref)
            dbias_ref[...] = jnp.zeros_like(dbias_ref)

        dw_ref[...] += jnp.concatenate(rows, axis=0)
        dbias_ref[...] += jnp.sum(dpre, axis=0, keepdims=True)

    return _pallas(body, grid=(nsteps,),
                   in_specs=gspecs + [cur, nxt, cur, prev, BS((K, C), lambda i: (0, 0))],
                   out_specs=[BS((tt, C), lambda i: (i, 0)), BS((HALO, C), lambda i: (0, 0)), BS((1, C), lambda i: (0, 0))],
                   out_shape=[SDS((T, C), CDT), SDS((HALO, C), F32), SDS((1, C), F32)],
                   compiler_params=_params("arbitrary"), name=name)(dxs, dxs, db_, db_, dc_, dc_, pre, pre, xpre, xpre, w)


def ssd_prep_fwd(dtpre, bias, alog, *, name):
    T = dtpre.shape[0]
    nc = T // CHUNK

    def body(p_ref, b_ref, a_ref, dt_ref, ac_ref, dtT_ref, acT_ref):
        xv = p_ref[...] + b_ref[...]
        dt = jnp.maximum(xv, 0.0) + _softplus_neg_abs(xv)
        row, col = _sb_tri()
        lower = (col <= row).astype(F32)
        ac = jnp.dot(lower, dt * -jnp.exp(a_ref[...]), precision=HI, preferred_element_type=F32)
        dt_ref[...] = dt
        ac_ref[...] = ac
        dtT_ref[...] = dt.T
        acT_ref[...] = ac.T

    blk = BS((CHUNK, LANES), lambda c: (c, 0))
    blkT = BS((LANES, CHUNK), lambda c: (0, c))
    vec = BS((1, LANES), lambda c: (0, 0))
    return _pallas(body, grid=(nc,), in_specs=[blk, vec, vec], out_specs=[blk, blk, blkT, blkT],
                   out_shape=[SDS((T, LANES), F32)] * 2 + [SDS((LANES, T), F32)] * 2,
                   compiler_params=_params("parallel"), name=name)(dtpre, bias, alog)


def ssd_prep_bwd(dac_c, ddt_c, dac_r, ddt_r, dtpre, bias, alog, *, name):
    T = dtpre.shape[0]
    nc = T // CHUNK

    def body(dac_ref, ddt_ref, dacr_ref, ddtr_ref, p_ref, b_ref, a_ref, o_ref, da_ref, db_ref):
        c = pl.program_id(0)
        xv = p_ref[...] + b_ref[...]
        dt = jnp.maximum(xv, 0.0) + _softplus_neg_abs(xv)
        av = -jnp.exp(a_ref[...])
        row, col = _sb_tri()
        after = (col >= row).astype(F32)
        dda = jnp.dot(after, dac_ref[...] + dacr_ref[...].T, precision=HI, preferred_element_type=F32)
        dpre = (ddt_ref[...] + ddtr_ref[...].T + av * dda) * _sigmoid(xv)
        o_ref[...] = dpre

        @pl.when(c == 0)
        def _():
            da_ref[...] = jnp.zeros_like(da_ref)
            db_ref[...] = jnp.zeros_like(db_ref)

        da_ref[...] += jnp.sum(dt * dda, axis=0, keepdims=True) * av
        db_ref[...] += jnp.sum(dpre, axis=0, keepdims=True)

    blk = BS((CHUNK, LANES), lambda c: (c, 0))
    blkT = BS((LANES, CHUNK), lambda c: (0, c))
    vec = BS((1, LANES), lambda c: (0, 0))
    return _pallas(body, grid=(nc,), in_specs=[blk, blk, blkT, blkT, blk, vec, vec], out_specs=[blk, vec, vec],
                   out_shape=[SDS((T, LANES), F32), SDS((1, LANES), F32), SDS((1, LANES), F32)],
                   compiler_params=_params("arbitrary"), name=name)(dac_c, ddt_c, dac_r, ddt_r, dtpre, bias, alog)


def _ssd_specs(cfg, T, rev):
    nc = T // CHUNK
    GW = SSD_HPG * HEAD_DIM
    bo = cfg.DI // LANES
    co = (cfg.DI + cfg.G * LANES) // LANES
    ci = (lambda c: nc - 1 - c) if rev else (lambda c: c)
    return dict(
        x=BS((CHUNK, GW), lambda g, c: (ci(c), g)),
        b=BS((CHUNK, LANES), lambda g, c: (ci(c), bo + g)),
        c=BS((CHUNK, LANES), lambda g, c: (ci(c), co + g)),
        col=BS((CHUNK, LANES), lambda g, c: (ci(c), 0)),
        row=BS((LANES, CHUNK), lambda g, c: (0, ci(c))),
        vec=BS((1, LANES), lambda g, c: (0, 0)),
        ng=BS((1, GW), lambda g, c: (0, g)),
        hp=BS((1, 1, SSD_HPG, HEAD_DIM, LANES), lambda g, c: (g, ci(c), 0, 0, 0)),
    )


def ssd_chunk_fwd(xbc, z, dt, ac, dtT, acT, dskip, ng, cfg, *, name):
    T = xbc.shape[0]
    nc = T // CHUNK
    G, DI = cfg.G, cfg.DI
    GW = SSD_HPG * HEAD_DIM
    sp = _ssd_specs(cfg, T, False)

    def body(x_ref, b_ref, c_ref, z_ref, dt_ref, ac_ref, dtT_ref, acT_ref, d_ref, ng_ref,
             y_ref, yn_ref, hp_ref, h_scr):
        g = pl.program_id(0)
        c = pl.program_id(1)

        @pl.when(c == 0)
        def _():
            h_scr[...] = jnp.zeros_like(h_scr)

        row, col = _sb_tri()
        causal = row >= col
        Bm = b_ref[...].astype(CDT)
        Cm = c_ref[...].astype(CDT)
        Gm = _nt(Cm, Bm)
        dtb, acb, dtTb, acTb, dv = dt_ref[...], ac_ref[...], dtT_ref[...], acT_ref[...], d_ref[...]
        for r in range(SSD_HPG):
            hh = g * SSD_HPG + r
            ac_c, ac_r = _colsel(acb, hh), _rowsel(acTb, hh)
            dt_c, dt_r = _colsel(dtb, hh), _rowsel(dtTb, hh)
            S = jnp.where(causal, jnp.exp(ac_c - ac_r), 0.0)
            Wm = Gm * S * dt_r
            Xr = x_ref[:, r * HEAD_DIM:(r + 1) * HEAD_DIM]
            Hp = h_scr[r]
            Q = _nt(Cm, Hp.astype(CDT))
            Y = _nn(Wm.astype(CDT), Xr.astype(CDT)) + jnp.exp(ac_c) * Q + _colsel(dv, hh) * Xr
            aL = _rowsel(ac_c, CHUNK - 1)
            e = jnp.exp(aL - ac_c) * dt_c
            hp_ref[0, 0, r] = Hp
            h_scr[r] = jnp.exp(aL) * Hp + _tn((Xr * e).astype(CDT), Bm)
            y_ref[:, r * HEAD_DIM:(r + 1) * HEAD_DIM] = Y
        zz = z_ref[...]
        yz = y_ref[...] * (zz * _sigmoid(zz))
        rstd = lax.rsqrt(jnp.mean(yz * yz, axis=-1, keepdims=True) + RMS_EPS)
        yn_ref[...] = (yz * rstd * ng_ref[...]).astype(yn_ref.dtype)

    return _pallas(body, grid=(G, nc),
                   in_specs=[sp["x"], sp["b"], sp["c"], sp["x"], sp["col"], sp["col"], sp["row"], sp["row"], sp["vec"], sp["ng"]],
                   out_specs=[sp["x"], sp["x"], sp["hp"]],
                   out_shape=[SDS((T, DI), F32), SDS((T, DI), CDT), SDS((G, nc, SSD_HPG, HEAD_DIM, LANES), F32)],
                   scratch_shapes=[pltpu.VMEM((SSD_HPG, HEAD_DIM, LANES), F32)],
                   compiler_params=_params("parallel", "arbitrary"), name=name)(xbc, xbc, xbc, z, dt, ac, dtT, acT, dskip, ng)


def ssd_chunk_bwd(dyn, xbc, z, y, hp, dt, ac, dtT, acT, dskip, ng, cfg, *, name):
    T = xbc.shape[0]
    nc = T // CHUNK
    G, DI = cfg.G, cfg.DI
    GW = SSD_HPG * HEAD_DIM
    sp = _ssd_specs(cfg, T, True)

    def body(dyn_ref, x_ref, b_ref, c_ref, z_ref, y_ref, hp_ref, dt_ref, ac_ref, dtT_ref, acT_ref, d_ref, ng_ref,
             dx_ref, db_ref, dc_ref, dz_ref, colp_ref, rowp_ref, dd_ref, dng_ref, dh_scr):
        g = pl.program_id(0)
        c = pl.program_id(1)

        @pl.when(c == 0)
        def _():
            dh_scr[...] = jnp.zeros_like(dh_scr)
            dd_ref[...] = jnp.zeros_like(dd_ref)
            dng_ref[...] = jnp.zeros_like(dng_ref)

        zz = z_ref[...]
        sg = _sigmoid(zz)
        gate = zz * sg
        yv = y_ref[...]
        yz = yv * gate
        rstd = lax.rsqrt(jnp.mean(yz * yz, axis=-1, keepdims=True) + RMS_EPS)
        nhat = yz * rstd
        dynv = dyn_ref[...].astype(F32)
        dng_ref[...] += jnp.sum(dynv * nhat, axis=0, keepdims=True)
        dnh = dynv * ng_ref[...]
        dyz = rstd * (dnh - nhat * jnp.mean(dnh * nhat, axis=-1, keepdims=True))
        dz_ref[...] = (dyz * yv * (sg * (1.0 + zz * (1.0 - sg)))).astype(dz_ref.dtype)
        dy_all = dyz * gate

        row, col = _sb_tri()
        causal = row >= col
        lane1 = lax.broadcasted_iota(jnp.int32, (1, LANES), 1)
        sub8 = lax.broadcasted_iota(jnp.int32, (HALO, CHUNK), 0)
        subc = lax.broadcasted_iota(jnp.int32, (CHUNK, 1), 0)
        Bf = b_ref[...]
        Cf = c_ref[...]
        Bm = Bf.astype(CDT)
        Cm = Cf.astype(CDT)
        Gm = _nt(Cm, Bm)
        dtb, acb, dtTb, acTb, dv = dt_ref[...], ac_ref[...], dtT_ref[...], acT_ref[...], d_ref[...]
        dG = jnp.zeros((CHUNK, CHUNK), F32)
        dBm = jnp.zeros((CHUNK, LANES), F32)
        dCm = jnp.zeros((CHUNK, LANES), F32)
        colp = jnp.zeros((CHUNK, LANES), F32)
        rowp = jnp.zeros((HALO, CHUNK), F32)
        ddv = jnp.zeros((1, LANES), F32)
        for r in range(SSD_HPG):
            hh = g * SSD_HPG + r
            ac_c, ac_r = _colsel(acb, hh), _rowsel(acTb, hh)
            dt_c, dt_r = _colsel(dtb, hh), _rowsel(dtTb, hh)
            S = jnp.where(causal, jnp.exp(ac_c - ac_r), 0.0)
            GS = Gm * S
            Wm = GS * dt_r
            Xr = x_ref[:, r * HEAD_DIM:(r + 1) * HEAD_DIM]
            Xc = Xr.astype(CDT)
            Hp = hp_ref[0, 0, r]
            Hc = Hp.astype(CDT)
            dY = dy_all[:, r * HEAD_DIM:(r + 1) * HEAD_DIM]
            dYc = dY.astype(CDT)
            dsk = _colsel(dv, hh)
            ddv = ddv + jnp.where(lane1 == r, jnp.sum(jnp.sum(dY * Xr, axis=1, keepdims=True), axis=0, keepdims=True), 0.0)
            dX = dY * dsk + _tn(Wm.astype(CDT), dYc)
            dWm = _nt(dYc, Xc)
            dG = dG + dWm * S * dt_r
            Mm = dWm * Wm
            Nm = dWm * GS
            dac_c = jnp.sum(Mm, axis=1, keepdims=True)
            dac_r = -jnp.sum(Mm, axis=0, keepdims=True)
            ddt_r = jnp.sum(Nm, axis=0, keepdims=True)
            E = jnp.exp(ac_c)
            Q = _nt(Cm, Hc)
            dQ = (dY * E).astype(CDT)
            dac_c = dac_c + jnp.sum(dY * Q, axis=1, keepdims=True) * E
            dCm = dCm + _nn(dQ, Hc)
            dHp = _tn(dQ, Cm)
            dHn = dh_scr[r]
            dHc = dHn.astype(CDT)
            aL = _rowsel(ac_c, CHUNK - 1)
            eaL = jnp.exp(aL)
            ex = jnp.exp(aL - ac_c)
            e = ex * dt_c
            dHp = dHp + eaL * dHn
            daL = eaL * jnp.sum(jnp.sum(dHn * Hp, axis=1, keepdims=True), axis=0, keepdims=True)
            T2 = _nt(Bm, dHc)
            dX = dX + T2 * e
            de = jnp.sum(Xr * T2, axis=1, keepdims=True)
            dBm = dBm + _nn((Xr * e).astype(CDT), dHc)
            dee = de * e
            daL = daL + jnp.sum(dee, axis=0, keepdims=True)
            dac_c = dac_c - dee + jnp.where(subc == CHUNK - 1, daL, 0.0)
            ddt_c = de * ex
            dh_scr[r] = dHp
            dx_ref[:, r * HEAD_DIM:(r + 1) * HEAD_DIM] = dX
            colp = jnp.where(col == r, dac_c, colp)
            colp = jnp.where(col == SSD_HPG + r, ddt_c, colp)
            rowp = jnp.where(sub8 == r, dac_r, rowp)
            rowp = jnp.where(sub8 == SSD_HPG + r, ddt_r, rowp)
        dGc = dG.astype(CDT)
        db_ref[...] = dBm + _tn(dGc, Cm)
        dc_ref[...] = dCm + _nn(dGc, Bm)
        colp_ref[0] = colp
        rowp_ref[0] = rowp
        dd_ref[0] += ddv

    f32 = lambda shape: SDS(shape, F32)
    return _pallas(body, grid=(G, nc),
                   in_specs=[sp["x"], sp["x"], sp["b"], sp["c"], sp["x"], sp["x"], sp["hp"], sp["col"], sp["col"],
                             sp["row"], sp["row"], sp["vec"], sp["ng"]],
                   out_specs=[sp["x"],
                              BS((CHUNK, LANES), lambda g, c: (nc - 1 - c, g)),
                              BS((CHUNK, LANES), lambda g, c: (nc - 1 - c, g)),
                              sp["x"],
                              BS((1, CHUNK, LANES), lambda g, c: (g, nc - 1 - c, 0)),
                              BS((1, HALO, CHUNK), lambda g, c: (g, 0, nc - 1 - c)),
                              BS((1, 1, LANES), lambda g, c: (g, 0, 0)),
                              sp["ng"]],
                   out_shape=[f32((T, DI)), f32((T, G * LANES)), f32((T, G * LANES)), SDS((T, DI), CDT),
                              f32((G, T, LANES)), f32((G, HALO, T)), f32((G, 1, LANES)), f32((1, DI))],
                   scratch_shapes=[pltpu.VMEM((SSD_HPG, HEAD_DIM, LANES), F32)],
                   compiler_params=_params("parallel", "arbitrary"), name=name)(
                       dyn, xbc, xbc, xbc, z, y, hp, dt, ac, dtT, acT, dskip, ng)


HBM = pl.BlockSpec(memory_space=pltpu.HBM)
VMEM = pl.BlockSpec(memory_space=pltpu.VMEM)


def _me():
    return lax.axis_index("x"), lax.axis_index("y"), lax.axis_index("c")


def _other_chips(x, y):
    return [(1 - x, y), (x, 1 - y), (1 - x, 1 - y)]


def _rcopy(src, dst, send_sem, recv_sem, to):
    return pltpu.make_async_remote_copy(src_ref=src, dst_ref=dst, send_sem=send_sem, recv_sem=recv_sem,
                                        device_id=to, device_id_type=MESH)


def allgather_weights(pack, *, name):
    _, Rh, W = pack.shape

    def body(src, out, send_sems, recv_sems, local_sem):
        x, y, c = _me()
        k = 2 * x + y
        sibling = (x, y, 1 - c)
        chips = _other_chips(x, y)
        mine = pltpu.make_async_copy(src, out.at[k], local_sem)
        mine.start()
        sends = [_rcopy(src.at[c], out.at[k, c], send_sems.at[j], recv_sems.at[j], (cx, cy, c))
                 for j, (cx, cy) in enumerate(chips)]
        for cp in sends:
            cp.start()
        passed = []
        for j, (cx, cy) in enumerate(chips):
            kj = 2 * cx + cy
            landed = out.at[kj, c]
            _rcopy(landed, landed, send_sems.at[j], recv_sems.at[j], (cx, cy, c)).wait_recv()
            fw = _rcopy(landed, landed, send_sems.at[3 + j], recv_sems.at[3 + j], sibling)
            fw.start()
            passed.append(fw)
        for j, (cx, cy) in enumerate(chips):
            theirs = out.at[2 * cx + cy, 1 - c]
            _rcopy(theirs, theirs, send_sems.at[3 + j], recv_sems.at[3 + j], sibling).wait_recv()
        for cp in sends + passed:
            cp.wait_send()
        mine.wait()

    return _pallas(body, in_specs=[HBM], out_specs=HBM, out_shape=SDS((N_CHIPS, 2, Rh, W), pack.dtype),
                   scratch_shapes=[pltpu.SemaphoreType.DMA((6,)), pltpu.SemaphoreType.DMA((6,)), pltpu.SemaphoreType.DMA],
                   name=name)(pack)


def sibling_swap_half(p, *, name):
    _, n, Rh, W = p.shape

    def body(src, out, send_sem, recv_sem):
        x, y, c = _me()
        cp = _rcopy(src.at[1 - c], out, send_sem, recv_sem, (x, y, 1 - c))
        cp.start()
        cp.wait()

    return _pallas(body, in_specs=[HBM], out_specs=HBM, out_shape=SDS((n, Rh, W), p.dtype),
                   scratch_shapes=[pltpu.SemaphoreType.DMA, pltpu.SemaphoreType.DMA], name=name)(p)


def chip_scatter(s1, *, name):
    n, Rh, W = s1.shape

    def body(src, out, send_sems, recv_sems, local_sem):
        x, y, c = _me()
        k = 2 * x + y
        chips = _other_chips(x, y)
        mine = pltpu.make_async_copy(src.at[k], out.at[k], local_sem)
        mine.start()
        sends = [_rcopy(src.at[2 * cx + cy], out.at[k], send_sems.at[j], recv_sems.at[j], (cx, cy, c))
                 for j, (cx, cy) in enumerate(chips)]
        for cp in sends:
            cp.start()
        for j, (cx, cy) in enumerate(chips):
            slot = out.at[2 * cx + cy]
            _rcopy(slot, slot, send_sems.at[j], recv_sems.at[j], (cx, cy, c)).wait_recv()
        for cp in sends:
            cp.wait_send()
        mine.wait()

    return _pallas(body, in_specs=[HBM], out_specs=HBM, out_shape=SDS((n, Rh, W), s1.dtype),
                   scratch_shapes=[pltpu.SemaphoreType.DMA((3,)), pltpu.SemaphoreType.DMA((3,)), pltpu.SemaphoreType.DMA],
                   name=name)(s1)


def sibling_pair(r, *, name):
    Rh, W = r.shape

    def body(src, out, send_sem, recv_sem, local_sem):
        x, y, c = _me()
        mine = pltpu.make_async_copy(src, out.at[c], local_sem)
        mine.start()
        cp = _rcopy(src, out.at[c], send_sem, recv_sem, (x, y, 1 - c))
        cp.start()
        theirs = out.at[1 - c]
        _rcopy(theirs, theirs, send_sem, recv_sem, (x, y, 1 - c)).wait_recv()
        cp.wait_send()
        mine.wait()

    return _pallas(body, in_specs=[HBM], out_specs=HBM, out_shape=SDS((2, Rh, W), r.dtype),
                   scratch_shapes=[pltpu.SemaphoreType.DMA, pltpu.SemaphoreType.DMA, pltpu.SemaphoreType.DMA],
                   name=name)(r)


def allgather_small(v, *, name):
    R, W = v.shape

    def body(v_ref, out_ref, sum_ref, send_sems, recv_sems):
        x, y, c = _me()
        me = 4 * x + 2 * y + c
        out_ref[me] = v_ref[...]
        peers = []
        for m in range(1, N_DEV):
            px = 1 - x if m & 4 else x
            py = 1 - y if m & 2 else y
            pc = 1 - c if m & 1 else c
            peers.append((px, py, pc))
        sends = [_rcopy(v_ref, out_ref.at[me], send_sems.at[j], recv_sems.at[j], p) for j, p in enumerate(peers)]
        for cp in sends:
            cp.start()
        for j, (px, py, pc) in enumerate(peers):
            slot = out_ref.at[4 * px + 2 * py + pc]
            _rcopy(slot, slot, send_sems.at[j], recv_sems.at[j], (px, py, pc)).wait_recv()
        for cp in sends:
            cp.wait_send()
        acc = out_ref[0]
        for d in range(1, N_DEV):
            acc = acc + out_ref[d]
        sum_ref[...] = acc

    return _pallas(body, in_specs=[VMEM], out_specs=[VMEM, VMEM],
                   out_shape=[SDS((N_DEV, R, W), v.dtype), SDS((R, W), v.dtype)],
                   scratch_shapes=[pltpu.SemaphoreType.DMA((N_DEV - 1,)), pltpu.SemaphoreType.DMA((N_DEV - 1,))],
                   name=name)(v)


def _row_tile(R, cap=512):
    best = None
    for t in range(32, cap + 1, 32):
        if R % t == 0:
            best = t
    assert best is not None, R
    return best


def add_halves(p, recv, c_idx, *, name):
    _, n, Rh, W = p.shape
    tr = _row_tile(Rh)

    def body(c_ref, p_ref, r_ref, o_ref):
        o_ref[...] = (p_ref[0].astype(F32) + r_ref[...].astype(F32)).astype(o_ref.dtype)

    gs = pltpu.PrefetchScalarGridSpec(
        num_scalar_prefetch=1, grid=(n, Rh // tr),
        in_specs=[BS((1, 1, tr, W), lambda s, i, c_ref: (c_ref[0], s, i, 0)), BS((1, tr, W), lambda s, i, c_ref: (s, i, 0))],
        out_specs=BS((1, tr, W), lambda s, i, c_ref: (s, i, 0)))
    return _pallas(body, grid_spec=gs, out_shape=SDS((n, Rh, W), p.dtype),
                   compiler_params=_params("parallel", "parallel"), name=name)(c_idx, p, recv)


def sum_slots(got, *, name):
    n, Rh, W = got.shape
    tr = _row_tile(Rh)

    def body(g_ref, o_ref):
        acc = g_ref[0].astype(F32)
        for s in range(1, n):
            acc = acc + g_ref[s].astype(F32)
        o_ref[...] = acc

    return _pallas(body, grid=(Rh // tr,), in_specs=[BS((n, tr, W), lambda i: (0, i, 0))],
                   out_specs=BS((tr, W), lambda i: (i, 0)), out_shape=SDS((Rh, W), F32),
                   compiler_params=_params("parallel"), name=name)(got)


BIG = [("ffn1_w_gu", True), ("ffn1_w_down", False), ("ffn2_w_gu", True), ("ffn2_w_down", False),
       ("sb_w_qkv", True), ("sb_w_o", False), ("ssd_w_in", True), ("ssd_w_out", False),
       ("sc_w_in", True), ("sc_w_out", False)]
PACK_ALIGN = 64


def _big_layout(shards):
    out, off = [], 0
    for name, cols in BIG:
        shape = shards[name].shape
        rows = math.prod(shape) // PACK_W
        assert rows * PACK_W == math.prod(shape), (name, shape)
        out.append((name, cols, shape, off, rows))
        off += rows
    return out, -(-off // PACK_ALIGN) * PACK_ALIGN


def _pack_rows(pieces, total, axis):
    used = sum(p.shape[axis] for p in pieces)
    if total > used:
        pad_shape = list(pieces[0].shape)
        pad_shape[axis] = total - used
        pieces = pieces + [jnp.zeros(pad_shape, pieces[0].dtype)]
    return jnp.concatenate(pieces, axis=axis)


def _gather_big(shards, tag):
    layout, total = _big_layout(shards)
    pack = _pack_rows([shards[n].astype(CDT).reshape(-1, PACK_W) for n, *_ in layout], total, 0)
    full = allgather_weights(pack.reshape(2, total // 2, PACK_W), name=f"{tag}_allgather")
    full = full.reshape(N_CHIPS, total, PACK_W)
    out = {}
    for name, cols, (L, K, n), off, rows in layout:
        w = full[:, off:off + rows].reshape(N_CHIPS, L, K, n)
        if cols:
            out[name] = w.transpose(1, 2, 0, 3).reshape(L, K, N_CHIPS * n)
        else:
            out[name] = w.transpose(1, 0, 2, 3).reshape(L, N_CHIPS * K, n)
    return out


def _reduce_big(grads, shards, tag):
    layout, total = _big_layout(shards)
    pieces = []
    for name, cols, (L, K, n), off, rows in layout:
        g = grads[name]
        if cols:
            g = g.reshape(L, K, N_CHIPS, n).transpose(2, 0, 1, 3)
        else:
            g = g.reshape(L, N_CHIPS, K, n).transpose(1, 0, 2, 3)
        pieces.append(g.astype(CDT).reshape(N_CHIPS, rows, PACK_W))
    pack = _pack_rows(pieces, total, 1)
    half = total // 2
    p = pack.reshape(N_CHIPS, 2, half, PACK_W).transpose(1, 0, 2, 3)
    c_idx = lax.axis_index("c").astype(jnp.int32).reshape(1)
    recv = sibling_swap_half(p, name=f"{tag}_sibling_swap")
    s1 = add_halves(p, recv, c_idx, name=f"{tag}_add_halves")
    got = chip_scatter(s1, name=f"{tag}_chip_scatter")
    r = sum_slots(got, name=f"{tag}_sum_slots")
    red = sibling_pair(r, name=f"{tag}_sibling_pair").reshape(total, PACK_W)
    return {name: red[off:off + rows].reshape(shape) for name, cols, shape, off, rows in layout}


def _pad_lanes(v, width=LANES):
    return jnp.pad(v, ((0, 0), (0, width - v.shape[1])))


def _small_pack(vecs):
    flat, meta, off = [], [], 0
    for v in vecs:
        flat.append(v.reshape(-1))
        meta.append((off, v.size, v.shape))
        off += v.size
    total = -(-off // (8 * LANES)) * 8 * LANES
    flat.append(jnp.zeros((total - off,), F32))
    return jnp.concatenate(flat).reshape(-1, LANES), meta


def _small_unpack(pack, meta):
    flat = pack.reshape(-1)
    return [flat[off:off + size].reshape(shape) for off, size, shape in meta]


def _ffn_fwd(x, g, wgu, wd, tag):
    h = rms_fwd(x, g, name=f"{tag}_norm")
    gu = matmul(h, wgu, out_dtype=CDT, name=f"{tag}_gu")
    a = swiglu_fwd(gu, name=f"{tag}_act")
    return matmul(a, wd, res=x, scale=0.5, name=f"{tag}_down"), (x, h, gu, a)


def _ffn_bwd(dout, saved, g, wgu, wd, tag):
    x, h, gu, a = saved
    dwd = matmul(a, dout, ta=True, scale=0.5, name=f"{tag}_dwd")
    da = matmul(dout, wd, tb=True, scale=0.5, name=f"{tag}_da")
    dgu = swiglu_bwd(da, gu, name=f"{tag}_dact")
    dwgu = matmul(h, dgu, ta=True, name=f"{tag}_dwgu")
    dh = matmul(dgu, wgu, tb=True, name=f"{tag}_dh")
    dx, dg = rms_bwd(dh, x, g, dout, name=f"{tag}_dnorm")
    return dx, dg, dwgu, dwd


def _sb_mixer_fwd(x, g, wqkv, wo, cfg, tag):
    T, D = x.shape
    H = cfg.sb_heads
    h = rms_fwd(x, g, name=f"{tag}_norm")
    qkv = matmul(h, wqkv, out_dtype=CDT, name=f"{tag}_qkv")
    qkv = qkv.reshape(T, 3, H, HEAD_DIM).transpose(1, 2, 0, 3)
    o, tab = sb_fwd(qkv[0], qkv[1], qkv[2], name=f"{tag}_attn")
    o2 = o.transpose(1, 0, 2).reshape(T, D)
    return matmul(o2, wo, res=x, name=f"{tag}_out"), (x, h, qkv, tab, o2)


def _sb_mixer_bwd(dout, saved, g, wqkv, wo, cfg, tag):
    x, h, qkv, tab, o2 = saved
    T, D = x.shape
    H = cfg.sb_heads
    dwo = matmul(o2, dout, ta=True, name=f"{tag}_dwo")
    do = matmul(dout, wo, tb=True, out_dtype=CDT, name=f"{tag}_do")
    do = do.reshape(T, H, HEAD_DIM).transpose(1, 0, 2)
    dq, dk, dv = sb_bwd(qkv[0], qkv[1], qkv[2], do, tab, name=f"{tag}_dattn")
    dqkv = jnp.stack([dq, dk, dv]).transpose(2, 0, 1, 3).reshape(T, 3 * D)
    dwqkv = matmul(h, dqkv, ta=True, name=f"{tag}_dwqkv")
    dh = matmul(dqkv, wqkv, tb=True, name=f"{tag}_dh")
    dx, dg = rms_bwd(dh, x, g, dout, name=f"{tag}_dnorm")
    return dx, dg, dwqkv, dwo


def _sc_mixer_fwd(x, g, win, convw, wout, tag):
    h = rms_fwd(x, g, name=f"{tag}_norm")
    proj = matmul(h, win, name=f"{tag}_in")
    m = sc_fwd(proj, convw, name=f"{tag}_conv")
    return matmul(m, wout, res=x, name=f"{tag}_out"), (x, h, proj, m)


def _sc_mixer_bwd(dout, saved, g, win, convw, wout, tag):
    x, h, proj, m = saved
    dwout = matmul(m, dout, ta=True, name=f"{tag}_dwout")
    dm = matmul(dout, wout, tb=True, name=f"{tag}_dm")
    dproj, dconvw = sc_bwd(dm, proj, convw, name=f"{tag}_dconv")
    dwin = matmul(h, dproj, ta=True, name=f"{tag}_dwin")
    dh = matmul(dproj, win, tb=True, name=f"{tag}_dh")
    dx, dg = rms_bwd(dh, x, g, dout, name=f"{tag}_dnorm")
    return dx, dg, dwin, dconvw[:convw.shape[0]], dwout


def _ssd_split_w(win, cfg):
    DI, CV = cfg.DI, cfg.conv_dim
    return win[:, :DI], win[:, DI:DI + CV], _pad_lanes(win[:, DI + CV:])


def _ssd_mixer_fwd(x, g, win, convw, convb, dtb, alog, dskip, ng, wout, cfg, tag):
    wz, wx, wdt = _ssd_split_w(win, cfg)
    h = rms_fwd(x, g, name=f"{tag}_norm")
    z = matmul(h, wz, name=f"{tag}_z")
    xpre = matmul(h, wx, name=f"{tag}_xbc")
    dtpre = matmul(h, wdt, name=f"{tag}_dt")
    pre, xbc = ssd_conv_fwd(xpre, convw, convb, name=f"{tag}_conv")
    dt, ac, dtT, acT = ssd_prep_fwd(dtpre, dtb, alog, name=f"{tag}_prep")
    y, yn, hp = ssd_chunk_fwd(xbc, z, dt, ac, dtT, acT, dskip, ng, cfg, name=f"{tag}_scan")
    out = matmul(yn, wout, res=x, name=f"{tag}_out")
    return out, (x, h, z, xpre, dtpre, pre, xbc, dt, ac, dtT, acT, y, yn, hp)


def _ssd_mixer_bwd(dout, saved, g, win, convw, convb, dtb, alog, dskip, ng, wout, cfg, tag):
    x, h, z, xpre, dtpre, pre, xbc, dt, ac, dtT, acT, y, yn, hp = saved
    T = x.shape[0]
    NH, G = cfg.NH, cfg.G
    wz, wx, wdt = _ssd_split_w(win, cfg)
    dwout = matmul(yn, dout, ta=True, name=f"{tag}_dwout")
    dyn = matmul(dout, wout, tb=True, name=f"{tag}_dyn")
    dxs, db, dc, dz, colp, rowp, dd, dng = ssd_chunk_bwd(dyn, xbc, z, y, hp, dt, ac, dtT, acT, dskip, ng, cfg,
                                                        name=f"{tag}_dscan")
    col = lambda lo: _pad_lanes(colp[:, :, lo:lo + SSD_HPG].transpose(1, 0, 2).reshape(T, NH))
    row = lambda lo: jnp.pad(rowp[:, lo:lo + SSD_HPG, :].reshape(NH, T), ((0, LANES - NH), (0, 0)))
    ddtpre, dalog, ddtb = ssd_prep_bwd(col(0), col(SSD_HPG), row(0), row(SSD_HPG), dtpre, dtb, alog, name=f"{tag}_dprep")
    dxpre, dconvw, dconvb = ssd_conv_bwd(dxs, db, dc, pre, xpre, convw, name=f"{tag}_dconv")
    dwz = matmul(h, dz, ta=True, name=f"{tag}_dwz")
    dwx = matmul(h, dxpre, ta=True, name=f"{tag}_dwx")
    dwdt = matmul(h, ddtpre, ta=True, name=f"{tag}_dwdt")
    dh = matmul(dz, wz, tb=True, name=f"{tag}_dh_z")
    dh = matmul(dxpre, wx, tb=True, res=dh, name=f"{tag}_dh_x")
    dh = matmul(ddtpre, wdt, tb=True, res=dh, name=f"{tag}_dh_dt")
    dx, dg = rms_bwd(dh, x, g, dout, name=f"{tag}_dnorm")
    dwin = jnp.concatenate([dwz, dwx, dwdt[:, :NH]], axis=1)
    small = dict(conv_w=dconvw[:convw.shape[0]], conv_b=dconvb, dt_bias=ddtb[:, :NH], a_log=dalog[:, :NH],
                 d=dd[:, 0, :SSD_HPG].reshape(1, NH), norm=dng)
    return dx, dg, dwin, dwout, small


INPUT_NAMES = ['x', 'ffn1_norm', 'ffn1_w_gu', 'ffn1_w_down', 'mix_norm', 'ffn2_norm', 'ffn2_w_gu', 'ffn2_w_down',
               'sb_w_qkv', 'sb_w_o', 'ssd_w_in', 'ssd_conv_w', 'ssd_conv_b', 'ssd_dt_bias', 'ssd_a_log', 'ssd_d',
               'ssd_norm', 'ssd_w_out', 'sc_w_in', 'sc_conv_w', 'sc_w_out', 'final_norm']
WEIGHT_NAMES = INPUT_NAMES[1:]
SHARDED_SMALL = ("ssd_conv_w", "sc_conv_w")


def train_step(a, cfg):
    x = a["x"][0]
    tgt = a["loss_target"][0]
    big_names = [n for n, _ in BIG]
    k_chip = 2 * lax.axis_index("x") + lax.axis_index("y")

    w = _gather_big({n: a[n] for n in big_names}, "w")
    conv_pack, conv_meta = _small_pack([a[n] for n in SHARDED_SMALL])
    conv_all, _ = allgather_small(conv_pack, name="conv_w_allgather")
    conv_full = {}
    for i, n in enumerate(SHARDED_SMALL):
        per_chip = [_small_unpack(conv_all[2 * kk], conv_meta)[i] for kk in range(N_CHIPS)]
        conv_full[n] = jnp.concatenate(per_chip, axis=-1)

    def vec(name, i):
        return a[name][i:i + 1]

    ssd_small = lambda j: (a["ssd_conv_b"][j:j + 1], _pad_lanes(a["ssd_dt_bias"][j:j + 1]),
                           _pad_lanes(a["ssd_a_log"][j:j + 1]), _pad_lanes(a["ssd_d"][j:j + 1]), a["ssd_norm"][j:j + 1])

    saved = []
    for i in range(cfg.depth):
        kind, j = i % 3, i // 3
        x, s1 = _ffn_fwd(x, vec("ffn1_norm", i), w["ffn1_w_gu"][i], w["ffn1_w_down"][i], f"l{i}_ffn1")
        if kind == 0:
            x, s2 = _sb_mixer_fwd(x, vec("mix_norm", i), w["sb_w_qkv"][j], w["sb_w_o"][j], cfg, f"l{i}_sb")
        elif kind == 1:
            x, s2 = _ssd_mixer_fwd(x, vec("mix_norm", i), w["ssd_w_in"][j], conv_full["ssd_conv_w"][j], *ssd_small(j),
                                   w["ssd_w_out"][j], cfg, f"l{i}_ssd")
        else:
            x, s2 = _sc_mixer_fwd(x, vec("mix_norm", i), w["sc_w_in"][j], conv_full["sc_conv_w"][j], w["sc_w_out"][j],
                                  f"l{i}_sc")
        x, s3 = _ffn_fwd(x, vec("ffn2_norm", i), w["ffn2_w_gu"][i], w["ffn2_w_down"][i], f"l{i}_ffn2")
        saved.append((s1, s2, s3))
    loss_part, dx, d_final = loss_head(x, a["final_norm"].reshape(1, -1), tgt, name="loss_head")

    gbig = {n: [None] * a[n].shape[0] for n in big_names}
    gsmall = {n: [None] * a[n].shape[0] for n in ("ffn1_norm", "mix_norm", "ffn2_norm", "ssd_conv_w", "ssd_conv_b",
                                                   "ssd_dt_bias", "ssd_a_log", "ssd_d", "ssd_norm", "sc_conv_w")}
    for i in reversed(range(cfg.depth)):
        kind, j = i % 3, i // 3
        s1, s2, s3 = saved[i]
        dx, gsmall["ffn2_norm"][i], gbig["ffn2_w_gu"][i], gbig["ffn2_w_down"][i] = _ffn_bwd(
            dx, s3, vec("ffn2_norm", i), w["ffn2_w_gu"][i], w["ffn2_w_down"][i], f"l{i}_ffn2")
        if kind == 0:
            dx, gsmall["mix_norm"][i], gbig["sb_w_qkv"][j], gbig["sb_w_o"][j] = _sb_mixer_bwd(
                dx, s2, vec("mix_norm", i), w["sb_w_qkv"][j], w["sb_w_o"][j], cfg, f"l{i}_sb")
        elif kind == 1:
            dx, gsmall["mix_norm"][i], gbig["ssd_w_in"][j], gbig["ssd_w_out"][j], sm = _ssd_mixer_bwd(
                dx, s2, vec("mix_norm", i), w["ssd_w_in"][j], conv_full["ssd_conv_w"][j], *ssd_small(j),
                w["ssd_w_out"][j], cfg, f"l{i}_ssd")
            for key, val in sm.items():
                gsmall["ssd_" + key][j] = val
        else:
            dx, gsmall["mix_norm"][i], gbig["sc_w_in"][j], gsmall["sc_conv_w"][j], gbig["sc_w_out"][j] = _sc_mixer_bwd(
                dx, s2, vec("mix_norm", i), w["sc_w_in"][j], conv_full["sc_conv_w"][j], w["sc_w_out"][j], f"l{i}_sc")
        dx, gsmall["ffn1_norm"][i], gbig["ffn1_w_gu"][i], gbig["ffn1_w_down"][i] = _ffn_bwd(
            dx, s1, vec("ffn1_norm", i), w["ffn1_w_gu"][i], w["ffn1_w_down"][i], f"l{i}_ffn1")
    grad_x = dx[None]

    red_big = _reduce_big({n: jnp.stack(v) for n, v in gbig.items()}, {n: a[n] for n in big_names}, "g")
    small_names = list(gsmall) + ["final_norm"]
    small_vals = [jnp.concatenate([p.reshape((1,) + a[n].shape[1:-1] + (-1,)) for p in gsmall[n]], axis=0)
                  for n in gsmall] + [d_final.reshape(-1)]
    gpack, gmeta = _small_pack(small_vals + [loss_part[:, :1]])
    _, gsum = allgather_small(gpack, name="small_allreduce")
    *small_red, loss = _small_unpack(gsum, gmeta)
    red = dict(red_big)
    for n, v in zip(small_names, small_red):
        if n in SHARDED_SMALL:
            width = a[n].shape[-1]
            v = lax.dynamic_slice_in_dim(v, k_chip * width, width, axis=v.ndim - 1)
        red[n] = v.reshape(a[n].shape)

    delta, new_m, new_v = {}, {}, {}
    two_d = lambda t: t.reshape(-1, t.shape[-1])
    for n in big_names:
        d_, m_, v_ = adamw(two_d(a[n]), two_d(red[n]), two_d(a["m_" + n]), two_d(a["v_" + n]), name=f"adamw_{n}")
        delta[n], new_m[n], new_v[n] = (t.reshape(a[n].shape) for t in (d_, m_, v_))
    rest = [n for n in WEIGHT_NAMES if n not in big_names]
    packs = [_small_pack([src[n] for n in rest]) for src in
             (a, red, {n: a["m_" + n] for n in rest}, {n: a["v_" + n] for n in rest})]
    outs = adamw(*[p for p, _ in packs], name="adamw_small")
    for dst, o in zip((delta, new_m, new_v), outs):
        for n, t in zip(rest, _small_unpack(o, packs[0][1])):
            dst[n] = t
    return (loss.reshape(()), grad_x, *[red[n] for n in WEIGHT_NAMES], *[delta[n] for n in WEIGHT_NAMES],
            *[new_m[n] for n in WEIGHT_NAMES], *[new_v[n] for n in WEIGHT_NAMES])


def kernel(x, ffn1_norm, ffn1_w_gu, ffn1_w_down, mix_norm, ffn2_norm, ffn2_w_gu, ffn2_w_down, sb_w_qkv, sb_w_o, ssd_w_in, ssd_conv_w, ssd_conv_b, ssd_dt_bias, ssd_a_log, ssd_d, ssd_norm, ssd_w_out, sc_w_in, sc_conv_w, sc_w_out, final_norm, loss_target, m_ffn1_norm, m_ffn1_w_gu, m_ffn1_w_down, m_mix_norm, m_ffn2_norm, m_ffn2_w_gu, m_ffn2_w_down, m_sb_w_qkv, m_sb_w_o, m_ssd_w_in, m_ssd_conv_w, m_ssd_conv_b, m_ssd_dt_bias, m_ssd_a_log, m_ssd_d, m_ssd_norm, m_ssd_w_out, m_sc_w_in, m_sc_conv_w, m_sc_w_out, m_final_norm, v_ffn1_norm, v_ffn1_w_gu, v_ffn1_w_down, v_mix_norm, v_ffn2_norm, v_ffn2_w_gu, v_ffn2_w_down, v_sb_w_qkv, v_sb_w_o, v_ssd_w_in, v_ssd_conv_w, v_ssd_conv_b, v_ssd_dt_bias, v_ssd_a_log, v_ssd_d, v_ssd_norm, v_ssd_w_out, v_sc_w_in, v_sc_conv_w, v_sc_w_out, v_final_norm):
    args = dict(locals())
    return train_step(args, Cfg())
```

```python
import functools
import math

import jax
import jax.numpy as jnp
from jax import lax
from jax.experimental import pallas as pl
from jax.experimental.pallas import tpu as pltpu

F32 = jnp.float32
CDT = jnp.bfloat16
HI = lax.Precision.HIGHEST
BS = pl.BlockSpec
SDS = jax.ShapeDtypeStruct
MESH = pl.DeviceIdType.MESH

RMS_EPS = 1e-6
ADAM_LR, ADAM_B1, ADAM_B2, ADAM_EPS, ADAM_WD, ADAM_STEP = 0.001, 0.9, 0.999, 1e-08, 0.01, 10
V7X_VMEM_LIMIT_BYTES = 56 * 1024 * 1024
LANES = 128
CHUNK = 128
HEAD_DIM = 64
SSD_HPG = 4
N_CHIPS = 4
N_DEV = 8
PACK_W = 1024


class Cfg:
    def __init__(self, d_model=1024, d_ff=2816, depth=4, sb_heads=16, ssd_groups=8, ssd_state=128,
                 ssd_conv=4, sc_width=3):
        self.D, self.F, self.depth = d_model, d_ff, depth
        self.sb_heads = sb_heads
        self.DI = 2 * d_model
        self.G = ssd_groups
        self.NH = self.DI // HEAD_DIM
        assert self.NH == SSD_HPG * self.G and ssd_state == LANES and sb_heads * HEAD_DIM == d_model
        self.conv_dim = self.DI + 2 * self.G * ssd_state
        self.in_dim = self.DI + self.conv_dim + self.NH
        self.ssd_conv, self.sc_width = ssd_conv, sc_width
        self.n_sb, self.n_ssd, self.n_sc = (depth + 2) // 3, (depth + 1) // 3, depth // 3


def _pallas(body, **kw):
    return pl.pallas_call(body, **kw)


def _params(*sem):
    return pltpu.CompilerParams(dimension_semantics=sem, vmem_limit_bytes=V7X_VMEM_LIMIT_BYTES)


def _tile(n, cap):
    if n <= cap:
        return n
    best = None
    for m in range(1, n // LANES + 1):
        t = m * LANES
        if n % t == 0 and t <= cap:
            best = t
    assert best is not None, (n, cap)
    return best


def _sigmoid(x):
    return 1.0 / (1.0 + jnp.exp(-x))


def _softplus_neg_abs(x):
    return jnp.log(1.0 + jnp.exp(-jnp.abs(x)))


def _colsel(blk, idx):
    lane = lax.broadcasted_iota(jnp.int32, blk.shape, 1)
    return jnp.sum(jnp.where(lane == idx, blk, 0.0), axis=1, keepdims=True)


def _rowsel(blk, idx):
    sub = lax.broadcasted_iota(jnp.int32, blk.shape, 0)
    return jnp.sum(jnp.where(sub == idx, blk, 0.0), axis=0, keepdims=True)


def _nt(a, b):
    return lax.dot_general(a, b, (((1,), (1,)), ((), ())), preferred_element_type=F32)


def _tn(a, b):
    return lax.dot_general(a, b, (((0,), (0,)), ((), ())), preferred_element_type=F32)


def _nn(a, b):
    return jnp.dot(a, b, preferred_element_type=F32)


def matmul(a, b, *, name, ta=False, tb=False, out_dtype=F32, res=None, scale=1.0, tm=512, tn=1408, tk=1408):
    M, K = (a.shape[1], a.shape[0]) if ta else a.shape
    N, K2 = b.shape if tb else (b.shape[1], b.shape[0])
    assert K == K2, (a.shape, b.shape, ta, tb)
    tm, tn, tk = _tile(M, tm), _tile(N, tn), _tile(K, tk)
    nk = K // tk
    dn = (((0 if ta else 1,), (1 if tb else 0,)), ((), ()))

    def body(*refs):
        if res is None:
            a_ref, b_ref, o_ref, acc = refs
        else:
            a_ref, b_ref, r_ref, o_ref, acc = refs
        k = pl.program_id(2)

        @pl.when(k == 0)
        def _():
            acc[...] = jnp.zeros_like(acc)

        acc[...] += lax.dot_general(a_ref[...].astype(CDT), b_ref[...].astype(CDT), dn,
                                    preferred_element_type=F32)

        @pl.when(k == nk - 1)
        def _():
            o = acc[...] * scale
            if res is not None:
                o = o + r_ref[...].astype(F32)
            o_ref[...] = o.astype(o_ref.dtype)

    a_spec = BS((tk, tm), lambda i, j, k: (k, i)) if ta else BS((tm, tk), lambda i, j, k: (i, k))
    b_spec = BS((tn, tk), lambda i, j, k: (j, k)) if tb else BS((tk, tn), lambda i, j, k: (k, j))
    o_spec = BS((tm, tn), lambda i, j, k: (i, j))
    ins, specs = [a, b], [a_spec, b_spec]
    if res is not None:
        ins.append(res)
        specs.append(o_spec)
    return _pallas(body, grid=(M // tm, N // tn, nk), in_specs=specs, out_specs=o_spec,
                   out_shape=SDS((M, N), out_dtype), scratch_shapes=[pltpu.VMEM((tm, tn), F32)],
                   compiler_params=_params("parallel", "parallel", "arbitrary"), name=name)(*ins)


def rms_fwd(x, g, *, name):
    T, D = x.shape
    tm = _tile(T, 512)

    def body(x_ref, g_ref, o_ref):
        xv = x_ref[...]
        r = lax.rsqrt(jnp.mean(xv * xv, axis=-1, keepdims=True) + RMS_EPS)
        o_ref[...] = (xv * r * g_ref[...]).astype(o_ref.dtype)

    return _pallas(body, grid=(T // tm,), in_specs=[BS((tm, D), lambda i: (i, 0)), BS((1, D), lambda i: (0, 0))],
                   out_specs=BS((tm, D), lambda i: (i, 0)), out_shape=SDS((T, D), CDT),
                   compiler_params=_params("parallel"), name=name)(x, g)


def rms_bwd(dh, x, g, dres, *, name):
    T, D = x.shape
    tm = _tile(T, 512)

    def body(dh_ref, x_ref, g_ref, r_ref, dx_ref, dg_ref):
        i = pl.program_id(0)
        xv = x_ref[...]
        r = lax.rsqrt(jnp.mean(xv * xv, axis=-1, keepdims=True) + RMS_EPS)
        xhat = xv * r
        dhv = dh_ref[...].astype(F32)
        dxh = dhv * g_ref[...]
        m = jnp.mean(dxh * xhat, axis=-1, keepdims=True)
        dx_ref[...] = r * (dxh - xhat * m) + r_ref[...]

        @pl.when(i == 0)
        def _():
            dg_ref[...] = jnp.zeros_like(dg_ref)

        dg_ref[...] += jnp.sum(dhv * xhat, axis=0, keepdims=True)

    row = BS((tm, D), lambda i: (i, 0))
    vec = BS((1, D), lambda i: (0, 0))
    return _pallas(body, grid=(T // tm,), in_specs=[row, row, vec, row], out_specs=[row, vec],
                   out_shape=[SDS((T, D), F32), SDS((1, D), F32)],
                   compiler_params=_params("arbitrary"), name=name)(dh, x, g, dres)


def swiglu_fwd(gu, *, name):
    T, F2 = gu.shape
    F = F2 // 2
    tm = _tile(T, 512)

    def body(gu_ref, o_ref):
        g = gu_ref[:, :F].astype(F32)
        u = gu_ref[:, F:].astype(F32)
        o_ref[...] = (g * _sigmoid(g) * u).astype(o_ref.dtype)

    return _pallas(body, grid=(T // tm,), in_specs=[BS((tm, F2), lambda i: (i, 0))],
                   out_specs=BS((tm, F), lambda i: (i, 0)), out_shape=SDS((T, F), CDT),
                   compiler_params=_params("parallel"), name=name)(gu)


def swiglu_bwd(da, gu, *, name):
    T, F2 = gu.shape
    F = F2 // 2
    tm = _tile(T, 512)

    def body(da_ref, gu_ref, o_ref):
        g = gu_ref[:, :F].astype(F32)
        u = gu_ref[:, F:].astype(F32)
        dav = da_ref[...].astype(F32)
        s = _sigmoid(g)
        o_ref[:, :F] = (dav * u * (s * (1.0 + g * (1.0 - s)))).astype(o_ref.dtype)
        o_ref[:, F:] = (dav * (g * s)).astype(o_ref.dtype)

    return _pallas(body, grid=(T // tm,), in_specs=[BS((tm, F), lambda i: (i, 0)), BS((tm, F2), lambda i: (i, 0))],
                   out_specs=BS((tm, F2), lambda i: (i, 0)), out_shape=SDS((T, F2), CDT),
                   compiler_params=_params("parallel"), name=name)(da, gu)


def loss_head(x, g, tgt, *, name):
    T, D = x.shape
    tm = _tile(T, 512)

    def body(x_ref, g_ref, t_ref, l_ref, dx_ref, dg_ref):
        i = pl.program_id(0)
        xv = x_ref[...]
        gv = g_ref[...]
        r = lax.rsqrt(jnp.mean(xv * xv, axis=-1, keepdims=True) + RMS_EPS)
        xhat = xv * r
        err = xhat * gv - t_ref[...]
        part = 0.5 * jnp.sum(jnp.mean(err * err, axis=-1, keepdims=True), axis=0, keepdims=True)
        dy = err * (1.0 / D)
        dxh = dy * gv
        m = jnp.mean(dxh * xhat, axis=-1, keepdims=True)
        dx_ref[...] = r * (dxh - xhat * m)

        @pl.when(i == 0)
        def _():
            dg_ref[...] = jnp.zeros_like(dg_ref)
            l_ref[...] = jnp.zeros_like(l_ref)

        dg_ref[...] += jnp.sum(dy * xhat, axis=0, keepdims=True)
        l_ref[...] += jnp.broadcast_to(part, l_ref.shape)

    row = BS((tm, D), lambda i: (i, 0))
    vec = BS((1, D), lambda i: (0, 0))
    return _pallas(body, grid=(T // tm,), in_specs=[row, vec, row],
                   out_specs=[BS((1, LANES), lambda i: (0, 0)), row, vec],
                   out_shape=[SDS((1, LANES), F32), SDS((T, D), F32), SDS((1, D), F32)],
                   compiler_params=_params("arbitrary"), name=name)(x, g, tgt)


def adamw(w, g, m, v, *, name):
    R, C = w.shape
    tr = R
    if R * C * 4 > (1 << 20):
        for cand in range(8, R + 1, 8):
            if R % cand == 0 and cand * C * 4 <= (1 << 20):
                tr = cand

    def body(w_ref, g_ref, m_ref, v_ref, d_ref, nm_ref, nv_ref):
        gv = g_ref[...]
        nm = ADAM_B1 * m_ref[...] + (1.0 - ADAM_B1) * gv
        nv = ADAM_B2 * v_ref[...] + (1.0 - ADAM_B2) * (gv * gv)
        m_hat = nm / (1.0 - ADAM_B1 ** ADAM_STEP)
        v_hat = nv / (1.0 - ADAM_B2 ** ADAM_STEP)
        d_ref[...] = -ADAM_LR * (m_hat / (jnp.sqrt(v_hat) + ADAM_EPS) + ADAM_WD * w_ref[...])
        nm_ref[...] = nm
        nv_ref[...] = nv

    blk = BS((tr, C), lambda i: (i, 0))
    return _pallas(body, grid=(R // tr,), in_specs=[blk] * 4, out_specs=[blk] * 3,
                   out_shape=[SDS((R, C), F32)] * 3, compiler_params=_params("parallel"), name=name)(w, g, m, v)


def _sb_tri():
    row = lax.broadcasted_iota(jnp.int32, (CHUNK, CHUNK), 0)
    col = lax.broadcasted_iota(jnp.int32, (CHUNK, CHUNK), 1)
    return row, col


def _tri3(keep):
    t = keep.astype(jnp.bfloat16)
    return jnp.concatenate([t, t, t], axis=0)


def _split_dot(x, tri3):
    hi = x.astype(jnp.bfloat16)
    r = x - hi.astype(F32)
    mid = r.astype(jnp.bfloat16)
    lo = (r - mid.astype(F32)).astype(jnp.bfloat16)
    return jnp.dot(jnp.concatenate([hi, mid, lo], axis=1), tri3, preferred_element_type=F32)


def _sb_heads_per_step(H, want):
    hb = min(H, want)
    assert H % hb == 0, (H, hb)
    return hb


SB_FWD_HEADS = 4
SB_BWD_HEADS = 2


def sb_fwd(q, k, v, *, name):
    H, T, dh = q.shape
    nq = T // CHUNK
    assert nq <= LANES
    scale = dh ** -0.5
    hb = _sb_heads_per_step(H, SB_FWD_HEADS)
    heads = range(hb)

    def body(q_ref, k_ref, v_ref, o_ref, tab_ref, acc_scr, cs_scr):
        qi = pl.program_id(1)
        row, col = _sb_tri()
        upper3 = _tri3(row > col)
        acc_scr[...] = jnp.zeros_like(acc_scr)
        cs_scr[...] = jnp.zeros_like(cs_scr)
        tab_ref[...] = jnp.zeros_like(tab_ref)

        def tile(j, diag):
            sl = pl.ds(pl.multiple_of(j * CHUNK, CHUNK), CHUNK)
            keep = (lambda t: jnp.where(col < row, t, 0.0)) if diag else (lambda t: t)
            zs = [_nt(q_ref[h], k_ref[h, sl, :]) * scale for h in heads]
            lszs = [jnp.minimum(z, 0.0) - _softplus_neg_abs(z) for z in zs]
            lks = [keep(lsz - z) for lsz, z in zip(lszs, zs)]
            tls = [_split_dot(lk, upper3) for lk in lks]
            css = [cs_scr[h] for h in heads]
            atts = [keep(jnp.exp(lsz + tl + cs)).astype(CDT) for lsz, tl, cs in zip(lszs, tls, css)]
            pvs = [_nn(att, v_ref[h, sl, :]) for h, att in zip(heads, atts)]
            for h in heads:
                acc_scr[h] += pvs[h]
                tab_ref[h, 0] = jnp.where(col == j, css[h], tab_ref[h, 0])
                cs_scr[h] = css[h] + jnp.sum(lks[h], axis=1, keepdims=True)

        tile(qi, True)

        def step(jj, carry):
            tile(qi - jj, False)
            return carry

        lax.fori_loop(1, qi + 1, step, 0)
        o_ref[...] = acc_scr[...].astype(o_ref.dtype)

    qs = BS((hb, CHUNK, dh), lambda h, i: (h, i, 0))
    ks = BS((hb, T, dh), lambda h, i: (h, 0, 0))
    return _pallas(body, grid=(H // hb, nq), in_specs=[qs, ks, ks],
                   out_specs=[qs, BS((hb, 1, CHUNK, CHUNK), lambda h, i: (h, i, 0, 0))],
                   out_shape=[SDS((H, T, dh), CDT), SDS((H, nq, CHUNK, CHUNK), F32)],
                   scratch_shapes=[pltpu.VMEM((hb, CHUNK, dh), F32), pltpu.VMEM((hb, CHUNK, 1), F32)],
                   compiler_params=_params("parallel", "parallel"), name=name)(q, k, v)


def sb_bwd(q, k, v, do, tab, *, name):
    H, T, dh = q.shape
    nq = T // CHUNK
    scale = dh ** -0.5
    hb = _sb_heads_per_step(H, SB_BWD_HEADS)
    heads = range(hb)

    def body(q_ref, k_ref, v_ref, do_ref, tab_ref, dq_ref, dk_ref, dv_ref, dk_acc, dv_acc, dq_scr, pfx_scr):
        qi = pl.program_id(1)

        @pl.when(qi == 0)
        def _():
            dk_acc[...] = jnp.zeros_like(dk_acc)
            dv_acc[...] = jnp.zeros_like(dv_acc)

        dq_scr[...] = jnp.zeros_like(dq_scr)
        pfx_scr[...] = jnp.zeros_like(pfx_scr)
        row, col = _sb_tri()
        upper3 = _tri3(row > col)
        before3 = _tri3(row < col)

        def tile(j, diag):
            sl = pl.ds(pl.multiple_of(j * CHUNK, CHUNK), CHUNK)
            keep = (lambda t: jnp.where(col < row, t, 0.0)) if diag else (lambda t: t)
            zs = [_nt(q_ref[h], k_ref[h, sl, :]) * scale for h in heads]
            das = [_nt(do_ref[h], v_ref[h, sl, :]) for h in heads]
            lszs = [jnp.minimum(z, 0.0) - _softplus_neg_abs(z) for z in zs]
            lks = [keep(lsz - z) for lsz, z in zip(lszs, zs)]
            tls = [_split_dot(lk, upper3) for lk in lks]
            css = [jnp.sum(jnp.where(col == j, tab_ref[h, 0], 0.0), axis=1, keepdims=True) for h in heads]
            atts = [keep(jnp.exp(lsz + tl + cs)) for lsz, tl, cs in zip(lszs, tls, css)]
            dls = [att * da for att, da in zip(atts, das)]
            pres = [_split_dot(dl, before3) for dl in dls]
            pfxs = [pfx_scr[h] for h in heads]
            sigs = [jnp.exp(lsz) for lsz in lszs]
            dzs = [((dl * (1.0 - sig) - keep(sig * (pre + pfx))) * scale).astype(CDT)
                   for dl, sig, pre, pfx in zip(dls, sigs, pres, pfxs)]
            dqs = [_nn(dz, k_ref[h, sl, :]) for h, dz in zip(heads, dzs)]
            dks = [_tn(dz, q_ref[h]) for h, dz in zip(heads, dzs)]
            dvs = [_tn(att.astype(CDT), do_ref[h]) for h, att in zip(heads, atts)]
            for h in heads:
                dq_scr[h] += dqs[h]
                dk_acc[h, sl, :] += dks[h]
                dv_acc[h, sl, :] += dvs[h]
                pfx_scr[h] = pfxs[h] + jnp.sum(dls[h], axis=1, keepdims=True)

        def step(j, carry):
            tile(j, False)
            return carry

        lax.fori_loop(0, qi, step, 0)
        tile(qi, True)
        dq_ref[...] = dq_scr[...].astype(dq_ref.dtype)

        @pl.when(qi == nq - 1)
        def _():
            dk_ref[...] = dk_acc[...].astype(dk_ref.dtype)
            dv_ref[...] = dv_acc[...].astype(dv_ref.dtype)

    qs = BS((hb, CHUNK, dh), lambda h, i: (h, i, 0))
    ks = BS((hb, T, dh), lambda h, i: (h, 0, 0))
    ts = BS((hb, 1, CHUNK, CHUNK), lambda h, i: (h, i, 0, 0))
    return _pallas(body, grid=(H // hb, nq), in_specs=[qs, ks, ks, qs, ts], out_specs=[qs, ks, ks],
                   out_shape=[SDS((H, T, dh), CDT)] * 3,
                   scratch_shapes=[pltpu.VMEM((hb, T, dh), F32), pltpu.VMEM((hb, T, dh), F32),
                                   pltpu.VMEM((hb, CHUNK, dh), F32), pltpu.VMEM((hb, CHUNK, 1), F32)],
                   compiler_params=_params("parallel", "arbitrary"), name=name)(q, k, v, do, tab)


HALO = 8


def _shift_down(cur, prev8, s):
    if s == 0:
        return cur
    tt = cur.shape[0]
    row = lax.broadcasted_iota(jnp.int32, cur.shape, 0)
    fix = jnp.concatenate([pltpu.roll(prev8, s, 0), jnp.zeros((tt - HALO, cur.shape[1]), cur.dtype)], axis=0)
    return jnp.where(row < s, fix, pltpu.roll(cur, s, 0))


def _shift_up(cur, next8, s):
    if s == 0:
        return cur
    tt = cur.shape[0]
    row = lax.broadcasted_iota(jnp.int32, cur.shape, 0)
    fix = jnp.concatenate([jnp.zeros((tt - HALO, cur.shape[1]), cur.dtype), pltpu.roll(next8, HALO - s, 0)], axis=0)
    return jnp.where(row >= tt - s, fix, pltpu.roll(cur, tt - s, 0))


def _halo_specs(tt, width, T):
    per = tt // HALO
    last = T // HALO - 1
    cur = BS((tt, width), lambda i: (i, 0))
    prev = BS((HALO, width), lambda i: (jnp.maximum(i * per - 1, 0), 0))
    nxt = BS((HALO, width), lambda i: (jnp.minimum((i + 1) * per, last), 0))
    return cur, prev, nxt


def sc_fwd(proj, w, *, name):
    T, D3 = proj.shape
    D = D3 // 3
    K = w.shape[0]
    tt = _tile(T, 256)
    cur, prev, _ = _halo_specs(tt, D3, T)

    def body(p_ref, pp_ref, w_ref, o_ref):
        i = pl.program_id(0)
        pc = p_ref[:, D:2 * D] * p_ref[:, 2 * D:]
        pp = jnp.where(i > 0, pp_ref[:, D:2 * D] * pp_ref[:, 2 * D:], 0.0)
        u = jnp.zeros((tt, D), F32)
        for kk in range(K):
            u = u + w_ref[kk:kk + 1, :] * _shift_down(pc, pp, K - 1 - kk)
        o_ref[...] = (p_ref[:, :D] * u).astype(o_ref.dtype)

    return _pallas(body, grid=(T // tt,), in_specs=[cur, prev, BS((K, D), lambda i: (0, 0))],
                   out_specs=BS((tt, D), lambda i: (i, 0)), out_shape=SDS((T, D), CDT),
                   compiler_params=_params("parallel"), name=name)(proj, proj, w)


def sc_bwd(dm, proj, w, *, name):
    T, D3 = proj.shape
    D = D3 // 3
    K = w.shape[0]
    tt = _tile(T, 256)
    cur, prev, nxt = _halo_specs(tt, D3, T)
    dcur, _, dnxt = _halo_specs(tt, D, T)
    nsteps = T // tt

    def body(dm_ref, dmn_ref, p_ref, pp_ref, pn_ref, w_ref, o_ref, dw_ref):
        i = pl.program_id(0)
        b = p_ref[:, :D]
        c = p_ref[:, D:2 * D]
        h = p_ref[:, 2 * D:]
        pc = c * h
        pp = jnp.where(i > 0, pp_ref[:, D:2 * D] * pp_ref[:, 2 * D:], 0.0)
        dmv = dm_ref[...]
        du = dmv * b
        dun = jnp.where(i < nsteps - 1, dmn_ref[...] * pn_ref[:, :D], 0.0)
        u = jnp.zeros((tt, D), F32)
        dp = jnp.zeros((tt, D), F32)
        rows = []
        for kk in range(K):
            sh = _shift_down(pc, pp, K - 1 - kk)
            u = u + w_ref[kk:kk + 1, :] * sh
            dp = dp + w_ref[kk:kk + 1, :] * _shift_up(du, dun, K - 1 - kk)
            rows.append(jnp.sum(du * sh, axis=0, keepdims=True))
        rows.append(jnp.zeros((HALO - K, D), F32))
        o_ref[:, :D] = (dmv * u).astype(o_ref.dtype)
        o_ref[:, D:2 * D] = (dp * h).astype(o_ref.dtype)
        o_ref[:, 2 * D:] = (dp * c).astype(o_ref.dtype)

        @pl.when(i == 0)
        def _():
            dw_ref[...] = jnp.zeros_like(dw_ref)

        dw_ref[...] += jnp.concatenate(rows, axis=0)

    return _pallas(body, grid=(nsteps,), in_specs=[dcur, dnxt, cur, prev, nxt, BS((K, D), lambda i: (0, 0))],
                   out_specs=[BS((tt, D3), lambda i: (i, 0)), BS((HALO, D), lambda i: (0, 0))],
                   out_shape=[SDS((T, D3), CDT), SDS((HALO, D), F32)],
                   compiler_params=_params("arbitrary"), name=name)(dm, dm, proj, proj, proj, w)


def ssd_conv_fwd(xpre, w, b, *, name):
    T, C = xpre.shape
    K = w.shape[0]
    tt = _tile(T, 256)
    cur, prev, _ = _halo_specs(tt, C, T)

    def body(x_ref, xp_ref, w_ref, b_ref, pre_ref, o_ref):
        i = pl.program_id(0)
        xv = x_ref[...]
        xp = jnp.where(i > 0, xp_ref[...], 0.0)
        acc = jnp.broadcast_to(b_ref[...], (tt, C))
        for kk in range(K):
            acc = acc + w_ref[kk:kk + 1, :] * _shift_down(xv, xp, K - 1 - kk)
        pre_ref[...] = acc
        o_ref[...] = acc * _sigmoid(acc)

    row = BS((tt, C), lambda i: (i, 0))
    return _pallas(body, grid=(T // tt,), in_specs=[cur, prev, BS((K, C), lambda i: (0, 0)), BS((1, C), lambda i: (0, 0))],
                   out_specs=[row, row], out_shape=[SDS((T, C), F32)] * 2,
                   compiler_params=_params("parallel"), name=name)(xpre, xpre, w, b)


def ssd_conv_bwd(dxs, db_, dc_, pre, xpre, w, *, name):
    T, C = xpre.shape
    K = w.shape[0]
    tt = _tile(T, 256)
    cur, prev, nxt = _halo_specs(tt, C, T)
    widths = (dxs.shape[1], db_.shape[1], dc_.shape[1])
    gspecs = []
    for wd in widths:
        c_, _, n_ = _halo_specs(tt, wd, T)
        gspecs += [c_, n_]
    nsteps = T // tt

    def dsilu(p):
        s = _sigmoid(p)
        return s * (1.0 + p * (1.0 - s))

    def body(g0, g0n, g1, g1n, g2, g2n, pre_ref, pren_ref, x_ref, xp_ref, w_ref, dx_ref, dw_ref, dbias_ref):
        i = pl.program_id(0)
        gcur = jnp.concatenate([g0[...], g1[...], g2[...]], axis=1)
        gnxt = jnp.concatenate([g0n[...], g1n[...], g2n[...]], axis=1)
        dpre = gcur * dsilu(pre_ref[...])
        dpren = jnp.where(i < nsteps - 1, gnxt * dsilu(pren_ref[...]), 0.0)
        xv = x_ref[...]
        xp = jnp.where(i > 0, xp_ref[...], 0.0)
        dx = jnp.zeros((tt, C), F32)
        rows = []
        for kk in range(K):
            dx = dx + w_ref[kk:kk + 1, :] * _shift_up(dpre, dpren, K - 1 - kk)
            rows.append(jnp.sum(dpre * _shift_down(xv, xp, K - 1 - kk), axis=0, keepdims=True))
        rows.append(jnp.zeros((HALO - K, C), F32))
        dx_ref[...] = dx.astype(dx_ref.dtype)

        @pl.when(i == 0)
        def _():
            dw_ref[...] = jnp.zeros_like(dw_ref)
            dbias_ref[...] = jnp.zeros_like(dbias_ref)

        dw_ref[...] += jnp.concatenate(rows, axis=0)
        dbias_ref[...] += jnp.sum(dpre, axis=0, keepdims=True)

    return _pallas(body, grid=(nsteps,),
                   in_specs=gspecs + [cur, nxt, cur, prev, BS((K, C), lambda i: (0, 0))],
                   out_specs=[BS((tt, C), lambda i: (i, 0)), BS((HALO, C), lambda i: (0, 0)), BS((1, C), lambda i: (0, 0))],
                   out_shape=[SDS((T, C), CDT), SDS((HALO, C), F32), SDS((1, C), F32)],
                   compiler_params=_params("arbitrary"), name=name)(dxs, dxs, db_, db_, dc_, dc_, pre, pre, xpre, xpre, w)


def ssd_prep_fwd(dtpre, bias, alog, *, name):
    T = dtpre.shape[0]
    nc = T // CHUNK

    def body(p_ref, b_ref, a_ref, dt_ref, ac_ref, dtT_ref, acT_ref):
        xv = p_ref[...] + b_ref[...]
        dt = jnp.maximum(xv, 0.0) + _softplus_neg_abs(xv)
        row, col = _sb_tri()
        lower = (col <= row).astype(F32)
        ac = jnp.dot(lower, dt * -jnp.exp(a_ref[...]), precision=HI, preferred_element_type=F32)
        dt_ref[...] = dt
        ac_ref[...] = ac
        dtT_ref[...] = dt.T
        acT_ref[...] = ac.T

    blk = BS((CHUNK, LANES), lambda c: (c, 0))
    blkT = BS((LANES, CHUNK), lambda c: (0, c))
    vec = BS((1, LANES), lambda c: (0, 0))
    return _pallas(body, grid=(nc,), in_specs=[blk, vec, vec], out_specs=[blk, blk, blkT, blkT],
                   out_shape=[SDS((T, LANES), F32)] * 2 + [SDS((LANES, T), F32)] * 2,
                   compiler_params=_params("parallel"), name=name)(dtpre, bias, alog)


def ssd_prep_bwd(dac_c, ddt_c, dac_r, ddt_r, dtpre, bias, alog, *, name):
    T = dtpre.shape[0]
    nc = T // CHUNK

    def body(dac_ref, ddt_ref, dacr_ref, ddtr_ref, p_ref, b_ref, a_ref, o_ref, da_ref, db_ref):
        c = pl.program_id(0)
        xv = p_ref[...] + b_ref[...]
        dt = jnp.maximum(xv, 0.0) + _softplus_neg_abs(xv)
        av = -jnp.exp(a_ref[...])
        row, col = _sb_tri()
        after = (col >= row).astype(F32)
        dda = jnp.dot(after, dac_ref[...] + dacr_ref[...].T, precision=HI, preferred_element_type=F32)
        dpre = (ddt_ref[...] + ddtr_ref[...].T + av * dda) * _sigmoid(xv)
        o_ref[...] = dpre

        @pl.when(c == 0)
        def _():
            da_ref[...] = jnp.zeros_like(da_ref)
            db_ref[...] = jnp.zeros_like(db_ref)

        da_ref[...] += jnp.sum(dt * dda, axis=0, keepdims=True) * av
        db_ref[...] += jnp.sum(dpre, axis=0, keepdims=True)

    blk = BS((CHUNK, LANES), lambda c: (c, 0))
    blkT = BS((LANES, CHUNK), lambda c: (0, c))
    vec = BS((1, LANES), lambda c: (0, 0))
    return _pallas(body, grid=(nc,), in_specs=[blk, blk, blkT, blkT, blk, vec, vec], out_specs=[blk, vec, vec],
                   out_shape=[SDS((T, LANES), F32), SDS((1, LANES), F32), SDS((1, LANES), F32)],
                   compiler_params=_params("arbitrary"), name=name)(dac_c, ddt_c, dac_r, ddt_r, dtpre, bias, alog)


def _ssd_specs(cfg, T, rev):
    nc = T // CHUNK
    GW = SSD_HPG * HEAD_DIM
    bo = cfg.DI // LANES
    co = (cfg.DI + cfg.G * LANES) // LANES
    ci = (lambda c: nc - 1 - c) if rev else (lambda c: c)
    return dict(
        x=BS((CHUNK, GW), lambda g, c: (ci(c), g)),
        b=BS((CHUNK, LANES), lambda g, c: (ci(c), bo + g)),
        c=BS((CHUNK, LANES), lambda g, c: (ci(c), co + g)),
        col=BS((CHUNK, LANES), lambda g, c: (ci(c), 0)),
        row=BS((LANES, CHUNK), lambda g, c: (0, ci(c))),
        vec=BS((1, LANES), lambda g, c: (0, 0)),
        ng=BS((1, GW), lambda g, c: (0, g)),
        hp=BS((1, 1, SSD_HPG, HEAD_DIM, LANES), lambda g, c: (g, ci(c), 0, 0, 0)),
    )


def ssd_chunk_fwd(xbc, z, dt, ac, dtT, acT, dskip, ng, cfg, *, name):
    T = xbc.shape[0]
    nc = T // CHUNK
    G, DI = cfg.G, cfg.DI
    GW = SSD_HPG * HEAD_DIM
    sp = _ssd_specs(cfg, T, False)

    def body(x_ref, b_ref, c_ref, z_ref, dt_ref, ac_ref, dtT_ref, acT_ref, d_ref, ng_ref,
             y_ref, yn_ref, hp_ref, h_scr):
        g = pl.program_id(0)
        c = pl.program_id(1)

        @pl.when(c == 0)
        def _():
            h_scr[...] = jnp.zeros_like(h_scr)

        row, col = _sb_tri()
        causal = row >= col
        Bm = b_ref[...].astype(CDT)
        Cm = c_ref[...].astype(CDT)
        Gm = _nt(Cm, Bm)
        dtb, acb, dtTb, acTb, dv = dt_ref[...], ac_ref[...], dtT_ref[...], acT_ref[...], d_ref[...]
        for r in range(SSD_HPG):
            hh = g * SSD_HPG + r
            ac_c, ac_r = _colsel(acb, hh), _rowsel(acTb, hh)
            dt_c, dt_r = _colsel(dtb, hh), _rowsel(dtTb, hh)
            S = jnp.where(causal, jnp.exp(ac_c - ac_r), 0.0)
            Wm = Gm * S * dt_r
            Xr = x_ref[:, r * HEAD_DIM:(r + 1) * HEAD_DIM]
            Hp = h_scr[r]
            Q = _nt(Cm, Hp.astype(CDT))
            Y = _nn(Wm.astype(CDT), Xr.astype(CDT)) + jnp.exp(ac_c) * Q + _colsel(dv, hh) * Xr
            aL = _rowsel(ac_c, CHUNK - 1)
            e = jnp.exp(aL - ac_c) * dt_c
            hp_ref[0, 0, r] = Hp
            h_scr[r] = jnp.exp(aL) * Hp + _tn((Xr * e).astype(CDT), Bm)
            y_ref[:, r * HEAD_DIM:(r + 1) * HEAD_DIM] = Y
        zz = z_ref[...]
        yz = y_ref[...] * (zz * _sigmoid(zz))
        rstd = lax.rsqrt(jnp.mean(yz * yz, axis=-1, keepdims=True) + RMS_EPS)
        yn_ref[...] = (yz * rstd * ng_ref[...]).astype(yn_ref.dtype)

    return _pallas(body, grid=(G, nc),
                   in_specs=[sp["x"], sp["b"], sp["c"], sp["x"], sp["col"], sp["col"], sp["row"], sp["row"], sp["vec"], sp["ng"]],
                   out_specs=[sp["x"], sp["x"], sp["hp"]],
                   out_shape=[SDS((T, DI), F32), SDS((T, DI), CDT), SDS((G, nc, SSD_HPG, HEAD_DIM, LANES), F32)],
                   scratch_shapes=[pltpu.VMEM((SSD_HPG, HEAD_DIM, LANES), F32)],
                   compiler_params=_params("parallel", "arbitrary"), name=name)(xbc, xbc, xbc, z, dt, ac, dtT, acT, dskip, ng)


def ssd_chunk_bwd(dyn, xbc, z, y, hp, dt, ac, dtT, acT, dskip, ng, cfg, *, name):
    T = xbc.shape[0]
    nc = T // CHUNK
    G, DI = cfg.G, cfg.DI
    GW = SSD_HPG * HEAD_DIM
    sp = _ssd_specs(cfg, T, True)

    def body(dyn_ref, x_ref, b_ref, c_ref, z_ref, y_ref, hp_ref, dt_ref, ac_ref, dtT_ref, acT_ref, d_ref, ng_ref,
             dx_ref, db_ref, dc_ref, dz_ref, colp_ref, rowp_ref, dd_ref, dng_ref, dh_scr):
        g = pl.program_id(0)
        c = pl.program_id(1)

        @pl.when(c == 0)
        def _():
            dh_scr[...] = jnp.zeros_like(dh_scr)
            dd_ref[...] = jnp.zeros_like(dd_ref)
            dng_ref[...] = jnp.zeros_like(dng_ref)

        zz = z_ref[...]
        sg = _sigmoid(zz)
        gate = zz * sg
        yv = y_ref[...]
        yz = yv * gate
        rstd = lax.rsqrt(jnp.mean(yz * yz, axis=-1, keepdims=True) + RMS_EPS)
        nhat = yz * rstd
        dynv = dyn_ref[...].astype(F32)
        dng_ref[...] += jnp.sum(dynv * nhat, axis=0, keepdims=True)
        dnh = dynv * ng_ref[...]
        dyz = rstd * (dnh - nhat * jnp.mean(dnh * nhat, axis=-1, keepdims=True))
        dz_ref[...] = (dyz * yv * (sg * (1.0 + zz * (1.0 - sg)))).astype(dz_ref.dtype)
        dy_all = dyz * gate

        row, col = _sb_tri()
        causal = row >= col
        lane1 = lax.broadcasted_iota(jnp.int32, (1, LANES), 1)
        sub8 = lax.broadcasted_iota(jnp.int32, (HALO, CHUNK), 0)
        subc = lax.broadcasted_iota(jnp.int32, (CHUNK, 1), 0)
        Bf = b_ref[...]
        Cf = c_ref[...]
        Bm = Bf.astype(CDT)
        Cm = Cf.astype(CDT)
        Gm = _nt(Cm, Bm)
        dtb, acb, dtTb, acTb, dv = dt_ref[...], ac_ref[...], dtT_ref[...], acT_ref[...], d_ref[...]
        dG = jnp.zeros((CHUNK, CHUNK), F32)
        dBm = jnp.zeros((CHUNK, LANES), F32)
        dCm = jnp.zeros((CHUNK, LANES), F32)
        colp = jnp.zeros((CHUNK, LANES), F32)
        rowp = jnp.zeros((HALO, CHUNK), F32)
        ddv = jnp.zeros((1, LANES), F32)
        for r in range(SSD_HPG):
            hh = g * SSD_HPG + r
            ac_c, ac_r = _colsel(acb, hh), _rowsel(acTb, hh)
            dt_c, dt_r = _colsel(dtb, hh), _rowsel(dtTb, hh)
            S = jnp.where(causal, jnp.exp(ac_c - ac_r), 0.0)
            GS = Gm * S
            Wm = GS * dt_r
            Xr = x_ref[:, r * HEAD_DIM:(r + 1) * HEAD_DIM]
            Xc = Xr.astype(CDT)
            Hp = hp_ref[0, 0, r]
            Hc = Hp.astype(CDT)
            dY = dy_all[:, r * HEAD_DIM:(r + 1) * HEAD_DIM]
            dYc = dY.astype(CDT)
            dsk = _colsel(dv, hh)
            ddv = ddv + jnp.where(lane1 == r, jnp.sum(jnp.sum(dY * Xr, axis=1, keepdims=True), axis=0, keepdims=True), 0.0)
            dX = dY * dsk + _tn(Wm.astype(CDT), dYc)
            dWm = _nt(dYc, Xc)
            dG = dG + dWm * S * dt_r
            Mm = dWm * Wm
            Nm = dWm * GS
            dac_c = jnp.sum(Mm, axis=1, keepdims=True)
            dac_r = -jnp.sum(Mm, axis=0, keepdims=True)
            ddt_r = jnp.sum(Nm, axis=0, keepdims=True)
            E = jnp.exp(ac_c)
            Q = _nt(Cm, Hc)
            dQ = (dY * E).astype(CDT)
            dac_c = dac_c + jnp.sum(dY * Q, axis=1, keepdims=True) * E
            dCm = dCm + _nn(dQ, Hc)
            dHp = _tn(dQ, Cm)
            dHn = dh_scr[r]
            dHc = dHn.astype(CDT)
            aL = _rowsel(ac_c, CHUNK - 1)
            eaL = jnp.exp(aL)
            ex = jnp.exp(aL - ac_c)
            e = ex * dt_c
            dHp = dHp + eaL * dHn
            daL = eaL * jnp.sum(jnp.sum(dHn * Hp, axis=1, keepdims=True), axis=0, keepdims=True)
            T2 = _nt(Bm, dHc)
            dX = dX + T2 * e
            de = jnp.sum(Xr * T2, axis=1, keepdims=True)
            dBm = dBm + _nn((Xr * e).astype(CDT), dHc)
            dee = de * e
            daL = daL + jnp.sum(dee, axis=0, keepdims=True)
            dac_c = dac_c - dee + jnp.where(subc == CHUNK - 1, daL, 0.0)
            ddt_c = de * ex
            dh_scr[r] = dHp
            dx_ref[:, r * HEAD_DIM:(r + 1) * HEAD_DIM] = dX
            colp = jnp.where(col == r, dac_c, colp)
            colp = jnp.where(col == SSD_HPG + r, ddt_c, colp)
            rowp = jnp.where(sub8 == r, dac_r, rowp)
            rowp = jnp.where(sub8 == SSD_HPG + r, ddt_r, rowp)
        dGc = dG.astype(CDT)
        db_ref[...] = dBm + _tn(dGc, Cm)
        dc_ref[...] = dCm + _nn(dGc, Bm)
        colp_ref[0] = colp
        rowp_ref[0] = rowp
        dd_ref[0] += ddv

    f32 = lambda shape: SDS(shape, F32)
    return _pallas(body, grid=(G, nc),
                   in_specs=[sp["x"], sp["x"], sp["b"], sp["c"], sp["x"], sp["x"], sp["hp"], sp["col"], sp["col"],
                             sp["row"], sp["row"], sp["vec"], sp["ng"]],
                   out_specs=[sp["x"],
                              BS((CHUNK, LANES), lambda g, c: (nc - 1 - c, g)),
                              BS((CHUNK, LANES), lambda g, c: (nc - 1 - c, g)),
                              sp["x"],
                              BS((1, CHUNK, LANES), lambda g, c: (g, nc - 1 - c, 0)),
                              BS((1, HALO, CHUNK), lambda g, c: (g, 0, nc - 1 - c)),
                              BS((1, 1, LANES), lambda g, c: (g, 0, 0)),
                              sp["ng"]],
                   out_shape=[f32((T, DI)), f32((T, G * LANES)), f32((T, G * LANES)), SDS((T, DI), CDT),
                              f32((G, T, LANES)), f32((G, HALO, T)), f32((G, 1, LANES)), f32((1, DI))],
                   scratch_shapes=[pltpu.VMEM((SSD_HPG, HEAD_DIM, LANES), F32)],
                   compiler_params=_params("parallel", "arbitrary"), name=name)(
                       dyn, xbc, xbc, xbc, z, y, hp, dt, ac, dtT, acT, dskip, ng)


HBM = pl.BlockSpec(memory_space=pltpu.HBM)
VMEM = pl.BlockSpec(memory_space=pltpu.VMEM)


def _me():
    return lax.axis_index("x"), lax.axis_index("y"), lax.axis_index("c")


def _other_chips(x, y):
    return [(1 - x, y), (x, 1 - y), (1 - x, 1 - y)]


def _rcopy(src, dst, send_sem, recv_sem, to):
    return pltpu.make_async_remote_copy(src_ref=src, dst_ref=dst, send_sem=send_sem, recv_sem=recv_sem,
                                        device_id=to, device_id_type=MESH)


def allgather_weights(pack, *, name):
    _, Rh, W = pack.shape

    def body(src, out, send_sems, recv_sems):
        x, y, c = _me()
        k = 2 * x + y
        sibling = (x, y, 1 - c)
        chips = _other_chips(x, y)
        sends = [_rcopy(src.at[c], out.at[k, c], send_sems.at[j], recv_sems.at[j], (cx, cy, c))
                 for j, (cx, cy) in enumerate(chips)]
        for cp in sends:
            cp.start()
        passed = []
        for j, (cx, cy) in enumerate(chips):
            kj = 2 * cx + cy
            landed = out.at[kj, c]
            _rcopy(landed, landed, send_sems.at[j], recv_sems.at[j], (cx, cy, c)).wait_recv()
            fw = _rcopy(landed, landed, send_sems.at[3 + j], recv_sems.at[3 + j], sibling)
            fw.start()
            passed.append(fw)
        for j, (cx, cy) in enumerate(chips):
            theirs = out.at[2 * cx + cy, 1 - c]
            _rcopy(theirs, theirs, send_sems.at[3 + j], recv_sems.at[3 + j], sibling).wait_recv()
        for cp in sends + passed:
            cp.wait_send()

    return _pallas(body, in_specs=[HBM], out_specs=HBM, out_shape=SDS((N_CHIPS, 2, Rh, W), pack.dtype),
                   scratch_shapes=[pltpu.SemaphoreType.DMA((6,)), pltpu.SemaphoreType.DMA((6,))],
                   name=name)(pack)


def sibling_swap_half(p, *, name):
    _, n, Rh, W = p.shape

    def body(src, out, send_sem, recv_sem):
        x, y, c = _me()
        cp = _rcopy(src.at[1 - c], out, send_sem, recv_sem, (x, y, 1 - c))
        cp.start()
        cp.wait()

    return _pallas(body, in_specs=[HBM], out_specs=HBM, out_shape=SDS((n, Rh, W), p.dtype),
                   scratch_shapes=[pltpu.SemaphoreType.DMA, pltpu.SemaphoreType.DMA], name=name)(p)


def chip_scatter(s1, *, name):
    n, Rh, W = s1.shape

    def body(src, out, send_sems, recv_sems):
        x, y, c = _me()
        k = 2 * x + y
        chips = _other_chips(x, y)
        sends = [_rcopy(src.at[2 * cx + cy], out.at[k], send_sems.at[j], recv_sems.at[j], (cx, cy, c))
                 for j, (cx, cy) in enumerate(chips)]
        for cp in sends:
            cp.start()
        for j, (cx, cy) in enumerate(chips):
            slot = out.at[2 * cx + cy]
            _rcopy(slot, slot, send_sems.at[j], recv_sems.at[j], (cx, cy, c)).wait_recv()
        for cp in sends:
            cp.wait_send()

    return _pallas(body, in_specs=[HBM], out_specs=HBM, out_shape=SDS((n, Rh, W), s1.dtype),
                   scratch_shapes=[pltpu.SemaphoreType.DMA((3,)), pltpu.SemaphoreType.DMA((3,))],
                   name=name)(s1)


def sibling_pair(r2, *, name):
    _, Rh, W = r2.shape

    def body(src, out, send_sem, recv_sem):
        x, y, c = _me()
        cp = _rcopy(src.at[c], out.at[c], send_sem, recv_sem, (x, y, 1 - c))
        cp.start()
        theirs = out.at[1 - c]
        _rcopy(theirs, theirs, send_sem, recv_sem, (x, y, 1 - c)).wait_recv()
        cp.wait_send()

    return _pallas(body, in_specs=[HBM], out_specs=HBM, out_shape=SDS((2, Rh, W), r2.dtype),
                   input_output_aliases={0: 0},
                   scratch_shapes=[pltpu.SemaphoreType.DMA, pltpu.SemaphoreType.DMA], name=name)(r2)


def allgather_small(v, *, name):
    R, W = v.shape

    def body(v_ref, out_ref, sum_ref, send_sems, recv_sems):
        x, y, c = _me()
        me = 4 * x + 2 * y + c
        out_ref[me] = v_ref[...]
        peers = []
        for m in range(1, N_DEV):
            px = 1 - x if m & 4 else x
            py = 1 - y if m & 2 else y
            pc = 1 - c if m & 1 else c
            peers.append((px, py, pc))
        sends = [_rcopy(v_ref, out_ref.at[me], send_sems.at[j], recv_sems.at[j], p) for j, p in enumerate(peers)]
        for cp in sends:
            cp.start()
        for j, (px, py, pc) in enumerate(peers):
            slot = out_ref.at[4 * px + 2 * py + pc]
            _rcopy(slot, slot, send_sems.at[j], recv_sems.at[j], (px, py, pc)).wait_recv()
        for cp in sends:
            cp.wait_send()
        acc = out_ref[0]
        for d in range(1, N_DEV):
            acc = acc + out_ref[d]
        sum_ref[...] = acc

    return _pallas(body, in_specs=[VMEM], out_specs=[VMEM, VMEM],
                   out_shape=[SDS((N_DEV, R, W), v.dtype), SDS((R, W), v.dtype)],
                   scratch_shapes=[pltpu.SemaphoreType.DMA((N_DEV - 1,)), pltpu.SemaphoreType.DMA((N_DEV - 1,))],
                   name=name)(v)


def _row_tile(R, cap=512):
    best = None
    for t in range(32, cap + 1, 32):
        if R % t == 0:
            best = t
    assert best is not None, R
    return best


def add_halves(p, recv, c_idx, *, name):
    _, n, Rh, W = p.shape
    tr = _row_tile(Rh)

    def body(c_ref, p_ref, r_ref, o_ref):
        o_ref[...] = (p_ref[0].astype(F32) + r_ref[...].astype(F32)).astype(o_ref.dtype)

    gs = pltpu.PrefetchScalarGridSpec(
        num_scalar_prefetch=1, grid=(n, Rh // tr),
        in_specs=[BS((1, 1, tr, W), lambda s, i, c_ref: (c_ref[0], s, i, 0)), BS((1, tr, W), lambda s, i, c_ref: (s, i, 0))],
        out_specs=BS((1, tr, W), lambda s, i, c_ref: (s, i, 0)))
    return _pallas(body, grid_spec=gs, out_shape=SDS((n, Rh, W), p.dtype),
                   compiler_params=_params("parallel", "parallel"), name=name)(c_idx, p, recv)


def sum_slots(s1, got, kc_idx, *, name):
    n, Rh, W = got.shape
    tr = _row_tile(Rh)

    def body(kc_ref, s_ref, g1, g2, g3, o_ref):
        o_ref[0] = s_ref[0].astype(F32) + g1[0].astype(F32) + g2[0].astype(F32) + g3[0].astype(F32)

    slot = lambda d: BS((1, tr, W), lambda i, kc: ((kc[0] + d) % n, i, 0))
    gs = pltpu.PrefetchScalarGridSpec(
        num_scalar_prefetch=1, grid=(Rh // tr,), in_specs=[slot(0), slot(1), slot(2), slot(3)],
        out_specs=BS((1, tr, W), lambda i, kc: (kc[1], i, 0)))
    return _pallas(body, grid_spec=gs, out_shape=SDS((2, Rh, W), F32),
                   compiler_params=_params("parallel"), name=name)(kc_idx, s1, got, got, got)


BIG = [("ffn1_w_gu", True), ("ffn1_w_down", False), ("ffn2_w_gu", True), ("ffn2_w_down", False),
       ("sb_w_qkv", True), ("sb_w_o", False), ("ssd_w_in", True), ("ssd_w_out", False),
       ("sc_w_in", True), ("sc_w_out", False)]
PACK_ALIGN = 64


def _big_layout(shards):
    out, off = [], 0
    for name, cols in BIG:
        shape = shards[name].shape
        rows = math.prod(shape) // PACK_W
        assert rows * PACK_W == math.prod(shape), (name, shape)
        out.append((name, cols, shape, off, rows))
        off += rows
    return out, -(-off // PACK_ALIGN) * PACK_ALIGN


def _pack_rows(pieces, total, axis):
    used = sum(p.shape[axis] for p in pieces)
    if total > used:
        pad_shape = list(pieces[0].shape)
        pad_shape[axis] = total - used
        pieces = pieces + [jnp.zeros(pad_shape, pieces[0].dtype)]
    return jnp.concatenate(pieces, axis=axis)


def _gather_big(shards, tag):
    layout, total = _big_layout(shards)
    pack = _pack_rows([shards[n].astype(CDT).reshape(-1, PACK_W) for n, *_ in layout], total, 0)
    pack = pack.reshape(2, total // 2, PACK_W)
    full = allgather_weights(pack, name=f"{tag}_allgather")
    k_chip = 2 * lax.axis_index("x") + lax.axis_index("y")
    full = lax.dynamic_update_slice(full, pack[None], (k_chip, 0, 0, 0)).reshape(N_CHIPS, total, PACK_W)
    out = {}
    for name, cols, (L, K, n), off, rows in layout:
        w = full[:, off:off + rows].reshape(N_CHIPS, L, K, n)
        if cols:
            out[name] = w.transpose(1, 2, 0, 3).reshape(L, K, N_CHIPS * n)
        else:
            out[name] = w.transpose(1, 0, 2, 3).reshape(L, N_CHIPS * K, n)
    return out


def _reduce_big(grads, shards, tag):
    layout, total = _big_layout(shards)
    pieces = []
    for name, cols, (L, K, n), off, rows in layout:
        g = grads[name]
        if cols:
            g = g.reshape(L, K, N_CHIPS, n).transpose(2, 0, 1, 3)
        else:
            g = g.reshape(L, N_CHIPS, K, n).transpose(1, 0, 2, 3)
        pieces.append(g.astype(CDT).reshape(N_CHIPS, rows, PACK_W))
    pack = _pack_rows(pieces, total, 1)
    half = total // 2
    p = pack.reshape(N_CHIPS, 2, half, PACK_W).transpose(1, 0, 2, 3)
    c_core = lax.axis_index("c").astype(jnp.int32)
    k_chip = (2 * lax.axis_index("x") + lax.axis_index("y")).astype(jnp.int32)
    recv = sibling_swap_half(p, name=f"{tag}_sibling_swap")
    s1 = add_halves(p, recv, c_core.reshape(1), name=f"{tag}_add_halves")
    got = chip_scatter(s1, name=f"{tag}_chip_scatter")
    r2 = sum_slots(s1, got, jnp.stack([k_chip, c_core]), name=f"{tag}_sum_slots")
    red = sibling_pair(r2, name=f"{tag}_sibling_pair").reshape(total, PACK_W)
    return {name: red[off:off + rows].reshape(shape) for name, cols, shape, off, rows in layout}


def _pad_lanes(v, width=LANES):
    return jnp.pad(v, ((0, 0), (0, width - v.shape[1])))


def _small_pack(vecs):
    flat, meta, off = [], [], 0
    for v in vecs:
        flat.append(v.reshape(-1))
        meta.append((off, v.size, v.shape))
        off += v.size
    total = -(-off // (8 * LANES)) * 8 * LANES
    flat.append(jnp.zeros((total - off,), F32))
    return jnp.concatenate(flat).reshape(-1, LANES), meta


def _small_unpack(pack, meta):
    flat = pack.reshape(-1)
    return [flat[off:off + size].reshape(shape) for off, size, shape in meta]


def _ffn_fwd(x, g, wgu, wd, tag):
    h = rms_fwd(x, g, name=f"{tag}_norm")
    gu = matmul(h, wgu, out_dtype=CDT, name=f"{tag}_gu")
    a = swiglu_fwd(gu, name=f"{tag}_act")
    return matmul(a, wd, res=x, scale=0.5, name=f"{tag}_down"), (x, h, gu, a)


def _ffn_bwd(dout, saved, g, wgu, wd, tag):
    x, h, gu, a = saved
    dwd = matmul(a, dout, ta=True, scale=0.5, name=f"{tag}_dwd")
    da = matmul(dout, wd, tb=True, scale=0.5, name=f"{tag}_da")
    dgu = swiglu_bwd(da, gu, name=f"{tag}_dact")
    dwgu = matmul(h, dgu, ta=True, name=f"{tag}_dwgu")
    dh = matmul(dgu, wgu, tb=True, name=f"{tag}_dh")
    dx, dg = rms_bwd(dh, x, g, dout, name=f"{tag}_dnorm")
    return dx, dg, dwgu, dwd


def _sb_mixer_fwd(x, g, wqkv, wo, cfg, tag):
    T, D = x.shape
    H = cfg.sb_heads
    h = rms_fwd(x, g, name=f"{tag}_norm")
    qkv = matmul(h, wqkv, out_dtype=CDT, name=f"{tag}_qkv")
    qkv = qkv.reshape(T, 3, H, HEAD_DIM).transpose(1, 2, 0, 3)
    o, tab = sb_fwd(qkv[0], qkv[1], qkv[2], name=f"{tag}_attn")
    o2 = o.transpose(1, 0, 2).reshape(T, D)
    return matmul(o2, wo, res=x, name=f"{tag}_out"), (x, h, qkv, tab, o2)


def _sb_mixer_bwd(dout, saved, g, wqkv, wo, cfg, tag):
    x, h, qkv, tab, o2 = saved
    T, D = x.shape
    H = cfg.sb_heads
    dwo = matmul(o2, dout, ta=True, name=f"{tag}_dwo")
    do = matmul(dout, wo, tb=True, out_dtype=CDT, name=f"{tag}_do")
    do = do.reshape(T, H, HEAD_DIM).transpose(1, 0, 2)
    dq, dk, dv = sb_bwd(qkv[0], qkv[1], qkv[2], do, tab, name=f"{tag}_dattn")
    dqkv = jnp.stack([dq, dk, dv]).transpose(2, 0, 1, 3).reshape(T, 3 * D)
    dwqkv = matmul(h, dqkv, ta=True, name=f"{tag}_dwqkv")
    dh = matmul(dqkv, wqkv, tb=True, name=f"{tag}_dh")
    dx, dg = rms_bwd(dh, x, g, dout, name=f"{tag}_dnorm")
    return dx, dg, dwqkv, dwo


def _sc_mixer_fwd(x, g, win, convw, wout, tag):
    h = rms_fwd(x, g, name=f"{tag}_norm")
    proj = matmul(h, win, name=f"{tag}_in")
    m = sc_fwd(proj, convw, name=f"{tag}_conv")
    return matmul(m, wout, res=x, name=f"{tag}_out"), (x, h, proj, m)


def _sc_mixer_bwd(dout, saved, g, win, convw, wout, tag):
    x, h, proj, m = saved
    dwout = matmul(m, dout, ta=True, name=f"{tag}_dwout")
    dm = matmul(dout, wout, tb=True, name=f"{tag}_dm")
    dproj, dconvw = sc_bwd(dm, proj, convw, name=f"{tag}_dconv")
    dwin = matmul(h, dproj, ta=True, name=f"{tag}_dwin")
    dh = matmul(dproj, win, tb=True, name=f"{tag}_dh")
    dx, dg = rms_bwd(dh, x, g, dout, name=f"{tag}_dnorm")
    return dx, dg, dwin, dconvw[:convw.shape[0]], dwout


def _ssd_split_w(win, cfg):
    DI, CV = cfg.DI, cfg.conv_dim
    return win[:, :DI], win[:, DI:DI + CV], _pad_lanes(win[:, DI + CV:])


def _ssd_mixer_fwd(x, g, win, convw, convb, dtb, alog, dskip, ng, wout, cfg, tag):
    wz, wx, wdt = _ssd_split_w(win, cfg)
    h = rms_fwd(x, g, name=f"{tag}_norm")
    z = matmul(h, wz, name=f"{tag}_z")
    xpre = matmul(h, wx, name=f"{tag}_xbc")
    dtpre = matmul(h, wdt, name=f"{tag}_dt")
    pre, xbc = ssd_conv_fwd(xpre, convw, convb, name=f"{tag}_conv")
    dt, ac, dtT, acT = ssd_prep_fwd(dtpre, dtb, alog, name=f"{tag}_prep")
    y, yn, hp = ssd_chunk_fwd(xbc, z, dt, ac, dtT, acT, dskip, ng, cfg, name=f"{tag}_scan")
    out = matmul(yn, wout, res=x, name=f"{tag}_out")
    return out, (x, h, z, xpre, dtpre, pre, xbc, dt, ac, dtT, acT, y, yn, hp)


def _ssd_mixer_bwd(dout, saved, g, win, convw, convb, dtb, alog, dskip, ng, wout, cfg, tag):
    x, h, z, xpre, dtpre, pre, xbc, dt, ac, dtT, acT, y, yn, hp = saved
    T = x.shape[0]
    NH, G = cfg.NH, cfg.G
    wz, wx, wdt = _ssd_split_w(win, cfg)
    dwout = matmul(yn, dout, ta=True, name=f"{tag}_dwout")
    dyn = matmul(dout, wout, tb=True, name=f"{tag}_dyn")
    dxs, db, dc, dz, colp, rowp, dd, dng = ssd_chunk_bwd(dyn, xbc, z, y, hp, dt, ac, dtT, acT, dskip, ng, cfg,
                                                        name=f"{tag}_dscan")
    col = lambda lo: _pad_lanes(colp[:, :, lo:lo + SSD_HPG].transpose(1, 0, 2).reshape(T, NH))
    row = lambda lo: jnp.pad(rowp[:, lo:lo + SSD_HPG, :].reshape(NH, T), ((0, LANES - NH), (0, 0)))
    ddtpre, dalog, ddtb = ssd_prep_bwd(col(0), col(SSD_HPG), row(0), row(SSD_HPG), dtpre, dtb, alog, name=f"{tag}_dprep")
    dxpre, dconvw, dconvb = ssd_conv_bwd(dxs, db, dc, pre, xpre, convw, name=f"{tag}_dconv")
    dwz = matmul(h, dz, ta=True, name=f"{tag}_dwz")
    dwx = matmul(h, dxpre, ta=True, name=f"{tag}_dwx")
    dwdt = matmul(h, ddtpre, ta=True, name=f"{tag}_dwdt")
    dh = matmul(dz, wz, tb=True, name=f"{tag}_dh_z")
    dh = matmul(dxpre, wx, tb=True, res=dh, name=f"{tag}_dh_x")
    dh = matmul(ddtpre, wdt, tb=True, res=dh, name=f"{tag}_dh_dt")
    dx, dg = rms_bwd(dh, x, g, dout, name=f"{tag}_dnorm")
    dwin = jnp.concatenate([dwz, dwx, dwdt[:, :NH]], axis=1)
    small = dict(conv_w=dconvw[:convw.shape[0]], conv_b=dconvb, dt_bias=ddtb[:, :NH], a_log=dalog[:, :NH],
                 d=dd[:, 0, :SSD_HPG].reshape(1, NH), norm=dng)
    return dx, dg, dwin, dwout, small


INPUT_NAMES = ['x', 'ffn1_norm', 'ffn1_w_gu', 'ffn1_w_down', 'mix_norm', 'ffn2_norm', 'ffn2_w_gu', 'ffn2_w_down',
               'sb_w_qkv', 'sb_w_o', 'ssd_w_in', 'ssd_conv_w', 'ssd_conv_b', 'ssd_dt_bias', 'ssd_a_log', 'ssd_d',
               'ssd_norm', 'ssd_w_out', 'sc_w_in', 'sc_conv_w', 'sc_w_out', 'final_norm']
WEIGHT_NAMES = INPUT_NAMES[1:]
SHARDED_SMALL = ("ssd_conv_w", "sc_conv_w")


def train_step(a, cfg):
    x = a["x"][0]
    tgt = a["loss_target"][0]
    big_names = [n for n, _ in BIG]
    k_chip = 2 * lax.axis_index("x") + lax.axis_index("y")

    w = _gather_big({n: a[n] for n in big_names}, "w")
    conv_pack, conv_meta = _small_pack([a[n] for n in SHARDED_SMALL])
    conv_all, _ = allgather_small(conv_pack, name="conv_w_allgather")
    conv_full = {}
    for i, n in enumerate(SHARDED_SMALL):
        per_chip = [_small_unpack(conv_all[2 * kk], conv_meta)[i] for kk in range(N_CHIPS)]
        conv_full[n] = jnp.concatenate(per_chip, axis=-1)

    def vec(name, i):
        return a[name][i:i + 1]

    ssd_small = lambda j: (a["ssd_conv_b"][j:j + 1], _pad_lanes(a["ssd_dt_bias"][j:j + 1]),
                           _pad_lanes(a["ssd_a_log"][j:j + 1]), _pad_lanes(a["ssd_d"][j:j + 1]), a["ssd_norm"][j:j + 1])

    saved = []
    for i in range(cfg.depth):
        kind, j = i % 3, i // 3
        x, s1 = _ffn_fwd(x, vec("ffn1_norm", i), w["ffn1_w_gu"][i], w["ffn1_w_down"][i], f"l{i}_ffn1")
        if kind == 0:
            x, s2 = _sb_mixer_fwd(x, vec("mix_norm", i), w["sb_w_qkv"][j], w["sb_w_o"][j], cfg, f"l{i}_sb")
        elif kind == 1:
            x, s2 = _ssd_mixer_fwd(x, vec("mix_norm", i), w["ssd_w_in"][j], conv_full["ssd_conv_w"][j], *ssd_small(j),
                                   w["ssd_w_out"][j], cfg, f"l{i}_ssd")
        else:
            x, s2 = _sc_mixer_fwd(x, vec("mix_norm", i), w["sc_w_in"][j], conv_full["sc_conv_w"][j], w["sc_w_out"][j],
                                  f"l{i}_sc")
        x, s3 = _ffn_fwd(x, vec("ffn2_norm", i), w["ffn2_w_gu"][i], w["ffn2_w_down"][i], f"l{i}_ffn2")
        saved.append((s1, s2, s3))
    loss_part, dx, d_final = loss_head(x, a["final_norm"].reshape(1, -1), tgt, name="loss_head")

    gbig = {n: [None] * a[n].shape[0] for n in big_names}
    gsmall = {n: [None] * a[n].shape[0] for n in ("ffn1_norm", "mix_norm", "ffn2_norm", "ssd_conv_w", "ssd_conv_b",
                                                   "ssd_dt_bias", "ssd_a_log", "ssd_d", "ssd_norm", "sc_conv_w")}
    for i in reversed(range(cfg.depth)):
        kind, j = i % 3, i // 3
        s1, s2, s3 = saved[i]
        dx, gsmall["ffn2_norm"][i], gbig["ffn2_w_gu"][i], gbig["ffn2_w_down"][i] = _ffn_bwd(
            dx, s3, vec("ffn2_norm", i), w["ffn2_w_gu"][i], w["ffn2_w_down"][i], f"l{i}_ffn2")
        if kind == 0:
            dx, gsmall["mix_norm"][i], gbig["sb_w_qkv"][j], gbig["sb_w_o"][j] = _sb_mixer_bwd(
                dx, s2, vec("mix_norm", i), w["sb_w_qkv"][j], w["sb_w_o"][j], cfg, f"l{i}_sb")
        elif kind == 1:
            dx, gsmall["mix_norm"][i], gbig["ssd_w_in"][j], gbig["ssd_w_out"][j], sm = _ssd_mixer_bwd(
                dx, s2, vec("mix_norm", i), w["ssd_w_in"][j], conv_full["ssd_conv_w"][j], *ssd_small(j),
                w["ssd_w_out"][j], cfg, f"l{i}_ssd")
            for key, val in sm.items():
                gsmall["ssd_" + key][j] = val
        else:
            dx, gsmall["mix_norm"][i], gbig["sc_w_in"][j], gsmall["sc_conv_w"][j], gbig["sc_w_out"][j] = _sc_mixer_bwd(
                dx, s2, vec("mix_norm", i), w["sc_w_in"][j], conv_full["sc_conv_w"][j], w["sc_w_out"][j], f"l{i}_sc")
        dx, gsmall["ffn1_norm"][i], gbig["ffn1_w_gu"][i], gbig["ffn1_w_down"][i] = _ffn_bwd(
            dx, s1, vec("ffn1_norm", i), w["ffn1_w_gu"][i], w["ffn1_w_down"][i], f"l{i}_ffn1")
    grad_x = dx[None]

    red_big = _reduce_big({n: jnp.stack(v) for n, v in gbig.items()}, {n: a[n] for n in big_names}, "g")
    small_names = list(gsmall) + ["final_norm"]
    small_vals = [jnp.concatenate([p.reshape((1,) + a[n].shape[1:-1] + (-1,)) for p in gsmall[n]], axis=0)
                  for n in gsmall] + [d_final.reshape(-1)]
    gpack, gmeta = _small_pack(small_vals + [loss_part[:, :1]])
    _, gsum = allgather_small(gpack, name="small_allreduce")
    *small_red, loss = _small_unpack(gsum, gmeta)
    red = dict(red_big)
    for n, v in zip(small_names, small_red):
        if n in SHARDED_SMALL:
            width = a[n].shape[-1]
            v = lax.dynamic_slice_in_dim(v, k_chip * width, width, axis=v.ndim - 1)
        red[n] = v.reshape(a[n].shape)

    delta, new_m, new_v = {}, {}, {}
    two_d = lambda t: t.reshape(-1, t.shape[-1])
    for n in big_names:
        d_, m_, v_ = adamw(two_d(a[n]), two_d(red[n]), two_d(a["m_" + n]), two_d(a["v_" + n]), name=f"adamw_{n}")
        delta[n], new_m[n], new_v[n] = (t.reshape(a[n].shape) for t in (d_, m_, v_))
    rest = [n for n in WEIGHT_NAMES if n not in big_names]
    packs = [_small_pack([src[n] for n in rest]) for src in
             (a, red, {n: a["m_" + n] for n in rest}, {n: a["v_" + n] for n in rest})]
    outs = adamw(*[p for p, _ in packs], name="adamw_small")
    for dst, o in zip((delta, new_m, new_v), outs):
        for n, t in zip(rest, _small_unpack(o, packs[0][1])):
            dst[n] = t
    return (loss.reshape(()), grad_x, *[red[n] for n in WEIGHT_NAMES], *[delta[n] for n in WEIGHT_NAMES],
            *[new_m[n] for n in WEIGHT_NAMES], *[new_v[n] for n in WEIGHT_NAMES])


def kernel(x, ffn1_norm, ffn1_w_gu, ffn1_w_down, mix_norm, ffn2_norm, ffn2_w_gu, ffn2_w_down, sb_w_qkv, sb_w_o, ssd_w_in, ssd_conv_w, ssd_conv_b, ssd_dt_bias, ssd_a_log, ssd_d, ssd_norm, ssd_w_out, sc_w_in, sc_conv_w, sc_w_out, final_norm, loss_target, m_ffn1_norm, m_ffn1_w_gu, m_ffn1_w_down, m_mix_norm, m_ffn2_norm, m_ffn2_w_gu, m_ffn2_w_down, m_sb_w_qkv, m_sb_w_o, m_ssd_w_in, m_ssd_conv_w, m_ssd_conv_b, m_ssd_dt_bias, m_ssd_a_log, m_ssd_d, m_ssd_norm, m_ssd_w_out, m_sc_w_in, m_sc_conv_w, m_sc_w_out, m_final_norm, v_ffn1_norm, v_ffn1_w_gu, v_ffn1_w_down, v_mix_norm, v_ffn2_norm, v_ffn2_w_gu, v_ffn2_w_down, v_sb_w_qkv, v_sb_w_o, v_ssd_w_in, v_ssd_conv_w, v_ssd_conv_b, v_ssd_dt_bias, v_ssd_a_log, v_ssd_d, v_ssd_norm, v_ssd_w_out, v_sc_w_in, v_sc_conv_w, v_sc_w_out, v_final_norm):
    args = dict(locals())
    return train_step(args, Cfg())
```

```python
import functools
import math

import jax
import jax.numpy as jnp
from jax import lax
from jax.experimental import pallas as pl
from jax.experimental.pallas import tpu as pltpu

F32 = jnp.float32
CDT = jnp.bfloat16
HI = lax.Precision.HIGHEST
BS = pl.BlockSpec
SDS = jax.ShapeDtypeStruct
MESH = pl.DeviceIdType.MESH

RMS_EPS = 1e-6
ADAM_LR, ADAM_B1, ADAM_B2, ADAM_EPS, ADAM_WD, ADAM_STEP = 0.001, 0.9, 0.999, 1e-08, 0.01, 10
V7X_VMEM_LIMIT_BYTES = 56 * 1024 * 1024
LANES = 128
CHUNK = 128
HEAD_DIM = 64
SSD_HPG = 4
N_CHIPS = 4
N_DEV = 8
PACK_W = 1024


class Cfg:
    def __init__(self, d_model=1024, d_ff=2816, depth=4, sb_heads=16, ssd_groups=8, ssd_state=128,
                 ssd_conv=4, sc_width=3):
        self.D, self.F, self.depth = d_model, d_ff, depth
        self.sb_heads = sb_heads
        self.DI = 2 * d_model
        self.G = ssd_groups
        self.NH = self.DI // HEAD_DIM
        assert self.NH == SSD_HPG * self.G and ssd_state == LANES and sb_heads * HEAD_DIM == d_model
        self.conv_dim = self.DI + 2 * self.G * ssd_state
        self.in_dim = self.DI + self.conv_dim + self.NH
        self.ssd_conv, self.sc_width = ssd_conv, sc_width
        self.n_sb, self.n_ssd, self.n_sc = (depth + 2) // 3, (depth + 1) // 3, depth // 3


def _pallas(body, **kw):
    return pl.pallas_call(body, **kw)


def _params(*sem):
    return pltpu.CompilerParams(dimension_semantics=sem, vmem_limit_bytes=V7X_VMEM_LIMIT_BYTES)


def _tile(n, cap):
    if n <= cap:
        return n
    best = None
    for m in range(1, n // LANES + 1):
        t = m * LANES
        if n % t == 0 and t <= cap:
            best = t
    assert best is not None, (n, cap)
    return best


def _sigmoid(x):
    return 1.0 / (1.0 + jnp.exp(-x))


def _softplus_neg_abs(x):
    return jnp.log(1.0 + jnp.exp(-jnp.abs(x)))


def _colsel(blk, idx):
    lane = lax.broadcasted_iota(jnp.int32, blk.shape, 1)
    return jnp.sum(jnp.where(lane == idx, blk, 0.0), axis=1, keepdims=True)


def _rowsel(blk, idx):
    sub = lax.broadcasted_iota(jnp.int32, blk.shape, 0)
    return jnp.sum(jnp.where(sub == idx, blk, 0.0), axis=0, keepdims=True)


def _nt(a, b):
    return lax.dot_general(a, b, (((1,), (1,)), ((), ())), preferred_element_type=F32)


def _tn(a, b):
    return lax.dot_general(a, b, (((0,), (0,)), ((), ())), preferred_element_type=F32)


def _nn(a, b):
    return jnp.dot(a, b, preferred_element_type=F32)


def matmul(a, b, *, name, ta=False, tb=False, out_dtype=F32, res=None, scale=1.0, tm=1408, tn=1408, tk=1408):
    M, K = (a.shape[1], a.shape[0]) if ta else a.shape
    N, K2 = b.shape if tb else (b.shape[1], b.shape[0])
    assert K == K2, (a.shape, b.shape, ta, tb)
    tm, tn, tk = _tile(M, tm), _tile(N, tn), _tile(K, tk)
    nk = K // tk
    dn = (((0 if ta else 1,), (1 if tb else 0,)), ((), ()))

    def body(*refs):
        a_ref, b_ref = refs[:2]
        r_ref = refs[2] if res is not None else None
        o_ref = refs[3 if res is not None else 2]
        part = lax.dot_general(a_ref[...].astype(CDT), b_ref[...].astype(CDT), dn, preferred_element_type=F32)

        def finish(total):
            o = total * scale
            if res is not None:
                o = o + r_ref[...].astype(F32)
            o_ref[...] = o.astype(o_ref.dtype)

        if nk == 1:
            finish(part)
            return
        acc = refs[-1]
        k = pl.program_id(2)

        @pl.when(k == 0)
        def _():
            acc[...] = part

        @pl.when(jnp.logical_and(k > 0, k < nk - 1))
        def _():
            acc[...] += part

        @pl.when(k == nk - 1)
        def _():
            finish(acc[...] + part)

    a_spec = BS((tk, tm), lambda i, j, k: (k, i)) if ta else BS((tm, tk), lambda i, j, k: (i, k))
    b_spec = BS((tn, tk), lambda i, j, k: (j, k)) if tb else BS((tk, tn), lambda i, j, k: (k, j))
    o_spec = BS((tm, tn), lambda i, j, k: (i, j))
    ins, specs = [a, b], [a_spec, b_spec]
    if res is not None:
        ins.append(res)
        specs.append(o_spec)
    return _pallas(body, grid=(M // tm, N // tn, nk), in_specs=specs, out_specs=o_spec,
                   out_shape=SDS((M, N), out_dtype),
                   scratch_shapes=[pltpu.VMEM((tm, tn), F32)] if nk > 1 else [],
                   compiler_params=_params("parallel", "parallel", "arbitrary"), name=name)(*ins)


def rms_fwd(x, g, *, name):
    T, D = x.shape
    tm = _tile(T, 512)

    def body(x_ref, g_ref, o_ref):
        xv = x_ref[...]
        r = lax.rsqrt(jnp.mean(xv * xv, axis=-1, keepdims=True) + RMS_EPS)
        o_ref[...] = (xv * r * g_ref[...]).astype(o_ref.dtype)

    return _pallas(body, grid=(T // tm,), in_specs=[BS((tm, D), lambda i: (i, 0)), BS((1, D), lambda i: (0, 0))],
                   out_specs=BS((tm, D), lambda i: (i, 0)), out_shape=SDS((T, D), CDT),
                   compiler_params=_params("parallel"), name=name)(x, g)


def rms_bwd(dh, x, g, dres, *, name):
    T, D = x.shape
    tm = _tile(T, 512)

    def body(dh_ref, x_ref, g_ref, r_ref, dx_ref, dg_ref):
        i = pl.program_id(0)
        xv = x_ref[...]
        r = lax.rsqrt(jnp.mean(xv * xv, axis=-1, keepdims=True) + RMS_EPS)
        xhat = xv * r
        dhv = dh_ref[...].astype(F32)
        dxh = dhv * g_ref[...]
        m = jnp.mean(dxh * xhat, axis=-1, keepdims=True)
        dx_ref[...] = r * (dxh - xhat * m) + r_ref[...]

        @pl.when(i == 0)
        def _():
            dg_ref[...] = jnp.zeros_like(dg_ref)

        dg_ref[...] += jnp.sum(dhv * xhat, axis=0, keepdims=True)

    row = BS((tm, D), lambda i: (i, 0))
    vec = BS((1, D), lambda i: (0, 0))
    return _pallas(body, grid=(T // tm,), in_specs=[row, row, vec, row], out_specs=[row, vec],
                   out_shape=[SDS((T, D), F32), SDS((1, D), F32)],
                   compiler_params=_params("arbitrary"), name=name)(dh, x, g, dres)


def swiglu_fwd(gu, *, name):
    T, F2 = gu.shape
    F = F2 // 2
    tm = _tile(T, 512)

    def body(gu_ref, o_ref):
        g = gu_ref[:, :F].astype(F32)
        u = gu_ref[:, F:].astype(F32)
        o_ref[...] = (g * _sigmoid(g) * u).astype(o_ref.dtype)

    return _pallas(body, grid=(T // tm,), in_specs=[BS((tm, F2), lambda i: (i, 0))],
                   out_specs=BS((tm, F), lambda i: (i, 0)), out_shape=SDS((T, F), CDT),
                   compiler_params=_params("parallel"), name=name)(gu)


def swiglu_bwd(da, gu, *, name):
    T, F2 = gu.shape
    F = F2 // 2
    tm = _tile(T, 512)

    def body(da_ref, gu_ref, o_ref):
        g = gu_ref[:, :F].astype(F32)
        u = gu_ref[:, F:].astype(F32)
        dav = da_ref[...].astype(F32)
        s = _sigmoid(g)
        o_ref[:, :F] = (dav * u * (s * (1.0 + g * (1.0 - s)))).astype(o_ref.dtype)
        o_ref[:, F:] = (dav * (g * s)).astype(o_ref.dtype)

    return _pallas(body, grid=(T // tm,), in_specs=[BS((tm, F), lambda i: (i, 0)), BS((tm, F2), lambda i: (i, 0))],
                   out_specs=BS((tm, F2), lambda i: (i, 0)), out_shape=SDS((T, F2), CDT),
                   compiler_params=_params("parallel"), name=name)(da, gu)


def loss_head(x, g, tgt, *, name):
    T, D = x.shape
    tm = _tile(T, 512)

    def body(x_ref, g_ref, t_ref, l_ref, dx_ref, dg_ref):
        i = pl.program_id(0)
        xv = x_ref[...]
        gv = g_ref[...]
        r = lax.rsqrt(jnp.mean(xv * xv, axis=-1, keepdims=True) + RMS_EPS)
        xhat = xv * r
        err = xhat * gv - t_ref[...]
        part = 0.5 * jnp.sum(jnp.mean(err * err, axis=-1, keepdims=True), axis=0, keepdims=True)
        dy = err * (1.0 / D)
        dxh = dy * gv
        m = jnp.mean(dxh * xhat, axis=-1, keepdims=True)
        dx_ref[...] = r * (dxh - xhat * m)

        @pl.when(i == 0)
        def _():
            dg_ref[...] = jnp.zeros_like(dg_ref)
            l_ref[...] = jnp.zeros_like(l_ref)

        dg_ref[...] += jnp.sum(dy * xhat, axis=0, keepdims=True)
        l_ref[...] += jnp.broadcast_to(part, l_ref.shape)

    row = BS((tm, D), lambda i: (i, 0))
    vec = BS((1, D), lambda i: (0, 0))
    return _pallas(body, grid=(T // tm,), in_specs=[row, vec, row],
                   out_specs=[BS((1, LANES), lambda i: (0, 0)), row, vec],
                   out_shape=[SDS((1, LANES), F32), SDS((T, D), F32), SDS((1, D), F32)],
                   compiler_params=_params("arbitrary"), name=name)(x, g, tgt)


def adamw(w, g, m, v, *, name):
    R, C = w.shape
    tr = R
    if R * C * 4 > (1 << 20):
        for cand in range(8, R + 1, 8):
            if R % cand == 0 and cand * C * 4 <= (1 << 20):
                tr = cand

    def body(w_ref, g_ref, m_ref, v_ref, d_ref, nm_ref, nv_ref):
        gv = g_ref[...]
        nm = ADAM_B1 * m_ref[...] + (1.0 - ADAM_B1) * gv
        nv = ADAM_B2 * v_ref[...] + (1.0 - ADAM_B2) * (gv * gv)
        m_hat = nm / (1.0 - ADAM_B1 ** ADAM_STEP)
        v_hat = nv / (1.0 - ADAM_B2 ** ADAM_STEP)
        d_ref[...] = -ADAM_LR * (m_hat / (jnp.sqrt(v_hat) + ADAM_EPS) + ADAM_WD * w_ref[...])
        nm_ref[...] = nm
        nv_ref[...] = nv

    blk = BS((tr, C), lambda i: (i, 0))
    return _pallas(body, grid=(R // tr,), in_specs=[blk] * 4, out_specs=[blk] * 3,
                   out_shape=[SDS((R, C), F32)] * 3, compiler_params=_params("parallel"), name=name)(w, g, m, v)


def _sb_tri():
    row = lax.broadcasted_iota(jnp.int32, (CHUNK, CHUNK), 0)
    col = lax.broadcasted_iota(jnp.int32, (CHUNK, CHUNK), 1)
    return row, col


def _tri3(keep):
    t = keep.astype(jnp.bfloat16)
    return jnp.concatenate([t, t, t], axis=0)


def _split_dot(x, tri3):
    hi = x.astype(jnp.bfloat16)
    r = x - hi.astype(F32)
    mid = r.astype(jnp.bfloat16)
    lo = (r - mid.astype(F32)).astype(jnp.bfloat16)
    return jnp.dot(jnp.concatenate([hi, mid, lo], axis=1), tri3, preferred_element_type=F32)


def _sb_heads_per_step(H, want):
    hb = min(H, want)
    assert H % hb == 0, (H, hb)
    return hb


SB_FWD_HEADS = 8
SB_BWD_HEADS = 4


def sb_fwd(q, k, v, *, name):
    H, T, dh = q.shape
    nq = T // CHUNK
    assert nq <= LANES
    scale = dh ** -0.5
    hb = _sb_heads_per_step(H, SB_FWD_HEADS)
    heads = range(hb)

    def body(q_ref, k_ref, v_ref, o_ref, tab_ref, acc_scr, cs_scr):
        qi = pl.program_id(1)
        row, col = _sb_tri()
        upper3 = _tri3(row > col)
        acc_scr[...] = jnp.zeros_like(acc_scr)
        cs_scr[...] = jnp.zeros_like(cs_scr)
        tab_ref[...] = jnp.zeros_like(tab_ref)

        def tile(j, diag):
            sl = pl.ds(pl.multiple_of(j * CHUNK, CHUNK), CHUNK)
            keep = (lambda t: jnp.where(col < row, t, 0.0)) if diag else (lambda t: t)
            zs = [_nt(q_ref[h], k_ref[h, sl, :]) * scale for h in heads]
            lszs = [jnp.minimum(z, 0.0) - _softplus_neg_abs(z) for z in zs]
            lks = [keep(lsz - z) for lsz, z in zip(lszs, zs)]
            tls = [_split_dot(lk, upper3) for lk in lks]
            css = [cs_scr[h] for h in heads]
            atts = [keep(jnp.exp(lsz + tl + cs)).astype(CDT) for lsz, tl, cs in zip(lszs, tls, css)]
            pvs = [_nn(att, v_ref[h, sl, :]) for h, att in zip(heads, atts)]
            for h in heads:
                acc_scr[h] += pvs[h]
                tab_ref[h, 0] = jnp.where(col == j, css[h], tab_ref[h, 0])
                cs_scr[h] = css[h] + jnp.sum(lks[h], axis=1, keepdims=True)

        tile(qi, True)

        def step(jj, carry):
            tile(qi - jj, False)
            return carry

        lax.fori_loop(1, qi + 1, step, 0)
        o_ref[...] = acc_scr[...].astype(o_ref.dtype)

    qs = BS((hb, CHUNK, dh), lambda h, i: (h, i, 0))
    ks = BS((hb, T, dh), lambda h, i: (h, 0, 0))
    return _pallas(body, grid=(H // hb, nq), in_specs=[qs, ks, ks],
                   out_specs=[qs, BS((hb, 1, CHUNK, CHUNK), lambda h, i: (h, i, 0, 0))],
                   out_shape=[SDS((H, T, dh), CDT), SDS((H, nq, CHUNK, CHUNK), F32)],
                   scratch_shapes=[pltpu.VMEM((hb, CHUNK, dh), F32), pltpu.VMEM((hb, CHUNK, 1), F32)],
                   compiler_params=_params("parallel", "parallel"), name=name)(q, k, v)


def sb_bwd(q, k, v, do, tab, *, name):
    H, T, dh = q.shape
    nq = T // CHUNK
    scale = dh ** -0.5
    hb = _sb_heads_per_step(H, SB_BWD_HEADS)
    heads = range(hb)

    def body(q_ref, k_ref, v_ref, do_ref, tab_ref, dq_ref, dk_ref, dv_ref, dk_acc, dv_acc, dq_scr, pfx_scr):
        qi = pl.program_id(1)

        @pl.when(qi == 0)
        def _():
            dk_acc[...] = jnp.zeros_like(dk_acc)
            dv_acc[...] = jnp.zeros_like(dv_acc)

        dq_scr[...] = jnp.zeros_like(dq_scr)
        pfx_scr[...] = jnp.zeros_like(pfx_scr)
        row, col = _sb_tri()
        upper3 = _tri3(row > col)
        before3 = _tri3(row < col)

        def tile(j, diag):
            sl = pl.ds(pl.multiple_of(j * CHUNK, CHUNK), CHUNK)
            keep = (lambda t: jnp.where(col < row, t, 0.0)) if diag else (lambda t: t)
            zs = [_nt(q_ref[h], k_ref[h, sl, :]) * scale for h in heads]
            das = [_nt(do_ref[h], v_ref[h, sl, :]) for h in heads]
            lszs = [jnp.minimum(z, 0.0) - _softplus_neg_abs(z) for z in zs]
            lks = [keep(lsz - z) for lsz, z in zip(lszs, zs)]
            tls = [_split_dot(lk, upper3) for lk in lks]
            css = [jnp.sum(jnp.where(col == j, tab_ref[h, 0], 0.0), axis=1, keepdims=True) for h in heads]
            atts = [keep(jnp.exp(lsz + tl + cs)) for lsz, tl, cs in zip(lszs, tls, css)]
            dls = [att * da for att, da in zip(atts, das)]
            pres = [_split_dot(dl, before3) for dl in dls]
            pfxs = [pfx_scr[h] for h in heads]
            sigs = [jnp.exp(lsz) for lsz in lszs]
            dzs = [((dl * (1.0 - sig) - keep(sig * (pre + pfx))) * scale).astype(CDT)
                   for dl, sig, pre, pfx in zip(dls, sigs, pres, pfxs)]
            dqs = [_nn(dz, k_ref[h, sl, :]) for h, dz in zip(heads, dzs)]
            dks = [_tn(dz, q_ref[h]) for h, dz in zip(heads, dzs)]
            dvs = [_tn(att.astype(CDT), do_ref[h]) for h, att in zip(heads, atts)]
            for h in heads:
                dq_scr[h] += dqs[h]
                dk_acc[h, sl, :] += dks[h]
                dv_acc[h, sl, :] += dvs[h]
                pfx_scr[h] = pfxs[h] + jnp.sum(dls[h], axis=1, keepdims=True)

        def step(j, carry):
            tile(j, False)
            return carry

        lax.fori_loop(0, qi, step, 0)
        tile(qi, True)
        dq_ref[...] = dq_scr[...].astype(dq_ref.dtype)

        @pl.when(qi == nq - 1)
        def _():
            dk_ref[...] = dk_acc[...].astype(dk_ref.dtype)
            dv_ref[...] = dv_acc[...].astype(dv_ref.dtype)

    qs = BS((hb, CHUNK, dh), lambda h, i: (h, i, 0))
    ks = BS((hb, T, dh), lambda h, i: (h, 0, 0))
    ts = BS((hb, 1, CHUNK, CHUNK), lambda h, i: (h, i, 0, 0))
    return _pallas(body, grid=(H // hb, nq), in_specs=[qs, ks, ks, qs, ts], out_specs=[qs, ks, ks],
                   out_shape=[SDS((H, T, dh), CDT)] * 3,
                   scratch_shapes=[pltpu.VMEM((hb, T, dh), F32), pltpu.VMEM((hb, T, dh), F32),
                                   pltpu.VMEM((hb, CHUNK, dh), F32), pltpu.VMEM((hb, CHUNK, 1), F32)],
                   compiler_params=_params("parallel", "arbitrary"), name=name)(q, k, v, do, tab)


HALO = 8


def _shift_down(cur, prev8, s):
    if s == 0:
        return cur
    tt = cur.shape[0]
    row = lax.broadcasted_iota(jnp.int32, cur.shape, 0)
    fix = jnp.concatenate([pltpu.roll(prev8, s, 0), jnp.zeros((tt - HALO, cur.shape[1]), cur.dtype)], axis=0)
    return jnp.where(row < s, fix, pltpu.roll(cur, s, 0))


def _shift_up(cur, next8, s):
    if s == 0:
        return cur
    tt = cur.shape[0]
    row = lax.broadcasted_iota(jnp.int32, cur.shape, 0)
    fix = jnp.concatenate([jnp.zeros((tt - HALO, cur.shape[1]), cur.dtype), pltpu.roll(next8, HALO - s, 0)], axis=0)
    return jnp.where(row >= tt - s, fix, pltpu.roll(cur, tt - s, 0))


def _halo_specs(tt, width, T):
    per = tt // HALO
    last = T // HALO - 1
    cur = BS((tt, width), lambda i: (i, 0))
    prev = BS((HALO, width), lambda i: (jnp.maximum(i * per - 1, 0), 0))
    nxt = BS((HALO, width), lambda i: (jnp.minimum((i + 1) * per, last), 0))
    return cur, prev, nxt


def sc_fwd(proj, w, *, name):
    T, D3 = proj.shape
    D = D3 // 3
    K = w.shape[0]
    tt = _tile(T, 256)
    cur, prev, _ = _halo_specs(tt, D3, T)

    def body(p_ref, pp_ref, w_ref, o_ref):
        i = pl.program_id(0)
        pc = p_ref[:, D:2 * D] * p_ref[:, 2 * D:]
        pp = jnp.where(i > 0, pp_ref[:, D:2 * D] * pp_ref[:, 2 * D:], 0.0)
        u = jnp.zeros((tt, D), F32)
        for kk in range(K):
            u = u + w_ref[kk:kk + 1, :] * _shift_down(pc, pp, K - 1 - kk)
        o_ref[...] = (p_ref[:, :D] * u).astype(o_ref.dtype)

    return _pallas(body, grid=(T // tt,), in_specs=[cur, prev, BS((K, D), lambda i: (0, 0))],
                   out_specs=BS((tt, D), lambda i: (i, 0)), out_shape=SDS((T, D), CDT),
                   compiler_params=_params("parallel"), name=name)(proj, proj, w)


def sc_bwd(dm, proj, w, *, name):
    T, D3 = proj.shape
    D = D3 // 3
    K = w.shape[0]
    tt = _tile(T, 256)
    cur, prev, nxt = _halo_specs(tt, D3, T)
    dcur, _, dnxt = _halo_specs(tt, D, T)
    nsteps = T // tt

    def body(dm_ref, dmn_ref, p_ref, pp_ref, pn_ref, w_ref, o_ref, dw_ref):
        i = pl.program_id(0)
        b = p_ref[:, :D]
        c = p_ref[:, D:2 * D]
        h = p_ref[:, 2 * D:]
        pc = c * h
        pp = jnp.where(i > 0, pp_ref[:, D:2 * D] * pp_ref[:, 2 * D:], 0.0)
        dmv = dm_ref[...]
        du = dmv * b
        dun = jnp.where(i < nsteps - 1, dmn_ref[...] * pn_ref[:, :D], 0.0)
        u = jnp.zeros((tt, D), F32)
        dp = jnp.zeros((tt, D), F32)
        rows = []
        for kk in range(K):
            sh = _shift_down(pc, pp, K - 1 - kk)
            u = u + w_ref[kk:kk + 1, :] * sh
            dp = dp + w_ref[kk:kk + 1, :] * _shift_up(du, dun, K - 1 - kk)
            rows.append(jnp.sum(du * sh, axis=0, keepdims=True))
        rows.append(jnp.zeros((HALO - K, D), F32))
        o_ref[:, :D] = (dmv * u).astype(o_ref.dtype)
        o_ref[:, D:2 * D] = (dp * h).astype(o_ref.dtype)
        o_ref[:, 2 * D:] = (dp * c).astype(o_ref.dtype)

        @pl.when(i == 0)
        def _():
            dw_ref[...] = jnp.zeros_like(dw_ref)

        dw_ref[...] += jnp.concatenate(rows, axis=0)

    return _pallas(body, grid=(nsteps,), in_specs=[dcur, dnxt, cur, prev, nxt, BS((K, D), lambda i: (0, 0))],
                   out_specs=[BS((tt, D3), lambda i: (i, 0)), BS((HALO, D), lambda i: (0, 0))],
                   out_shape=[SDS((T, D3), CDT), SDS((HALO, D), F32)],
                   compiler_params=_params("arbitrary"), name=name)(dm, dm, proj, proj, proj, w)


def ssd_conv_fwd(xpre, w, b, *, name):
    T, C = xpre.shape
    K = w.shape[0]
    tt = _tile(T, 256)
    cur, prev, _ = _halo_specs(tt, C, T)

    def body(x_ref, xp_ref, w_ref, b_ref, pre_ref, o_ref):
        i = pl.program_id(0)
        xv = x_ref[...]
        xp = jnp.where(i > 0, xp_ref[...], 0.0)
        acc = jnp.broadcast_to(b_ref[...], (tt, C))
        for kk in range(K):
            acc = acc + w_ref[kk:kk + 1, :] * _shift_down(xv, xp, K - 1 - kk)
        pre_ref[...] = acc
        o_ref[...] = acc * _sigmoid(acc)

    row = BS((tt, C), lambda i: (i, 0))
    return _pallas(body, grid=(T // tt,), in_specs=[cur, prev, BS((K, C), lambda i: (0, 0)), BS((1, C), lambda i: (0, 0))],
                   out_specs=[row, row], out_shape=[SDS((T, C), F32)] * 2,
                   compiler_params=_params("parallel"), name=name)(xpre, xpre, w, b)


def ssd_conv_bwd(dxs, db_, dc_, pre, xpre, w, *, name):
    T, C = xpre.shape
    K = w.shape[0]
    tt = _tile(T, 256)
    cur, prev, nxt = _halo_specs(tt, C, T)
    widths = (dxs.shape[1], db_.shape[1], dc_.shape[1])
    gspecs = []
    for wd in widths:
        c_, _, n_ = _halo_specs(tt, wd, T)
        gspecs += [c_, n_]
    nsteps = T // tt

    def dsilu(p):
        s = _sigmoid(p)
        return s * (1.0 + p * (1.0 - s))

    def body(g0, g0n, g1, g1n, g2, g2n, pre_ref, pren_ref, x_ref, xp_ref, w_ref, dx_ref, dw_ref, dbias_ref):
        i = pl.program_id(0)
        gcur = jnp.concatenate([g0[...], g1[...], g2[...]], axis=1)
        gnxt = jnp.concatenate([g0n[...], g1n[...], g2n[...]], axis=1)
        dpre = gcur * dsilu(pre_ref[...])
        dpren = jnp.where(i < nsteps - 1, gnxt * dsilu(pren_ref[...]), 0.0)
        xv = x_ref[...]
        xp = jnp.where(i > 0, xp_ref[...], 0.0)
        dx = jnp.zeros((tt, C), F32)
        rows = []
        for kk in range(K):
            dx = dx + w_ref[kk:kk + 1, :] * _shift_up(dpre, dpren, K - 1 - kk)
            rows.append(jnp.sum(dpre * _shift_down(xv, xp, K - 1 - kk), axis=0, keepdims=True))
        rows.append(jnp.zeros((HALO - K, C), F32))
        dx_ref[...] = dx.astype(dx_ref.dtype)

        @pl.when(i == 0)
        def _():
            dw_ref[...] = jnp.zeros_like(dw_ref)
            dbias_ref[...] = jnp.zeros_like(dbias_ref)

        dw_ref[...] += jnp.concatenate(rows, axis=0)
        dbias_ref[...] += jnp.sum(dpre, axis=0, keepdims=True)

    return _pallas(body, grid=(nsteps,),
                   in_specs=gspecs + [cur, nxt, cur, prev, BS((K, C), lambda i: (0, 0))],
                   out_specs=[BS((tt, C), lambda i: (i, 0)), BS((HALO, C), lambda i: (0, 0)), BS((1, C), lambda i: (0, 0))],
                   out_shape=[SDS((T, C), CDT), SDS((HALO, C), F32), SDS((1, C), F32)],
                   compiler_params=_params("arbitrary"), name=name)(dxs, dxs, db_, db_, dc_, dc_, pre, pre, xpre, xpre, w)


def ssd_prep_fwd(dtpre, bias, alog, *, name):
    T = dtpre.shape[0]
    nc = T // CHUNK

    def body(p_ref, b_ref, a_ref, dt_ref, ac_ref, dtT_ref, acT_ref):
        xv = p_ref[...] + b_ref[...]
        dt = jnp.maximum(xv, 0.0) + _softplus_neg_abs(xv)
        row, col = _sb_tri()
        lower = (col <= row).astype(F32)
        ac = jnp.dot(lower, dt * -jnp.exp(a_ref[...]), precision=HI, preferred_element_type=F32)
        dt_ref[...] = dt
        ac_ref[...] = ac
        dtT_ref[...] = dt.T
        acT_ref[...] = ac.T

    blk = BS((CHUNK, LANES), lambda c: (c, 0))
    blkT = BS((LANES, CHUNK), lambda c: (0, c))
    vec = BS((1, LANES), lambda c: (0, 0))
    return _pallas(body, grid=(nc,), in_specs=[blk, vec, vec], out_specs=[blk, blk, blkT, blkT],
                   out_shape=[SDS((T, LANES), F32)] * 2 + [SDS((LANES, T), F32)] * 2,
                   compiler_params=_params("parallel"), name=name)(dtpre, bias, alog)


def ssd_prep_bwd(dac_c, ddt_c, dac_r, ddt_r, dtpre, bias, alog, *, name):
    T = dtpre.shape[0]
    nc = T // CHUNK

    def body(dac_ref, ddt_ref, dacr_ref, ddtr_ref, p_ref, b_ref, a_ref, o_ref, da_ref, db_ref):
        c = pl.program_id(0)
        xv = p_ref[...] + b_ref[...]
        dt = jnp.maximum(xv, 0.0) + _softplus_neg_abs(xv)
        av = -jnp.exp(a_ref[...])
        row, col = _sb_tri()
        after = (col >= row).astype(F32)
        dda = jnp.dot(after, dac_ref[...] + dacr_ref[...].T, precision=HI, preferred_element_type=F32)
        dpre = (ddt_ref[...] + ddtr_ref[...].T + av * dda) * _sigmoid(xv)
        o_ref[...] = dpre

        @pl.when(c == 0)
        def _():
            da_ref[...] = jnp.zeros_like(da_ref)
            db_ref[...] = jnp.zeros_like(db_ref)

        da_ref[...] += jnp.sum(dt * dda, axis=0, keepdims=True) * av
        db_ref[...] += jnp.sum(dpre, axis=0, keepdims=True)

    blk = BS((CHUNK, LANES), lambda c: (c, 0))
    blkT = BS((LANES, CHUNK), lambda c: (0, c))
    vec = BS((1, LANES), lambda c: (0, 0))
    return _pallas(body, grid=(nc,), in_specs=[blk, blk, blkT, blkT, blk, vec, vec], out_specs=[blk, vec, vec],
                   out_shape=[SDS((T, LANES), F32), SDS((1, LANES), F32), SDS((1, LANES), F32)],
                   compiler_params=_params("arbitrary"), name=name)(dac_c, ddt_c, dac_r, ddt_r, dtpre, bias, alog)


def _ssd_specs(cfg, T, rev):
    nc = T // CHUNK
    GW = SSD_HPG * HEAD_DIM
    bo = cfg.DI // LANES
    co = (cfg.DI + cfg.G * LANES) // LANES
    ci = (lambda c: nc - 1 - c) if rev else (lambda c: c)
    return dict(
        x=BS((CHUNK, GW), lambda g, c: (ci(c), g)),
        b=BS((CHUNK, LANES), lambda g, c: (ci(c), bo + g)),
        c=BS((CHUNK, LANES), lambda g, c: (ci(c), co + g)),
        col=BS((CHUNK, LANES), lambda g, c: (ci(c), 0)),
        row=BS((LANES, CHUNK), lambda g, c: (0, ci(c))),
        vec=BS((1, LANES), lambda g, c: (0, 0)),
        ng=BS((1, GW), lambda g, c: (0, g)),
        hp=BS((1, 1, SSD_HPG, HEAD_DIM, LANES), lambda g, c: (g, ci(c), 0, 0, 0)),
    )


def ssd_chunk_fwd(xbc, z, dt, ac, dtT, acT, dskip, ng, cfg, *, name):
    T = xbc.shape[0]
    nc = T // CHUNK
    G, DI = cfg.G, cfg.DI
    GW = SSD_HPG * HEAD_DIM
    sp = _ssd_specs(cfg, T, False)

    def body(x_ref, b_ref, c_ref, z_ref, dt_ref, ac_ref, dtT_ref, acT_ref, d_ref, ng_ref,
             y_ref, yn_ref, hp_ref, h_scr):
        g = pl.program_id(0)
        c = pl.program_id(1)

        @pl.when(c == 0)
        def _():
            h_scr[...] = jnp.zeros_like(h_scr)

        row, col = _sb_tri()
        causal = row >= col
        Bm = b_ref[...].astype(CDT)
        Cm = c_ref[...].astype(CDT)
        Gm = _nt(Cm, Bm)
        dtb, acb, dtTb, acTb, dv = dt_ref[...], ac_ref[...], dtT_ref[...], acT_ref[...], d_ref[...]
        for r in range(SSD_HPG):
            hh = g * SSD_HPG + r
            ac_c, ac_r = _colsel(acb, hh), _rowsel(acTb, hh)
            dt_c, dt_r = _colsel(dtb, hh), _rowsel(dtTb, hh)
            S = jnp.where(causal, jnp.exp(ac_c - ac_r), 0.0)
            Wm = Gm * S * dt_r
            Xr = x_ref[:, r * HEAD_DIM:(r + 1) * HEAD_DIM]
            Hp = h_scr[r]
            Q = _nt(Cm, Hp.astype(CDT))
            Y = _nn(Wm.astype(CDT), Xr.astype(CDT)) + jnp.exp(ac_c) * Q + _colsel(dv, hh) * Xr
            aL = _rowsel(ac_c, CHUNK - 1)
            e = jnp.exp(aL - ac_c) * dt_c
            hp_ref[0, 0, r] = Hp
            h_scr[r] = jnp.exp(aL) * Hp + _tn((Xr * e).astype(CDT), Bm)
            y_ref[:, r * HEAD_DIM:(r + 1) * HEAD_DIM] = Y
        zz = z_ref[...]
        yz = y_ref[...] * (zz * _sigmoid(zz))
        rstd = lax.rsqrt(jnp.mean(yz * yz, axis=-1, keepdims=True) + RMS_EPS)
        yn_ref[...] = (yz * rstd * ng_ref[...]).astype(yn_ref.dtype)

    return _pallas(body, grid=(G, nc),
                   in_specs=[sp["x"], sp["b"], sp["c"], sp["x"], sp["col"], sp["col"], sp["row"], sp["row"], sp["vec"], sp["ng"]],
                   out_specs=[sp["x"], sp["x"], sp["hp"]],
                   out_shape=[SDS((T, DI), F32), SDS((T, DI), CDT), SDS((G, nc, SSD_HPG, HEAD_DIM, LANES), F32)],
                   scratch_shapes=[pltpu.VMEM((SSD_HPG, HEAD_DIM, LANES), F32)],
                   compiler_params=_params("parallel", "arbitrary"), name=name)(xbc, xbc, xbc, z, dt, ac, dtT, acT, dskip, ng)


def ssd_chunk_bwd(dyn, xbc, z, y, hp, dt, ac, dtT, acT, dskip, ng, cfg, *, name):
    T = xbc.shape[0]
    nc = T // CHUNK
    G, DI = cfg.G, cfg.DI
    GW = SSD_HPG * HEAD_DIM
    sp = _ssd_specs(cfg, T, True)

    def body(dyn_ref, x_ref, b_ref, c_ref, z_ref, y_ref, hp_ref, dt_ref, ac_ref, dtT_ref, acT_ref, d_ref, ng_ref,
             dx_ref, db_ref, dc_ref, dz_ref, colp_ref, rowp_ref, dd_ref, dng_ref, dh_scr):
        g = pl.program_id(0)
        c = pl.program_id(1)

        @pl.when(c == 0)
        def _():
            dh_scr[...] = jnp.zeros_like(dh_scr)
            dd_ref[...] = jnp.zeros_like(dd_ref)
            dng_ref[...] = jnp.zeros_like(dng_ref)

        zz = z_ref[...]
        sg = _sigmoid(zz)
        gate = zz * sg
        yv = y_ref[...]
        yz = yv * gate
        rstd = lax.rsqrt(jnp.mean(yz * yz, axis=-1, keepdims=True) + RMS_EPS)
        nhat = yz * rstd
        dynv = dyn_ref[...].astype(F32)
        dng_ref[...] += jnp.sum(dynv * nhat, axis=0, keepdims=True)
        dnh = dynv * ng_ref[...]
        dyz = rstd * (dnh - nhat * jnp.mean(dnh * nhat, axis=-1, keepdims=True))
        dz_ref[...] = (dyz * yv * (sg * (1.0 + zz * (1.0 - sg)))).astype(dz_ref.dtype)
        dy_all = dyz * gate

        row, col = _sb_tri()
        causal = row >= col
        lane1 = lax.broadcasted_iota(jnp.int32, (1, LANES), 1)
        sub8 = lax.broadcasted_iota(jnp.int32, (HALO, CHUNK), 0)
        subc = lax.broadcasted_iota(jnp.int32, (CHUNK, 1), 0)
        Bf = b_ref[...]
        Cf = c_ref[...]
        Bm = Bf.astype(CDT)
        Cm = Cf.astype(CDT)
        Gm = _nt(Cm, Bm)
        dtb, acb, dtTb, acTb, dv = dt_ref[...], ac_ref[...], dtT_ref[...], acT_ref[...], d_ref[...]
        dG = jnp.zeros((CHUNK, CHUNK), F32)
        dBm = jnp.zeros((CHUNK, LANES), F32)
        dCm = jnp.zeros((CHUNK, LANES), F32)
        colp = jnp.zeros((CHUNK, LANES), F32)
        rowp = jnp.zeros((HALO, CHUNK), F32)
        ddv = jnp.zeros((1, LANES), F32)
        for r in range(SSD_HPG):
            hh = g * SSD_HPG + r
            ac_c, ac_r = _colsel(acb, hh), _rowsel(acTb, hh)
            dt_c, dt_r = _colsel(dtb, hh), _rowsel(dtTb, hh)
            S = jnp.where(causal, jnp.exp(ac_c - ac_r), 0.0)
            GS = Gm * S
            Wm = GS * dt_r
            Xr = x_ref[:, r * HEAD_DIM:(r + 1) * HEAD_DIM]
            Xc = Xr.astype(CDT)
            Hp = hp_ref[0, 0, r]
            Hc = Hp.astype(CDT)
            dY = dy_all[:, r * HEAD_DIM:(r + 1) * HEAD_DIM]
            dYc = dY.astype(CDT)
            dsk = _colsel(dv, hh)
            ddv = ddv + jnp.where(lane1 == r, jnp.sum(jnp.sum(dY * Xr, axis=1, keepdims=True), axis=0, keepdims=True), 0.0)
            dX = dY * dsk + _tn(Wm.astype(CDT), dYc)
            dWm = _nt(dYc, Xc)
            dG = dG + dWm * S * dt_r
            Mm = dWm * Wm
            Nm = dWm * GS
            dac_c = jnp.sum(Mm, axis=1, keepdims=True)
            dac_r = -jnp.sum(Mm, axis=0, keepdims=True)
            ddt_r = jnp.sum(Nm, axis=0, keepdims=True)
            E = jnp.exp(ac_c)
            Q = _nt(Cm, Hc)
            dQ = (dY * E).astype(CDT)
            dac_c = dac_c + jnp.sum(dY * Q, axis=1, keepdims=True) * E
            dCm = dCm + _nn(dQ, Hc)
            dHp = _tn(dQ, Cm)
            dHn = dh_scr[r]
            dHc = dHn.astype(CDT)
            aL = _rowsel(ac_c, CHUNK - 1)
            eaL = jnp.exp(aL)
            ex = jnp.exp(aL - ac_c)
            e = ex * dt_c
            dHp = dHp + eaL * dHn
            daL = eaL * jnp.sum(jnp.sum(dHn * Hp, axis=1, keepdims=True), axis=0, keepdims=True)
            T2 = _nt(Bm, dHc)
            dX = dX + T2 * e
            de = jnp.sum(Xr * T2, axis=1, keepdims=True)
            dBm = dBm + _nn((Xr * e).astype(CDT), dHc)
            dee = de * e
            daL = daL + jnp.sum(dee, axis=0, keepdims=True)
            dac_c = dac_c - dee + jnp.where(subc == CHUNK - 1, daL, 0.0)
            ddt_c = de * ex
            dh_scr[r] = dHp
            dx_ref[:, r * HEAD_DIM:(r + 1) * HEAD_DIM] = dX
            colp = jnp.where(col == r, dac_c, colp)
            colp = jnp.where(col == SSD_HPG + r, ddt_c, colp)
            rowp = jnp.where(sub8 == r, dac_r, rowp)
            rowp = jnp.where(sub8 == SSD_HPG + r, ddt_r, rowp)
        dGc = dG.astype(CDT)
        db_ref[...] = dBm + _tn(dGc, Cm)
        dc_ref[...] = dCm + _nn(dGc, Bm)
        colp_ref[0] = colp
        rowp_ref[0] = rowp
        dd_ref[0] += ddv

    f32 = lambda shape: SDS(shape, F32)
    return _pallas(body, grid=(G, nc),
                   in_specs=[sp["x"], sp["x"], sp["b"], sp["c"], sp["x"], sp["x"], sp["hp"], sp["col"], sp["col"],
                             sp["row"], sp["row"], sp["vec"], sp["ng"]],
                   out_specs=[sp["x"],
                              BS((CHUNK, LANES), lambda g, c: (nc - 1 - c, g)),
                              BS((CHUNK, LANES), lambda g, c: (nc - 1 - c, g)),
                              sp["x"],
                              BS((1, CHUNK, LANES), lambda g, c: (g, nc - 1 - c, 0)),
                              BS((1, HALO, CHUNK), lambda g, c: (g, 0, nc - 1 - c)),
                              BS((1, 1, LANES), lambda g, c: (g, 0, 0)),
                              sp["ng"]],
                   out_shape=[f32((T, DI)), f32((T, G * LANES)), f32((T, G * LANES)), SDS((T, DI), CDT),
                              f32((G, T, LANES)), f32((G, HALO, T)), f32((G, 1, LANES)), f32((1, DI))],
                   scratch_shapes=[pltpu.VMEM((SSD_HPG, HEAD_DIM, LANES), F32)],
                   compiler_params=_params("parallel", "arbitrary"), name=name)(
                       dyn, xbc, xbc, xbc, z, y, hp, dt, ac, dtT, acT, dskip, ng)


HBM = pl.BlockSpec(memory_space=pltpu.HBM)
VMEM = pl.BlockSpec(memory_space=pltpu.VMEM)


def _me():
    return lax.axis_index("x"), lax.axis_index("y"), lax.axis_index("c")


def _other_chips(x, y):
    return [(1 - x, y), (x, 1 - y), (1 - x, 1 - y)]


def _rcopy(src, dst, send_sem, recv_sem, to):
    return pltpu.make_async_remote_copy(src_ref=src, dst_ref=dst, send_sem=send_sem, recv_sem=recv_sem,
                                        device_id=to, device_id_type=MESH)


def allgather_weights(pack, *, name):
    _, Rh, W = pack.shape

    def body(src, out, send_sems, recv_sems):
        x, y, c = _me()
        k = 2 * x + y
        sibling = (x, y, 1 - c)
        chips = _other_chips(x, y)
        sends = [_rcopy(src.at[c], out.at[k, c], send_sems.at[j], recv_sems.at[j], (cx, cy, c))
                 for j, (cx, cy) in enumerate(chips)]
        for cp in sends:
            cp.start()
        passed = []
        for j, (cx, cy) in enumerate(chips):
            kj = 2 * cx + cy
            landed = out.at[kj, c]
            _rcopy(landed, landed, send_sems.at[j], recv_sems.at[j], (cx, cy, c)).wait_recv()
            fw = _rcopy(landed, landed, send_sems.at[3 + j], recv_sems.at[3 + j], sibling)
            fw.start()
            passed.append(fw)
        for j, (cx, cy) in enumerate(chips):
            theirs = out.at[2 * cx + cy, 1 - c]
            _rcopy(theirs, theirs, send_sems.at[3 + j], recv_sems.at[3 + j], sibling).wait_recv()
        for cp in sends + passed:
            cp.wait_send()

    return _pallas(body, in_specs=[HBM], out_specs=HBM, out_shape=SDS((N_CHIPS, 2, Rh, W), pack.dtype),
                   scratch_shapes=[pltpu.SemaphoreType.DMA((6,)), pltpu.SemaphoreType.DMA((6,))],
                   name=name)(pack)


def sibling_swap_half(p, *, name):
    _, n, Rh, W = p.shape

    def body(src, out, send_sem, recv_sem):
        x, y, c = _me()
        cp = _rcopy(src.at[1 - c], out, send_sem, recv_sem, (x, y, 1 - c))
        cp.start()
        cp.wait()

    return _pallas(body, in_specs=[HBM], out_specs=HBM, out_shape=SDS((n, Rh, W), p.dtype),
                   scratch_shapes=[pltpu.SemaphoreType.DMA, pltpu.SemaphoreType.DMA], name=name)(p)


def chip_scatter(s1, *, name):
    n, Rh, W = s1.shape

    def body(src, out, send_sems, recv_sems):
        x, y, c = _me()
        k = 2 * x + y
        chips = _other_chips(x, y)
        sends = [_rcopy(src.at[2 * cx + cy], out.at[k], send_sems.at[j], recv_sems.at[j], (cx, cy, c))
                 for j, (cx, cy) in enumerate(chips)]
        for cp in sends:
            cp.start()
        for j, (cx, cy) in enumerate(chips):
            slot = out.at[2 * cx + cy]
            _rcopy(slot, slot, send_sems.at[j], recv_sems.at[j], (cx, cy, c)).wait_recv()
        for cp in sends:
            cp.wait_send()

    return _pallas(body, in_specs=[HBM], out_specs=HBM, out_shape=SDS((n, Rh, W), s1.dtype),
                   scratch_shapes=[pltpu.SemaphoreType.DMA((3,)), pltpu.SemaphoreType.DMA((3,))],
                   name=name)(s1)


def sibling_pair(r2, *, name):
    _, Rh, W = r2.shape

    def body(src, out, send_sem, recv_sem):
        x, y, c = _me()
        cp = _rcopy(src.at[c], out.at[c], send_sem, recv_sem, (x, y, 1 - c))
        cp.start()
        theirs = out.at[1 - c]
        _rcopy(theirs, theirs, send_sem, recv_sem, (x, y, 1 - c)).wait_recv()
        cp.wait_send()

    return _pallas(body, in_specs=[HBM], out_specs=HBM, out_shape=SDS((2, Rh, W), r2.dtype),
                   input_output_aliases={0: 0},
                   scratch_shapes=[pltpu.SemaphoreType.DMA, pltpu.SemaphoreType.DMA], name=name)(r2)


def allgather_small(v, *, name):
    R, W = v.shape

    def body(v_ref, out_ref, sum_ref, send_sems, recv_sems):
        x, y, c = _me()
        me = 4 * x + 2 * y + c
        out_ref[me] = v_ref[...]
        peers = []
        for m in range(1, N_DEV):
            px = 1 - x if m & 4 else x
            py = 1 - y if m & 2 else y
            pc = 1 - c if m & 1 else c
            peers.append((px, py, pc))
        sends = [_rcopy(v_ref, out_ref.at[me], send_sems.at[j], recv_sems.at[j], p) for j, p in enumerate(peers)]
        for cp in sends:
            cp.start()
        for j, (px, py, pc) in enumerate(peers):
            slot = out_ref.at[4 * px + 2 * py + pc]
            _rcopy(slot, slot, send_sems.at[j], recv_sems.at[j], (px, py, pc)).wait_recv()
        for cp in sends:
            cp.wait_send()
        acc = out_ref[0]
        for d in range(1, N_DEV):
            acc = acc + out_ref[d]
        sum_ref[...] = acc

    return _pallas(body, in_specs=[VMEM], out_specs=[VMEM, VMEM],
                   out_shape=[SDS((N_DEV, R, W), v.dtype), SDS((R, W), v.dtype)],
                   scratch_shapes=[pltpu.SemaphoreType.DMA((N_DEV - 1,)), pltpu.SemaphoreType.DMA((N_DEV - 1,))],
                   name=name)(v)


def _row_tile(R, cap=512):
    best = None
    for t in range(32, cap + 1, 32):
        if R % t == 0:
            best = t
    assert best is not None, R
    return best


def add_halves(p, recv, c_idx, *, name):
    _, n, Rh, W = p.shape
    tr = _row_tile(Rh)

    def body(c_ref, p_ref, r_ref, o_ref):
        o_ref[...] = (p_ref[0].astype(F32) + r_ref[...].astype(F32)).astype(o_ref.dtype)

    gs = pltpu.PrefetchScalarGridSpec(
        num_scalar_prefetch=1, grid=(n, Rh // tr),
        in_specs=[BS((1, 1, tr, W), lambda s, i, c_ref: (c_ref[0], s, i, 0)), BS((1, tr, W), lambda s, i, c_ref: (s, i, 0))],
        out_specs=BS((1, tr, W), lambda s, i, c_ref: (s, i, 0)))
    return _pallas(body, grid_spec=gs, out_shape=SDS((n, Rh, W), p.dtype),
                   compiler_params=_params("parallel", "parallel"), name=name)(c_idx, p, recv)


def sum_slots(s1, got, kc_idx, *, name):
    n, Rh, W = got.shape
    tr = _row_tile(Rh)

    def body(kc_ref, s_ref, g1, g2, g3, o_ref):
        o_ref[0] = s_ref[0].astype(F32) + g1[0].astype(F32) + g2[0].astype(F32) + g3[0].astype(F32)

    slot = lambda d: BS((1, tr, W), lambda i, kc: ((kc[0] + d) % n, i, 0))
    gs = pltpu.PrefetchScalarGridSpec(
        num_scalar_prefetch=1, grid=(Rh // tr,), in_specs=[slot(0), slot(1), slot(2), slot(3)],
        out_specs=BS((1, tr, W), lambda i, kc: (kc[1], i, 0)))
    return _pallas(body, grid_spec=gs, out_shape=SDS((2, Rh, W), F32),
                   compiler_params=_params("parallel"), name=name)(kc_idx, s1, got, got, got)


BIG = [("ffn1_w_gu", True), ("ffn1_w_down", False), ("ffn2_w_gu", True), ("ffn2_w_down", False),
       ("sb_w_qkv", True), ("sb_w_o", False), ("ssd_w_in", True), ("ssd_w_out", False),
       ("sc_w_in", True), ("sc_w_out", False)]
PACK_ALIGN = 64


def _big_layout(shards):
    out, off = [], 0
    for name, cols in BIG:
        shape = shards[name].shape
        rows = math.prod(shape) // PACK_W
        assert rows * PACK_W == math.prod(shape), (name, shape)
        out.append((name, cols, shape, off, rows))
        off += rows
    return out, -(-off // PACK_ALIGN) * PACK_ALIGN


def _pack_rows(pieces, total, axis):
    used = sum(p.shape[axis] for p in pieces)
    if total > used:
        pad_shape = list(pieces[0].shape)
        pad_shape[axis] = total - used
        pieces = pieces + [jnp.zeros(pad_shape, pieces[0].dtype)]
    return jnp.concatenate(pieces, axis=axis)


def _gather_big(shards, tag):
    layout, total = _big_layout(shards)
    pack = _pack_rows([shards[n].astype(CDT).reshape(-1, PACK_W) for n, *_ in layout], total, 0)
    pack = pack.reshape(2, total // 2, PACK_W)
    full = allgather_weights(pack, name=f"{tag}_allgather")
    k_chip = 2 * lax.axis_index("x") + lax.axis_index("y")
    full = lax.dynamic_update_slice(full, pack[None], (k_chip, 0, 0, 0)).reshape(N_CHIPS, total, PACK_W)
    out = {}
    for name, cols, (L, K, n), off, rows in layout:
        w = full[:, off:off + rows].reshape(N_CHIPS, L, K, n)
        if cols:
            out[name] = w.transpose(1, 2, 0, 3).reshape(L, K, N_CHIPS * n)
        else:
            out[name] = w.transpose(1, 0, 2, 3).reshape(L, N_CHIPS * K, n)
    return out


def _reduce_big(grads, shards, tag):
    layout, total = _big_layout(shards)
    pieces = []
    for name, cols, (L, K, n), off, rows in layout:
        g = grads[name]
        if cols:
            g = g.reshape(L, K, N_CHIPS, n).transpose(2, 0, 1, 3)
        else:
            g = g.reshape(L, N_CHIPS, K, n).transpose(1, 0, 2, 3)
        pieces.append(g.astype(CDT).reshape(N_CHIPS, rows, PACK_W))
    pack = _pack_rows(pieces, total, 1)
    half = total // 2
    p = pack.reshape(N_CHIPS, 2, half, PACK_W).transpose(1, 0, 2, 3)
    c_core = lax.axis_index("c").astype(jnp.int32)
    k_chip = (2 * lax.axis_index("x") + lax.axis_index("y")).astype(jnp.int32)
    recv = sibling_swap_half(p, name=f"{tag}_sibling_swap")
    s1 = add_halves(p, recv, c_core.reshape(1), name=f"{tag}_add_halves")
    got = chip_scatter(s1, name=f"{tag}_chip_scatter")
    r2 = sum_slots(s1, got, jnp.stack([k_chip, c_core]), name=f"{tag}_sum_slots")
    red = sibling_pair(r2, name=f"{tag}_sibling_pair").reshape(total, PACK_W)
    return {name: red[off:off + rows].reshape(shape) for name, cols, shape, off, rows in layout}


def _pad_lanes(v, width=LANES):
    return jnp.pad(v, ((0, 0), (0, width - v.shape[1])))


def _small_pack(vecs):
    flat, meta, off = [], [], 0
    for v in vecs:
        flat.append(v.reshape(-1))
        meta.append((off, v.size, v.shape))
        off += v.size
    total = -(-off // (8 * LANES)) * 8 * LANES
    flat.append(jnp.zeros((total - off,), F32))
    return jnp.concatenate(flat).reshape(-1, LANES), meta


def _small_unpack(pack, meta):
    flat = pack.reshape(-1)
    return [flat[off:off + size].reshape(shape) for off, size, shape in meta]


def _ffn_fwd(x, g, wgu, wd, tag):
    h = rms_fwd(x, g, name=f"{tag}_norm")
    gu = matmul(h, wgu, out_dtype=CDT, name=f"{tag}_gu")
    a = swiglu_fwd(gu, name=f"{tag}_act")
    return matmul(a, wd, res=x, scale=0.5, name=f"{tag}_down"), (x, h, gu, a)


def _ffn_bwd(dout, saved, g, wgu, wd, tag):
    x, h, gu, a = saved
    dwd = matmul(a, dout, ta=True, scale=0.5, name=f"{tag}_dwd")
    da = matmul(dout, wd, tb=True, scale=0.5, name=f"{tag}_da")
    dgu = swiglu_bwd(da, gu, name=f"{tag}_dact")
    dwgu = matmul(h, dgu, ta=True, name=f"{tag}_dwgu")
    dh = matmul(dgu, wgu, tb=True, name=f"{tag}_dh")
    dx, dg = rms_bwd(dh, x, g, dout, name=f"{tag}_dnorm")
    return dx, dg, dwgu, dwd


def _sb_mixer_fwd(x, g, wqkv, wo, cfg, tag):
    T, D = x.shape
    H = cfg.sb_heads
    h = rms_fwd(x, g, name=f"{tag}_norm")
    qkv = matmul(h, wqkv, out_dtype=CDT, name=f"{tag}_qkv")
    qkv = qkv.reshape(T, 3, H, HEAD_DIM).transpose(1, 2, 0, 3)
    o, tab = sb_fwd(qkv[0], qkv[1], qkv[2], name=f"{tag}_attn")
    o2 = o.transpose(1, 0, 2).reshape(T, D)
    return matmul(o2, wo, res=x, name=f"{tag}_out"), (x, h, qkv, tab, o2)


def _sb_mixer_bwd(dout, saved, g, wqkv, wo, cfg, tag):
    x, h, qkv, tab, o2 = saved
    T, D = x.shape
    H = cfg.sb_heads
    dwo = matmul(o2, dout, ta=True, name=f"{tag}_dwo")
    do = matmul(dout, wo, tb=True, out_dtype=CDT, name=f"{tag}_do")
    do = do.reshape(T, H, HEAD_DIM).transpose(1, 0, 2)
    dq, dk, dv = sb_bwd(qkv[0], qkv[1], qkv[2], do, tab, name=f"{tag}_dattn")
    dqkv = jnp.stack([dq, dk, dv]).transpose(2, 0, 1, 3).reshape(T, 3 * D)
    dwqkv = matmul(h, dqkv, ta=True, name=f"{tag}_dwqkv")
    dh = matmul(dqkv, wqkv, tb=True, name=f"{tag}_dh")
    dx, dg = rms_bwd(dh, x, g, dout, name=f"{tag}_dnorm")
    return dx, dg, dwqkv, dwo


def _sc_mixer_fwd(x, g, win, convw, wout, tag):
    h = rms_fwd(x, g, name=f"{tag}_norm")
    proj = matmul(h, win, name=f"{tag}_in")
    m = sc_fwd(proj, convw, name=f"{tag}_conv")
    return matmul(m, wout, res=x, name=f"{tag}_out"), (x, h, proj, m)


def _sc_mixer_bwd(dout, saved, g, win, convw, wout, tag):
    x, h, proj, m = saved
    dwout = matmul(m, dout, ta=True, name=f"{tag}_dwout")
    dm = matmul(dout, wout, tb=True, name=f"{tag}_dm")
    dproj, dconvw = sc_bwd(dm, proj, convw, name=f"{tag}_dconv")
    dwin = matmul(h, dproj, ta=True, name=f"{tag}_dwin")
    dh = matmul(dproj, win, tb=True, name=f"{tag}_dh")
    dx, dg = rms_bwd(dh, x, g, dout, name=f"{tag}_dnorm")
    return dx, dg, dwin, dconvw[:convw.shape[0]], dwout


def _ssd_split_w(win, cfg):
    DI, CV = cfg.DI, cfg.conv_dim
    return win[:, :DI], win[:, DI:DI + CV], _pad_lanes(win[:, DI + CV:])


def _ssd_mixer_fwd(x, g, win, convw, convb, dtb, alog, dskip, ng, wout, cfg, tag):
    wz, wx, wdt = _ssd_split_w(win, cfg)
    h = rms_fwd(x, g, name=f"{tag}_norm")
    z = matmul(h, wz, name=f"{tag}_z")
    xpre = matmul(h, wx, name=f"{tag}_xbc")
    dtpre = matmul(h, wdt, name=f"{tag}_dt")
    pre, xbc = ssd_conv_fwd(xpre, convw, convb, name=f"{tag}_conv")
    dt, ac, dtT, acT = ssd_prep_fwd(dtpre, dtb, alog, name=f"{tag}_prep")
    y, yn, hp = ssd_chunk_fwd(xbc, z, dt, ac, dtT, acT, dskip, ng, cfg, name=f"{tag}_scan")
    out = matmul(yn, wout, res=x, name=f"{tag}_out")
    return out, (x, h, z, xpre, dtpre, pre, xbc, dt, ac, dtT, acT, y, yn, hp)


def _ssd_mixer_bwd(dout, saved, g, win, convw, convb, dtb, alog, dskip, ng, wout, cfg, tag):
    x, h, z, xpre, dtpre, pre, xbc, dt, ac, dtT, acT, y, yn, hp = saved
    T = x.shape[0]
    NH, G = cfg.NH, cfg.G
    wz, wx, wdt = _ssd_split_w(win, cfg)
    dwout = matmul(yn, dout, ta=True, name=f"{tag}_dwout")
    dyn = matmul(dout, wout, tb=True, name=f"{tag}_dyn")
    dxs, db, dc, dz, colp, rowp, dd, dng = ssd_chunk_bwd(dyn, xbc, z, y, hp, dt, ac, dtT, acT, dskip, ng, cfg,
                                                        name=f"{tag}_dscan")
    col = lambda lo: _pad_lanes(colp[:, :, lo:lo + SSD_HPG].transpose(1, 0, 2).reshape(T, NH))
    row = lambda lo: jnp.pad(rowp[:, lo:lo + SSD_HPG, :].reshape(NH, T), ((0, LANES - NH), (0, 0)))
    ddtpre, dalog, ddtb = ssd_prep_bwd(col(0), col(SSD_HPG), row(0), row(SSD_HPG), dtpre, dtb, alog, name=f"{tag}_dprep")
    dxpre, dconvw, dconvb = ssd_conv_bwd(dxs, db, dc, pre, xpre, convw, name=f"{tag}_dconv")
    dwz = matmul(h, dz, ta=True, name=f"{tag}_dwz")
    dwx = matmul(h, dxpre, ta=True, name=f"{tag}_dwx")
    dwdt = matmul(h, ddtpre, ta=True, name=f"{tag}_dwdt")
    dh = matmul(dz, wz, tb=True, name=f"{tag}_dh_z")
    dh = matmul(dxpre, wx, tb=True, res=dh, name=f"{tag}_dh_x")
    dh = matmul(ddtpre, wdt, tb=True, res=dh, name=f"{tag}_dh_dt")
    dx, dg = rms_bwd(dh, x, g, dout, name=f"{tag}_dnorm")
    dwin = jnp.concatenate([dwz, dwx, dwdt[:, :NH]], axis=1)
    small = dict(conv_w=dconvw[:convw.shape[0]], conv_b=dconvb, dt_bias=ddtb[:, :NH], a_log=dalog[:, :NH],
                 d=dd[:, 0, :SSD_HPG].reshape(1, NH), norm=dng)
    return dx, dg, dwin, dwout, small


INPUT_NAMES = ['x', 'ffn1_norm', 'ffn1_w_gu', 'ffn1_w_down', 'mix_norm', 'ffn2_norm', 'ffn2_w_gu', 'ffn2_w_down',
               'sb_w_qkv', 'sb_w_o', 'ssd_w_in', 'ssd_conv_w', 'ssd_conv_b', 'ssd_dt_bias', 'ssd_a_log', 'ssd_d',
               'ssd_norm', 'ssd_w_out', 'sc_w_in', 'sc_conv_w', 'sc_w_out', 'final_norm']
WEIGHT_NAMES = INPUT_NAMES[1:]
SHARDED_SMALL = ("ssd_conv_w", "sc_conv_w")


def train_step(a, cfg):
    x = a["x"][0]
    tgt = a["loss_target"][0]
    big_names = [n for n, _ in BIG]
    k_chip = 2 * lax.axis_index("x") + lax.axis_index("y")

    w = _gather_big({n: a[n] for n in big_names}, "w")
    conv_pack, conv_meta = _small_pack([a[n] for n in SHARDED_SMALL])
    conv_all, _ = allgather_small(conv_pack, name="conv_w_allgather")
    conv_full = {}
    for i, n in enumerate(SHARDED_SMALL):
        per_chip = [_small_unpack(conv_all[2 * kk], conv_meta)[i] for kk in range(N_CHIPS)]
        conv_full[n] = jnp.concatenate(per_chip, axis=-1)

    def vec(name, i):
        return a[name][i:i + 1]

    ssd_small = lambda j: (a["ssd_conv_b"][j:j + 1], _pad_lanes(a["ssd_dt_bias"][j:j + 1]),
                           _pad_lanes(a["ssd_a_log"][j:j + 1]), _pad_lanes(a["ssd_d"][j:j + 1]), a["ssd_norm"][j:j + 1])

    saved = []
    for i in range(cfg.depth):
        kind, j = i % 3, i // 3
        x, s1 = _ffn_fwd(x, vec("ffn1_norm", i), w["ffn1_w_gu"][i], w["ffn1_w_down"][i], f"l{i}_ffn1")
        if kind == 0:
            x, s2 = _sb_mixer_fwd(x, vec("mix_norm", i), w["sb_w_qkv"][j], w["sb_w_o"][j], cfg, f"l{i}_sb")
        elif kind == 1:
            x, s2 = _ssd_mixer_fwd(x, vec("mix_norm", i), w["ssd_w_in"][j], conv_full["ssd_conv_w"][j], *ssd_small(j),
                                   w["ssd_w_out"][j], cfg, f"l{i}_ssd")
        else:
            x, s2 = _sc_mixer_fwd(x, vec("mix_norm", i), w["sc_w_in"][j], conv_full["sc_conv_w"][j], w["sc_w_out"][j],
                                  f"l{i}_sc")
        x, s3 = _ffn_fwd(x, vec("ffn2_norm", i), w["ffn2_w_gu"][i], w["ffn2_w_down"][i], f"l{i}_ffn2")
        saved.append((s1, s2, s3))
    loss_part, dx, d_final = loss_head(x, a["final_norm"].reshape(1, -1), tgt, name="loss_head")

    gbig = {n: [None] * a[n].shape[0] for n in big_names}
    gsmall = {n: [None] * a[n].shape[0] for n in ("ffn1_norm", "mix_norm", "ffn2_norm", "ssd_conv_w", "ssd_conv_b",
                                                   "ssd_dt_bias", "ssd_a_log", "ssd_d", "ssd_norm", "sc_conv_w")}
    for i in reversed(range(cfg.depth)):
        kind, j = i % 3, i // 3
        s1, s2, s3 = saved[i]
        dx, gsmall["ffn2_norm"][i], gbig["ffn2_w_gu"][i], gbig["ffn2_w_down"][i] = _ffn_bwd(
            dx, s3, vec("ffn2_norm", i), w["ffn2_w_gu"][i], w["ffn2_w_down"][i], f"l{i}_ffn2")
        if kind == 0:
            dx, gsmall["mix_norm"][i], gbig["sb_w_qkv"][j], gbig["sb_w_o"][j] = _sb_mixer_bwd(
                dx, s2, vec("mix_norm", i), w["sb_w_qkv"][j], w["sb_w_o"][j], cfg, f"l{i}_sb")
        elif kind == 1:
            dx, gsmall["mix_norm"][i], gbig["ssd_w_in"][j], gbig["ssd_w_out"][j], sm = _ssd_mixer_bwd(
                dx, s2, vec("mix_norm", i), w["ssd_w_in"][j], conv_full["ssd_conv_w"][j], *ssd_small(j),
                w["ssd_w_out"][j], cfg, f"l{i}_ssd")
            for key, val in sm.items():
                gsmall["ssd_" + key][j] = val
        else:
            dx, gsmall["mix_norm"][i], gbig["sc_w_in"][j], gsmall["sc_conv_w"][j], gbig["sc_w_out"][j] = _sc_mixer_bwd(
                dx, s2, vec("mix_norm", i), w["sc_w_in"][j], conv_full["sc_conv_w"][j], w["sc_w_out"][j], f"l{i}_sc")
        dx, gsmall["ffn1_norm"][i], gbig["ffn1_w_gu"][i], gbig["ffn1_w_down"][i] = _ffn_bwd(
            dx, s1, vec("ffn1_norm", i), w["ffn1_w_gu"][i], w["ffn1_w_down"][i], f"l{i}_ffn1")
    grad_x = dx[None]

    red_big = _reduce_big({n: jnp.stack(v) for n, v in gbig.items()}, {n: a[n] for n in big_names}, "g")
    small_names = list(gsmall) + ["final_norm"]
    small_vals = [jnp.concatenate([p.reshape((1,) + a[n].shape[1:-1] + (-1,)) for p in gsmall[n]], axis=0)
                  for n in gsmall] + [d_final.reshape(-1)]
    gpack, gmeta = _small_pack(small_vals + [loss_part[:, :1]])
    _, gsum = allgather_small(gpack, name="small_allreduce")
    *small_red, loss = _small_unpack(gsum, gmeta)
    red = dict(red_big)
    for n, v in zip(small_names, small_red):
        if n in SHARDED_SMALL:
            width = a[n].shape[-1]
            v = lax.dynamic_slice_in_dim(v, k_chip * width, width, axis=v.ndim - 1)
        red[n] = v.reshape(a[n].shape)

    delta, new_m, new_v = {}, {}, {}
    two_d = lambda t: t.reshape(-1, t.shape[-1])
    for n in big_names:
        d_, m_, v_ = adamw(two_d(a[n]), two_d(red[n]), two_d(a["m_" + n]), two_d(a["v_" + n]), name=f"adamw_{n}")
        delta[n], new_m[n], new_v[n] = (t.reshape(a[n].shape) for t in (d_, m_, v_))
    rest = [n for n in WEIGHT_NAMES if n not in big_names]
    packs = [_small_pack([src[n] for n in rest]) for src in
             (a, red, {n: a["m_" + n] for n in rest}, {n: a["v_" + n] for n in rest})]
    outs = adamw(*[p for p, _ in packs], name="adamw_small")
    for dst, o in zip((delta, new_m, new_v), outs):
        for n, t in zip(rest, _small_unpack(o, packs[0][1])):
            dst[n] = t
    return (loss.reshape(()), grad_x, *[red[n] for n in WEIGHT_NAMES], *[delta[n] for n in WEIGHT_NAMES],
            *[new_m[n] for n in WEIGHT_NAMES], *[new_v[n] for n in WEIGHT_NAMES])


def kernel(x, ffn1_norm, ffn1_w_gu, ffn1_w_down, mix_norm, ffn2_norm, ffn2_w_gu, ffn2_w_down, sb_w_qkv, sb_w_o, ssd_w_in, ssd_conv_w, ssd_conv_b, ssd_dt_bias, ssd_a_log, ssd_d, ssd_norm, ssd_w_out, sc_w_in, sc_conv_w, sc_w_out, final_norm, loss_target, m_ffn1_norm, m_ffn1_w_gu, m_ffn1_w_down, m_mix_norm, m_ffn2_norm, m_ffn2_w_gu, m_ffn2_w_down, m_sb_w_qkv, m_sb_w_o, m_ssd_w_in, m_ssd_conv_w, m_ssd_conv_b, m_ssd_dt_bias, m_ssd_a_log, m_ssd_d, m_ssd_norm, m_ssd_w_out, m_sc_w_in, m_sc_conv_w, m_sc_w_out, m_final_norm, v_ffn1_norm, v_ffn1_w_gu, v_ffn1_w_down, v_mix_norm, v_ffn2_norm, v_ffn2_w_gu, v_ffn2_w_down, v_sb_w_qkv, v_sb_w_o, v_ssd_w_in, v_ssd_conv_w, v_ssd_conv_b, v_ssd_dt_bias, v_ssd_a_log, v_ssd_d, v_ssd_norm, v_ssd_w_out, v_sc_w_in, v_sc_conv_w, v_sc_w_out, v_final_norm):
    args = dict(locals())
    return train_step(args, Cfg())
```

```python
import functools
import math

import jax
import jax.numpy as jnp
from jax import lax
from jax.experimental import pallas as pl
from jax.experimental.pallas import tpu as pltpu

F32 = jnp.float32
CDT = jnp.bfloat16
HI = lax.Precision.HIGHEST
BS = pl.BlockSpec
SDS = jax.ShapeDtypeStruct
MESH = pl.DeviceIdType.MESH

RMS_EPS = 1e-6
ADAM_LR, ADAM_B1, ADAM_B2, ADAM_EPS, ADAM_WD, ADAM_STEP = 0.001, 0.9, 0.999, 1e-08, 0.01, 10
V7X_VMEM_LIMIT_BYTES = 56 * 1024 * 1024
LANES = 128
CHUNK = 128
HEAD_DIM = 64
SSD_HPG = 4
N_CHIPS = 4
N_DEV = 8
PACK_W = 1024


class Cfg:
    def __init__(self, d_model=1024, d_ff=2816, depth=4, sb_heads=16, ssd_groups=8, ssd_state=128,
                 ssd_conv=4, sc_width=3):
        self.D, self.F, self.depth = d_model, d_ff, depth
        self.sb_heads = sb_heads
        self.DI = 2 * d_model
        self.G = ssd_groups
        self.NH = self.DI // HEAD_DIM
        assert self.NH == SSD_HPG * self.G and ssd_state == LANES and sb_heads * HEAD_DIM == d_model
        self.conv_dim = self.DI + 2 * self.G * ssd_state
        self.in_dim = self.DI + self.conv_dim + self.NH
        self.ssd_conv, self.sc_width = ssd_conv, sc_width
        self.n_sb, self.n_ssd, self.n_sc = (depth + 2) // 3, (depth + 1) // 3, depth // 3


def _pallas(body, **kw):
    return pl.pallas_call(body, **kw)


def _params(*sem):
    return pltpu.CompilerParams(dimension_semantics=sem, vmem_limit_bytes=V7X_VMEM_LIMIT_BYTES)


def _tile(n, cap):
    if n <= cap:
        return n
    best = None
    for m in range(1, n // LANES + 1):
        t = m * LANES
        if n % t == 0 and t <= cap:
            best = t
    assert best is not None, (n, cap)
    return best


def _sigmoid(x):
    return 1.0 / (1.0 + jnp.exp(-x))


def _softplus_neg_abs(x):
    return jnp.log(1.0 + jnp.exp(-jnp.abs(x)))


def _colsel(blk, idx):
    lane = lax.broadcasted_iota(jnp.int32, blk.shape, 1)
    return jnp.sum(jnp.where(lane == idx, blk, 0.0), axis=1, keepdims=True)


def _rowsel(blk, idx):
    sub = lax.broadcasted_iota(jnp.int32, blk.shape, 0)
    return jnp.sum(jnp.where(sub == idx, blk, 0.0), axis=0, keepdims=True)


def _nt(a, b):
    return lax.dot_general(a, b, (((1,), (1,)), ((), ())), preferred_element_type=F32)


def _tn(a, b):
    return lax.dot_general(a, b, (((0,), (0,)), ((), ())), preferred_element_type=F32)


def _nn(a, b):
    return jnp.dot(a, b, preferred_element_type=F32)


def matmul(a, b, *, name, ta=False, tb=False, out_dtype=F32, res=None, scale=1.0, tm=1408, tn=1408, tk=1408):
    M, K = (a.shape[1], a.shape[0]) if ta else a.shape
    N, K2 = b.shape if tb else (b.shape[1], b.shape[0])
    assert K == K2, (a.shape, b.shape, ta, tb)
    tm, tn, tk = _tile(M, tm), _tile(N, tn), _tile(K, tk)
    nk = K // tk
    dn = (((0 if ta else 1,), (1 if tb else 0,)), ((), ()))

    def body(*refs):
        a_ref, b_ref = refs[:2]
        r_ref = refs[2] if res is not None else None
        o_ref = refs[3 if res is not None else 2]
        part = lax.dot_general(a_ref[...].astype(CDT), b_ref[...].astype(CDT), dn, preferred_element_type=F32)

        def finish(total):
            o = total * scale
            if res is not None:
                o = o + r_ref[...].astype(F32)
            o_ref[...] = o.astype(o_ref.dtype)

        if nk == 1:
            finish(part)
            return
        acc = refs[-1]
        k = pl.program_id(2)

        @pl.when(k == 0)
        def _():
            acc[...] = part

        @pl.when(jnp.logical_and(k > 0, k < nk - 1))
        def _():
            acc[...] += part

        @pl.when(k == nk - 1)
        def _():
            finish(acc[...] + part)

    a_spec = BS((tk, tm), lambda i, j, k: (k, i)) if ta else BS((tm, tk), lambda i, j, k: (i, k))
    b_spec = BS((tn, tk), lambda i, j, k: (j, k)) if tb else BS((tk, tn), lambda i, j, k: (k, j))
    o_spec = BS((tm, tn), lambda i, j, k: (i, j))
    ins, specs = [a, b], [a_spec, b_spec]
    if res is not None:
        ins.append(res)
        specs.append(o_spec)
    return _pallas(body, grid=(M // tm, N // tn, nk), in_specs=specs, out_specs=o_spec,
                   out_shape=SDS((M, N), out_dtype),
                   scratch_shapes=[pltpu.VMEM((tm, tn), F32)] if nk > 1 else [],
                   compiler_params=_params("parallel", "parallel", "arbitrary"), name=name)(*ins)


def rms_fwd(x, g, *, name):
    T, D = x.shape
    tm = _tile(T, 512)

    def body(x_ref, g_ref, o_ref):
        xv = x_ref[...]
        r = lax.rsqrt(jnp.mean(xv * xv, axis=-1, keepdims=True) + RMS_EPS)
        o_ref[...] = (xv * r * g_ref[...]).astype(o_ref.dtype)

    return _pallas(body, grid=(T // tm,), in_specs=[BS((tm, D), lambda i: (i, 0)), BS((1, D), lambda i: (0, 0))],
                   out_specs=BS((tm, D), lambda i: (i, 0)), out_shape=SDS((T, D), CDT),
                   compiler_params=_params("parallel"), name=name)(x, g)


def rms_bwd(dh, x, g, dres, *, name):
    T, D = x.shape
    tm = _tile(T, 512)

    def body(dh_ref, x_ref, g_ref, r_ref, dx_ref, dg_ref):
        i = pl.program_id(0)
        xv = x_ref[...]
        r = lax.rsqrt(jnp.mean(xv * xv, axis=-1, keepdims=True) + RMS_EPS)
        xhat = xv * r
        dhv = dh_ref[...].astype(F32)
        dxh = dhv * g_ref[...]
        m = jnp.mean(dxh * xhat, axis=-1, keepdims=True)
        dx_ref[...] = r * (dxh - xhat * m) + r_ref[...]

        @pl.when(i == 0)
        def _():
            dg_ref[...] = jnp.zeros_like(dg_ref)

        dg_ref[...] += jnp.sum(dhv * xhat, axis=0, keepdims=True)

    row = BS((tm, D), lambda i: (i, 0))
    vec = BS((1, D), lambda i: (0, 0))
    return _pallas(body, grid=(T // tm,), in_specs=[row, row, vec, row], out_specs=[row, vec],
                   out_shape=[SDS((T, D), F32), SDS((1, D), F32)],
                   compiler_params=_params("arbitrary"), name=name)(dh, x, g, dres)


def swiglu_fwd(gu, *, name):
    T, F2 = gu.shape
    F = F2 // 2
    tm = _tile(T, 512)

    def body(gu_ref, o_ref):
        g = gu_ref[:, :F].astype(F32)
        u = gu_ref[:, F:].astype(F32)
        o_ref[...] = (g * _sigmoid(g) * u).astype(o_ref.dtype)

    return _pallas(body, grid=(T // tm,), in_specs=[BS((tm, F2), lambda i: (i, 0))],
                   out_specs=BS((tm, F), lambda i: (i, 0)), out_shape=SDS((T, F), CDT),
                   compiler_params=_params("parallel"), name=name)(gu)


def swiglu_bwd(da, gu, *, name):
    T, F2 = gu.shape
    F = F2 // 2
    tm = _tile(T, 512)

    def body(da_ref, gu_ref, o_ref):
        g = gu_ref[:, :F].astype(F32)
        u = gu_ref[:, F:].astype(F32)
        dav = da_ref[...].astype(F32)
        s = _sigmoid(g)
        o_ref[:, :F] = (dav * u * (s * (1.0 + g * (1.0 - s)))).astype(o_ref.dtype)
        o_ref[:, F:] = (dav * (g * s)).astype(o_ref.dtype)

    return _pallas(body, grid=(T // tm,), in_specs=[BS((tm, F), lambda i: (i, 0)), BS((tm, F2), lambda i: (i, 0))],
                   out_specs=BS((tm, F2), lambda i: (i, 0)), out_shape=SDS((T, F2), CDT),
                   compiler_params=_params("parallel"), name=name)(da, gu)


def loss_head(x, g, tgt, *, name):
    T, D = x.shape
    tm = _tile(T, 512)

    def body(x_ref, g_ref, t_ref, l_ref, dx_ref, dg_ref):
        i = pl.program_id(0)
        xv = x_ref[...]
        gv = g_ref[...]
        r = lax.rsqrt(jnp.mean(xv * xv, axis=-1, keepdims=True) + RMS_EPS)
        xhat = xv * r
        err = xhat * gv - t_ref[...]
        part = 0.5 * jnp.sum(jnp.mean(err * err, axis=-1, keepdims=True), axis=0, keepdims=True)
        dy = err * (1.0 / D)
        dxh = dy * gv
        m = jnp.mean(dxh * xhat, axis=-1, keepdims=True)
        dx_ref[...] = r * (dxh - xhat * m)

        @pl.when(i == 0)
        def _():
            dg_ref[...] = jnp.zeros_like(dg_ref)
            l_ref[...] = jnp.zeros_like(l_ref)

        dg_ref[...] += jnp.sum(dy * xhat, axis=0, keepdims=True)
        l_ref[...] += jnp.broadcast_to(part, l_ref.shape)

    row = BS((tm, D), lambda i: (i, 0))
    vec = BS((1, D), lambda i: (0, 0))
    return _pallas(body, grid=(T // tm,), in_specs=[row, vec, row],
                   out_specs=[BS((1, LANES), lambda i: (0, 0)), row, vec],
                   out_shape=[SDS((1, LANES), F32), SDS((T, D), F32), SDS((1, D), F32)],
                   compiler_params=_params("arbitrary"), name=name)(x, g, tgt)


def adamw(w, g, m, v, *, name):
    R, C = w.shape
    tr = R
    if R * C * 4 > (1 << 20):
        for cand in range(8, R + 1, 8):
            if R % cand == 0 and cand * C * 4 <= (1 << 20):
                tr = cand

    def body(w_ref, g_ref, m_ref, v_ref, d_ref, nm_ref, nv_ref):
        gv = g_ref[...]
        nm = ADAM_B1 * m_ref[...] + (1.0 - ADAM_B1) * gv
        nv = ADAM_B2 * v_ref[...] + (1.0 - ADAM_B2) * (gv * gv)
        m_hat = nm / (1.0 - ADAM_B1 ** ADAM_STEP)
        v_hat = nv / (1.0 - ADAM_B2 ** ADAM_STEP)
        d_ref[...] = -ADAM_LR * (m_hat / (jnp.sqrt(v_hat) + ADAM_EPS) + ADAM_WD * w_ref[...])
        nm_ref[...] = nm
        nv_ref[...] = nv

    blk = BS((tr, C), lambda i: (i, 0))
    return _pallas(body, grid=(R // tr,), in_specs=[blk] * 4, out_specs=[blk] * 3,
                   out_shape=[SDS((R, C), F32)] * 3, compiler_params=_params("parallel"), name=name)(w, g, m, v)


def _sb_tri():
    row = lax.broadcasted_iota(jnp.int32, (CHUNK, CHUNK), 0)
    col = lax.broadcasted_iota(jnp.int32, (CHUNK, CHUNK), 1)
    return row, col


def _tri3(keep):
    t = keep.astype(jnp.bfloat16)
    return jnp.concatenate([t, t, t], axis=0)


def _split_dot(x, tri3):
    hi = x.astype(jnp.bfloat16)
    r = x - hi.astype(F32)
    mid = r.astype(jnp.bfloat16)
    lo = (r - mid.astype(F32)).astype(jnp.bfloat16)
    return jnp.dot(jnp.concatenate([hi, mid, lo], axis=1), tri3, preferred_element_type=F32)


def _sb_heads_per_step(H, want):
    hb = min(H, want)
    assert H % hb == 0, (H, hb)
    return hb


SB_FWD_HEADS = 8
SB_BWD_HEADS = 8


def _head_pair_masked(blk, lo):
    zero = jnp.zeros_like(blk)
    return jnp.where(lo, blk, zero), jnp.where(lo, zero, blk)


def sb_fwd(qkv, n_heads, *, name):
    T, D3 = qkv.shape
    D = D3 // 3
    nq = T // CHUNK
    assert nq <= LANES
    scale = HEAD_DIM ** -0.5
    hb = _sb_heads_per_step(n_heads, SB_FWD_HEADS)
    assert hb % 2 == 0
    LW = hb * HEAD_DIM
    ngrp = D // LW
    heads, pairs = range(hb), range(hb // 2)
    lanes = lambda p: slice(p * LANES, (p + 1) * LANES)

    def body(q_ref, k_ref, v_ref, o_ref, tab_ref, acc_scr, cs_scr, qm_scr):
        qi = pl.program_id(1)
        row, col = _sb_tri()
        lo = col < HEAD_DIM
        upper3 = _tri3(row > col)
        acc_scr[...] = jnp.zeros_like(acc_scr)
        cs_scr[...] = jnp.zeros_like(cs_scr)
        tab_ref[...] = jnp.zeros_like(tab_ref)
        for p in pairs:
            qm_scr[2 * p], qm_scr[2 * p + 1] = _head_pair_masked(q_ref[:, lanes(p)] * scale, lo)

        def tile(j, diag):
            sl = pl.ds(pl.multiple_of(j * CHUNK, CHUNK), CHUNK)
            keep = (lambda t: jnp.where(col < row, t, 0.0)) if diag else (lambda t: t)
            kbs = [k_ref[sl, lanes(p)] for p in pairs]
            zs = [_nt(qm_scr[h], kbs[h // 2]) for h in heads]
            lszs = [jnp.minimum(z, 0.0) - _softplus_neg_abs(z) for z in zs]
            lks = [keep(lsz - z) for lsz, z in zip(lszs, zs)]
            tls = [_split_dot(lk, upper3) for lk in lks]
            css = [cs_scr[h] for h in heads]
            atts = [keep(jnp.exp(lsz + tl + cs)).astype(CDT) for lsz, tl, cs in zip(lszs, tls, css)]
            vms = [jnp.concatenate(_head_pair_masked(v_ref[sl, lanes(p)], lo), axis=0) for p in pairs]
            pvs = [_nn(jnp.concatenate([atts[2 * p], atts[2 * p + 1]], axis=1), vms[p]) for p in pairs]
            for p in pairs:
                acc_scr[:, lanes(p)] += pvs[p]
            for h in heads:
                tab_ref[h, 0] = jnp.where(col == j, css[h], tab_ref[h, 0])
                cs_scr[h] = css[h] + (tls[h][:, :1] + lks[h][:, :1])

        tile(qi, True)

        def step(jj, carry):
            tile(qi - jj, False)
            return carry

        lax.fori_loop(1, qi + 1, step, 0)
        o_ref[...] = acc_scr[...].astype(o_ref.dtype)

    once = pl.Buffered(1)
    qs = BS((CHUNK, LW), lambda g, i: (i, g))
    return _pallas(body, grid=(ngrp, nq),
                   in_specs=[qs, BS((T, LW), lambda g, i: (0, ngrp + g), pipeline_mode=once),
                             BS((T, LW), lambda g, i: (0, 2 * ngrp + g), pipeline_mode=once)],
                   out_specs=[qs, BS((hb, 1, CHUNK, CHUNK), lambda g, i: (g, i, 0, 0))],
                   out_shape=[SDS((T, D), CDT), SDS((n_heads, nq, CHUNK, CHUNK), F32)],
                   scratch_shapes=[pltpu.VMEM((CHUNK, LW), F32), pltpu.VMEM((hb, CHUNK, 1), F32),
                                   pltpu.VMEM((hb, CHUNK, LANES), CDT)],
                   compiler_params=_params("parallel", "arbitrary"), name=name)(qkv, qkv, qkv)


def sb_bwd(qkv, do, tab, n_heads, *, name):
    T, D3 = qkv.shape
    D = D3 // 3
    nq = T // CHUNK
    scale = HEAD_DIM ** -0.5
    hb = _sb_heads_per_step(n_heads, SB_BWD_HEADS)
    assert hb % 2 == 0
    LW = hb * HEAD_DIM
    ngrp = D // LW
    heads, pairs = range(hb), range(hb // 2)
    lanes = lambda p: slice(p * LANES, (p + 1) * LANES)

    def body(q_ref, k_ref, v_ref, do_ref, tab_ref, dq_ref, dk_ref, dv_ref,
             dk_acc, dv_acc, dq_scr, pfx_scr, qm_scr, dom_scr):
        qi = pl.program_id(1)

        @pl.when(qi == 0)
        def _():
            dk_acc[...] = jnp.zeros_like(dk_acc)
            dv_acc[...] = jnp.zeros_like(dv_acc)

        dq_scr[...] = jnp.zeros_like(dq_scr)
        pfx_scr[...] = jnp.zeros_like(pfx_scr)
        row, col = _sb_tri()
        lo = col < HEAD_DIM
        upper3 = _tri3(row > col)
        before3 = _tri3(row < col)
        for p in pairs:
            qm_scr[2 * p], qm_scr[2 * p + 1] = _head_pair_masked(q_ref[:, lanes(p)] * scale, lo)
            dom_scr[2 * p], dom_scr[2 * p + 1] = _head_pair_masked(do_ref[:, lanes(p)], lo)

        def tile(j, diag):
            sl = pl.ds(pl.multiple_of(j * CHUNK, CHUNK), CHUNK)
            keep = (lambda t: jnp.where(col < row, t, 0.0)) if diag else (lambda t: t)
            kbs = [k_ref[sl, lanes(p)] for p in pairs]
            vbs = [v_ref[sl, lanes(p)] for p in pairs]
            zs = [_nt(qm_scr[h], kbs[h // 2]) for h in heads]
            das = [_nt(dom_scr[h], vbs[h // 2]) for h in heads]
            lszs = [jnp.minimum(z, 0.0) - _softplus_neg_abs(z) for z in zs]
            lks = [keep(lsz - z) for lsz, z in zip(lszs, zs)]
            tls = [_split_dot(lk, upper3) for lk in lks]
            css = [jnp.sum(jnp.where(col == j, tab_ref[h, 0], 0.0), axis=1, keepdims=True) for h in heads]
            atts = [keep(jnp.exp(lsz + tl + cs)) for lsz, tl, cs in zip(lszs, tls, css)]
            dls = [att * da for att, da in zip(atts, das)]
            pres = [_split_dot(dl, before3) for dl in dls]
            pfxs = [pfx_scr[h] for h in heads]
            sigs = [jnp.exp(lsz) for lsz in lszs]
            dzs = [(dl * (1.0 - sig) - keep(sig * (pre + pfx))).astype(CDT)
                   for dl, sig, pre, pfx in zip(dls, sigs, pres, pfxs)]
            attc = [att.astype(CDT) for att in atts]
            kms = [jnp.concatenate(_head_pair_masked(kb, lo), axis=0) for kb in kbs]
            dqs = [_nn(jnp.concatenate([dzs[2 * p], dzs[2 * p + 1]], axis=1), kms[p]) for p in pairs]
            both = lambda xs, p: jnp.concatenate([xs[2 * p], xs[2 * p + 1]], axis=0)
            dks = [_tn(both(dzs, p), jnp.concatenate([qm_scr[2 * p], qm_scr[2 * p + 1]], axis=0)) for p in pairs]
            dvs = [_tn(both(attc, p), jnp.concatenate([dom_scr[2 * p], dom_scr[2 * p + 1]], axis=0)) for p in pairs]
            last = CHUNK - 1
            for p in pairs:
                dq_scr[:, lanes(p)] += dqs[p]
                dk_acc[sl, lanes(p)] += dks[p]
                dv_acc[sl, lanes(p)] += dvs[p]
            for h in heads:
                pfx_scr[h] = pfxs[h] + (pres[h][:, last:] + dls[h][:, last:])

        def step(j, carry):
            tile(j, False)
            return carry

        lax.fori_loop(0, qi, step, 0)
        tile(qi, True)
        dq_ref[...] = (dq_scr[...] * scale).astype(dq_ref.dtype)

        @pl.when(qi == nq - 1)
        def _():
            dk_ref[...] = dk_acc[...].astype(dk_ref.dtype)
            dv_ref[...] = dv_acc[...].astype(dv_ref.dtype)

    once = pl.Buffered(1)
    qs = BS((CHUNK, LW), lambda g, i: (i, g))
    ks = BS((T, LW), lambda g, i: (0, g))
    return _pallas(body, grid=(ngrp, nq),
                   in_specs=[qs, BS((T, LW), lambda g, i: (0, ngrp + g), pipeline_mode=once),
                             BS((T, LW), lambda g, i: (0, 2 * ngrp + g), pipeline_mode=once), qs,
                             BS((hb, 1, CHUNK, CHUNK), lambda g, i: (g, i, 0, 0))],
                   out_specs=[qs, ks, ks],
                   out_shape=[SDS((T, D), CDT)] * 3,
                   scratch_shapes=[pltpu.VMEM((T, LW), F32), pltpu.VMEM((T, LW), F32),
                                   pltpu.VMEM((CHUNK, LW), F32), pltpu.VMEM((hb, CHUNK, 1), F32),
                                   pltpu.VMEM((hb, CHUNK, LANES), CDT), pltpu.VMEM((hb, CHUNK, LANES), CDT)],
                   compiler_params=_params("parallel", "arbitrary"), name=name)(qkv, qkv, qkv, do, tab)


HALO = 8


def _shift_down(cur, prev8, s):
    if s == 0:
        return cur
    tt = cur.shape[0]
    row = lax.broadcasted_iota(jnp.int32, cur.shape, 0)
    fix = jnp.concatenate([pltpu.roll(prev8, s, 0), jnp.zeros((tt - HALO, cur.shape[1]), cur.dtype)], axis=0)
    return jnp.where(row < s, fix, pltpu.roll(cur, s, 0))


def _shift_up(cur, next8, s):
    if s == 0:
        return cur
    tt = cur.shape[0]
    row = lax.broadcasted_iota(jnp.int32, cur.shape, 0)
    fix = jnp.concatenate([jnp.zeros((tt - HALO, cur.shape[1]), cur.dtype), pltpu.roll(next8, HALO - s, 0)], axis=0)
    return jnp.where(row >= tt - s, fix, pltpu.roll(cur, tt - s, 0))


def _halo_specs(tt, width, T):
    per = tt // HALO
    last = T // HALO - 1
    cur = BS((tt, width), lambda i: (i, 0))
    prev = BS((HALO, width), lambda i: (jnp.maximum(i * per - 1, 0), 0))
    nxt = BS((HALO, width), lambda i: (jnp.minimum((i + 1) * per, last), 0))
    return cur, prev, nxt


def sc_fwd(proj, w, *, name):
    T, D3 = proj.shape
    D = D3 // 3
    K = w.shape[0]
    tt = _tile(T, 256)
    cur, prev, _ = _halo_specs(tt, D3, T)

    def body(p_ref, pp_ref, w_ref, o_ref):
        i = pl.program_id(0)
        pc = p_ref[:, D:2 * D] * p_ref[:, 2 * D:]
        pp = jnp.where(i > 0, pp_ref[:, D:2 * D] * pp_ref[:, 2 * D:], 0.0)
        u = jnp.zeros((tt, D), F32)
        for kk in range(K):
            u = u + w_ref[kk:kk + 1, :] * _shift_down(pc, pp, K - 1 - kk)
        o_ref[...] = (p_ref[:, :D] * u).astype(o_ref.dtype)

    return _pallas(body, grid=(T // tt,), in_specs=[cur, prev, BS((K, D), lambda i: (0, 0))],
                   out_specs=BS((tt, D), lambda i: (i, 0)), out_shape=SDS((T, D), CDT),
                   compiler_params=_params("parallel"), name=name)(proj, proj, w)


def sc_bwd(dm, proj, w, *, name):
    T, D3 = proj.shape
    D = D3 // 3
    K = w.shape[0]
    tt = _tile(T, 256)
    cur, prev, nxt = _halo_specs(tt, D3, T)
    dcur, _, dnxt = _halo_specs(tt, D, T)
    nsteps = T // tt

    def body(dm_ref, dmn_ref, p_ref, pp_ref, pn_ref, w_ref, o_ref, dw_ref):
        i = pl.program_id(0)
        b = p_ref[:, :D]
        c = p_ref[:, D:2 * D]
        h = p_ref[:, 2 * D:]
        pc = c * h
        pp = jnp.where(i > 0, pp_ref[:, D:2 * D] * pp_ref[:, 2 * D:], 0.0)
        dmv = dm_ref[...]
        du = dmv * b
        dun = jnp.where(i < nsteps - 1, dmn_ref[...] * pn_ref[:, :D], 0.0)
        u = jnp.zeros((tt, D), F32)
        dp = jnp.zeros((tt, D), F32)
        rows = []
        for kk in range(K):
            sh = _shift_down(pc, pp, K - 1 - kk)
            u = u + w_ref[kk:kk + 1, :] * sh
            dp = dp + w_ref[kk:kk + 1, :] * _shift_up(du, dun, K - 1 - kk)
            rows.append(jnp.sum(du * sh, axis=0, keepdims=True))
        rows.append(jnp.zeros((HALO - K, D), F32))
        o_ref[:, :D] = (dmv * u).astype(o_ref.dtype)
        o_ref[:, D:2 * D] = (dp * h).astype(o_ref.dtype)
        o_ref[:, 2 * D:] = (dp * c).astype(o_ref.dtype)

        @pl.when(i == 0)
        def _():
            dw_ref[...] = jnp.zeros_like(dw_ref)

        dw_ref[...] += jnp.concatenate(rows, axis=0)

    return _pallas(body, grid=(nsteps,), in_specs=[dcur, dnxt, cur, prev, nxt, BS((K, D), lambda i: (0, 0))],
                   out_specs=[BS((tt, D3), lambda i: (i, 0)), BS((HALO, D), lambda i: (0, 0))],
                   out_shape=[SDS((T, D3), CDT), SDS((HALO, D), F32)],
                   compiler_params=_params("arbitrary"), name=name)(dm, dm, proj, proj, proj, w)


def ssd_conv_fwd(xpre, w, b, *, name):
    T, C = xpre.shape
    K = w.shape[0]
    tt = _tile(T, 256)
    cur, prev, _ = _halo_specs(tt, C, T)

    def body(x_ref, xp_ref, w_ref, b_ref, pre_ref, o_ref):
        i = pl.program_id(0)
        xv = x_ref[...]
        xp = jnp.where(i > 0, xp_ref[...], 0.0)
        acc = jnp.broadcast_to(b_ref[...], (tt, C))
        for kk in range(K):
            acc = acc + w_ref[kk:kk + 1, :] * _shift_down(xv, xp, K - 1 - kk)
        pre_ref[...] = acc
        o_ref[...] = acc * _sigmoid(acc)

    row = BS((tt, C), lambda i: (i, 0))
    return _pallas(body, grid=(T // tt,), in_specs=[cur, prev, BS((K, C), lambda i: (0, 0)), BS((1, C), lambda i: (0, 0))],
                   out_specs=[row, row], out_shape=[SDS((T, C), F32)] * 2,
                   compiler_params=_params("parallel"), name=name)(xpre, xpre, w, b)


def ssd_conv_bwd(dxs, db_, dc_, pre, xpre, w, *, name):
    T, C = xpre.shape
    K = w.shape[0]
    tt = _tile(T, 256)
    cur, prev, nxt = _halo_specs(tt, C, T)
    widths = (dxs.shape[1], db_.shape[1], dc_.shape[1])
    gspecs = []
    for wd in widths:
        c_, _, n_ = _halo_specs(tt, wd, T)
        gspecs += [c_, n_]
    nsteps = T // tt

    def dsilu(p):
        s = _sigmoid(p)
        return s * (1.0 + p * (1.0 - s))

    def body(g0, g0n, g1, g1n, g2, g2n, pre_ref, pren_ref, x_ref, xp_ref, w_ref, dx_ref, dw_ref, dbias_ref):
        i = pl.program_id(0)
        gcur = jnp.concatenate([g0[...], g1[...], g2[...]], axis=1)
        gnxt = jnp.concatenate([g0n[...], g1n[...], g2n[...]], axis=1)
        dpre = gcur * dsilu(pre_ref[...])
        dpren = jnp.where(i < nsteps - 1, gnxt * dsilu(pren_ref[...]), 0.0)
        xv = x_ref[...]
        xp = jnp.where(i > 0, xp_ref[...], 0.0)
        dx = jnp.zeros((tt, C), F32)
        rows = []
        for kk in range(K):
            dx = dx + w_ref[kk:kk + 1, :] * _shift_up(dpre, dpren, K - 1 - kk)
            rows.append(jnp.sum(dpre * _shift_down(xv, xp, K - 1 - kk), axis=0, keepdims=True))
        rows.append(jnp.zeros((HALO - K, C), F32))
        dx_ref[...] = dx.astype(dx_ref.dtype)

        @pl.when(i == 0)
        def _():
            dw_ref[...] = jnp.zeros_like(dw_ref)
            dbias_ref[...] = jnp.zeros_like(dbias_ref)

        dw_ref[...] += jnp.concatenate(rows, axis=0)
        dbias_ref[...] += jnp.sum(dpre, axis=0, keepdims=True)

    return _pallas(body, grid=(nsteps,),
                   in_specs=gspecs + [cur, nxt, cur, prev, BS((K, C), lambda i: (0, 0))],
                   out_specs=[BS((tt, C), lambda i: (i, 0)), BS((HALO, C), lambda i: (0, 0)), BS((1, C), lambda i: (0, 0))],
                   out_shape=[SDS((T, C), CDT), SDS((HALO, C), F32), SDS((1, C), F32)],
                   compiler_params=_params("arbitrary"), name=name)(dxs, dxs, db_, db_, dc_, dc_, pre, pre, xpre, xpre, w)


def ssd_prep_fwd(dtpre, bias, alog, *, name):
    T = dtpre.shape[0]
    nc = T // CHUNK

    def body(p_ref, b_ref, a_ref, dt_ref, ac_ref, dtT_ref, acT_ref):
        xv = p_ref[...] + b_ref[...]
        dt = jnp.maximum(xv, 0.0) + _softplus_neg_abs(xv)
        row, col = _sb_tri()
        lower = (col <= row).astype(F32)
        ac = jnp.dot(lower, dt * -jnp.exp(a_ref[...]), precision=HI, preferred_element_type=F32)
        dt_ref[...] = dt
        ac_ref[...] = ac
        dtT_ref[...] = dt.T
        acT_ref[...] = ac.T

    blk = BS((CHUNK, LANES), lambda c: (c, 0))
    blkT = BS((LANES, CHUNK), lambda c: (0, c))
    vec = BS((1, LANES), lambda c: (0, 0))
    return _pallas(body, grid=(nc,), in_specs=[blk, vec, vec], out_specs=[blk, blk, blkT, blkT],
                   out_shape=[SDS((T, LANES), F32)] * 2 + [SDS((LANES, T), F32)] * 2,
                   compiler_params=_params("parallel"), name=name)(dtpre, bias, alog)


def ssd_prep_bwd(dac_c, ddt_c, dac_r, ddt_r, dtpre, bias, alog, *, name):
    T = dtpre.shape[0]
    nc = T // CHUNK

    def body(dac_ref, ddt_ref, dacr_ref, ddtr_ref, p_ref, b_ref, a_ref, o_ref, da_ref, db_ref):
        c = pl.program_id(0)
        xv = p_ref[...] + b_ref[...]
        dt = jnp.maximum(xv, 0.0) + _softplus_neg_abs(xv)
        av = -jnp.exp(a_ref[...])
        row, col = _sb_tri()
        after = (col >= row).astype(F32)
        dda = jnp.dot(after, dac_ref[...] + dacr_ref[...].T, precision=HI, preferred_element_type=F32)
        dpre = (ddt_ref[...] + ddtr_ref[...].T + av * dda) * _sigmoid(xv)
        o_ref[...] = dpre

        @pl.when(c == 0)
        def _():
            da_ref[...] = jnp.zeros_like(da_ref)
            db_ref[...] = jnp.zeros_like(db_ref)

        da_ref[...] += jnp.sum(dt * dda, axis=0, keepdims=True) * av
        db_ref[...] += jnp.sum(dpre, axis=0, keepdims=True)

    blk = BS((CHUNK, LANES), lambda c: (c, 0))
    blkT = BS((LANES, CHUNK), lambda c: (0, c))
    vec = BS((1, LANES), lambda c: (0, 0))
    return _pallas(body, grid=(nc,), in_specs=[blk, blk, blkT, blkT, blk, vec, vec], out_specs=[blk, vec, vec],
                   out_shape=[SDS((T, LANES), F32), SDS((1, LANES), F32), SDS((1, LANES), F32)],
                   compiler_params=_params("arbitrary"), name=name)(dac_c, ddt_c, dac_r, ddt_r, dtpre, bias, alog)


def _ssd_specs(cfg, T, rev):
    nc = T // CHUNK
    GW = SSD_HPG * HEAD_DIM
    bo = cfg.DI // LANES
    co = (cfg.DI + cfg.G * LANES) // LANES
    ci = (lambda c: nc - 1 - c) if rev else (lambda c: c)
    return dict(
        x=BS((CHUNK, GW), lambda g, c: (ci(c), g)),
        b=BS((CHUNK, LANES), lambda g, c: (ci(c), bo + g)),
        c=BS((CHUNK, LANES), lambda g, c: (ci(c), co + g)),
        col=BS((CHUNK, LANES), lambda g, c: (ci(c), 0)),
        row=BS((LANES, CHUNK), lambda g, c: (0, ci(c))),
        vec=BS((1, LANES), lambda g, c: (0, 0)),
        ng=BS((1, GW), lambda g, c: (0, g)),
        hp=BS((1, 1, SSD_HPG, HEAD_DIM, LANES), lambda g, c: (g, ci(c), 0, 0, 0)),
    )


def ssd_chunk_fwd(xbc, z, dt, ac, dtT, acT, dskip, ng, cfg, *, name):
    T = xbc.shape[0]
    nc = T // CHUNK
    G, DI = cfg.G, cfg.DI
    GW = SSD_HPG * HEAD_DIM
    sp = _ssd_specs(cfg, T, False)

    def body(x_ref, b_ref, c_ref, z_ref, dt_ref, ac_ref, dtT_ref, acT_ref, d_ref, ng_ref,
             y_ref, yn_ref, hp_ref, h_scr):
        g = pl.program_id(0)
        c = pl.program_id(1)

        @pl.when(c == 0)
        def _():
            h_scr[...] = jnp.zeros_like(h_scr)

        row, col = _sb_tri()
        causal = row >= col
        Bm = b_ref[...].astype(CDT)
        Cm = c_ref[...].astype(CDT)
        Gm = _nt(Cm, Bm)
        dtb, acb, dtTb, acTb, dv = dt_ref[...], ac_ref[...], dtT_ref[...], acT_ref[...], d_ref[...]
        heads = range(SSD_HPG)
        hhs = [g * SSD_HPG + r for r in heads]
        ac_cs = [_colsel(acb, hh) for hh in hhs]
        dt_cs = [_colsel(dtb, hh) for hh in hhs]
        Xs = [x_ref[:, r * HEAD_DIM:(r + 1) * HEAD_DIM] for r in heads]
        Hps = [h_scr[r] for r in heads]
        Ws = [(Gm * jnp.where(causal, jnp.exp(ac_c - _rowsel(acTb, hh)), 0.0) * _rowsel(dtTb, hh)).astype(CDT)
              for ac_c, hh in zip(ac_cs, hhs)]
        Qs = [_nt(Cm, Hp.astype(CDT)) for Hp in Hps]
        Yds = [_nn(Wm, Xr.astype(CDT)) for Wm, Xr in zip(Ws, Xs)]
        aLs = [_rowsel(ac_c, CHUNK - 1) for ac_c in ac_cs]
        Xes = [(Xr * (jnp.exp(aL - ac_c) * dt_c)).astype(CDT) for Xr, aL, ac_c, dt_c in zip(Xs, aLs, ac_cs, dt_cs)]
        Sts = [_tn(Xe, Bm) for Xe in Xes]
        for r in heads:
            hp_ref[0, 0, r] = Hps[r]
            h_scr[r] = jnp.exp(aLs[r]) * Hps[r] + Sts[r]
            y_ref[:, r * HEAD_DIM:(r + 1) * HEAD_DIM] = (Yds[r] + jnp.exp(ac_cs[r]) * Qs[r]
                                                         + _colsel(dv, hhs[r]) * Xs[r])
        zz = z_ref[...]
        yz = y_ref[...] * (zz * _sigmoid(zz))
        rstd = lax.rsqrt(jnp.mean(yz * yz, axis=-1, keepdims=True) + RMS_EPS)
        yn_ref[...] = (yz * rstd * ng_ref[...]).astype(yn_ref.dtype)

    return _pallas(body, grid=(G, nc),
                   in_specs=[sp["x"], sp["b"], sp["c"], sp["x"], sp["col"], sp["col"], sp["row"], sp["row"], sp["vec"], sp["ng"]],
                   out_specs=[sp["x"], sp["x"], sp["hp"]],
                   out_shape=[SDS((T, DI), F32), SDS((T, DI), CDT), SDS((G, nc, SSD_HPG, HEAD_DIM, LANES), F32)],
                   scratch_shapes=[pltpu.VMEM((SSD_HPG, HEAD_DIM, LANES), F32)],
                   compiler_params=_params("parallel", "arbitrary"), name=name)(xbc, xbc, xbc, z, dt, ac, dtT, acT, dskip, ng)


def ssd_chunk_bwd(dyn, xbc, z, y, hp, dt, ac, dtT, acT, dskip, ng, cfg, *, name):
    T = xbc.shape[0]
    nc = T // CHUNK
    G, DI = cfg.G, cfg.DI
    GW = SSD_HPG * HEAD_DIM
    sp = _ssd_specs(cfg, T, True)

    def body(dyn_ref, x_ref, b_ref, c_ref, z_ref, y_ref, hp_ref, dt_ref, ac_ref, dtT_ref, acT_ref, d_ref, ng_ref,
             dx_ref, db_ref, dc_ref, dz_ref, colp_ref, rowp_ref, dd_ref, dng_ref, dh_scr):
        g = pl.program_id(0)
        c = pl.program_id(1)

        @pl.when(c == 0)
        def _():
            dh_scr[...] = jnp.zeros_like(dh_scr)
            dd_ref[...] = jnp.zeros_like(dd_ref)
            dng_ref[...] = jnp.zeros_like(dng_ref)

        zz = z_ref[...]
        sg = _sigmoid(zz)
        gate = zz * sg
        yv = y_ref[...]
        yz = yv * gate
        rstd = lax.rsqrt(jnp.mean(yz * yz, axis=-1, keepdims=True) + RMS_EPS)
        nhat = yz * rstd
        dynv = dyn_ref[...].astype(F32)
        dng_ref[...] += jnp.sum(dynv * nhat, axis=0, keepdims=True)
        dnh = dynv * ng_ref[...]
        dyz = rstd * (dnh - nhat * jnp.mean(dnh * nhat, axis=-1, keepdims=True))
        dz_ref[...] = (dyz * yv * (sg * (1.0 + zz * (1.0 - sg)))).astype(dz_ref.dtype)
        dy_all = dyz * gate

        row, col = _sb_tri()
        causal = row >= col
        lane1 = lax.broadcasted_iota(jnp.int32, (1, LANES), 1)
        sub8 = lax.broadcasted_iota(jnp.int32, (HALO, CHUNK), 0)
        subc = lax.broadcasted_iota(jnp.int32, (CHUNK, 1), 0)
        Bf = b_ref[...]
        Cf = c_ref[...]
        Bm = Bf.astype(CDT)
        Cm = Cf.astype(CDT)
        Gm = _nt(Cm, Bm)
        dtb, acb, dtTb, acTb, dv = dt_ref[...], ac_ref[...], dtT_ref[...], acT_ref[...], d_ref[...]
        dG = jnp.zeros((CHUNK, CHUNK), F32)
        dBm = jnp.zeros((CHUNK, LANES), F32)
        dCm = jnp.zeros((CHUNK, LANES), F32)
        colp = jnp.zeros((CHUNK, LANES), F32)
        rowp = jnp.zeros((HALO, CHUNK), F32)
        ddv = jnp.zeros((1, LANES), F32)
        for r in range(SSD_HPG):
            hh = g * SSD_HPG + r
            ac_c, ac_r = _colsel(acb, hh), _rowsel(acTb, hh)
            dt_c, dt_r = _colsel(dtb, hh), _rowsel(dtTb, hh)
            S = jnp.where(causal, jnp.exp(ac_c - ac_r), 0.0)
            GS = Gm * S
            Wm = GS * dt_r
            Xr = x_ref[:, r * HEAD_DIM:(r + 1) * HEAD_DIM]
            Xc = Xr.astype(CDT)
            Hp = hp_ref[0, 0, r]
            Hc = Hp.astype(CDT)
            dY = dy_all[:, r * HEAD_DIM:(r + 1) * HEAD_DIM]
            dYc = dY.astype(CDT)
            dsk = _colsel(dv, hh)
            ddv = ddv + jnp.where(lane1 == r, jnp.sum(jnp.sum(dY * Xr, axis=1, keepdims=True), axis=0, keepdims=True), 0.0)
            dX = dY * dsk + _tn(Wm.astype(CDT), dYc)
            dWm = _nt(dYc, Xc)
            dG = dG + dWm * S * dt_r
            Mm = dWm * Wm
            Nm = dWm * GS
            dac_c = jnp.sum(Mm, axis=1, keepdims=True)
            dac_r = -jnp.sum(Mm, axis=0, keepdims=True)
            ddt_r = jnp.sum(Nm, axis=0, keepdims=True)
            E = jnp.exp(ac_c)
            Q = _nt(Cm, Hc)
            dQ = (dY * E).astype(CDT)
            dac_c = dac_c + jnp.sum(dY * Q, axis=1, keepdims=True) * E
            dCm = dCm + _nn(dQ, Hc)
            dHp = _tn(dQ, Cm)
            dHn = dh_scr[r]
            dHc = dHn.astype(CDT)
            aL = _rowsel(ac_c, CHUNK - 1)
            eaL = jnp.exp(aL)
            ex = jnp.exp(aL - ac_c)
            e = ex * dt_c
            dHp = dHp + eaL * dHn
            daL = eaL * jnp.sum(jnp.sum(dHn * Hp, axis=1, keepdims=True), axis=0, keepdims=True)
            T2 = _nt(Bm, dHc)
            dX = dX + T2 * e
            de = jnp.sum(Xr * T2, axis=1, keepdims=True)
            dBm = dBm + _nn((Xr * e).astype(CDT), dHc)
            dee = de * e
            daL = daL + jnp.sum(dee, axis=0, keepdims=True)
            dac_c = dac_c - dee + jnp.where(subc == CHUNK - 1, daL, 0.0)
            ddt_c = de * ex
            dh_scr[r] = dHp
            dx_ref[:, r * HEAD_DIM:(r + 1) * HEAD_DIM] = dX
            colp = jnp.where(col == r, dac_c, colp)
            colp = jnp.where(col == SSD_HPG + r, ddt_c, colp)
            rowp = jnp.where(sub8 == r, dac_r, rowp)
            rowp = jnp.where(sub8 == SSD_HPG + r, ddt_r, rowp)
        dGc = dG.astype(CDT)
        db_ref[...] = dBm + _tn(dGc, Cm)
        dc_ref[...] = dCm + _nn(dGc, Bm)
        colp_ref[0] = colp
        rowp_ref[0] = rowp
        dd_ref[0] += ddv

    f32 = lambda shape: SDS(shape, F32)
    return _pallas(body, grid=(G, nc),
                   in_specs=[sp["x"], sp["x"], sp["b"], sp["c"], sp["x"], sp["x"], sp["hp"], sp["col"], sp["col"],
                             sp["row"], sp["row"], sp["vec"], sp["ng"]],
                   out_specs=[sp["x"],
                              BS((CHUNK, LANES), lambda g, c: (nc - 1 - c, g)),
                              BS((CHUNK, LANES), lambda g, c: (nc - 1 - c, g)),
                              sp["x"],
                              BS((1, CHUNK, LANES), lambda g, c: (g, nc - 1 - c, 0)),
                              BS((1, HALO, CHUNK), lambda g, c: (g, 0, nc - 1 - c)),
                              BS((1, 1, LANES), lambda g, c: (g, 0, 0)),
                              sp["ng"]],
                   out_shape=[f32((T, DI)), f32((T, G * LANES)), f32((T, G * LANES)), SDS((T, DI), CDT),
                              f32((G, T, LANES)), f32((G, HALO, T)), f32((G, 1, LANES)), f32((1, DI))],
                   scratch_shapes=[pltpu.VMEM((SSD_HPG, HEAD_DIM, LANES), F32)],
                   compiler_params=_params("parallel", "arbitrary"), name=name)(
                       dyn, xbc, xbc, xbc, z, y, hp, dt, ac, dtT, acT, dskip, ng)


HBM = pl.BlockSpec(memory_space=pltpu.HBM)
VMEM = pl.BlockSpec(memory_space=pltpu.VMEM)


def _me():
    return lax.axis_index("x"), lax.axis_index("y"), lax.axis_index("c")


def _other_chips(x, y):
    return [(1 - x, y), (x, 1 - y), (1 - x, 1 - y)]


def _rcopy(src, dst, send_sem, recv_sem, to):
    return pltpu.make_async_remote_copy(src_ref=src, dst_ref=dst, send_sem=send_sem, recv_sem=recv_sem,
                                        device_id=to, device_id_type=MESH)


def allgather_weights(pack, *, name):
    _, Rh, W = pack.shape

    def body(src, out, send_sems, recv_sems):
        x, y, c = _me()
        k = 2 * x + y
        sibling = (x, y, 1 - c)
        chips = _other_chips(x, y)
        sends = [_rcopy(src.at[c], out.at[k, c], send_sems.at[j], recv_sems.at[j], (cx, cy, c))
                 for j, (cx, cy) in enumerate(chips)]
        for cp in sends:
            cp.start()
        passed = []
        for j, (cx, cy) in enumerate(chips):
            kj = 2 * cx + cy
            landed = out.at[kj, c]
            _rcopy(landed, landed, send_sems.at[j], recv_sems.at[j], (cx, cy, c)).wait_recv()
            fw = _rcopy(landed, landed, send_sems.at[3 + j], recv_sems.at[3 + j], sibling)
            fw.start()
            passed.append(fw)
        for j, (cx, cy) in enumerate(chips):
            theirs = out.at[2 * cx + cy, 1 - c]
            _rcopy(theirs, theirs, send_sems.at[3 + j], recv_sems.at[3 + j], sibling).wait_recv()
        for cp in sends + passed:
            cp.wait_send()

    return _pallas(body, in_specs=[HBM], out_specs=HBM, out_shape=SDS((N_CHIPS, 2, Rh, W), pack.dtype),
                   scratch_shapes=[pltpu.SemaphoreType.DMA((6,)), pltpu.SemaphoreType.DMA((6,))],
                   name=name)(pack)


def sibling_swap_half(p, *, name):
    _, n, Rh, W = p.shape

    def body(src, out, send_sem, recv_sem):
        x, y, c = _me()
        cp = _rcopy(src.at[1 - c], out, send_sem, recv_sem, (x, y, 1 - c))
        cp.start()
        cp.wait()

    return _pallas(body, in_specs=[HBM], out_specs=HBM, out_shape=SDS((n, Rh, W), p.dtype),
                   scratch_shapes=[pltpu.SemaphoreType.DMA, pltpu.SemaphoreType.DMA], name=name)(p)


def chip_scatter(s1, *, name):
    n, Rh, W = s1.shape

    def body(src, out, send_sems, recv_sems):
        x, y, c = _me()
        k = 2 * x + y
        chips = _other_chips(x, y)
        sends = [_rcopy(src.at[2 * cx + cy], out.at[k], send_sems.at[j], recv_sems.at[j], (cx, cy, c))
                 for j, (cx, cy) in enumerate(chips)]
        for cp in sends:
            cp.start()
        for j, (cx, cy) in enumerate(chips):
            slot = out.at[2 * cx + cy]
            _rcopy(slot, slot, send_sems.at[j], recv_sems.at[j], (cx, cy, c)).wait_recv()
        for cp in sends:
            cp.wait_send()

    return _pallas(body, in_specs=[HBM], out_specs=HBM, out_shape=SDS((n, Rh, W), s1.dtype),
                   scratch_shapes=[pltpu.SemaphoreType.DMA((3,)), pltpu.SemaphoreType.DMA((3,))],
                   name=name)(s1)


def sibling_pair(r2, *, name):
    _, Rh, W = r2.shape

    def body(src, out, send_sem, recv_sem):
        x, y, c = _me()
        cp = _rcopy(src.at[c], out.at[c], send_sem, recv_sem, (x, y, 1 - c))
        cp.start()
        theirs = out.at[1 - c]
        _rcopy(theirs, theirs, send_sem, recv_sem, (x, y, 1 - c)).wait_recv()
        cp.wait_send()

    return _pallas(body, in_specs=[HBM], out_specs=HBM, out_shape=SDS((2, Rh, W), r2.dtype),
                   input_output_aliases={0: 0},
                   scratch_shapes=[pltpu.SemaphoreType.DMA, pltpu.SemaphoreType.DMA], name=name)(r2)


def allgather_small(v, *, name):
    R, W = v.shape

    def body(v_ref, out_ref, sum_ref, send_sems, recv_sems):
        x, y, c = _me()
        me = 4 * x + 2 * y + c
        out_ref[me] = v_ref[...]
        peers = []
        for m in range(1, N_DEV):
            px = 1 - x if m & 4 else x
            py = 1 - y if m & 2 else y
            pc = 1 - c if m & 1 else c
            peers.append((px, py, pc))
        sends = [_rcopy(v_ref, out_ref.at[me], send_sems.at[j], recv_sems.at[j], p) for j, p in enumerate(peers)]
        for cp in sends:
            cp.start()
        for j, (px, py, pc) in enumerate(peers):
            slot = out_ref.at[4 * px + 2 * py + pc]
            _rcopy(slot, slot, send_sems.at[j], recv_sems.at[j], (px, py, pc)).wait_recv()
        for cp in sends:
            cp.wait_send()
        acc = out_ref[0]
        for d in range(1, N_DEV):
            acc = acc + out_ref[d]
        sum_ref[...] = acc

    return _pallas(body, in_specs=[VMEM], out_specs=[VMEM, VMEM],
                   out_shape=[SDS((N_DEV, R, W), v.dtype), SDS((R, W), v.dtype)],
                   scratch_shapes=[pltpu.SemaphoreType.DMA((N_DEV - 1,)), pltpu.SemaphoreType.DMA((N_DEV - 1,))],
                   name=name)(v)


def _row_tile(R, cap=512):
    best = None
    for t in range(32, cap + 1, 32):
        if R % t == 0:
            best = t
    assert best is not None, R
    return best


def add_halves(p, recv, c_idx, *, name):
    _, n, Rh, W = p.shape
    tr = _row_tile(Rh)

    def body(c_ref, p_ref, r_ref, o_ref):
        o_ref[...] = (p_ref[0].astype(F32) + r_ref[...].astype(F32)).astype(o_ref.dtype)

    gs = pltpu.PrefetchScalarGridSpec(
        num_scalar_prefetch=1, grid=(n, Rh // tr),
        in_specs=[BS((1, 1, tr, W), lambda s, i, c_ref: (c_ref[0], s, i, 0)), BS((1, tr, W), lambda s, i, c_ref: (s, i, 0))],
        out_specs=BS((1, tr, W), lambda s, i, c_ref: (s, i, 0)))
    return _pallas(body, grid_spec=gs, out_shape=SDS((n, Rh, W), p.dtype),
                   compiler_params=_params("parallel", "parallel"), name=name)(c_idx, p, recv)


def sum_slots(s1, got, kc_idx, *, name):
    n, Rh, W = got.shape
    tr = _row_tile(Rh)

    def body(kc_ref, s_ref, g1, g2, g3, o_ref):
        o_ref[0] = s_ref[0].astype(F32) + g1[0].astype(F32) + g2[0].astype(F32) + g3[0].astype(F32)

    slot = lambda d: BS((1, tr, W), lambda i, kc: ((kc[0] + d) % n, i, 0))
    gs = pltpu.PrefetchScalarGridSpec(
        num_scalar_prefetch=1, grid=(Rh // tr,), in_specs=[slot(0), slot(1), slot(2), slot(3)],
        out_specs=BS((1, tr, W), lambda i, kc: (kc[1], i, 0)))
    return _pallas(body, grid_spec=gs, out_shape=SDS((2, Rh, W), F32),
                   compiler_params=_params("parallel"), name=name)(kc_idx, s1, got, got, got)


BIG = [("ffn1_w_gu", True), ("ffn1_w_down", False), ("ffn2_w_gu", True), ("ffn2_w_down", False),
       ("sb_w_qkv", True), ("sb_w_o", False), ("ssd_w_in", True), ("ssd_w_out", False),
       ("sc_w_in", True), ("sc_w_out", False)]
PACK_ALIGN = 64


def _big_layout(shards):
    out, off = [], 0
    for name, cols in BIG:
        shape = shards[name].shape
        rows = math.prod(shape) // PACK_W
        assert rows * PACK_W == math.prod(shape), (name, shape)
        out.append((name, cols, shape, off, rows))
        off += rows
    return out, -(-off // PACK_ALIGN) * PACK_ALIGN


def _pack_rows(pieces, total, axis):
    used = sum(p.shape[axis] for p in pieces)
    if total > used:
        pad_shape = list(pieces[0].shape)
        pad_shape[axis] = total - used
        pieces = pieces + [jnp.zeros(pad_shape, pieces[0].dtype)]
    return jnp.concatenate(pieces, axis=axis)


def _gather_big(shards, tag):
    layout, total = _big_layout(shards)
    pack = _pack_rows([shards[n].astype(CDT).reshape(-1, PACK_W) for n, *_ in layout], total, 0)
    pack = pack.reshape(2, total // 2, PACK_W)
    full = allgather_weights(pack, name=f"{tag}_allgather")
    k_chip = 2 * lax.axis_index("x") + lax.axis_index("y")
    full = lax.dynamic_update_slice(full, pack[None], (k_chip, 0, 0, 0)).reshape(N_CHIPS, total, PACK_W)
    out = {}
    for name, cols, (L, K, n), off, rows in layout:
        w = full[:, off:off + rows].reshape(N_CHIPS, L, K, n)
        if cols:
            out[name] = w.transpose(1, 2, 0, 3).reshape(L, K, N_CHIPS * n)
        else:
            out[name] = w.transpose(1, 0, 2, 3).reshape(L, N_CHIPS * K, n)
    return out


def _reduce_big(grads, shards, tag):
    layout, total = _big_layout(shards)
    pieces = []
    for name, cols, (L, K, n), off, rows in layout:
        g = grads[name]
        if cols:
            g = g.reshape(L, K, N_CHIPS, n).transpose(2, 0, 1, 3)
        else:
            g = g.reshape(L, N_CHIPS, K, n).transpose(1, 0, 2, 3)
        pieces.append(g.astype(CDT).reshape(N_CHIPS, rows, PACK_W))
    pack = _pack_rows(pieces, total, 1)
    half = total // 2
    p = pack.reshape(N_CHIPS, 2, half, PACK_W).transpose(1, 0, 2, 3)
    c_core = lax.axis_index("c").astype(jnp.int32)
    k_chip = (2 * lax.axis_index("x") + lax.axis_index("y")).astype(jnp.int32)
    recv = sibling_swap_half(p, name=f"{tag}_sibling_swap")
    s1 = add_halves(p, recv, c_core.reshape(1), name=f"{tag}_add_halves")
    got = chip_scatter(s1, name=f"{tag}_chip_scatter")
    r2 = sum_slots(s1, got, jnp.stack([k_chip, c_core]), name=f"{tag}_sum_slots")
    red = sibling_pair(r2, name=f"{tag}_sibling_pair").reshape(total, PACK_W)
    return {name: red[off:off + rows].reshape(shape) for name, cols, shape, off, rows in layout}


def _pad_lanes(v, width=LANES):
    return jnp.pad(v, ((0, 0), (0, width - v.shape[1])))


def _small_pack(vecs):
    flat, meta, off = [], [], 0
    for v in vecs:
        flat.append(v.reshape(-1))
        meta.append((off, v.size, v.shape))
        off += v.size
    total = -(-off // (8 * LANES)) * 8 * LANES
    flat.append(jnp.zeros((total - off,), F32))
    return jnp.concatenate(flat).reshape(-1, LANES), meta


def _small_unpack(pack, meta):
    flat = pack.reshape(-1)
    return [flat[off:off + size].reshape(shape) for off, size, shape in meta]


def _ffn_fwd(x, g, wgu, wd, tag):
    h = rms_fwd(x, g, name=f"{tag}_norm")
    gu = matmul(h, wgu, out_dtype=CDT, name=f"{tag}_gu")
    a = swiglu_fwd(gu, name=f"{tag}_act")
    return matmul(a, wd, res=x, scale=0.5, name=f"{tag}_down"), (x, h, gu, a)


def _ffn_bwd(dout, saved, g, wgu, wd, tag):
    x, h, gu, a = saved
    dwd = matmul(a, dout, ta=True, scale=0.5, name=f"{tag}_dwd")
    da = matmul(dout, wd, tb=True, scale=0.5, name=f"{tag}_da")
    dgu = swiglu_bwd(da, gu, name=f"{tag}_dact")
    dwgu = matmul(h, dgu, ta=True, name=f"{tag}_dwgu")
    dh = matmul(dgu, wgu, tb=True, name=f"{tag}_dh")
    dx, dg = rms_bwd(dh, x, g, dout, name=f"{tag}_dnorm")
    return dx, dg, dwgu, dwd


def _sb_mixer_fwd(x, g, wqkv, wo, cfg, tag):
    h = rms_fwd(x, g, name=f"{tag}_norm")
    qkv = matmul(h, wqkv, out_dtype=CDT, name=f"{tag}_qkv")
    o, tab = sb_fwd(qkv, cfg.sb_heads, name=f"{tag}_attn")
    return matmul(o, wo, res=x, name=f"{tag}_out"), (x, h, qkv, tab, o)


def _sb_mixer_bwd(dout, saved, g, wqkv, wo, cfg, tag):
    x, h, qkv, tab, o = saved
    dwo = matmul(o, dout, ta=True, name=f"{tag}_dwo")
    do = matmul(dout, wo, tb=True, out_dtype=CDT, name=f"{tag}_do")
    dqkv = jnp.concatenate(sb_bwd(qkv, do, tab, cfg.sb_heads, name=f"{tag}_dattn"), axis=1)
    dwqkv = matmul(h, dqkv, ta=True, name=f"{tag}_dwqkv")
    dh = matmul(dqkv, wqkv, tb=True, name=f"{tag}_dh")
    dx, dg = rms_bwd(dh, x, g, dout, name=f"{tag}_dnorm")
    return dx, dg, dwqkv, dwo


def _sc_mixer_fwd(x, g, win, convw, wout, tag):
    h = rms_fwd(x, g, name=f"{tag}_norm")
    proj = matmul(h, win, name=f"{tag}_in")
    m = sc_fwd(proj, convw, name=f"{tag}_conv")
    return matmul(m, wout, res=x, name=f"{tag}_out"), (x, h, proj, m)


def _sc_mixer_bwd(dout, saved, g, win, convw, wout, tag):
    x, h, proj, m = saved
    dwout = matmul(m, dout, ta=True, name=f"{tag}_dwout")
    dm = matmul(dout, wout, tb=True, name=f"{tag}_dm")
    dproj, dconvw = sc_bwd(dm, proj, convw, name=f"{tag}_dconv")
    dwin = matmul(h, dproj, ta=True, name=f"{tag}_dwin")
    dh = matmul(dproj, win, tb=True, name=f"{tag}_dh")
    dx, dg = rms_bwd(dh, x, g, dout, name=f"{tag}_dnorm")
    return dx, dg, dwin, dconvw[:convw.shape[0]], dwout


def _ssd_split_w(win, cfg):
    DI, CV = cfg.DI, cfg.conv_dim
    return win[:, :DI], win[:, DI:DI + CV], _pad_lanes(win[:, DI + CV:])


def _ssd_mixer_fwd(x, g, win, convw, convb, dtb, alog, dskip, ng, wout, cfg, tag):
    wz, wx, wdt = _ssd_split_w(win, cfg)
    h = rms_fwd(x, g, name=f"{tag}_norm")
    z = matmul(h, wz, name=f"{tag}_z")
    xpre = matmul(h, wx, name=f"{tag}_xbc")
    dtpre = matmul(h, wdt, name=f"{tag}_dt")
    pre, xbc = ssd_conv_fwd(xpre, convw, convb, name=f"{tag}_conv")
    dt, ac, dtT, acT = ssd_prep_fwd(dtpre, dtb, alog, name=f"{tag}_prep")
    y, yn, hp = ssd_chunk_fwd(xbc, z, dt, ac, dtT, acT, dskip, ng, cfg, name=f"{tag}_scan")
    out = matmul(yn, wout, res=x, name=f"{tag}_out")
    return out, (x, h, z, xpre, dtpre, pre, xbc, dt, ac, dtT, acT, y, yn, hp)


def _ssd_mixer_bwd(dout, saved, g, win, convw, convb, dtb, alog, dskip, ng, wout, cfg, tag):
    x, h, z, xpre, dtpre, pre, xbc, dt, ac, dtT, acT, y, yn, hp = saved
    T = x.shape[0]
    NH, G = cfg.NH, cfg.G
    wz, wx, wdt = _ssd_split_w(win, cfg)
    dwout = matmul(yn, dout, ta=True, name=f"{tag}_dwout")
    dyn = matmul(dout, wout, tb=True, name=f"{tag}_dyn")
    dxs, db, dc, dz, colp, rowp, dd, dng = ssd_chunk_bwd(dyn, xbc, z, y, hp, dt, ac, dtT, acT, dskip, ng, cfg,
                                                        name=f"{tag}_dscan")
    col = lambda lo: _pad_lanes(colp[:, :, lo:lo + SSD_HPG].transpose(1, 0, 2).reshape(T, NH))
    row = lambda lo: jnp.pad(rowp[:, lo:lo + SSD_HPG, :].reshape(NH, T), ((0, LANES - NH), (0, 0)))
    ddtpre, dalog, ddtb = ssd_prep_bwd(col(0), col(SSD_HPG), row(0), row(SSD_HPG), dtpre, dtb, alog, name=f"{tag}_dprep")
    dxpre, dconvw, dconvb = ssd_conv_bwd(dxs, db, dc, pre, xpre, convw, name=f"{tag}_dconv")
    dwz = matmul(h, dz, ta=True, name=f"{tag}_dwz")
    dwx = matmul(h, dxpre, ta=True, name=f"{tag}_dwx")
    dwdt = matmul(h, ddtpre, ta=True, name=f"{tag}_dwdt")
    dh = matmul(dz, wz, tb=True, name=f"{tag}_dh_z")
    dh = matmul(dxpre, wx, tb=True, res=dh, name=f"{tag}_dh_x")
    dh = matmul(ddtpre, wdt, tb=True, res=dh, name=f"{tag}_dh_dt")
    dx, dg = rms_bwd(dh, x, g, dout, name=f"{tag}_dnorm")
    dwin = jnp.concatenate([dwz, dwx, dwdt[:, :NH]], axis=1)
    small = dict(conv_w=dconvw[:convw.shape[0]], conv_b=dconvb, dt_bias=ddtb[:, :NH], a_log=dalog[:, :NH],
                 d=dd[:, 0, :SSD_HPG].reshape(1, NH), norm=dng)
    return dx, dg, dwin, dwout, small


INPUT_NAMES = ['x', 'ffn1_norm', 'ffn1_w_gu', 'ffn1_w_down', 'mix_norm', 'ffn2_norm', 'ffn2_w_gu', 'ffn2_w_down',
               'sb_w_qkv', 'sb_w_o', 'ssd_w_in', 'ssd_conv_w', 'ssd_conv_b', 'ssd_dt_bias', 'ssd_a_log', 'ssd_d',
               'ssd_norm', 'ssd_w_out', 'sc_w_in', 'sc_conv_w', 'sc_w_out', 'final_norm']
WEIGHT_NAMES = INPUT_NAMES[1:]
SHARDED_SMALL = ("ssd_conv_w", "sc_conv_w")


def train_step(a, cfg):
    x = a["x"][0]
    tgt = a["loss_target"][0]
    big_names = [n for n, _ in BIG]
    k_chip = 2 * lax.axis_index("x") + lax.axis_index("y")

    w = _gather_big({n: a[n] for n in big_names}, "w")
    conv_pack, conv_meta = _small_pack([a[n] for n in SHARDED_SMALL])
    conv_all, _ = allgather_small(conv_pack, name="conv_w_allgather")
    conv_full = {}
    for i, n in enumerate(SHARDED_SMALL):
        per_chip = [_small_unpack(conv_all[2 * kk], conv_meta)[i] for kk in range(N_CHIPS)]
        conv_full[n] = jnp.concatenate(per_chip, axis=-1)

    def vec(name, i):
        return a[name][i:i + 1]

    ssd_small = lambda j: (a["ssd_conv_b"][j:j + 1], _pad_lanes(a["ssd_dt_bias"][j:j + 1]),
                           _pad_lanes(a["ssd_a_log"][j:j + 1]), _pad_lanes(a["ssd_d"][j:j + 1]), a["ssd_norm"][j:j + 1])

    saved = []
    for i in range(cfg.depth):
        kind, j = i % 3, i // 3
        x, s1 = _ffn_fwd(x, vec("ffn1_norm", i), w["ffn1_w_gu"][i], w["ffn1_w_down"][i], f"l{i}_ffn1")
        if kind == 0:
            x, s2 = _sb_mixer_fwd(x, vec("mix_norm", i), w["sb_w_qkv"][j], w["sb_w_o"][j], cfg, f"l{i}_sb")
        elif kind == 1:
            x, s2 = _ssd_mixer_fwd(x, vec("mix_norm", i), w["ssd_w_in"][j], conv_full["ssd_conv_w"][j], *ssd_small(j),
                                   w["ssd_w_out"][j], cfg, f"l{i}_ssd")
        else:
            x, s2 = _sc_mixer_fwd(x, vec("mix_norm", i), w["sc_w_in"][j], conv_full["sc_conv_w"][j], w["sc_w_out"][j],
                                  f"l{i}_sc")
        x, s3 = _ffn_fwd(x, vec("ffn2_norm", i), w["ffn2_w_gu"][i], w["ffn2_w_down"][i], f"l{i}_ffn2")
        saved.append((s1, s2, s3))
    loss_part, dx, d_final = loss_head(x, a["final_norm"].reshape(1, -1), tgt, name="loss_head")

    gbig = {n: [None] * a[n].shape[0] for n in big_names}
    gsmall = {n: [None] * a[n].shape[0] for n in ("ffn1_norm", "mix_norm", "ffn2_norm", "ssd_conv_w", "ssd_conv_b",
                                                   "ssd_dt_bias", "ssd_a_log", "ssd_d", "ssd_norm", "sc_conv_w")}
    for i in reversed(range(cfg.depth)):
        kind, j = i % 3, i // 3
        s1, s2, s3 = saved[i]
        dx, gsmall["ffn2_norm"][i], gbig["ffn2_w_gu"][i], gbig["ffn2_w_down"][i] = _ffn_bwd(
            dx, s3, vec("ffn2_norm", i), w["ffn2_w_gu"][i], w["ffn2_w_down"][i], f"l{i}_ffn2")
        if kind == 0:
            dx, gsmall["mix_norm"][i], gbig["sb_w_qkv"][j], gbig["sb_w_o"][j] = _sb_mixer_bwd(
                dx, s2, vec("mix_norm", i), w["sb_w_qkv"][j], w["sb_w_o"][j], cfg, f"l{i}_sb")
        elif kind == 1:
            dx, gsmall["mix_norm"][i], gbig["ssd_w_in"][j], gbig["ssd_w_out"][j], sm = _ssd_mixer_bwd(
                dx, s2, vec("mix_norm", i), w["ssd_w_in"][j], conv_full["ssd_conv_w"][j], *ssd_small(j),
                w["ssd_w_out"][j], cfg, f"l{i}_ssd")
            for key, val in sm.items():
                gsmall["ssd_" + key][j] = val
        else:
            dx, gsmall["mix_norm"][i], gbig["sc_w_in"][j], gsmall["sc_conv_w"][j], gbig["sc_w_out"][j] = _sc_mixer_bwd(
                dx, s2, vec("mix_norm", i), w["sc_w_in"][j], conv_full["sc_conv_w"][j], w["sc_w_out"][j], f"l{i}_sc")
        dx, gsmall["ffn1_norm"][i], gbig["ffn1_w_gu"][i], gbig["ffn1_w_down"][i] = _ffn_bwd(
            dx, s1, vec("ffn1_norm", i), w["ffn1_w_gu"][i], w["ffn1_w_down"][i], f"l{i}_ffn1")
    grad_x = dx[None]

    red_big = _reduce_big({n: jnp.stack(v) for n, v in gbig.items()}, {n: a[n] for n in big_names}, "g")
    small_names = list(gsmall) + ["final_norm"]
    small_vals = [jnp.concatenate([p.reshape((1,) + a[n].shape[1:-1] + (-1,)) for p in gsmall[n]], axis=0)
                  for n in gsmall] + [d_final.reshape(-1)]
    gpack, gmeta = _small_pack(small_vals + [loss_part[:, :1]])
    _, gsum = allgather_small(gpack, name="small_allreduce")
    *small_red, loss = _small_unpack(gsum, gmeta)
    red = dict(red_big)
    for n, v in zip(small_names, small_red):
        if n in SHARDED_SMALL:
            width = a[n].shape[-1]
            v = lax.dynamic_slice_in_dim(v, k_chip * width, width, axis=v.ndim - 1)
        red[n] = v.reshape(a[n].shape)

    delta, new_m, new_v = {}, {}, {}
    two_d = lambda t: t.reshape(-1, t.shape[-1])
    for n in big_names:
        d_, m_, v_ = adamw(two_d(a[n]), two_d(red[n]), two_d(a["m_" + n]), two_d(a["v_" + n]), name=f"adamw_{n}")
        delta[n], new_m[n], new_v[n] = (t.reshape(a[n].shape) for t in (d_, m_, v_))
    rest = [n for n in WEIGHT_NAMES if n not in big_names]
    packs = [_small_pack([src[n] for n in rest]) for src in
             (a, red, {n: a["m_" + n] for n in rest}, {n: a["v_" + n] for n in rest})]
    outs = adamw(*[p for p, _ in packs], name="adamw_small")
    for dst, o in zip((delta, new_m, new_v), outs):
        for n, t in zip(rest, _small_unpack(o, packs[0][1])):
            dst[n] = t
    return (loss.reshape(()), grad_x, *[red[n] for n in WEIGHT_NAMES], *[delta[n] for n in WEIGHT_NAMES],
            *[new_m[n] for n in WEIGHT_NAMES], *[new_v[n] for n in WEIGHT_NAMES])


def kernel(x, ffn1_norm, ffn1_w_gu, ffn1_w_down, mix_norm, ffn2_norm, ffn2_w_gu, ffn2_w_down, sb_w_qkv, sb_w_o, ssd_w_in, ssd_conv_w, ssd_conv_b, ssd_dt_bias, ssd_a_log, ssd_d, ssd_norm, ssd_w_out, sc_w_in, sc_conv_w, sc_w_out, final_norm, loss_target, m_ffn1_norm, m_ffn1_w_gu, m_ffn1_w_down, m_mix_norm, m_ffn2_norm, m_ffn2_w_gu, m_ffn2_w_down, m_sb_w_qkv, m_sb_w_o, m_ssd_w_in, m_ssd_conv_w, m_ssd_conv_b, m_ssd_dt_bias, m_ssd_a_log, m_ssd_d, m_ssd_norm, m_ssd_w_out, m_sc_w_in, m_sc_conv_w, m_sc_w_out, m_final_norm, v_ffn1_norm, v_ffn1_w_gu, v_ffn1_w_down, v_mix_norm, v_ffn2_norm, v_ffn2_w_gu, v_ffn2_w_down, v_sb_w_qkv, v_sb_w_o, v_ssd_w_in, v_ssd_conv_w, v_ssd_conv_b, v_ssd_dt_bias, v_ssd_a_log, v_ssd_d, v_ssd_norm, v_ssd_w_out, v_sc_w_in, v_sc_conv_w, v_sc_w_out, v_final_norm):
    args = dict(locals())
    return train_step(args, Cfg())
```

```python
import functools
import math

import jax
import jax.numpy as jnp
from jax import lax
from jax.experimental import pallas as pl
from jax.experimental.pallas import tpu as pltpu

F32 = jnp.float32
CDT = jnp.bfloat16
HI = lax.Precision.HIGHEST
BS = pl.BlockSpec
SDS = jax.ShapeDtypeStruct
MESH = pl.DeviceIdType.MESH

RMS_EPS = 1e-6
ADAM_LR, ADAM_B1, ADAM_B2, ADAM_EPS, ADAM_WD, ADAM_STEP = 0.001, 0.9, 0.999, 1e-08, 0.01, 10
V7X_VMEM_LIMIT_BYTES = 56 * 1024 * 1024
LANES = 128
CHUNK = 128
HEAD_DIM = 64
SSD_HPG = 4
N_CHIPS = 4
N_DEV = 8


class Cfg:
    def __init__(self, d_model=1024, d_ff=2816, depth=4, sb_heads=16, ssd_groups=8, ssd_state=128,
                 ssd_conv=4, sc_width=3):
        self.D, self.F, self.depth = d_model, d_ff, depth
        self.sb_heads = sb_heads
        self.DI = 2 * d_model
        self.G = ssd_groups
        self.NH = self.DI // HEAD_DIM
        assert self.NH == SSD_HPG * self.G and ssd_state == LANES and sb_heads * HEAD_DIM == d_model
        self.conv_dim = self.DI + 2 * self.G * ssd_state
        self.in_dim = self.DI + self.conv_dim + self.NH
        self.ssd_conv, self.sc_width = ssd_conv, sc_width
        self.n_sb, self.n_ssd, self.n_sc = (depth + 2) // 3, (depth + 1) // 3, depth // 3


def _pallas(body, **kw):
    return pl.pallas_call(body, **kw)


def _params(*sem):
    return pltpu.CompilerParams(dimension_semantics=sem, vmem_limit_bytes=V7X_VMEM_LIMIT_BYTES)


def _tile(n, cap):
    if n <= cap:
        return n
    best = None
    for m in range(1, n // LANES + 1):
        t = m * LANES
        if n % t == 0 and t <= cap:
            best = t
    assert best is not None, (n, cap)
    return best


def _sigmoid(x):
    return 1.0 / (1.0 + jnp.exp(-x))


def _softplus_neg_abs(x):
    return jnp.log(1.0 + jnp.exp(-jnp.abs(x)))


def _colsel(blk, idx):
    lane = lax.broadcasted_iota(jnp.int32, blk.shape, 1)
    return jnp.sum(jnp.where(lane == idx, blk, 0.0), axis=1, keepdims=True)


def _rowsel(blk, idx):
    sub = lax.broadcasted_iota(jnp.int32, blk.shape, 0)
    return jnp.sum(jnp.where(sub == idx, blk, 0.0), axis=0, keepdims=True)


def _nt(a, b):
    return lax.dot_general(a, b, (((1,), (1,)), ((), ())), preferred_element_type=F32)


def _tn(a, b):
    return lax.dot_general(a, b, (((0,), (0,)), ((), ())), preferred_element_type=F32)


def _nn(a, b):
    return jnp.dot(a, b, preferred_element_type=F32)


def matmul(a, b, *, name, ta=False, tb=False, out_dtype=F32, res=None, scale=1.0, tm=1408, tn=1408, tk=1408):
    M, K = (a.shape[1], a.shape[0]) if ta else a.shape
    N, K2 = b.shape if tb else (b.shape[1], b.shape[0])
    assert K == K2, (a.shape, b.shape, ta, tb)
    tm, tn, tk = _tile(M, tm), _tile(N, tn), _tile(K, tk)
    nk = K // tk
    dn = (((0 if ta else 1,), (1 if tb else 0,)), ((), ()))

    def body(*refs):
        a_ref, b_ref = refs[:2]
        r_ref = refs[2] if res is not None else None
        o_ref = refs[3 if res is not None else 2]
        part = lax.dot_general(a_ref[...].astype(CDT), b_ref[...].astype(CDT), dn, preferred_element_type=F32)

        def finish(total):
            o = total * scale
            if res is not None:
                o = o + r_ref[...].astype(F32)
            o_ref[...] = o.astype(o_ref.dtype)

        if nk == 1:
            finish(part)
            return
        acc = refs[-1]
        k = pl.program_id(2)

        @pl.when(k == 0)
        def _():
            acc[...] = part

        @pl.when(jnp.logical_and(k > 0, k < nk - 1))
        def _():
            acc[...] += part

        @pl.when(k == nk - 1)
        def _():
            finish(acc[...] + part)

    a_spec = BS((tk, tm), lambda i, j, k: (k, i)) if ta else BS((tm, tk), lambda i, j, k: (i, k))
    b_spec = BS((tn, tk), lambda i, j, k: (j, k)) if tb else BS((tk, tn), lambda i, j, k: (k, j))
    o_spec = BS((tm, tn), lambda i, j, k: (i, j))
    ins, specs = [a, b], [a_spec, b_spec]
    if res is not None:
        ins.append(res)
        specs.append(o_spec)
    return _pallas(body, grid=(M // tm, N // tn, nk), in_specs=specs, out_specs=o_spec,
                   out_shape=SDS((M, N), out_dtype),
                   scratch_shapes=[pltpu.VMEM((tm, tn), F32)] if nk > 1 else [],
                   compiler_params=_params("parallel", "parallel", "arbitrary"), name=name)(*ins)


def rms_fwd(x, g, *, name):
    T, D = x.shape
    tm = _tile(T, 512)

    def body(x_ref, g_ref, o_ref):
        xv = x_ref[...]
        r = lax.rsqrt(jnp.mean(xv * xv, axis=-1, keepdims=True) + RMS_EPS)
        o_ref[...] = (xv * r * g_ref[...]).astype(o_ref.dtype)

    return _pallas(body, grid=(T // tm,), in_specs=[BS((tm, D), lambda i: (i, 0)), BS((1, D), lambda i: (0, 0))],
                   out_specs=BS((tm, D), lambda i: (i, 0)), out_shape=SDS((T, D), CDT),
                   compiler_params=_params("parallel"), name=name)(x, g)


def rms_bwd(dh, x, g, dres, *, name):
    T, D = x.shape
    tm = _tile(T, 512)

    def body(dh_ref, x_ref, g_ref, r_ref, dx_ref, dg_ref):
        i = pl.program_id(0)
        xv = x_ref[...]
        r = lax.rsqrt(jnp.mean(xv * xv, axis=-1, keepdims=True) + RMS_EPS)
        xhat = xv * r
        dhv = dh_ref[...].astype(F32)
        dxh = dhv * g_ref[...]
        m = jnp.mean(dxh * xhat, axis=-1, keepdims=True)
        dx_ref[...] = r * (dxh - xhat * m) + r_ref[...]

        @pl.when(i == 0)
        def _():
            dg_ref[...] = jnp.zeros_like(dg_ref)

        dg_ref[...] += jnp.sum(dhv * xhat, axis=0, keepdims=True)

    row = BS((tm, D), lambda i: (i, 0))
    vec = BS((1, D), lambda i: (0, 0))
    return _pallas(body, grid=(T // tm,), in_specs=[row, row, vec, row], out_specs=[row, vec],
                   out_shape=[SDS((T, D), F32), SDS((1, D), F32)],
                   compiler_params=_params("arbitrary"), name=name)(dh, x, g, dres)


def swiglu_fwd(gu, *, name):
    T, F2 = gu.shape
    F = F2 // 2
    tm = _tile(T, 512)

    def body(gu_ref, o_ref):
        g = gu_ref[:, :F].astype(F32)
        u = gu_ref[:, F:].astype(F32)
        o_ref[...] = (g * _sigmoid(g) * u).astype(o_ref.dtype)

    return _pallas(body, grid=(T // tm,), in_specs=[BS((tm, F2), lambda i: (i, 0))],
                   out_specs=BS((tm, F), lambda i: (i, 0)), out_shape=SDS((T, F), CDT),
                   compiler_params=_params("parallel"), name=name)(gu)


def swiglu_bwd(da, gu, *, name):
    T, F2 = gu.shape
    F = F2 // 2
    tm = _tile(T, 512)

    def body(da_ref, gu_ref, o_ref):
        g = gu_ref[:, :F].astype(F32)
        u = gu_ref[:, F:].astype(F32)
        dav = da_ref[...].astype(F32)
        s = _sigmoid(g)
        o_ref[:, :F] = (dav * u * (s * (1.0 + g * (1.0 - s)))).astype(o_ref.dtype)
        o_ref[:, F:] = (dav * (g * s)).astype(o_ref.dtype)

    return _pallas(body, grid=(T // tm,), in_specs=[BS((tm, F), lambda i: (i, 0)), BS((tm, F2), lambda i: (i, 0))],
                   out_specs=BS((tm, F2), lambda i: (i, 0)), out_shape=SDS((T, F2), CDT),
                   compiler_params=_params("parallel"), name=name)(da, gu)


def loss_head(x, g, tgt, *, name):
    T, D = x.shape
    tm = _tile(T, 512)

    def body(x_ref, g_ref, t_ref, l_ref, dx_ref, dg_ref):
        i = pl.program_id(0)
        xv = x_ref[...]
        gv = g_ref[...]
        r = lax.rsqrt(jnp.mean(xv * xv, axis=-1, keepdims=True) + RMS_EPS)
        xhat = xv * r
        err = xhat * gv - t_ref[...]
        part = 0.5 * jnp.sum(jnp.mean(err * err, axis=-1, keepdims=True), axis=0, keepdims=True)
        dy = err * (1.0 / D)
        dxh = dy * gv
        m = jnp.mean(dxh * xhat, axis=-1, keepdims=True)
        dx_ref[...] = r * (dxh - xhat * m)

        @pl.when(i == 0)
        def _():
            dg_ref[...] = jnp.zeros_like(dg_ref)
            l_ref[...] = jnp.zeros_like(l_ref)

        dg_ref[...] += jnp.sum(dy * xhat, axis=0, keepdims=True)
        l_ref[...] += jnp.broadcast_to(part, l_ref.shape)

    row = BS((tm, D), lambda i: (i, 0))
    vec = BS((1, D), lambda i: (0, 0))
    return _pallas(body, grid=(T // tm,), in_specs=[row, vec, row],
                   out_specs=[BS((1, LANES), lambda i: (0, 0)), row, vec],
                   out_shape=[SDS((1, LANES), F32), SDS((T, D), F32), SDS((1, D), F32)],
                   compiler_params=_params("arbitrary"), name=name)(x, g, tgt)


def adamw(w, g, m, v, *, name):
    R, C = w.shape
    tr = R
    if R * C * 4 > (1 << 20):
        for cand in range(8, R + 1, 8):
            if R % cand == 0 and cand * C * 4 <= (1 << 20):
                tr = cand

    def body(w_ref, g_ref, m_ref, v_ref, d_ref, nm_ref, nv_ref):
        gv = g_ref[...]
        nm = ADAM_B1 * m_ref[...] + (1.0 - ADAM_B1) * gv
        nv = ADAM_B2 * v_ref[...] + (1.0 - ADAM_B2) * (gv * gv)
        m_hat = nm / (1.0 - ADAM_B1 ** ADAM_STEP)
        v_hat = nv / (1.0 - ADAM_B2 ** ADAM_STEP)
        d_ref[...] = -ADAM_LR * (m_hat / (jnp.sqrt(v_hat) + ADAM_EPS) + ADAM_WD * w_ref[...])
        nm_ref[...] = nm
        nv_ref[...] = nv

    blk = BS((tr, C), lambda i: (i, 0))
    return _pallas(body, grid=(R // tr,), in_specs=[blk] * 4, out_specs=[blk] * 3,
                   out_shape=[SDS((R, C), F32)] * 3, compiler_params=_params("parallel"), name=name)(w, g, m, v)


def _sb_tri():
    row = lax.broadcasted_iota(jnp.int32, (CHUNK, CHUNK), 0)
    col = lax.broadcasted_iota(jnp.int32, (CHUNK, CHUNK), 1)
    return row, col


def _tri3(keep):
    t = keep.astype(jnp.bfloat16)
    return jnp.concatenate([t, t, t], axis=0)


def _split_dot(x, tri3):
    hi = x.astype(jnp.bfloat16)
    r = x - hi.astype(F32)
    mid = r.astype(jnp.bfloat16)
    lo = (r - mid.astype(F32)).astype(jnp.bfloat16)
    return jnp.dot(jnp.concatenate([hi, mid, lo], axis=1), tri3, preferred_element_type=F32)


def _sb_heads_per_step(H, want):
    hb = min(H, want)
    assert H % hb == 0, (H, hb)
    return hb


SB_FWD_HEADS = 8
SB_BWD_HEADS = 8


def _head_pair_masked(blk, lo):
    zero = jnp.zeros_like(blk)
    return jnp.where(lo, blk, zero), jnp.where(lo, zero, blk)


def sb_fwd(qkv, n_heads, *, name):
    T, D3 = qkv.shape
    D = D3 // 3
    nq = T // CHUNK
    assert nq <= LANES
    scale = HEAD_DIM ** -0.5
    hb = _sb_heads_per_step(n_heads, SB_FWD_HEADS)
    assert hb % 2 == 0
    LW = hb * HEAD_DIM
    ngrp = D // LW
    heads, pairs = range(hb), range(hb // 2)
    lanes = lambda p: slice(p * LANES, (p + 1) * LANES)

    def body(q_ref, k_ref, v_ref, o_ref, tab_ref, acc_scr, cs_scr, qm_scr):
        qi = pl.program_id(1)
        row, col = _sb_tri()
        lo = col < HEAD_DIM
        upper3 = _tri3(row > col)
        acc_scr[...] = jnp.zeros_like(acc_scr)
        cs_scr[...] = jnp.zeros_like(cs_scr)
        tab_ref[...] = jnp.zeros_like(tab_ref)
        for p in pairs:
            qm_scr[2 * p], qm_scr[2 * p + 1] = _head_pair_masked(q_ref[:, lanes(p)] * scale, lo)

        def tile(j, diag):
            sl = pl.ds(pl.multiple_of(j * CHUNK, CHUNK), CHUNK)
            keep = (lambda t: jnp.where(col < row, t, 0.0)) if diag else (lambda t: t)
            kbs = [k_ref[sl, lanes(p)] for p in pairs]
            zs = [_nt(qm_scr[h], kbs[h // 2]) for h in heads]
            lszs = [jnp.minimum(z, 0.0) - _softplus_neg_abs(z) for z in zs]
            lks = [keep(lsz - z) for lsz, z in zip(lszs, zs)]
            tls = [_split_dot(lk, upper3) for lk in lks]
            css = [cs_scr[h] for h in heads]
            atts = [keep(jnp.exp(lsz + tl + cs)).astype(CDT) for lsz, tl, cs in zip(lszs, tls, css)]
            vms = [jnp.concatenate(_head_pair_masked(v_ref[sl, lanes(p)], lo), axis=0) for p in pairs]
            pvs = [_nn(jnp.concatenate([atts[2 * p], atts[2 * p + 1]], axis=1), vms[p]) for p in pairs]
            for p in pairs:
                acc_scr[:, lanes(p)] += pvs[p]
            for h in heads:
                tab_ref[h, 0] = jnp.where(col == j, css[h], tab_ref[h, 0])
                cs_scr[h] = css[h] + (tls[h][:, :1] + lks[h][:, :1])

        tile(qi, True)

        def step(jj, carry):
            tile(qi - jj, False)
            return carry

        lax.fori_loop(1, qi + 1, step, 0)
        o_ref[...] = acc_scr[...].astype(o_ref.dtype)

    once = pl.Buffered(1)
    qs = BS((CHUNK, LW), lambda g, i: (i, g))
    return _pallas(body, grid=(ngrp, nq),
                   in_specs=[qs, BS((T, LW), lambda g, i: (0, ngrp + g), pipeline_mode=once),
                             BS((T, LW), lambda g, i: (0, 2 * ngrp + g), pipeline_mode=once)],
                   out_specs=[qs, BS((hb, 1, CHUNK, CHUNK), lambda g, i: (g, i, 0, 0))],
                   out_shape=[SDS((T, D), CDT), SDS((n_heads, nq, CHUNK, CHUNK), F32)],
                   scratch_shapes=[pltpu.VMEM((CHUNK, LW), F32), pltpu.VMEM((hb, CHUNK, 1), F32),
                                   pltpu.VMEM((hb, CHUNK, LANES), CDT)],
                   compiler_params=_params("parallel", "arbitrary"), name=name)(qkv, qkv, qkv)


def sb_bwd(qkv, do, tab, n_heads, *, name):
    T, D3 = qkv.shape
    D = D3 // 3
    nq = T // CHUNK
    scale = HEAD_DIM ** -0.5
    hb = _sb_heads_per_step(n_heads, SB_BWD_HEADS)
    assert hb % 2 == 0
    LW = hb * HEAD_DIM
    ngrp = D // LW
    heads, pairs = range(hb), range(hb // 2)
    lanes = lambda p: slice(p * LANES, (p + 1) * LANES)

    def body(q_ref, k_ref, v_ref, do_ref, tab_ref, dq_ref, dk_ref, dv_ref,
             dk_acc, dv_acc, dq_scr, pfx_scr, qm_scr, dom_scr):
        qi = pl.program_id(1)

        @pl.when(qi == 0)
        def _():
            dk_acc[...] = jnp.zeros_like(dk_acc)
            dv_acc[...] = jnp.zeros_like(dv_acc)

        dq_scr[...] = jnp.zeros_like(dq_scr)
        pfx_scr[...] = jnp.zeros_like(pfx_scr)
        row, col = _sb_tri()
        lo = col < HEAD_DIM
        upper3 = _tri3(row > col)
        before3 = _tri3(row < col)
        for p in pairs:
            qm_scr[2 * p], qm_scr[2 * p + 1] = _head_pair_masked(q_ref[:, lanes(p)] * scale, lo)
            dom_scr[2 * p], dom_scr[2 * p + 1] = _head_pair_masked(do_ref[:, lanes(p)], lo)

        def tile(j, diag):
            sl = pl.ds(pl.multiple_of(j * CHUNK, CHUNK), CHUNK)
            keep = (lambda t: jnp.where(col < row, t, 0.0)) if diag else (lambda t: t)
            kbs = [k_ref[sl, lanes(p)] for p in pairs]
            vbs = [v_ref[sl, lanes(p)] for p in pairs]
            zs = [_nt(qm_scr[h], kbs[h // 2]) for h in heads]
            das = [_nt(dom_scr[h], vbs[h // 2]) for h in heads]
            lszs = [jnp.minimum(z, 0.0) - _softplus_neg_abs(z) for z in zs]
            lks = [keep(lsz - z) for lsz, z in zip(lszs, zs)]
            tls = [_split_dot(lk, upper3) for lk in lks]
            css = [jnp.sum(jnp.where(col == j, tab_ref[h, 0], 0.0), axis=1, keepdims=True) for h in heads]
            atts = [keep(jnp.exp(lsz + tl + cs)) for lsz, tl, cs in zip(lszs, tls, css)]
            dls = [att * da for att, da in zip(atts, das)]
            pres = [_split_dot(dl, before3) for dl in dls]
            pfxs = [pfx_scr[h] for h in heads]
            sigs = [jnp.exp(lsz) for lsz in lszs]
            dzs = [(dl * (1.0 - sig) - keep(sig * (pre + pfx))).astype(CDT)
                   for dl, sig, pre, pfx in zip(dls, sigs, pres, pfxs)]
            attc = [att.astype(CDT) for att in atts]
            kms = [jnp.concatenate(_head_pair_masked(kb, lo), axis=0) for kb in kbs]
            dqs = [_nn(jnp.concatenate([dzs[2 * p], dzs[2 * p + 1]], axis=1), kms[p]) for p in pairs]
            both = lambda xs, p: jnp.concatenate([xs[2 * p], xs[2 * p + 1]], axis=0)
            dks = [_tn(both(dzs, p), jnp.concatenate([qm_scr[2 * p], qm_scr[2 * p + 1]], axis=0)) for p in pairs]
            dvs = [_tn(both(attc, p), jnp.concatenate([dom_scr[2 * p], dom_scr[2 * p + 1]], axis=0)) for p in pairs]
            last = CHUNK - 1
            for p in pairs:
                dq_scr[:, lanes(p)] += dqs[p]
                dk_acc[sl, lanes(p)] += dks[p]
                dv_acc[sl, lanes(p)] += dvs[p]
            for h in heads:
                pfx_scr[h] = pfxs[h] + (pres[h][:, last:] + dls[h][:, last:])

        def step(j, carry):
            tile(j, False)
            return carry

        lax.fori_loop(0, qi, step, 0)
        tile(qi, True)
        dq_ref[...] = (dq_scr[...] * scale).astype(dq_ref.dtype)

        @pl.when(qi == nq - 1)
        def _():
            dk_ref[...] = dk_acc[...].astype(dk_ref.dtype)
            dv_ref[...] = dv_acc[...].astype(dv_ref.dtype)

    once = pl.Buffered(1)
    qs = BS((CHUNK, LW), lambda g, i: (i, g))
    ks = BS((T, LW), lambda g, i: (0, g))
    return _pallas(body, grid=(ngrp, nq),
                   in_specs=[qs, BS((T, LW), lambda g, i: (0, ngrp + g), pipeline_mode=once),
                             BS((T, LW), lambda g, i: (0, 2 * ngrp + g), pipeline_mode=once), qs,
                             BS((hb, 1, CHUNK, CHUNK), lambda g, i: (g, i, 0, 0))],
                   out_specs=[qs, ks, ks],
                   out_shape=[SDS((T, D), CDT)] * 3,
                   scratch_shapes=[pltpu.VMEM((T, LW), F32), pltpu.VMEM((T, LW), F32),
                                   pltpu.VMEM((CHUNK, LW), F32), pltpu.VMEM((hb, CHUNK, 1), F32),
                                   pltpu.VMEM((hb, CHUNK, LANES), CDT), pltpu.VMEM((hb, CHUNK, LANES), CDT)],
                   compiler_params=_params("parallel", "arbitrary"), name=name)(qkv, qkv, qkv, do, tab)


HALO = 8


def _shift_down(cur, prev8, s):
    if s == 0:
        return cur
    tt = cur.shape[0]
    row = lax.broadcasted_iota(jnp.int32, cur.shape, 0)
    fix = jnp.concatenate([pltpu.roll(prev8, s, 0), jnp.zeros((tt - HALO, cur.shape[1]), cur.dtype)], axis=0)
    return jnp.where(row < s, fix, pltpu.roll(cur, s, 0))


def _shift_up(cur, next8, s):
    if s == 0:
        return cur
    tt = cur.shape[0]
    row = lax.broadcasted_iota(jnp.int32, cur.shape, 0)
    fix = jnp.concatenate([jnp.zeros((tt - HALO, cur.shape[1]), cur.dtype), pltpu.roll(next8, HALO - s, 0)], axis=0)
    return jnp.where(row >= tt - s, fix, pltpu.roll(cur, tt - s, 0))


def _halo_specs(tt, width, T):
    per = tt // HALO
    last = T // HALO - 1
    cur = BS((tt, width), lambda i: (i, 0))
    prev = BS((HALO, width), lambda i: (jnp.maximum(i * per - 1, 0), 0))
    nxt = BS((HALO, width), lambda i: (jnp.minimum((i + 1) * per, last), 0))
    return cur, prev, nxt


def sc_fwd(proj, w, *, name):
    T, D3 = proj.shape
    D = D3 // 3
    K = w.shape[0]
    tt = _tile(T, 256)
    cur, prev, _ = _halo_specs(tt, D3, T)

    def body(p_ref, pp_ref, w_ref, o_ref):
        i = pl.program_id(0)
        pc = p_ref[:, D:2 * D] * p_ref[:, 2 * D:]
        pp = jnp.where(i > 0, pp_ref[:, D:2 * D] * pp_ref[:, 2 * D:], 0.0)
        u = jnp.zeros((tt, D), F32)
        for kk in range(K):
            u = u + w_ref[kk:kk + 1, :] * _shift_down(pc, pp, K - 1 - kk)
        o_ref[...] = (p_ref[:, :D] * u).astype(o_ref.dtype)

    return _pallas(body, grid=(T // tt,), in_specs=[cur, prev, BS((K, D), lambda i: (0, 0))],
                   out_specs=BS((tt, D), lambda i: (i, 0)), out_shape=SDS((T, D), CDT),
                   compiler_params=_params("parallel"), name=name)(proj, proj, w)


def sc_bwd(dm, proj, w, *, name):
    T, D3 = proj.shape
    D = D3 // 3
    K = w.shape[0]
    tt = _tile(T, 256)
    cur, prev, nxt = _halo_specs(tt, D3, T)
    dcur, _, dnxt = _halo_specs(tt, D, T)
    nsteps = T // tt

    def body(dm_ref, dmn_ref, p_ref, pp_ref, pn_ref, w_ref, o_ref, dw_ref):
        i = pl.program_id(0)
        b = p_ref[:, :D]
        c = p_ref[:, D:2 * D]
        h = p_ref[:, 2 * D:]
        pc = c * h
        pp = jnp.where(i > 0, pp_ref[:, D:2 * D] * pp_ref[:, 2 * D:], 0.0)
        dmv = dm_ref[...]
        du = dmv * b
        dun = jnp.where(i < nsteps - 1, dmn_ref[...] * pn_ref[:, :D], 0.0)
        u = jnp.zeros((tt, D), F32)
        dp = jnp.zeros((tt, D), F32)
        rows = []
        for kk in range(K):
            sh = _shift_down(pc, pp, K - 1 - kk)
            u = u + w_ref[kk:kk + 1, :] * sh
            dp = dp + w_ref[kk:kk + 1, :] * _shift_up(du, dun, K - 1 - kk)
            rows.append(jnp.sum(du * sh, axis=0, keepdims=True))
        rows.append(jnp.zeros((HALO - K, D), F32))
        o_ref[:, :D] = (dmv * u).astype(o_ref.dtype)
        o_ref[:, D:2 * D] = (dp * h).astype(o_ref.dtype)
        o_ref[:, 2 * D:] = (dp * c).astype(o_ref.dtype)

        @pl.when(i == 0)
        def _():
            dw_ref[...] = jnp.zeros_like(dw_ref)

        dw_ref[...] += jnp.concatenate(rows, axis=0)

    return _pallas(body, grid=(nsteps,), in_specs=[dcur, dnxt, cur, prev, nxt, BS((K, D), lambda i: (0, 0))],
                   out_specs=[BS((tt, D3), lambda i: (i, 0)), BS((HALO, D), lambda i: (0, 0))],
                   out_shape=[SDS((T, D3), CDT), SDS((HALO, D), F32)],
                   compiler_params=_params("arbitrary"), name=name)(dm, dm, proj, proj, proj, w)


def ssd_conv_fwd(xpre, w, b, *, name):
    T, C = xpre.shape
    K = w.shape[0]
    tt = _tile(T, 256)
    cur, prev, _ = _halo_specs(tt, C, T)

    def body(x_ref, xp_ref, w_ref, b_ref, pre_ref, o_ref):
        i = pl.program_id(0)
        xv = x_ref[...]
        xp = jnp.where(i > 0, xp_ref[...], 0.0)
        acc = jnp.broadcast_to(b_ref[...], (tt, C))
        for kk in range(K):
            acc = acc + w_ref[kk:kk + 1, :] * _shift_down(xv, xp, K - 1 - kk)
        pre_ref[...] = acc
        o_ref[...] = acc * _sigmoid(acc)

    row = BS((tt, C), lambda i: (i, 0))
    return _pallas(body, grid=(T // tt,), in_specs=[cur, prev, BS((K, C), lambda i: (0, 0)), BS((1, C), lambda i: (0, 0))],
                   out_specs=[row, row], out_shape=[SDS((T, C), F32)] * 2,
                   compiler_params=_params("parallel"), name=name)(xpre, xpre, w, b)


def ssd_conv_bwd(dxs, db_, dc_, pre, xpre, w, *, name):
    T, C = xpre.shape
    K = w.shape[0]
    tt = _tile(T, 256)
    cur, prev, nxt = _halo_specs(tt, C, T)
    widths = (dxs.shape[1], db_.shape[1], dc_.shape[1])
    gspecs = []
    for wd in widths:
        c_, _, n_ = _halo_specs(tt, wd, T)
        gspecs += [c_, n_]
    nsteps = T // tt

    def dsilu(p):
        s = _sigmoid(p)
        return s * (1.0 + p * (1.0 - s))

    def body(g0, g0n, g1, g1n, g2, g2n, pre_ref, pren_ref, x_ref, xp_ref, w_ref, dx_ref, dw_ref, dbias_ref):
        i = pl.program_id(0)
        gcur = jnp.concatenate([g0[...], g1[...], g2[...]], axis=1)
        gnxt = jnp.concatenate([g0n[...], g1n[...], g2n[...]], axis=1)
        dpre = gcur * dsilu(pre_ref[...])
        dpren = jnp.where(i < nsteps - 1, gnxt * dsilu(pren_ref[...]), 0.0)
        xv = x_ref[...]
        xp = jnp.where(i > 0, xp_ref[...], 0.0)
        dx = jnp.zeros((tt, C), F32)
        rows = []
        for kk in range(K):
            dx = dx + w_ref[kk:kk + 1, :] * _shift_up(dpre, dpren, K - 1 - kk)
            rows.append(jnp.sum(dpre * _shift_down(xv, xp, K - 1 - kk), axis=0, keepdims=True))
        rows.append(jnp.zeros((HALO - K, C), F32))
        dx_ref[...] = dx.astype(dx_ref.dtype)

        @pl.when(i == 0)
        def _():
            dw_ref[...] = jnp.zeros_like(dw_ref)
            dbias_ref[...] = jnp.zeros_like(dbias_ref)

        dw_ref[...] += jnp.concatenate(rows, axis=0)
        dbias_ref[...] += jnp.sum(dpre, axis=0, keepdims=True)

    return _pallas(body, grid=(nsteps,),
                   in_specs=gspecs + [cur, nxt, cur, prev, BS((K, C), lambda i: (0, 0))],
                   out_specs=[BS((tt, C), lambda i: (i, 0)), BS((HALO, C), lambda i: (0, 0)), BS((1, C), lambda i: (0, 0))],
                   out_shape=[SDS((T, C), CDT), SDS((HALO, C), F32), SDS((1, C), F32)],
                   compiler_params=_params("arbitrary"), name=name)(dxs, dxs, db_, db_, dc_, dc_, pre, pre, xpre, xpre, w)


def ssd_prep_fwd(dtpre, bias, alog, *, name):
    T = dtpre.shape[0]
    nc = T // CHUNK

    def body(p_ref, b_ref, a_ref, dt_ref, ac_ref, dtT_ref, acT_ref):
        xv = p_ref[...] + b_ref[...]
        dt = jnp.maximum(xv, 0.0) + _softplus_neg_abs(xv)
        row, col = _sb_tri()
        lower = (col <= row).astype(F32)
        ac = jnp.dot(lower, dt * -jnp.exp(a_ref[...]), precision=HI, preferred_element_type=F32)
        dt_ref[...] = dt
        ac_ref[...] = ac
        dtT_ref[...] = dt.T
        acT_ref[...] = ac.T

    blk = BS((CHUNK, LANES), lambda c: (c, 0))
    blkT = BS((LANES, CHUNK), lambda c: (0, c))
    vec = BS((1, LANES), lambda c: (0, 0))
    return _pallas(body, grid=(nc,), in_specs=[blk, vec, vec], out_specs=[blk, blk, blkT, blkT],
                   out_shape=[SDS((T, LANES), F32)] * 2 + [SDS((LANES, T), F32)] * 2,
                   compiler_params=_params("parallel"), name=name)(dtpre, bias, alog)


def ssd_prep_bwd(dac_c, ddt_c, dac_r, ddt_r, dtpre, bias, alog, *, name):
    T = dtpre.shape[0]
    nc = T // CHUNK

    def body(dac_ref, ddt_ref, dacr_ref, ddtr_ref, p_ref, b_ref, a_ref, o_ref, da_ref, db_ref):
        c = pl.program_id(0)
        xv = p_ref[...] + b_ref[...]
        dt = jnp.maximum(xv, 0.0) + _softplus_neg_abs(xv)
        av = -jnp.exp(a_ref[...])
        row, col = _sb_tri()
        after = (col >= row).astype(F32)
        dda = jnp.dot(after, dac_ref[...] + dacr_ref[...].T, precision=HI, preferred_element_type=F32)
        dpre = (ddt_ref[...] + ddtr_ref[...].T + av * dda) * _sigmoid(xv)
        o_ref[...] = dpre

        @pl.when(c == 0)
        def _():
            da_ref[...] = jnp.zeros_like(da_ref)
            db_ref[...] = jnp.zeros_like(db_ref)

        da_ref[...] += jnp.sum(dt * dda, axis=0, keepdims=True) * av
        db_ref[...] += jnp.sum(dpre, axis=0, keepdims=True)

    blk = BS((CHUNK, LANES), lambda c: (c, 0))
    blkT = BS((LANES, CHUNK), lambda c: (0, c))
    vec = BS((1, LANES), lambda c: (0, 0))
    return _pallas(body, grid=(nc,), in_specs=[blk, blk, blkT, blkT, blk, vec, vec], out_specs=[blk, vec, vec],
                   out_shape=[SDS((T, LANES), F32), SDS((1, LANES), F32), SDS((1, LANES), F32)],
                   compiler_params=_params("arbitrary"), name=name)(dac_c, ddt_c, dac_r, ddt_r, dtpre, bias, alog)


def _ssd_specs(cfg, T, rev):
    nc = T // CHUNK
    GW = SSD_HPG * HEAD_DIM
    bo = cfg.DI // LANES
    co = (cfg.DI + cfg.G * LANES) // LANES
    ci = (lambda c: nc - 1 - c) if rev else (lambda c: c)
    return dict(
        x=BS((CHUNK, GW), lambda g, c: (ci(c), g)),
        b=BS((CHUNK, LANES), lambda g, c: (ci(c), bo + g)),
        c=BS((CHUNK, LANES), lambda g, c: (ci(c), co + g)),
        col=BS((CHUNK, LANES), lambda g, c: (ci(c), 0)),
        row=BS((LANES, CHUNK), lambda g, c: (0, ci(c))),
        vec=BS((1, LANES), lambda g, c: (0, 0)),
        ng=BS((1, GW), lambda g, c: (0, g)),
        hp=BS((1, 1, SSD_HPG, HEAD_DIM, LANES), lambda g, c: (g, ci(c), 0, 0, 0)),
    )


def ssd_chunk_fwd(xbc, z, dt, ac, dtT, acT, dskip, ng, cfg, *, name):
    T = xbc.shape[0]
    nc = T // CHUNK
    G, DI = cfg.G, cfg.DI
    GW = SSD_HPG * HEAD_DIM
    sp = _ssd_specs(cfg, T, False)

    def body(x_ref, b_ref, c_ref, z_ref, dt_ref, ac_ref, dtT_ref, acT_ref, d_ref, ng_ref,
             y_ref, yn_ref, hp_ref, h_scr):
        g = pl.program_id(0)
        c = pl.program_id(1)

        @pl.when(c == 0)
        def _():
            h_scr[...] = jnp.zeros_like(h_scr)

        row, col = _sb_tri()
        causal = row >= col
        Bm = b_ref[...].astype(CDT)
        Cm = c_ref[...].astype(CDT)
        Gm = _nt(Cm, Bm)
        dtb, acb, dtTb, acTb, dv = dt_ref[...], ac_ref[...], dtT_ref[...], acT_ref[...], d_ref[...]
        heads = range(SSD_HPG)
        hhs = [g * SSD_HPG + r for r in heads]
        ac_cs = [_colsel(acb, hh) for hh in hhs]
        dt_cs = [_colsel(dtb, hh) for hh in hhs]
        Xs = [x_ref[:, r * HEAD_DIM:(r + 1) * HEAD_DIM] for r in heads]
        Hps = [h_scr[r] for r in heads]
        Ws = [(Gm * jnp.where(causal, jnp.exp(ac_c - _rowsel(acTb, hh)), 0.0) * _rowsel(dtTb, hh)).astype(CDT)
              for ac_c, hh in zip(ac_cs, hhs)]
        Qs = [_nt(Cm, Hp.astype(CDT)) for Hp in Hps]
        Yds = [_nn(Wm, Xr.astype(CDT)) for Wm, Xr in zip(Ws, Xs)]
        aLs = [_rowsel(ac_c, CHUNK - 1) for ac_c in ac_cs]
        Xes = [(Xr * (jnp.exp(aL - ac_c) * dt_c)).astype(CDT) for Xr, aL, ac_c, dt_c in zip(Xs, aLs, ac_cs, dt_cs)]
        Sts = [_tn(Xe, Bm) for Xe in Xes]
        for r in heads:
            hp_ref[0, 0, r] = Hps[r]
            h_scr[r] = jnp.exp(aLs[r]) * Hps[r] + Sts[r]
            y_ref[:, r * HEAD_DIM:(r + 1) * HEAD_DIM] = (Yds[r] + jnp.exp(ac_cs[r]) * Qs[r]
                                                         + _colsel(dv, hhs[r]) * Xs[r])
        zz = z_ref[...]
        yz = y_ref[...] * (zz * _sigmoid(zz))
        rstd = lax.rsqrt(jnp.mean(yz * yz, axis=-1, keepdims=True) + RMS_EPS)
        yn_ref[...] = (yz * rstd * ng_ref[...]).astype(yn_ref.dtype)

    return _pallas(body, grid=(G, nc),
                   in_specs=[sp["x"], sp["b"], sp["c"], sp["x"], sp["col"], sp["col"], sp["row"], sp["row"], sp["vec"], sp["ng"]],
                   out_specs=[sp["x"], sp["x"], sp["hp"]],
                   out_shape=[SDS((T, DI), F32), SDS((T, DI), CDT), SDS((G, nc, SSD_HPG, HEAD_DIM, LANES), F32)],
                   scratch_shapes=[pltpu.VMEM((SSD_HPG, HEAD_DIM, LANES), F32)],
                   compiler_params=_params("parallel", "arbitrary"), name=name)(xbc, xbc, xbc, z, dt, ac, dtT, acT, dskip, ng)


def ssd_chunk_bwd(dyn, xbc, z, y, hp, dt, ac, dtT, acT, dskip, ng, cfg, *, name):
    T = xbc.shape[0]
    nc = T // CHUNK
    G, DI = cfg.G, cfg.DI
    GW = SSD_HPG * HEAD_DIM
    sp = _ssd_specs(cfg, T, True)

    def body(dyn_ref, x_ref, b_ref, c_ref, z_ref, y_ref, hp_ref, dt_ref, ac_ref, dtT_ref, acT_ref, d_ref, ng_ref,
             dx_ref, db_ref, dc_ref, dz_ref, colp_ref, rowp_ref, dd_ref, dng_ref, dh_scr):
        g = pl.program_id(0)
        c = pl.program_id(1)

        @pl.when(c == 0)
        def _():
            dh_scr[...] = jnp.zeros_like(dh_scr)
            dd_ref[...] = jnp.zeros_like(dd_ref)
            dng_ref[...] = jnp.zeros_like(dng_ref)

        zz = z_ref[...]
        sg = _sigmoid(zz)
        gate = zz * sg
        yv = y_ref[...]
        yz = yv * gate
        rstd = lax.rsqrt(jnp.mean(yz * yz, axis=-1, keepdims=True) + RMS_EPS)
        nhat = yz * rstd
        dynv = dyn_ref[...].astype(F32)
        dng_ref[...] += jnp.sum(dynv * nhat, axis=0, keepdims=True)
        dnh = dynv * ng_ref[...]
        dyz = rstd * (dnh - nhat * jnp.mean(dnh * nhat, axis=-1, keepdims=True))
        dz_ref[...] = (dyz * yv * (sg * (1.0 + zz * (1.0 - sg)))).astype(dz_ref.dtype)
        dy_all = dyz * gate

        row, col = _sb_tri()
        causal = row >= col
        lane1 = lax.broadcasted_iota(jnp.int32, (1, LANES), 1)
        sub8 = lax.broadcasted_iota(jnp.int32, (HALO, CHUNK), 0)
        subc = lax.broadcasted_iota(jnp.int32, (CHUNK, 1), 0)
        Bf = b_ref[...]
        Cf = c_ref[...]
        Bm = Bf.astype(CDT)
        Cm = Cf.astype(CDT)
        Gm = _nt(Cm, Bm)
        dtb, acb, dtTb, acTb, dv = dt_ref[...], ac_ref[...], dtT_ref[...], acT_ref[...], d_ref[...]
        dG = jnp.zeros((CHUNK, CHUNK), F32)
        dBm = jnp.zeros((CHUNK, LANES), F32)
        dCm = jnp.zeros((CHUNK, LANES), F32)
        colp = jnp.zeros((CHUNK, LANES), F32)
        rowp = jnp.zeros((HALO, CHUNK), F32)
        ddv = jnp.zeros((1, LANES), F32)
        for r in range(SSD_HPG):
            hh = g * SSD_HPG + r
            ac_c, ac_r = _colsel(acb, hh), _rowsel(acTb, hh)
            dt_c, dt_r = _colsel(dtb, hh), _rowsel(dtTb, hh)
            S = jnp.where(causal, jnp.exp(ac_c - ac_r), 0.0)
            GS = Gm * S
            Wm = GS * dt_r
            Xr = x_ref[:, r * HEAD_DIM:(r + 1) * HEAD_DIM]
            Xc = Xr.astype(CDT)
            Hp = hp_ref[0, 0, r]
            Hc = Hp.astype(CDT)
            dY = dy_all[:, r * HEAD_DIM:(r + 1) * HEAD_DIM]
            dYc = dY.astype(CDT)
            dsk = _colsel(dv, hh)
            ddv = ddv + jnp.where(lane1 == r, jnp.sum(jnp.sum(dY * Xr, axis=1, keepdims=True), axis=0, keepdims=True), 0.0)
            dX = dY * dsk + _tn(Wm.astype(CDT), dYc)
            dWm = _nt(dYc, Xc)
            dG = dG + dWm * S * dt_r
            Mm = dWm * Wm
            Nm = dWm * GS
            dac_c = jnp.sum(Mm, axis=1, keepdims=True)
            dac_r = -jnp.sum(Mm, axis=0, keepdims=True)
            ddt_r = jnp.sum(Nm, axis=0, keepdims=True)
            E = jnp.exp(ac_c)
            Q = _nt(Cm, Hc)
            dQ = (dY * E).astype(CDT)
            dac_c = dac_c + jnp.sum(dY * Q, axis=1, keepdims=True) * E
            dCm = dCm + _nn(dQ, Hc)
            dHp = _tn(dQ, Cm)
            dHn = dh_scr[r]
            dHc = dHn.astype(CDT)
            aL = _rowsel(ac_c, CHUNK - 1)
            eaL = jnp.exp(aL)
            ex = jnp.exp(aL - ac_c)
            e = ex * dt_c
            dHp = dHp + eaL * dHn
            daL = eaL * jnp.sum(jnp.sum(dHn * Hp, axis=1, keepdims=True), axis=0, keepdims=True)
            T2 = _nt(Bm, dHc)
            dX = dX + T2 * e
            de = jnp.sum(Xr * T2, axis=1, keepdims=True)
            dBm = dBm + _nn((Xr * e).astype(CDT), dHc)
            dee = de * e
            daL = daL + jnp.sum(dee, axis=0, keepdims=True)
            dac_c = dac_c - dee + jnp.where(subc == CHUNK - 1, daL, 0.0)
            ddt_c = de * ex
            dh_scr[r] = dHp
            dx_ref[:, r * HEAD_DIM:(r + 1) * HEAD_DIM] = dX
            colp = jnp.where(col == r, dac_c, colp)
            colp = jnp.where(col == SSD_HPG + r, ddt_c, colp)
            rowp = jnp.where(sub8 == r, dac_r, rowp)
            rowp = jnp.where(sub8 == SSD_HPG + r, ddt_r, rowp)
        dGc = dG.astype(CDT)
        db_ref[...] = dBm + _tn(dGc, Cm)
        dc_ref[...] = dCm + _nn(dGc, Bm)
        colp_ref[0] = colp
        rowp_ref[0] = rowp
        dd_ref[0] += ddv

    f32 = lambda shape: SDS(shape, F32)
    return _pallas(body, grid=(G, nc),
                   in_specs=[sp["x"], sp["x"], sp["b"], sp["c"], sp["x"], sp["x"], sp["hp"], sp["col"], sp["col"],
                             sp["row"], sp["row"], sp["vec"], sp["ng"]],
                   out_specs=[sp["x"],
                              BS((CHUNK, LANES), lambda g, c: (nc - 1 - c, g)),
                              BS((CHUNK, LANES), lambda g, c: (nc - 1 - c, g)),
                              sp["x"],
                              BS((1, CHUNK, LANES), lambda g, c: (g, nc - 1 - c, 0)),
                              BS((1, HALO, CHUNK), lambda g, c: (g, 0, nc - 1 - c)),
                              BS((1, 1, LANES), lambda g, c: (g, 0, 0)),
                              sp["ng"]],
                   out_shape=[f32((T, DI)), f32((T, G * LANES)), f32((T, G * LANES)), SDS((T, DI), CDT),
                              f32((G, T, LANES)), f32((G, HALO, T)), f32((G, 1, LANES)), f32((1, DI))],
                   scratch_shapes=[pltpu.VMEM((SSD_HPG, HEAD_DIM, LANES), F32)],
                   compiler_params=_params("parallel", "arbitrary"), name=name)(
                       dyn, xbc, xbc, xbc, z, y, hp, dt, ac, dtT, acT, dskip, ng)


HBM = pl.BlockSpec(memory_space=pltpu.HBM)
VMEM = pl.BlockSpec(memory_space=pltpu.VMEM)


def _me():
    return lax.axis_index("x"), lax.axis_index("y"), lax.axis_index("c")


def _other_chips(x, y):
    return [(1 - x, y), (x, 1 - y), (1 - x, 1 - y)]


def _rcopy(src, dst, send_sem, recv_sem, to):
    return pltpu.make_async_remote_copy(src_ref=src, dst_ref=dst, send_sem=send_sem, recv_sem=recv_sem,
                                        device_id=to, device_id_type=MESH)


def allgather_weights(packs, *, name):
    n = len(packs)

    def body(*refs):
        srcs, outs, (send_sems, recv_sems) = refs[:n], refs[n:2 * n], refs[2 * n:]
        x, y, c = _me()
        k = 2 * x + y
        sibling = (x, y, 1 - c)
        chips = _other_chips(x, y)
        sends, passed = [], []
        for a, (src, out) in enumerate(zip(srcs, outs)):
            for j, (cx, cy) in enumerate(chips):
                s = 6 * a + j
                sends.append(_rcopy(src.at[c], out.at[k, c], send_sems.at[s], recv_sems.at[s], (cx, cy, c)))
        for cp in sends:
            cp.start()
        for a, out in enumerate(outs):
            for j, (cx, cy) in enumerate(chips):
                s = 6 * a + j
                landed = out.at[2 * cx + cy, c]
                _rcopy(landed, landed, send_sems.at[s], recv_sems.at[s], (cx, cy, c)).wait_recv()
                fw = _rcopy(landed, landed, send_sems.at[s + 3], recv_sems.at[s + 3], sibling)
                fw.start()
                passed.append(fw)
        for a, out in enumerate(outs):
            for j, (cx, cy) in enumerate(chips):
                s = 6 * a + j + 3
                theirs = out.at[2 * cx + cy, 1 - c]
                _rcopy(theirs, theirs, send_sems.at[s], recv_sems.at[s], sibling).wait_recv()
        for cp in sends + passed:
            cp.wait_send()

    return _pallas(body, in_specs=[HBM] * n, out_specs=[HBM] * n,
                   out_shape=[SDS((N_CHIPS,) + p.shape, p.dtype) for p in packs],
                   scratch_shapes=[pltpu.SemaphoreType.DMA((6 * n,)), pltpu.SemaphoreType.DMA((6 * n,))],
                   name=name)(*packs)


def sibling_swap_half(ps, *, name):
    n = len(ps)

    def body(*refs):
        srcs, outs, (send_sems, recv_sems) = refs[:n], refs[n:2 * n], refs[2 * n:]
        x, y, c = _me()
        cps = [_rcopy(src.at[1 - c], out, send_sems.at[a], recv_sems.at[a], (x, y, 1 - c))
               for a, (src, out) in enumerate(zip(srcs, outs))]
        for cp in cps:
            cp.start()
        for cp in cps:
            cp.wait()

    return _pallas(body, in_specs=[HBM] * n, out_specs=[HBM] * n,
                   out_shape=[SDS(p.shape[1:], p.dtype) for p in ps],
                   scratch_shapes=[pltpu.SemaphoreType.DMA((n,)), pltpu.SemaphoreType.DMA((n,))], name=name)(*ps)


def chip_scatter(s1s, *, name):
    n = len(s1s)

    def body(*refs):
        srcs, outs, (send_sems, recv_sems) = refs[:n], refs[n:2 * n], refs[2 * n:]
        x, y, c = _me()
        k = 2 * x + y
        chips = _other_chips(x, y)
        sends = [_rcopy(src.at[2 * cx + cy], out.at[k], send_sems.at[3 * a + j], recv_sems.at[3 * a + j], (cx, cy, c))
                 for a, (src, out) in enumerate(zip(srcs, outs)) for j, (cx, cy) in enumerate(chips)]
        for cp in sends:
            cp.start()
        for a, out in enumerate(outs):
            for j, (cx, cy) in enumerate(chips):
                slot = out.at[2 * cx + cy]
                _rcopy(slot, slot, send_sems.at[3 * a + j], recv_sems.at[3 * a + j], (cx, cy, c)).wait_recv()
        for cp in sends:
            cp.wait_send()

    return _pallas(body, in_specs=[HBM] * n, out_specs=[HBM] * n, out_shape=[SDS(p.shape, p.dtype) for p in s1s],
                   scratch_shapes=[pltpu.SemaphoreType.DMA((3 * n,)), pltpu.SemaphoreType.DMA((3 * n,))],
                   name=name)(*s1s)


def sibling_pair(r2s, *, name):
    n = len(r2s)

    def body(*refs):
        srcs, outs, (send_sems, recv_sems) = refs[:n], refs[n:2 * n], refs[2 * n:]
        x, y, c = _me()
        cps = [_rcopy(src.at[c], out.at[c], send_sems.at[a], recv_sems.at[a], (x, y, 1 - c))
               for a, (src, out) in enumerate(zip(srcs, outs))]
        for cp in cps:
            cp.start()
        for a, out in enumerate(outs):
            theirs = out.at[1 - c]
            _rcopy(theirs, theirs, send_sems.at[a], recv_sems.at[a], (x, y, 1 - c)).wait_recv()
        for cp in cps:
            cp.wait_send()

    return _pallas(body, in_specs=[HBM] * n, out_specs=[HBM] * n, out_shape=[SDS(p.shape, p.dtype) for p in r2s],
                   input_output_aliases={a: a for a in range(n)},
                   scratch_shapes=[pltpu.SemaphoreType.DMA((n,)), pltpu.SemaphoreType.DMA((n,))], name=name)(*r2s)


def allgather_small(v, *, name):
    R, W = v.shape

    def body(v_ref, out_ref, sum_ref, send_sems, recv_sems):
        x, y, c = _me()
        me = 4 * x + 2 * y + c
        out_ref[me] = v_ref[...]
        peers = []
        for m in range(1, N_DEV):
            px = 1 - x if m & 4 else x
            py = 1 - y if m & 2 else y
            pc = 1 - c if m & 1 else c
            peers.append((px, py, pc))
        sends = [_rcopy(v_ref, out_ref.at[me], send_sems.at[j], recv_sems.at[j], p) for j, p in enumerate(peers)]
        for cp in sends:
            cp.start()
        for j, (px, py, pc) in enumerate(peers):
            slot = out_ref.at[4 * px + 2 * py + pc]
            _rcopy(slot, slot, send_sems.at[j], recv_sems.at[j], (px, py, pc)).wait_recv()
        for cp in sends:
            cp.wait_send()
        acc = out_ref[0]
        for d in range(1, N_DEV):
            acc = acc + out_ref[d]
        sum_ref[...] = acc

    return _pallas(body, in_specs=[VMEM], out_specs=[VMEM, VMEM],
                   out_shape=[SDS((N_DEV, R, W), v.dtype), SDS((R, W), v.dtype)],
                   scratch_shapes=[pltpu.SemaphoreType.DMA((N_DEV - 1,)), pltpu.SemaphoreType.DMA((N_DEV - 1,))],
                   name=name)(v)


def _row_tile(R, cap=512):
    best = None
    for t in range(32, cap + 1, 32):
        if R % t == 0:
            best = t
    assert best is not None, R
    return best


def add_halves(p, recv, c_idx, *, name):
    _, n, Rh, W = p.shape
    tr = _row_tile(Rh)

    def body(c_ref, p_ref, r_ref, o_ref):
        o_ref[...] = (p_ref[0].astype(F32) + r_ref[...].astype(F32)).astype(o_ref.dtype)

    gs = pltpu.PrefetchScalarGridSpec(
        num_scalar_prefetch=1, grid=(n, Rh // tr),
        in_specs=[BS((1, 1, tr, W), lambda s, i, c_ref: (c_ref[0], s, i, 0)), BS((1, tr, W), lambda s, i, c_ref: (s, i, 0))],
        out_specs=BS((1, tr, W), lambda s, i, c_ref: (s, i, 0)))
    return _pallas(body, grid_spec=gs, out_shape=SDS((n, Rh, W), p.dtype),
                   compiler_params=_params("parallel", "parallel"), name=name)(c_idx, p, recv)


def sum_slots(s1, got, kc_idx, *, name):
    n, Rh, W = got.shape
    tr = _row_tile(Rh)

    def body(kc_ref, s_ref, g1, g2, g3, o_ref):
        o_ref[0] = s_ref[0].astype(F32) + g1[0].astype(F32) + g2[0].astype(F32) + g3[0].astype(F32)

    slot = lambda d: BS((1, tr, W), lambda i, kc: ((kc[0] + d) % n, i, 0))
    gs = pltpu.PrefetchScalarGridSpec(
        num_scalar_prefetch=1, grid=(Rh // tr,), in_specs=[slot(0), slot(1), slot(2), slot(3)],
        out_specs=BS((1, tr, W), lambda i, kc: (kc[1], i, 0)))
    return _pallas(body, grid_spec=gs, out_shape=SDS((2, Rh, W), F32),
                   compiler_params=_params("parallel"), name=name)(kc_idx, s1, got, got, got)


BIG = [("ffn1_w_gu", True), ("ffn1_w_down", False), ("ffn2_w_gu", True), ("ffn2_w_down", False),
       ("sb_w_qkv", True), ("sb_w_o", False), ("ssd_w_in", True), ("ssd_w_out", False),
       ("sc_w_in", True), ("sc_w_out", False)]
PACK_ALIGN = 64


def _big_layout(shards):
    groups = {}
    for name, cols in BIG:
        L, K, n = shards[name].shape
        items, off = groups.setdefault(n, ([], 0))
        items.append((name, cols, (L, K, n), off, L * K))
        groups[n] = (items, off + L * K)
    return {w: (items, -(-rows // PACK_ALIGN) * PACK_ALIGN) for w, (items, rows) in groups.items()}


def _pack_rows(pieces, total, axis):
    used = sum(p.shape[axis] for p in pieces)
    if total > used:
        pad_shape = list(pieces[0].shape)
        pad_shape[axis] = total - used
        pieces = pieces + [jnp.zeros(pad_shape, pieces[0].dtype)]
    return jnp.concatenate(pieces, axis=axis) if len(pieces) > 1 else pieces[0]


def _gather_big(shards, tag):
    layout = _big_layout(shards)
    packs = [_pack_rows([shards[n].astype(CDT).reshape(-1, w) for n, *_ in items], total, 0).reshape(2, total // 2, w)
             for w, (items, total) in layout.items()]
    fulls = allgather_weights(packs, name=f"{tag}_allgather")
    k_chip = 2 * lax.axis_index("x") + lax.axis_index("y")
    out = {}
    for (w, (items, total)), pack, full in zip(layout.items(), packs, fulls):
        full = lax.dynamic_update_slice(full, pack[None], (k_chip, 0, 0, 0)).reshape(N_CHIPS, total, w)
        for name, cols, (L, K, n), off, rows in items:
            piece = full[:, off:off + rows].reshape(N_CHIPS, L, K, n)
            if cols:
                out[name] = piece.transpose(1, 2, 0, 3).reshape(L, K, N_CHIPS * n)
            else:
                out[name] = piece.transpose(1, 0, 2, 3).reshape(L, N_CHIPS * K, n)
    return out


def _reduce_big(grads, shards, tag):
    layout = _big_layout(shards)
    ps = []
    for w, (items, total) in layout.items():
        pieces = []
        for name, cols, (L, K, n), off, rows in items:
            for g in grads[name]:
                if cols:
                    g = g.reshape(K, N_CHIPS, n).transpose(1, 0, 2)
                else:
                    g = g.reshape(N_CHIPS, K, n)
                pieces.append(g.astype(CDT))
        pack = _pack_rows(pieces, total, 1)
        ps.append(pack.reshape(N_CHIPS, 2, total // 2, w).transpose(1, 0, 2, 3))
    c_core = lax.axis_index("c").astype(jnp.int32)
    k_chip = (2 * lax.axis_index("x") + lax.axis_index("y")).astype(jnp.int32)
    recvs = sibling_swap_half(ps, name=f"{tag}_sibling_swap")
    s1s = [add_halves(p, r, c_core.reshape(1), name=f"{tag}_add_halves_{p.shape[-1]}") for p, r in zip(ps, recvs)]
    gots = chip_scatter(s1s, name=f"{tag}_chip_scatter")
    r2s = [sum_slots(s1, got, jnp.stack([k_chip, c_core]), name=f"{tag}_sum_slots_{s1.shape[-1]}")
           for s1, got in zip(s1s, gots)]
    reds = sibling_pair(r2s, name=f"{tag}_sibling_pair")
    out = {}
    for (w, (items, total)), red in zip(layout.items(), reds):
        red = red.reshape(total, w)
        for name, cols, shape, off, rows in items:
            out[name] = red[off:off + rows].reshape(shape)
    return out


def _pad_lanes(v, width=LANES):
    return jnp.pad(v, ((0, 0), (0, width - v.shape[1])))


def _small_pack(vecs):
    flat, meta, off = [], [], 0
    for v in vecs:
        flat.append(v.reshape(-1))
        meta.append((off, v.size, v.shape))
        off += v.size
    total = -(-off // (8 * LANES)) * 8 * LANES
    flat.append(jnp.zeros((total - off,), F32))
    return jnp.concatenate(flat).reshape(-1, LANES), meta


def _small_unpack(pack, meta):
    flat = pack.reshape(-1)
    return [flat[off:off + size].reshape(shape) for off, size, shape in meta]


def _ffn_fwd(x, g, wgu, wd, tag):
    h = rms_fwd(x, g, name=f"{tag}_norm")
    gu = matmul(h, wgu, out_dtype=CDT, name=f"{tag}_gu")
    a = swiglu_fwd(gu, name=f"{tag}_act")
    return matmul(a, wd, res=x, scale=0.5, name=f"{tag}_down"), (x, h, gu, a)


def _ffn_bwd(dout, saved, g, wgu, wd, tag):
    x, h, gu, a = saved
    dwd = matmul(a, dout, ta=True, out_dtype=CDT, scale=0.5, name=f"{tag}_dwd")
    da = matmul(dout, wd, tb=True, scale=0.5, name=f"{tag}_da")
    dgu = swiglu_bwd(da, gu, name=f"{tag}_dact")
    dwgu = matmul(h, dgu, ta=True, out_dtype=CDT, name=f"{tag}_dwgu")
    dh = matmul(dgu, wgu, tb=True, name=f"{tag}_dh")
    dx, dg = rms_bwd(dh, x, g, dout, name=f"{tag}_dnorm")
    return dx, dg, dwgu, dwd


def _sb_mixer_fwd(x, g, wqkv, wo, cfg, tag):
    h = rms_fwd(x, g, name=f"{tag}_norm")
    qkv = matmul(h, wqkv, out_dtype=CDT, name=f"{tag}_qkv")
    o, tab = sb_fwd(qkv, cfg.sb_heads, name=f"{tag}_attn")
    return matmul(o, wo, res=x, name=f"{tag}_out"), (x, h, qkv, tab, o)


def _sb_mixer_bwd(dout, saved, g, wqkv, wo, cfg, tag):
    x, h, qkv, tab, o = saved
    dwo = matmul(o, dout, ta=True, out_dtype=CDT, name=f"{tag}_dwo")
    do = matmul(dout, wo, tb=True, out_dtype=CDT, name=f"{tag}_do")
    dqkv = jnp.concatenate(sb_bwd(qkv, do, tab, cfg.sb_heads, name=f"{tag}_dattn"), axis=1)
    dwqkv = matmul(h, dqkv, ta=True, out_dtype=CDT, name=f"{tag}_dwqkv")
    dh = matmul(dqkv, wqkv, tb=True, name=f"{tag}_dh")
    dx, dg = rms_bwd(dh, x, g, dout, name=f"{tag}_dnorm")
    return dx, dg, dwqkv, dwo


def _sc_mixer_fwd(x, g, win, convw, wout, tag):
    h = rms_fwd(x, g, name=f"{tag}_norm")
    proj = matmul(h, win, name=f"{tag}_in")
    m = sc_fwd(proj, convw, name=f"{tag}_conv")
    return matmul(m, wout, res=x, name=f"{tag}_out"), (x, h, proj, m)


def _sc_mixer_bwd(dout, saved, g, win, convw, wout, tag):
    x, h, proj, m = saved
    dwout = matmul(m, dout, ta=True, out_dtype=CDT, name=f"{tag}_dwout")
    dm = matmul(dout, wout, tb=True, name=f"{tag}_dm")
    dproj, dconvw = sc_bwd(dm, proj, convw, name=f"{tag}_dconv")
    dwin = matmul(h, dproj, ta=True, out_dtype=CDT, name=f"{tag}_dwin")
    dh = matmul(dproj, win, tb=True, name=f"{tag}_dh")
    dx, dg = rms_bwd(dh, x, g, dout, name=f"{tag}_dnorm")
    return dx, dg, dwin, dconvw[:convw.shape[0]], dwout


def _ssd_split_w(win, cfg):
    DI, CV = cfg.DI, cfg.conv_dim
    return win[:, :DI], win[:, DI:DI + CV], _pad_lanes(win[:, DI + CV:])


def _ssd_mixer_fwd(x, g, win, convw, convb, dtb, alog, dskip, ng, wout, cfg, tag):
    wz, wx, wdt = _ssd_split_w(win, cfg)
    h = rms_fwd(x, g, name=f"{tag}_norm")
    z = matmul(h, wz, name=f"{tag}_z")
    xpre = matmul(h, wx, name=f"{tag}_xbc")
    dtpre = matmul(h, wdt, name=f"{tag}_dt")
    pre, xbc = ssd_conv_fwd(xpre, convw, convb, name=f"{tag}_conv")
    dt, ac, dtT, acT = ssd_prep_fwd(dtpre, dtb, alog, name=f"{tag}_prep")
    y, yn, hp = ssd_chunk_fwd(xbc, z, dt, ac, dtT, acT, dskip, ng, cfg, name=f"{tag}_scan")
    out = matmul(yn, wout, res=x, name=f"{tag}_out")
    return out, (x, h, z, xpre, dtpre, pre, xbc, dt, ac, dtT, acT, y, yn, hp)


def _ssd_mixer_bwd(dout, saved, g, win, convw, convb, dtb, alog, dskip, ng, wout, cfg, tag):
    x, h, z, xpre, dtpre, pre, xbc, dt, ac, dtT, acT, y, yn, hp = saved
    T = x.shape[0]
    NH, G = cfg.NH, cfg.G
    wz, wx, wdt = _ssd_split_w(win, cfg)
    dwout = matmul(yn, dout, ta=True, out_dtype=CDT, name=f"{tag}_dwout")
    dyn = matmul(dout, wout, tb=True, name=f"{tag}_dyn")
    dxs, db, dc, dz, colp, rowp, dd, dng = ssd_chunk_bwd(dyn, xbc, z, y, hp, dt, ac, dtT, acT, dskip, ng, cfg,
                                                        name=f"{tag}_dscan")
    col = lambda lo: _pad_lanes(colp[:, :, lo:lo + SSD_HPG].transpose(1, 0, 2).reshape(T, NH))
    row = lambda lo: jnp.pad(rowp[:, lo:lo + SSD_HPG, :].reshape(NH, T), ((0, LANES - NH), (0, 0)))
    ddtpre, dalog, ddtb = ssd_prep_bwd(col(0), col(SSD_HPG), row(0), row(SSD_HPG), dtpre, dtb, alog, name=f"{tag}_dprep")
    dxpre, dconvw, dconvb = ssd_conv_bwd(dxs, db, dc, pre, xpre, convw, name=f"{tag}_dconv")
    dwz = matmul(h, dz, ta=True, out_dtype=CDT, name=f"{tag}_dwz")
    dwx = matmul(h, dxpre, ta=True, out_dtype=CDT, name=f"{tag}_dwx")
    dwdt = matmul(h, ddtpre, ta=True, out_dtype=CDT, name=f"{tag}_dwdt")
    dh = matmul(dz, wz, tb=True, name=f"{tag}_dh_z")
    dh = matmul(dxpre, wx, tb=True, res=dh, name=f"{tag}_dh_x")
    dh = matmul(ddtpre, wdt, tb=True, res=dh, name=f"{tag}_dh_dt")
    dx, dg = rms_bwd(dh, x, g, dout, name=f"{tag}_dnorm")
    dwin = jnp.concatenate([dwz, dwx, dwdt[:, :NH]], axis=1)
    small = dict(conv_w=dconvw[:convw.shape[0]], conv_b=dconvb, dt_bias=ddtb[:, :NH], a_log=dalog[:, :NH],
                 d=dd[:, 0, :SSD_HPG].reshape(1, NH), norm=dng)
    return dx, dg, dwin, dwout, small


INPUT_NAMES = ['x', 'ffn1_norm', 'ffn1_w_gu', 'ffn1_w_down', 'mix_norm', 'ffn2_norm', 'ffn2_w_gu', 'ffn2_w_down',
               'sb_w_qkv', 'sb_w_o', 'ssd_w_in', 'ssd_conv_w', 'ssd_conv_b', 'ssd_dt_bias', 'ssd_a_log', 'ssd_d',
               'ssd_norm', 'ssd_w_out', 'sc_w_in', 'sc_conv_w', 'sc_w_out', 'final_norm']
WEIGHT_NAMES = INPUT_NAMES[1:]
SHARDED_SMALL = ("ssd_conv_w", "sc_conv_w")


def train_step(a, cfg):
    x = a["x"][0]
    tgt = a["loss_target"][0]
    big_names = [n for n, _ in BIG]
    k_chip = 2 * lax.axis_index("x") + lax.axis_index("y")

    w = _gather_big({n: a[n] for n in big_names}, "w")
    conv_pack, conv_meta = _small_pack([a[n] for n in SHARDED_SMALL])
    conv_all, _ = allgather_small(conv_pack, name="conv_w_allgather")
    conv_full = {}
    for i, n in enumerate(SHARDED_SMALL):
        per_chip = [_small_unpack(conv_all[2 * kk], conv_meta)[i] for kk in range(N_CHIPS)]
        conv_full[n] = jnp.concatenate(per_chip, axis=-1)

    def vec(name, i):
        return a[name][i:i + 1]

    ssd_small = lambda j: (a["ssd_conv_b"][j:j + 1], _pad_lanes(a["ssd_dt_bias"][j:j + 1]),
                           _pad_lanes(a["ssd_a_log"][j:j + 1]), _pad_lanes(a["ssd_d"][j:j + 1]), a["ssd_norm"][j:j + 1])

    saved = []
    for i in range(cfg.depth):
        kind, j = i % 3, i // 3
        x, s1 = _ffn_fwd(x, vec("ffn1_norm", i), w["ffn1_w_gu"][i], w["ffn1_w_down"][i], f"l{i}_ffn1")
        if kind == 0:
            x, s2 = _sb_mixer_fwd(x, vec("mix_norm", i), w["sb_w_qkv"][j], w["sb_w_o"][j], cfg, f"l{i}_sb")
        elif kind == 1:
            x, s2 = _ssd_mixer_fwd(x, vec("mix_norm", i), w["ssd_w_in"][j], conv_full["ssd_conv_w"][j], *ssd_small(j),
                                   w["ssd_w_out"][j], cfg, f"l{i}_ssd")
        else:
            x, s2 = _sc_mixer_fwd(x, vec("mix_norm", i), w["sc_w_in"][j], conv_full["sc_conv_w"][j], w["sc_w_out"][j],
                                  f"l{i}_sc")
        x, s3 = _ffn_fwd(x, vec("ffn2_norm", i), w["ffn2_w_gu"][i], w["ffn2_w_down"][i], f"l{i}_ffn2")
        saved.append((s1, s2, s3))
    loss_part, dx, d_final = loss_head(x, a["final_norm"].reshape(1, -1), tgt, name="loss_head")

    gbig = {n: [None] * a[n].shape[0] for n in big_names}
    gsmall = {n: [None] * a[n].shape[0] for n in ("ffn1_norm", "mix_norm", "ffn2_norm", "ssd_conv_w", "ssd_conv_b",
                                                   "ssd_dt_bias", "ssd_a_log", "ssd_d", "ssd_norm", "sc_conv_w")}
    for i in reversed(range(cfg.depth)):
        kind, j = i % 3, i // 3
        s1, s2, s3 = saved[i]
        dx, gsmall["ffn2_norm"][i], gbig["ffn2_w_gu"][i], gbig["ffn2_w_down"][i] = _ffn_bwd(
            dx, s3, vec("ffn2_norm", i), w["ffn2_w_gu"][i], w["ffn2_w_down"][i], f"l{i}_ffn2")
        if kind == 0:
            dx, gsmall["mix_norm"][i], gbig["sb_w_qkv"][j], gbig["sb_w_o"][j] = _sb_mixer_bwd(
                dx, s2, vec("mix_norm", i), w["sb_w_qkv"][j], w["sb_w_o"][j], cfg, f"l{i}_sb")
        elif kind == 1:
            dx, gsmall["mix_norm"][i], gbig["ssd_w_in"][j], gbig["ssd_w_out"][j], sm = _ssd_mixer_bwd(
                dx, s2, vec("mix_norm", i), w["ssd_w_in"][j], conv_full["ssd_conv_w"][j], *ssd_small(j),
                w["ssd_w_out"][j], cfg, f"l{i}_ssd")
            for key, val in sm.items():
                gsmall["ssd_" + key][j] = val
        else:
            dx, gsmall["mix_norm"][i], gbig["sc_w_in"][j], gsmall["sc_conv_w"][j], gbig["sc_w_out"][j] = _sc_mixer_bwd(
                dx, s2, vec("mix_norm", i), w["sc_w_in"][j], conv_full["sc_conv_w"][j], w["sc_w_out"][j], f"l{i}_sc")
        dx, gsmall["ffn1_norm"][i], gbig["ffn1_w_gu"][i], gbig["ffn1_w_down"][i] = _ffn_bwd(
            dx, s1, vec("ffn1_norm", i), w["ffn1_w_gu"][i], w["ffn1_w_down"][i], f"l{i}_ffn1")
    grad_x = dx[None]

    red_big = _reduce_big(gbig, {n: a[n] for n in big_names}, "g")
    small_names = list(gsmall) + ["final_norm"]
    small_vals = [jnp.concatenate([p.reshape((1,) + a[n].shape[1:-1] + (-1,)) for p in gsmall[n]], axis=0)
                  for n in gsmall] + [d_final.reshape(-1)]
    gpack, gmeta = _small_pack(small_vals + [loss_part[:, :1]])
    _, gsum = allgather_small(gpack, name="small_allreduce")
    *small_red, loss = _small_unpack(gsum, gmeta)
    red = dict(red_big)
    for n, v in zip(small_names, small_red):
        if n in SHARDED_SMALL:
            width = a[n].shape[-1]
            v = lax.dynamic_slice_in_dim(v, k_chip * width, width, axis=v.ndim - 1)
        red[n] = v.reshape(a[n].shape)

    delta, new_m, new_v = {}, {}, {}
    two_d = lambda t: t.reshape(-1, t.shape[-1])
    for n in big_names:
        d_, m_, v_ = adamw(two_d(a[n]), two_d(red[n]), two_d(a["m_" + n]), two_d(a["v_" + n]), name=f"adamw_{n}")
        delta[n], new_m[n], new_v[n] = (t.reshape(a[n].shape) for t in (d_, m_, v_))
    rest = [n for n in WEIGHT_NAMES if n not in big_names]
    packs = [_small_pack([src[n] for n in rest]) for src in
             (a, red, {n: a["m_" + n] for n in rest}, {n: a["v_" + n] for n in rest})]
    outs = adamw(*[p for p, _ in packs], name="adamw_small")
    for dst, o in zip((delta, new_m, new_v), outs):
        for n, t in zip(rest, _small_unpack(o, packs[0][1])):
            dst[n] = t
    return (loss.reshape(()), grad_x, *[red[n] for n in WEIGHT_NAMES], *[delta[n] for n in WEIGHT_NAMES],
            *[new_m[n] for n in WEIGHT_NAMES], *[new_v[n] for n in WEIGHT_NAMES])


def kernel(x, ffn1_norm, ffn1_w_gu, ffn1_w_down, mix_norm, ffn2_norm, ffn2_w_gu, ffn2_w_down, sb_w_qkv, sb_w_o, ssd_w_in, ssd_conv_w, ssd_conv_b, ssd_dt_bias, ssd_a_log, ssd_d, ssd_norm, ssd_w_out, sc_w_in, sc_conv_w, sc_w_out, final_norm, loss_target, m_ffn1_norm, m_ffn1_w_gu, m_ffn1_w_down, m_mix_norm, m_ffn2_norm, m_ffn2_w_gu, m_ffn2_w_down, m_sb_w_qkv, m_sb_w_o, m_ssd_w_in, m_ssd_conv_w, m_ssd_conv_b, m_ssd_dt_bias, m_ssd_a_log, m_ssd_d, m_ssd_norm, m_ssd_w_out, m_sc_w_in, m_sc_conv_w, m_sc_w_out, m_final_norm, v_ffn1_norm, v_ffn1_w_gu, v_ffn1_w_down, v_mix_norm, v_ffn2_norm, v_ffn2_w_gu, v_ffn2_w_down, v_sb_w_qkv, v_sb_w_o, v_ssd_w_in, v_ssd_conv_w, v_ssd_conv_b, v_ssd_dt_bias, v_ssd_a_log, v_ssd_d, v_ssd_norm, v_ssd_w_out, v_sc_w_in, v_sc_conv_w, v_sc_w_out, v_final_norm):
    args = dict(locals())
    return train_step(args, Cfg())
```

```python
import functools
import math

import jax
import jax.numpy as jnp
from jax import lax
from jax.experimental import pallas as pl
from jax.experimental.pallas import tpu as pltpu

F32 = jnp.float32
CDT = jnp.bfloat16
HI = lax.Precision.HIGHEST
BS = pl.BlockSpec
SDS = jax.ShapeDtypeStruct
MESH = pl.DeviceIdType.MESH

RMS_EPS = 1e-6
ADAM_LR, ADAM_B1, ADAM_B2, ADAM_EPS, ADAM_WD, ADAM_STEP = 0.001, 0.9, 0.999, 1e-08, 0.01, 10
V7X_VMEM_LIMIT_BYTES = 56 * 1024 * 1024
LANES = 128
CHUNK = 128
HEAD_DIM = 64
SSD_HPG = 4
N_CHIPS = 4
N_DEV = 8


class Cfg:
    def __init__(self, d_model=1024, d_ff=2816, depth=4, sb_heads=16, ssd_groups=8, ssd_state=128,
                 ssd_conv=4, sc_width=3):
        self.D, self.F, self.depth = d_model, d_ff, depth
        self.sb_heads = sb_heads
        self.DI = 2 * d_model
        self.G = ssd_groups
        self.NH = self.DI // HEAD_DIM
        assert self.NH == SSD_HPG * self.G and ssd_state == LANES and sb_heads * HEAD_DIM == d_model
        self.conv_dim = self.DI + 2 * self.G * ssd_state
        self.in_dim = self.DI + self.conv_dim + self.NH
        self.ssd_conv, self.sc_width = ssd_conv, sc_width
        self.n_sb, self.n_ssd, self.n_sc = (depth + 2) // 3, (depth + 1) // 3, depth // 3


def _pallas(body, **kw):
    return pl.pallas_call(body, **kw)


def _params(*sem):
    return pltpu.CompilerParams(dimension_semantics=sem, vmem_limit_bytes=V7X_VMEM_LIMIT_BYTES)


def _tile(n, cap):
    if n <= cap:
        return n
    best = None
    for m in range(1, n // LANES + 1):
        t = m * LANES
        if n % t == 0 and t <= cap:
            best = t
    assert best is not None, (n, cap)
    return best


def _sigmoid(x):
    return 1.0 / (1.0 + jnp.exp(-x))


def _softplus_neg_abs(x):
    return jnp.log(1.0 + jnp.exp(-jnp.abs(x)))


def _colsel(blk, idx):
    lane = lax.broadcasted_iota(jnp.int32, blk.shape, 1)
    return jnp.sum(jnp.where(lane == idx, blk, 0.0), axis=1, keepdims=True)


def _rowsel(blk, idx):
    sub = lax.broadcasted_iota(jnp.int32, blk.shape, 0)
    return jnp.sum(jnp.where(sub == idx, blk, 0.0), axis=0, keepdims=True)


def _nt(a, b):
    return lax.dot_general(a, b, (((1,), (1,)), ((), ())), preferred_element_type=F32)


def _tn(a, b):
    return lax.dot_general(a, b, (((0,), (0,)), ((), ())), preferred_element_type=F32)


def _nn(a, b):
    return jnp.dot(a, b, preferred_element_type=F32)


def matmul(a, b, *, name, ta=False, tb=False, out_dtype=F32, res=None, scale=1.0, tm=1408, tn=1408, tk=1408,
           b_k_start=0):
    M, K = (a.shape[1], a.shape[0]) if ta else a.shape
    N, K2 = b.shape if tb else (b.shape[1], b.shape[0])
    assert K == K2 or (tb and b_k_start + K <= K2), (a.shape, b.shape, ta, tb)
    tm, tn, tk = _tile(M, tm), _tile(N, tn), _tile(K, tk)
    nk = K // tk
    assert b_k_start % tk == 0
    kb0 = b_k_start // tk
    dn = (((0 if ta else 1,), (1 if tb else 0,)), ((), ()))

    def body(*refs):
        a_ref, b_ref = refs[:2]
        r_ref = refs[2] if res is not None else None
        o_ref = refs[3 if res is not None else 2]
        part = lax.dot_general(a_ref[...].astype(CDT), b_ref[...].astype(CDT), dn, preferred_element_type=F32)

        def finish(total):
            o = total * scale
            if res is not None:
                o = o + r_ref[...].astype(F32)
            o_ref[...] = o.astype(o_ref.dtype)

        if nk == 1:
            finish(part)
            return
        acc = refs[-1]
        k = pl.program_id(2)

        @pl.when(k == 0)
        def _():
            acc[...] = part

        @pl.when(jnp.logical_and(k > 0, k < nk - 1))
        def _():
            acc[...] += part

        @pl.when(k == nk - 1)
        def _():
            finish(acc[...] + part)

    a_spec = BS((tk, tm), lambda i, j, k: (k, i)) if ta else BS((tm, tk), lambda i, j, k: (i, k))
    b_spec = BS((tn, tk), lambda i, j, k: (j, k + kb0)) if tb else BS((tk, tn), lambda i, j, k: (k, j))
    o_spec = BS((tm, tn), lambda i, j, k: (i, j))
    ins, specs = [a, b], [a_spec, b_spec]
    if res is not None:
        ins.append(res)
        specs.append(o_spec)
    return _pallas(body, grid=(M // tm, N // tn, nk), in_specs=specs, out_specs=o_spec,
                   out_shape=SDS((M, N), out_dtype),
                   scratch_shapes=[pltpu.VMEM((tm, tn), F32)] if nk > 1 else [],
                   compiler_params=_params("parallel", "parallel", "arbitrary"), name=name)(*ins)


def rms_fwd(x, g, *, name):
    T, D = x.shape
    tm = _tile(T, 512)

    def body(x_ref, g_ref, o_ref):
        xv = x_ref[...]
        r = lax.rsqrt(jnp.mean(xv * xv, axis=-1, keepdims=True) + RMS_EPS)
        o_ref[...] = (xv * r * g_ref[...]).astype(o_ref.dtype)

    return _pallas(body, grid=(T // tm,), in_specs=[BS((tm, D), lambda i: (i, 0)), BS((1, D), lambda i: (0, 0))],
                   out_specs=BS((tm, D), lambda i: (i, 0)), out_shape=SDS((T, D), CDT),
                   compiler_params=_params("parallel"), name=name)(x, g)


def rms_bwd(dh, x, g, dres, *, name):
    T, D = x.shape
    tm = _tile(T, 512)

    def body(dh_ref, x_ref, g_ref, r_ref, dx_ref, dg_ref):
        i = pl.program_id(0)
        xv = x_ref[...]
        r = lax.rsqrt(jnp.mean(xv * xv, axis=-1, keepdims=True) + RMS_EPS)
        xhat = xv * r
        dhv = dh_ref[...].astype(F32)
        dxh = dhv * g_ref[...]
        m = jnp.mean(dxh * xhat, axis=-1, keepdims=True)
        dx_ref[...] = r * (dxh - xhat * m) + r_ref[...]

        @pl.when(i == 0)
        def _():
            dg_ref[...] = jnp.zeros_like(dg_ref)

        dg_ref[...] += jnp.sum(dhv * xhat, axis=0, keepdims=True)

    row = BS((tm, D), lambda i: (i, 0))
    vec = BS((1, D), lambda i: (0, 0))
    return _pallas(body, grid=(T // tm,), in_specs=[row, row, vec, row], out_specs=[row, vec],
                   out_shape=[SDS((T, D), F32), SDS((1, D), F32)],
                   compiler_params=_params("arbitrary"), name=name)(dh, x, g, dres)


def ffn_up(h, wgu, *, name):
    T, D = h.shape
    F = wgu.shape[1] // 2
    tm, tn = _tile(T, 512), _tile(F, 1408)
    nj = F // tn

    def body(h_ref, wg_ref, wu_ref, g_ref, u_ref, a_ref):
        hv = h_ref[...].astype(CDT)
        g = jnp.dot(hv, wg_ref[...].astype(CDT), preferred_element_type=F32)
        u = jnp.dot(hv, wu_ref[...].astype(CDT), preferred_element_type=F32)
        g_ref[...] = g.astype(g_ref.dtype)
        u_ref[...] = u.astype(u_ref.dtype)
        a_ref[...] = (g * _sigmoid(g) * u).astype(a_ref.dtype)

    out = BS((tm, tn), lambda j, i: (i, j))
    return _pallas(body, grid=(nj, T // tm),
                   in_specs=[BS((tm, D), lambda j, i: (i, 0)), BS((D, tn), lambda j, i: (0, j)),
                             BS((D, tn), lambda j, i: (0, j + nj))],
                   out_specs=[out, out, out], out_shape=[SDS((T, F), CDT)] * 3,
                   compiler_params=_params("parallel", "parallel"), name=name)(h, wgu, wgu)


def ffn_dact(dout, wd, g, u, scale, *, name):
    T, D = dout.shape
    F = wd.shape[0]
    tm, tn = _tile(T, 512), _tile(F, 1408)

    def body(do_ref, wd_ref, g_ref, u_ref, dg_ref, du_ref):
        da = scale * _nt(do_ref[...].astype(CDT), wd_ref[...].astype(CDT))
        gv = g_ref[...].astype(F32)
        uv = u_ref[...].astype(F32)
        sg = _sigmoid(gv)
        dg_ref[...] = (da * uv * (sg * (1.0 + gv * (1.0 - sg)))).astype(dg_ref.dtype)
        du_ref[...] = (da * (gv * sg)).astype(du_ref.dtype)

    blk = BS((tm, tn), lambda j, i: (i, j))
    return _pallas(body, grid=(F // tn, T // tm),
                   in_specs=[BS((tm, D), lambda j, i: (i, 0)), BS((tn, D), lambda j, i: (j, 0)), blk, blk],
                   out_specs=[blk, blk], out_shape=[SDS((T, F), CDT)] * 2,
                   compiler_params=_params("parallel", "parallel"), name=name)(dout, wd, g, u)


def loss_head(x, g, tgt, *, name):
    T, D = x.shape
    tm = _tile(T, 512)

    def body(x_ref, g_ref, t_ref, l_ref, dx_ref, dg_ref):
        i = pl.program_id(0)
        xv = x_ref[...]
        gv = g_ref[...]
        r = lax.rsqrt(jnp.mean(xv * xv, axis=-1, keepdims=True) + RMS_EPS)
        xhat = xv * r
        err = xhat * gv - t_ref[...]
        part = 0.5 * jnp.sum(jnp.mean(err * err, axis=-1, keepdims=True), axis=0, keepdims=True)
        dy = err * (1.0 / D)
        dxh = dy * gv
        m = jnp.mean(dxh * xhat, axis=-1, keepdims=True)
        dx_ref[...] = r * (dxh - xhat * m)

        @pl.when(i == 0)
        def _():
            dg_ref[...] = jnp.zeros_like(dg_ref)
            l_ref[...] = jnp.zeros_like(l_ref)

        dg_ref[...] += jnp.sum(dy * xhat, axis=0, keepdims=True)
        l_ref[...] += jnp.broadcast_to(part, l_ref.shape)

    row = BS((tm, D), lambda i: (i, 0))
    vec = BS((1, D), lambda i: (0, 0))
    return _pallas(body, grid=(T // tm,), in_specs=[row, vec, row],
                   out_specs=[BS((1, LANES), lambda i: (0, 0)), row, vec],
                   out_shape=[SDS((1, LANES), F32), SDS((T, D), F32), SDS((1, D), F32)],
                   compiler_params=_params("arbitrary"), name=name)(x, g, tgt)


def adamw(w, g, m, v, *, name):
    R, C = w.shape
    tr = R
    if R * C * 4 > (1 << 20):
        for cand in range(8, R + 1, 8):
            if R % cand == 0 and cand * C * 4 <= (1 << 20):
                tr = cand

    def body(w_ref, g_ref, m_ref, v_ref, d_ref, nm_ref, nv_ref):
        gv = g_ref[...]
        nm = ADAM_B1 * m_ref[...] + (1.0 - ADAM_B1) * gv
        nv = ADAM_B2 * v_ref[...] + (1.0 - ADAM_B2) * (gv * gv)
        m_hat = nm / (1.0 - ADAM_B1 ** ADAM_STEP)
        v_hat = nv / (1.0 - ADAM_B2 ** ADAM_STEP)
        d_ref[...] = -ADAM_LR * (m_hat / (jnp.sqrt(v_hat) + ADAM_EPS) + ADAM_WD * w_ref[...])
        nm_ref[...] = nm
        nv_ref[...] = nv

    blk = BS((tr, C), lambda i: (i, 0))
    return _pallas(body, grid=(R // tr,), in_specs=[blk] * 4, out_specs=[blk] * 3,
                   out_shape=[SDS((R, C), F32)] * 3, compiler_params=_params("parallel"), name=name)(w, g, m, v)


def _sb_tri():
    row = lax.broadcasted_iota(jnp.int32, (CHUNK, CHUNK), 0)
    col = lax.broadcasted_iota(jnp.int32, (CHUNK, CHUNK), 1)
    return row, col


def _tri_stack(keep):
    t = keep.astype(jnp.bfloat16)
    return jnp.concatenate([t, t], axis=0)


def _split_dot(x, tri_stack):
    hi = x.astype(jnp.bfloat16)
    mid = (x - hi.astype(F32)).astype(jnp.bfloat16)
    return jnp.dot(jnp.concatenate([hi, mid], axis=1), tri_stack, preferred_element_type=F32)


def _sb_heads_per_step(H, want):
    hb = min(H, want)
    assert H % hb == 0, (H, hb)
    return hb


SB_FWD_HEADS = 8
SB_BWD_HEADS = 8


def _head_pair_masked(blk, lo):
    zero = jnp.zeros_like(blk)
    return jnp.where(lo, blk, zero), jnp.where(lo, zero, blk)


def sb_fwd(qkv, n_heads, *, name):
    T, D3 = qkv.shape
    D = D3 // 3
    nq = T // CHUNK
    assert nq <= LANES
    scale = HEAD_DIM ** -0.5
    hb = _sb_heads_per_step(n_heads, SB_FWD_HEADS)
    assert hb % 2 == 0
    LW = hb * HEAD_DIM
    ngrp = D // LW
    heads, pairs = range(hb), range(hb // 2)
    lanes = lambda p: slice(p * LANES, (p + 1) * LANES)

    def body(q_ref, k_ref, v_ref, o_ref, tab_ref, acc_scr, cs_scr, qm_scr):
        qi = pl.program_id(1)
        row, col = _sb_tri()
        lo = col < HEAD_DIM
        upper_stack = _tri_stack(row > col)
        acc_scr[...] = jnp.zeros_like(acc_scr)
        cs_scr[...] = jnp.zeros_like(cs_scr)
        tab_ref[...] = jnp.zeros_like(tab_ref)
        for p in pairs:
            qm_scr[2 * p], qm_scr[2 * p + 1] = _head_pair_masked(q_ref[:, lanes(p)] * scale, lo)

        def tile(j, diag):
            sl = pl.ds(pl.multiple_of(j * CHUNK, CHUNK), CHUNK)
            keep = (lambda t: jnp.where(col < row, t, 0.0)) if diag else (lambda t: t)
            kbs = [k_ref[sl, lanes(p)] for p in pairs]
            zs = [_nt(qm_scr[h], kbs[h // 2]) for h in heads]
            lszs = [jnp.minimum(z, 0.0) - _softplus_neg_abs(z) for z in zs]
            lks = [keep(lsz - z) for lsz, z in zip(lszs, zs)]
            tls = [_split_dot(lk, upper_stack) for lk in lks]
            css = [cs_scr[h] for h in heads]
            atts = [keep(jnp.exp(lsz + tl + cs)).astype(CDT) for lsz, tl, cs in zip(lszs, tls, css)]
            vms = [jnp.concatenate(_head_pair_masked(v_ref[sl, lanes(p)], lo), axis=0) for p in pairs]
            pvs = [_nn(jnp.concatenate([atts[2 * p], atts[2 * p + 1]], axis=1), vms[p]) for p in pairs]
            for p in pairs:
                acc_scr[:, lanes(p)] += pvs[p]
            for h in heads:
                tab_ref[h, 0] = jnp.where(col == j, css[h], tab_ref[h, 0])
                cs_scr[h] = css[h] + (tls[h][:, :1] + lks[h][:, :1])

        tile(qi, True)

        def step(jj, carry):
            tile(qi - jj, False)
            return carry

        lax.fori_loop(1, qi + 1, step, 0)
        o_ref[...] = acc_scr[...].astype(o_ref.dtype)

    once = pl.Buffered(1)
    qs = BS((CHUNK, LW), lambda g, i: (i, g))
    return _pallas(body, grid=(ngrp, nq),
                   in_specs=[qs, BS((T, LW), lambda g, i: (0, ngrp + g), pipeline_mode=once),
                             BS((T, LW), lambda g, i: (0, 2 * ngrp + g), pipeline_mode=once)],
                   out_specs=[qs, BS((hb, 1, CHUNK, CHUNK), lambda g, i: (g, i, 0, 0))],
                   out_shape=[SDS((T, D), CDT), SDS((n_heads, nq, CHUNK, CHUNK), F32)],
                   scratch_shapes=[pltpu.VMEM((CHUNK, LW), F32), pltpu.VMEM((hb, CHUNK, 1), F32),
                                   pltpu.VMEM((hb, CHUNK, LANES), CDT)],
                   compiler_params=_params("parallel", "arbitrary"), name=name)(qkv, qkv, qkv)


def sb_bwd(qkv, do, tab, n_heads, *, name):
    T, D3 = qkv.shape
    D = D3 // 3
    nq = T // CHUNK
    scale = HEAD_DIM ** -0.5
    hb = _sb_heads_per_step(n_heads, SB_BWD_HEADS)
    assert hb % 2 == 0
    LW = hb * HEAD_DIM
    ngrp = D // LW
    heads, pairs = range(hb), range(hb // 2)
    lanes = lambda p: slice(p * LANES, (p + 1) * LANES)

    def body(q_ref, k_ref, v_ref, do_ref, tab_ref, dq_ref, dk_ref, dv_ref,
             dk_acc, dv_acc, dq_scr, pfx_scr, qm_scr, dom_scr):
        qi = pl.program_id(1)

        @pl.when(qi == 0)
        def _():
            dk_acc[...] = jnp.zeros_like(dk_acc)
            dv_acc[...] = jnp.zeros_like(dv_acc)

        dq_scr[...] = jnp.zeros_like(dq_scr)
        pfx_scr[...] = jnp.zeros_like(pfx_scr)
        row, col = _sb_tri()
        lo = col < HEAD_DIM
        upper_stack = _tri_stack(row > col)
        before_stack = _tri_stack(row < col)
        for p in pairs:
            qm_scr[2 * p], qm_scr[2 * p + 1] = _head_pair_masked(q_ref[:, lanes(p)] * scale, lo)
            dom_scr[2 * p], dom_scr[2 * p + 1] = _head_pair_masked(do_ref[:, lanes(p)], lo)

        def tile(j, diag):
            sl = pl.ds(pl.multiple_of(j * CHUNK, CHUNK), CHUNK)
            keep = (lambda t: jnp.where(col < row, t, 0.0)) if diag else (lambda t: t)
            kbs = [k_ref[sl, lanes(p)] for p in pairs]
            vbs = [v_ref[sl, lanes(p)] for p in pairs]
            zs = [_nt(qm_scr[h], kbs[h // 2]) for h in heads]
            das = [_nt(dom_scr[h], vbs[h // 2]) for h in heads]
            lszs = [jnp.minimum(z, 0.0) - _softplus_neg_abs(z) for z in zs]
            lks = [keep(lsz - z) for lsz, z in zip(lszs, zs)]
            tls = [_split_dot(lk, upper_stack) for lk in lks]
            css = [jnp.sum(jnp.where(col == j, tab_ref[h, 0], 0.0), axis=1, keepdims=True) for h in heads]
            atts = [keep(jnp.exp(lsz + tl + cs)) for lsz, tl, cs in zip(lszs, tls, css)]
            dls = [att * da for att, da in zip(atts, das)]
            pres = [_split_dot(dl, before_stack) for dl in dls]
            pfxs = [pfx_scr[h] for h in heads]
            sigs = [jnp.exp(lsz) for lsz in lszs]
            dzs = [(dl * (1.0 - sig) - keep(sig * (pre + pfx))).astype(CDT)
                   for dl, sig, pre, pfx in zip(dls, sigs, pres, pfxs)]
            attc = [att.astype(CDT) for att in atts]
            kms = [jnp.concatenate(_head_pair_masked(kb, lo), axis=0) for kb in kbs]
            dqs = [_nn(jnp.concatenate([dzs[2 * p], dzs[2 * p + 1]], axis=1), kms[p]) for p in pairs]
            both = lambda xs, p: jnp.concatenate([xs[2 * p], xs[2 * p + 1]], axis=0)
            dks = [_tn(both(dzs, p), jnp.concatenate([qm_scr[2 * p], qm_scr[2 * p + 1]], axis=0)) for p in pairs]
            dvs = [_tn(both(attc, p), jnp.concatenate([dom_scr[2 * p], dom_scr[2 * p + 1]], axis=0)) for p in pairs]
            last = CHUNK - 1
            for p in pairs:
                dq_scr[:, lanes(p)] += dqs[p]
                dk_acc[sl, lanes(p)] += dks[p]
                dv_acc[sl, lanes(p)] += dvs[p]
            for h in heads:
                pfx_scr[h] = pfxs[h] + (pres[h][:, last:] + dls[h][:, last:])

        def step(j, carry):
            tile(j, False)
            return carry

        lax.fori_loop(0, qi, step, 0)
        tile(qi, True)
        dq_ref[...] = (dq_scr[...] * scale).astype(dq_ref.dtype)

        @pl.when(qi == nq - 1)
        def _():
            dk_ref[...] = dk_acc[...].astype(dk_ref.dtype)
            dv_ref[...] = dv_acc[...].astype(dv_ref.dtype)

    once = pl.Buffered(1)
    qs = BS((CHUNK, LW), lambda g, i: (i, g))
    ks = BS((T, LW), lambda g, i: (0, g))
    return _pallas(body, grid=(ngrp, nq),
                   in_specs=[qs, BS((T, LW), lambda g, i: (0, ngrp + g), pipeline_mode=once),
                             BS((T, LW), lambda g, i: (0, 2 * ngrp + g), pipeline_mode=once), qs,
                             BS((hb, 1, CHUNK, CHUNK), lambda g, i: (g, i, 0, 0))],
                   out_specs=[qs, ks, ks],
                   out_shape=[SDS((T, D), CDT)] * 3,
                   scratch_shapes=[pltpu.VMEM((T, LW), F32), pltpu.VMEM((T, LW), F32),
                                   pltpu.VMEM((CHUNK, LW), F32), pltpu.VMEM((hb, CHUNK, 1), F32),
                                   pltpu.VMEM((hb, CHUNK, LANES), CDT), pltpu.VMEM((hb, CHUNK, LANES), CDT)],
                   compiler_params=_params("parallel", "arbitrary"), name=name)(qkv, qkv, qkv, do, tab)


HALO = 8


def _shift_down(cur, prev8, s):
    if s == 0:
        return cur
    tt = cur.shape[0]
    row = lax.broadcasted_iota(jnp.int32, cur.shape, 0)
    fix = jnp.concatenate([pltpu.roll(prev8, s, 0), jnp.zeros((tt - HALO, cur.shape[1]), cur.dtype)], axis=0)
    return jnp.where(row < s, fix, pltpu.roll(cur, s, 0))


def _shift_up(cur, next8, s):
    if s == 0:
        return cur
    tt = cur.shape[0]
    row = lax.broadcasted_iota(jnp.int32, cur.shape, 0)
    fix = jnp.concatenate([jnp.zeros((tt - HALO, cur.shape[1]), cur.dtype), pltpu.roll(next8, HALO - s, 0)], axis=0)
    return jnp.where(row >= tt - s, fix, pltpu.roll(cur, tt - s, 0))


def _halo_specs(tt, width, T):
    per = tt // HALO
    last = T // HALO - 1
    cur = BS((tt, width), lambda i: (i, 0))
    prev = BS((HALO, width), lambda i: (jnp.maximum(i * per - 1, 0), 0))
    nxt = BS((HALO, width), lambda i: (jnp.minimum((i + 1) * per, last), 0))
    return cur, prev, nxt


def sc_fwd(proj, w, *, name):
    T, D3 = proj.shape
    D = D3 // 3
    K = w.shape[0]
    tt = _tile(T, 256)
    cur, prev, _ = _halo_specs(tt, D3, T)

    def body(p_ref, pp_ref, w_ref, o_ref):
        i = pl.program_id(0)
        pc = p_ref[:, D:2 * D] * p_ref[:, 2 * D:]
        pp = jnp.where(i > 0, pp_ref[:, D:2 * D] * pp_ref[:, 2 * D:], 0.0)
        u = jnp.zeros((tt, D), F32)
        for kk in range(K):
            u = u + w_ref[kk:kk + 1, :] * _shift_down(pc, pp, K - 1 - kk)
        o_ref[...] = (p_ref[:, :D] * u).astype(o_ref.dtype)

    return _pallas(body, grid=(T // tt,), in_specs=[cur, prev, BS((K, D), lambda i: (0, 0))],
                   out_specs=BS((tt, D), lambda i: (i, 0)), out_shape=SDS((T, D), CDT),
                   compiler_params=_params("parallel"), name=name)(proj, proj, w)


def sc_bwd(dm, proj, w, *, name):
    T, D3 = proj.shape
    D = D3 // 3
    K = w.shape[0]
    tt = _tile(T, 256)
    cur, prev, nxt = _halo_specs(tt, D3, T)
    dcur, _, dnxt = _halo_specs(tt, D, T)
    nsteps = T // tt

    def body(dm_ref, dmn_ref, p_ref, pp_ref, pn_ref, w_ref, o_ref, dw_ref):
        i = pl.program_id(0)
        b = p_ref[:, :D]
        c = p_ref[:, D:2 * D]
        h = p_ref[:, 2 * D:]
        pc = c * h
        pp = jnp.where(i > 0, pp_ref[:, D:2 * D] * pp_ref[:, 2 * D:], 0.0)
        dmv = dm_ref[...]
        du = dmv * b
        dun = jnp.where(i < nsteps - 1, dmn_ref[...] * pn_ref[:, :D], 0.0)
        u = jnp.zeros((tt, D), F32)
        dp = jnp.zeros((tt, D), F32)
        rows = []
        for kk in range(K):
            sh = _shift_down(pc, pp, K - 1 - kk)
            u = u + w_ref[kk:kk + 1, :] * sh
            dp = dp + w_ref[kk:kk + 1, :] * _shift_up(du, dun, K - 1 - kk)
            rows.append(jnp.sum(du * sh, axis=0, keepdims=True))
        rows.append(jnp.zeros((HALO - K, D), F32))
        o_ref[:, :D] = (dmv * u).astype(o_ref.dtype)
        o_ref[:, D:2 * D] = (dp * h).astype(o_ref.dtype)
        o_ref[:, 2 * D:] = (dp * c).astype(o_ref.dtype)

        @pl.when(i == 0)
        def _():
            dw_ref[...] = jnp.zeros_like(dw_ref)

        dw_ref[...] += jnp.concatenate(rows, axis=0)

    return _pallas(body, grid=(nsteps,), in_specs=[dcur, dnxt, cur, prev, nxt, BS((K, D), lambda i: (0, 0))],
                   out_specs=[BS((tt, D3), lambda i: (i, 0)), BS((HALO, D), lambda i: (0, 0))],
                   out_shape=[SDS((T, D3), CDT), SDS((HALO, D), F32)],
                   compiler_params=_params("arbitrary"), name=name)(dm, dm, proj, proj, proj, w)


def ssd_conv_fwd(xpre, w, b, *, name):
    T, C = xpre.shape
    K = w.shape[0]
    tt = _tile(T, 256)
    cur, prev, _ = _halo_specs(tt, C, T)

    def body(x_ref, xp_ref, w_ref, b_ref, pre_ref, o_ref):
        i = pl.program_id(0)
        xv = x_ref[...]
        xp = jnp.where(i > 0, xp_ref[...], 0.0)
        acc = jnp.broadcast_to(b_ref[...], (tt, C))
        for kk in range(K):
            acc = acc + w_ref[kk:kk + 1, :] * _shift_down(xv, xp, K - 1 - kk)
        pre_ref[...] = acc
        o_ref[...] = acc * _sigmoid(acc)

    row = BS((tt, C), lambda i: (i, 0))
    return _pallas(body, grid=(T // tt,), in_specs=[cur, prev, BS((K, C), lambda i: (0, 0)), BS((1, C), lambda i: (0, 0))],
                   out_specs=[row, row], out_shape=[SDS((T, C), F32)] * 2,
                   compiler_params=_params("parallel"), name=name)(xpre, xpre, w, b)


def ssd_conv_bwd(dxs, db_, dc_, pre, xpre, w, *, name):
    T, C = xpre.shape
    K = w.shape[0]
    tt = _tile(T, 256)
    cur, prev, nxt = _halo_specs(tt, C, T)
    widths = (dxs.shape[1], db_.shape[1], dc_.shape[1])
    gspecs = []
    for wd in widths:
        c_, _, n_ = _halo_specs(tt, wd, T)
        gspecs += [c_, n_]
    nsteps = T // tt

    def dsilu(p):
        s = _sigmoid(p)
        return s * (1.0 + p * (1.0 - s))

    def body(g0, g0n, g1, g1n, g2, g2n, pre_ref, pren_ref, x_ref, xp_ref, w_ref, dx_ref, dw_ref, dbias_ref):
        i = pl.program_id(0)
        gcur = jnp.concatenate([g0[...], g1[...], g2[...]], axis=1)
        gnxt = jnp.concatenate([g0n[...], g1n[...], g2n[...]], axis=1)
        dpre = gcur * dsilu(pre_ref[...])
        dpren = jnp.where(i < nsteps - 1, gnxt * dsilu(pren_ref[...]), 0.0)
        xv = x_ref[...]
        xp = jnp.where(i > 0, xp_ref[...], 0.0)
        dx = jnp.zeros((tt, C), F32)
        rows = []
        for kk in range(K):
            dx = dx + w_ref[kk:kk + 1, :] * _shift_up(dpre, dpren, K - 1 - kk)
            rows.append(jnp.sum(dpre * _shift_down(xv, xp, K - 1 - kk), axis=0, keepdims=True))
        rows.append(jnp.zeros((HALO - K, C), F32))
        dx_ref[...] = dx.astype(dx_ref.dtype)

        @pl.when(i == 0)
        def _():
            dw_ref[...] = jnp.zeros_like(dw_ref)
            dbias_ref[...] = jnp.zeros_like(dbias_ref)

        dw_ref[...] += jnp.concatenate(rows, axis=0)
        dbias_ref[...] += jnp.sum(dpre, axis=0, keepdims=True)

    return _pallas(body, grid=(nsteps,),
                   in_specs=gspecs + [cur, nxt, cur, prev, BS((K, C), lambda i: (0, 0))],
                   out_specs=[BS((tt, C), lambda i: (i, 0)), BS((HALO, C), lambda i: (0, 0)), BS((1, C), lambda i: (0, 0))],
                   out_shape=[SDS((T, C), CDT), SDS((HALO, C), F32), SDS((1, C), F32)],
                   compiler_params=_params("arbitrary"), name=name)(dxs, dxs, db_, db_, dc_, dc_, pre, pre, xpre, xpre, w)


def ssd_prep_fwd(dtpre, bias, alog, *, name):
    T = dtpre.shape[0]
    nc = T // CHUNK

    def body(p_ref, b_ref, a_ref, dt_ref, ac_ref, dtT_ref, acT_ref):
        xv = p_ref[...] + b_ref[...]
        dt = jnp.maximum(xv, 0.0) + _softplus_neg_abs(xv)
        row, col = _sb_tri()
        lower = (col <= row).astype(F32)
        ac = jnp.dot(lower, dt * -jnp.exp(a_ref[...]), precision=HI, preferred_element_type=F32)
        dt_ref[...] = dt
        ac_ref[...] = ac
        dtT_ref[...] = dt.T
        acT_ref[...] = ac.T

    blk = BS((CHUNK, LANES), lambda c: (c, 0))
    blkT = BS((LANES, CHUNK), lambda c: (0, c))
    vec = BS((1, LANES), lambda c: (0, 0))
    return _pallas(body, grid=(nc,), in_specs=[blk, vec, vec], out_specs=[blk, blk, blkT, blkT],
                   out_shape=[SDS((T, LANES), F32)] * 2 + [SDS((LANES, T), F32)] * 2,
                   compiler_params=_params("parallel"), name=name)(dtpre, bias, alog)


def ssd_prep_bwd(dac_c, ddt_c, dac_r, ddt_r, dtpre, bias, alog, *, name):
    T = dtpre.shape[0]
    nc = T // CHUNK

    def body(dac_ref, ddt_ref, dacr_ref, ddtr_ref, p_ref, b_ref, a_ref, o_ref, da_ref, db_ref):
        c = pl.program_id(0)
        xv = p_ref[...] + b_ref[...]
        dt = jnp.maximum(xv, 0.0) + _softplus_neg_abs(xv)
        av = -jnp.exp(a_ref[...])
        row, col = _sb_tri()
        after = (col >= row).astype(F32)
        dda = jnp.dot(after, dac_ref[...] + dacr_ref[...].T, precision=HI, preferred_element_type=F32)
        dpre = (ddt_ref[...] + ddtr_ref[...].T + av * dda) * _sigmoid(xv)
        o_ref[...] = dpre

        @pl.when(c == 0)
        def _():
            da_ref[...] = jnp.zeros_like(da_ref)
            db_ref[...] = jnp.zeros_like(db_ref)

        da_ref[...] += jnp.sum(dt * dda, axis=0, keepdims=True) * av
        db_ref[...] += jnp.sum(dpre, axis=0, keepdims=True)

    blk = BS((CHUNK, LANES), lambda c: (c, 0))
    blkT = BS((LANES, CHUNK), lambda c: (0, c))
    vec = BS((1, LANES), lambda c: (0, 0))
    return _pallas(body, grid=(nc,), in_specs=[blk, blk, blkT, blkT, blk, vec, vec], out_specs=[blk, vec, vec],
                   out_shape=[SDS((T, LANES), F32), SDS((1, LANES), F32), SDS((1, LANES), F32)],
                   compiler_params=_params("arbitrary"), name=name)(dac_c, ddt_c, dac_r, ddt_r, dtpre, bias, alog)


def _ssd_specs(cfg, T, rev):
    nc = T // CHUNK
    GW = SSD_HPG * HEAD_DIM
    bo = cfg.DI // LANES
    co = (cfg.DI + cfg.G * LANES) // LANES
    ci = (lambda c: nc - 1 - c) if rev else (lambda c: c)
    return dict(
        x=BS((CHUNK, GW), lambda g, c: (ci(c), g)),
        b=BS((CHUNK, LANES), lambda g, c: (ci(c), bo + g)),
        c=BS((CHUNK, LANES), lambda g, c: (ci(c), co + g)),
        col=BS((CHUNK, LANES), lambda g, c: (ci(c), 0)),
        row=BS((LANES, CHUNK), lambda g, c: (0, ci(c))),
        vec=BS((1, LANES), lambda g, c: (0, 0)),
        ng=BS((1, GW), lambda g, c: (0, g)),
        hp=BS((1, 1, SSD_HPG, HEAD_DIM, LANES), lambda g, c: (g, ci(c), 0, 0, 0)),
    )


def ssd_chunk_fwd(xbc, z, dt, ac, dtT, acT, dskip, ng, cfg, *, name):
    T = xbc.shape[0]
    nc = T // CHUNK
    G, DI = cfg.G, cfg.DI
    GW = SSD_HPG * HEAD_DIM
    sp = _ssd_specs(cfg, T, False)

    def body(x_ref, b_ref, c_ref, z_ref, dt_ref, ac_ref, dtT_ref, acT_ref, d_ref, ng_ref,
             y_ref, yn_ref, hp_ref, h_scr):
        g = pl.program_id(0)
        c = pl.program_id(1)

        @pl.when(c == 0)
        def _():
            h_scr[...] = jnp.zeros_like(h_scr)

        row, col = _sb_tri()
        causal = row >= col
        Bm = b_ref[...].astype(CDT)
        Cm = c_ref[...].astype(CDT)
        Gm = _nt(Cm, Bm)
        dtb, acb, dtTb, acTb, dv = dt_ref[...], ac_ref[...], dtT_ref[...], acT_ref[...], d_ref[...]
        heads = range(SSD_HPG)
        hhs = [g * SSD_HPG + r for r in heads]
        ac_cs = [_colsel(acb, hh) for hh in hhs]
        dt_cs = [_colsel(dtb, hh) for hh in hhs]
        Xs = [x_ref[:, r * HEAD_DIM:(r + 1) * HEAD_DIM] for r in heads]
        Hps = [h_scr[r] for r in heads]
        Ws = [(Gm * jnp.where(causal, jnp.exp(ac_c - _rowsel(acTb, hh)), 0.0) * _rowsel(dtTb, hh)).astype(CDT)
              for ac_c, hh in zip(ac_cs, hhs)]
        Qs = [_nt(Cm, Hp.astype(CDT)) for Hp in Hps]
        Yds = [_nn(Wm, Xr.astype(CDT)) for Wm, Xr in zip(Ws, Xs)]
        aLs = [_rowsel(ac_c, CHUNK - 1) for ac_c in ac_cs]
        Xes = [(Xr * (jnp.exp(aL - ac_c) * dt_c)).astype(CDT) for Xr, aL, ac_c, dt_c in zip(Xs, aLs, ac_cs, dt_cs)]
        Sts = [_tn(Xe, Bm) for Xe in Xes]
        for r in heads:
            hp_ref[0, 0, r] = Hps[r]
            h_scr[r] = jnp.exp(aLs[r]) * Hps[r] + Sts[r]
            y_ref[:, r * HEAD_DIM:(r + 1) * HEAD_DIM] = (Yds[r] + jnp.exp(ac_cs[r]) * Qs[r]
                                                         + _colsel(dv, hhs[r]) * Xs[r])
        zz = z_ref[...]
        yz = y_ref[...] * (zz * _sigmoid(zz))
        rstd = lax.rsqrt(jnp.mean(yz * yz, axis=-1, keepdims=True) + RMS_EPS)
        yn_ref[...] = (yz * rstd * ng_ref[...]).astype(yn_ref.dtype)

    return _pallas(body, grid=(G, nc),
                   in_specs=[sp["x"], sp["b"], sp["c"], sp["x"], sp["col"], sp["col"], sp["row"], sp["row"], sp["vec"], sp["ng"]],
                   out_specs=[sp["x"], sp["x"], sp["hp"]],
                   out_shape=[SDS((T, DI), F32), SDS((T, DI), CDT), SDS((G, nc, SSD_HPG, HEAD_DIM, LANES), F32)],
                   scratch_shapes=[pltpu.VMEM((SSD_HPG, HEAD_DIM, LANES), F32)],
                   compiler_params=_params("parallel", "arbitrary"), name=name)(xbc, xbc, xbc, z, dt, ac, dtT, acT, dskip, ng)


def ssd_chunk_bwd(dyn, xbc, z, y, hp, dt, ac, dtT, acT, dskip, ng, cfg, *, name):
    T = xbc.shape[0]
    nc = T // CHUNK
    G, DI = cfg.G, cfg.DI
    GW = SSD_HPG * HEAD_DIM
    sp = _ssd_specs(cfg, T, True)

    def body(dyn_ref, x_ref, b_ref, c_ref, z_ref, y_ref, hp_ref, dt_ref, ac_ref, dtT_ref, acT_ref, d_ref, ng_ref,
             dx_ref, db_ref, dc_ref, dz_ref, colp_ref, rowp_ref, dd_ref, dng_ref, dh_scr):
        g = pl.program_id(0)
        c = pl.program_id(1)

        @pl.when(c == 0)
        def _():
            dh_scr[...] = jnp.zeros_like(dh_scr)
            dd_ref[...] = jnp.zeros_like(dd_ref)
            dng_ref[...] = jnp.zeros_like(dng_ref)

        zz = z_ref[...]
        sg = _sigmoid(zz)
        gate = zz * sg
        yv = y_ref[...]
        yz = yv * gate
        rstd = lax.rsqrt(jnp.mean(yz * yz, axis=-1, keepdims=True) + RMS_EPS)
        nhat = yz * rstd
        dynv = dyn_ref[...].astype(F32)
        dng_ref[...] += jnp.sum(dynv * nhat, axis=0, keepdims=True)
        dnh = dynv * ng_ref[...]
        dyz = rstd * (dnh - nhat * jnp.mean(dnh * nhat, axis=-1, keepdims=True))
        dz_ref[...] = (dyz * yv * (sg * (1.0 + zz * (1.0 - sg)))).astype(dz_ref.dtype)
        dy_all = dyz * gate

        row, col = _sb_tri()
        causal = row >= col
        lane1 = lax.broadcasted_iota(jnp.int32, (1, LANES), 1)
        sub8 = lax.broadcasted_iota(jnp.int32, (HALO, CHUNK), 0)
        subc = lax.broadcasted_iota(jnp.int32, (CHUNK, 1), 0)
        Bf = b_ref[...]
        Cf = c_ref[...]
        Bm = Bf.astype(CDT)
        Cm = Cf.astype(CDT)
        Gm = _nt(Cm, Bm)
        dtb, acb, dtTb, acTb, dv = dt_ref[...], ac_ref[...], dtT_ref[...], acT_ref[...], d_ref[...]
        dG = jnp.zeros((CHUNK, CHUNK), F32)
        dBm = jnp.zeros((CHUNK, LANES), F32)
        dCm = jnp.zeros((CHUNK, LANES), F32)
        colp = jnp.zeros((CHUNK, LANES), F32)
        rowp = jnp.zeros((HALO, CHUNK), F32)
        ddv = jnp.zeros((1, LANES), F32)
        for r in range(SSD_HPG):
            hh = g * SSD_HPG + r
            ac_c, ac_r = _colsel(acb, hh), _rowsel(acTb, hh)
            dt_c, dt_r = _colsel(dtb, hh), _rowsel(dtTb, hh)
            S = jnp.where(causal, jnp.exp(ac_c - ac_r), 0.0)
            GS = Gm * S
            Wm = GS * dt_r
            Xr = x_ref[:, r * HEAD_DIM:(r + 1) * HEAD_DIM]
            Xc = Xr.astype(CDT)
            Hp = hp_ref[0, 0, r]
            Hc = Hp.astype(CDT)
            dY = dy_all[:, r * HEAD_DIM:(r + 1) * HEAD_DIM]
            dYc = dY.astype(CDT)
            dsk = _colsel(dv, hh)
            ddv = ddv + jnp.where(lane1 == r, jnp.sum(jnp.sum(dY * Xr, axis=1, keepdims=True), axis=0, keepdims=True), 0.0)
            dX = dY * dsk + _tn(Wm.astype(CDT), dYc)
            dWm = _nt(dYc, Xc)
            dG = dG + dWm * S * dt_r
            Mm = dWm * Wm
            Nm = dWm * GS
            dac_c = jnp.sum(Mm, axis=1, keepdims=True)
            dac_r = -jnp.sum(Mm, axis=0, keepdims=True)
            ddt_r = jnp.sum(Nm, axis=0, keepdims=True)
            E = jnp.exp(ac_c)
            Q = _nt(Cm, Hc)
            dQ = (dY * E).astype(CDT)
            dac_c = dac_c + jnp.sum(dY * Q, axis=1, keepdims=True) * E
            dCm = dCm + _nn(dQ, Hc)
            dHp = _tn(dQ, Cm)
            dHn = dh_scr[r]
            dHc = dHn.astype(CDT)
            aL = _rowsel(ac_c, CHUNK - 1)
            eaL = jnp.exp(aL)
            ex = jnp.exp(aL - ac_c)
            e = ex * dt_c
            dHp = dHp + eaL * dHn
            daL = eaL * jnp.sum(jnp.sum(dHn * Hp, axis=1, keepdims=True), axis=0, keepdims=True)
            T2 = _nt(Bm, dHc)
            dX = dX + T2 * e
            de = jnp.sum(Xr * T2, axis=1, keepdims=True)
            dBm = dBm + _nn((Xr * e).astype(CDT), dHc)
            dee = de * e
            daL = daL + jnp.sum(dee, axis=0, keepdims=True)
            dac_c = dac_c - dee + jnp.where(subc == CHUNK - 1, daL, 0.0)
            ddt_c = de * ex
            dh_scr[r] = dHp
            dx_ref[:, r * HEAD_DIM:(r + 1) * HEAD_DIM] = dX
            colp = jnp.where(col == r, dac_c, colp)
            colp = jnp.where(col == SSD_HPG + r, ddt_c, colp)
            rowp = jnp.where(sub8 == r, dac_r, rowp)
            rowp = jnp.where(sub8 == SSD_HPG + r, ddt_r, rowp)
        dGc = dG.astype(CDT)
        db_ref[...] = dBm + _tn(dGc, Cm)
        dc_ref[...] = dCm + _nn(dGc, Bm)
        colp_ref[0] = colp
        rowp_ref[0] = rowp
        dd_ref[0] += ddv

    f32 = lambda shape: SDS(shape, F32)
    return _pallas(body, grid=(G, nc),
                   in_specs=[sp["x"], sp["x"], sp["b"], sp["c"], sp["x"], sp["x"], sp["hp"], sp["col"], sp["col"],
                             sp["row"], sp["row"], sp["vec"], sp["ng"]],
                   out_specs=[sp["x"],
                              BS((CHUNK, LANES), lambda g, c: (nc - 1 - c, g)),
                              BS((CHUNK, LANES), lambda g, c: (nc - 1 - c, g)),
                              sp["x"],
                              BS((1, CHUNK, LANES), lambda g, c: (g, nc - 1 - c, 0)),
                              BS((1, HALO, CHUNK), lambda g, c: (g, 0, nc - 1 - c)),
                              BS((1, 1, LANES), lambda g, c: (g, 0, 0)),
                              sp["ng"]],
                   out_shape=[f32((T, DI)), f32((T, G * LANES)), f32((T, G * LANES)), SDS((T, DI), CDT),
                              f32((G, T, LANES)), f32((G, HALO, T)), f32((G, 1, LANES)), f32((1, DI))],
                   scratch_shapes=[pltpu.VMEM((SSD_HPG, HEAD_DIM, LANES), F32)],
                   compiler_params=_params("parallel", "arbitrary"), name=name)(
                       dyn, xbc, xbc, xbc, z, y, hp, dt, ac, dtT, acT, dskip, ng)


HBM = pl.BlockSpec(memory_space=pltpu.HBM)
VMEM = pl.BlockSpec(memory_space=pltpu.VMEM)


def _me():
    return lax.axis_index("x"), lax.axis_index("y"), lax.axis_index("c")


def _other_chips(x, y):
    return [(1 - x, y), (x, 1 - y), (1 - x, 1 - y)]


def _rcopy(src, dst, send_sem, recv_sem, to):
    return pltpu.make_async_remote_copy(src_ref=src, dst_ref=dst, send_sem=send_sem, recv_sem=recv_sem,
                                        device_id=to, device_id_type=MESH)


def allgather_weights(packs, *, name):
    n = len(packs)

    def body(*refs):
        srcs, outs, (send_sems, recv_sems) = refs[:n], refs[n:2 * n], refs[2 * n:]
        x, y, c = _me()
        k = 2 * x + y
        sibling = (x, y, 1 - c)
        chips = _other_chips(x, y)
        sends, passed = [], []
        for a, (src, out) in enumerate(zip(srcs, outs)):
            for j, (cx, cy) in enumerate(chips):
                s = 6 * a + j
                sends.append(_rcopy(src.at[c], out.at[k, c], send_sems.at[s], recv_sems.at[s], (cx, cy, c)))
        for cp in sends:
            cp.start()
        for a, out in enumerate(outs):
            for j, (cx, cy) in enumerate(chips):
                s = 6 * a + j
                landed = out.at[2 * cx + cy, c]
                _rcopy(landed, landed, send_sems.at[s], recv_sems.at[s], (cx, cy, c)).wait_recv()
                fw = _rcopy(landed, landed, send_sems.at[s + 3], recv_sems.at[s + 3], sibling)
                fw.start()
                passed.append(fw)
        for a, out in enumerate(outs):
            for j, (cx, cy) in enumerate(chips):
                s = 6 * a + j + 3
                theirs = out.at[2 * cx + cy, 1 - c]
                _rcopy(theirs, theirs, send_sems.at[s], recv_sems.at[s], sibling).wait_recv()
        for cp in sends + passed:
            cp.wait_send()

    return _pallas(body, in_specs=[HBM] * n, out_specs=[HBM] * n,
                   out_shape=[SDS((N_CHIPS,) + p.shape, p.dtype) for p in packs],
                   scratch_shapes=[pltpu.SemaphoreType.DMA((6 * n,)), pltpu.SemaphoreType.DMA((6 * n,))],
                   name=name)(*packs)


def sibling_swap_half(ps, *, name):
    n = len(ps)

    def body(*refs):
        srcs, outs, (send_sems, recv_sems) = refs[:n], refs[n:2 * n], refs[2 * n:]
        x, y, c = _me()
        cps = [_rcopy(src.at[1 - c], out, send_sems.at[a], recv_sems.at[a], (x, y, 1 - c))
               for a, (src, out) in enumerate(zip(srcs, outs))]
        for cp in cps:
            cp.start()
        for cp in cps:
            cp.wait()

    return _pallas(body, in_specs=[HBM] * n, out_specs=[HBM] * n,
                   out_shape=[SDS(p.shape[1:], p.dtype) for p in ps],
                   scratch_shapes=[pltpu.SemaphoreType.DMA((n,)), pltpu.SemaphoreType.DMA((n,))], name=name)(*ps)


def chip_scatter(s1s, *, name):
    n = len(s1s)

    def body(*refs):
        srcs, outs, (send_sems, recv_sems) = refs[:n], refs[n:2 * n], refs[2 * n:]
        x, y, c = _me()
        k = 2 * x + y
        chips = _other_chips(x, y)
        sends = [_rcopy(src.at[2 * cx + cy], out.at[k], send_sems.at[3 * a + j], recv_sems.at[3 * a + j], (cx, cy, c))
                 for a, (src, out) in enumerate(zip(srcs, outs)) for j, (cx, cy) in enumerate(chips)]
        for cp in sends:
            cp.start()
        for a, out in enumerate(outs):
            for j, (cx, cy) in enumerate(chips):
                slot = out.at[2 * cx + cy]
                _rcopy(slot, slot, send_sems.at[3 * a + j], recv_sems.at[3 * a + j], (cx, cy, c)).wait_recv()
        for cp in sends:
            cp.wait_send()

    return _pallas(body, in_specs=[HBM] * n, out_specs=[HBM] * n, out_shape=[SDS(p.shape, p.dtype) for p in s1s],
                   scratch_shapes=[pltpu.SemaphoreType.DMA((3 * n,)), pltpu.SemaphoreType.DMA((3 * n,))],
                   name=name)(*s1s)


def sibling_pair(r2s, *, name):
    n = len(r2s)

    def body(*refs):
        srcs, outs, (send_sems, recv_sems) = refs[:n], refs[n:2 * n], refs[2 * n:]
        x, y, c = _me()
        cps = [_rcopy(src.at[c], out.at[c], send_sems.at[a], recv_sems.at[a], (x, y, 1 - c))
               for a, (src, out) in enumerate(zip(srcs, outs))]
        for cp in cps:
            cp.start()
        for a, out in enumerate(outs):
            theirs = out.at[1 - c]
            _rcopy(theirs, theirs, send_sems.at[a], recv_sems.at[a], (x, y, 1 - c)).wait_recv()
        for cp in cps:
            cp.wait_send()

    return _pallas(body, in_specs=[HBM] * n, out_specs=[HBM] * n, out_shape=[SDS(p.shape, p.dtype) for p in r2s],
                   input_output_aliases={a: a for a in range(n)},
                   scratch_shapes=[pltpu.SemaphoreType.DMA((n,)), pltpu.SemaphoreType.DMA((n,))], name=name)(*r2s)


def allgather_small(v, *, name):
    R, W = v.shape

    def body(v_ref, out_ref, sum_ref, send_sems, recv_sems):
        x, y, c = _me()
        me = 4 * x + 2 * y + c
        out_ref[me] = v_ref[...]
        peers = []
        for m in range(1, N_DEV):
            px = 1 - x if m & 4 else x
            py = 1 - y if m & 2 else y
            pc = 1 - c if m & 1 else c
            peers.append((px, py, pc))
        sends = [_rcopy(v_ref, out_ref.at[me], send_sems.at[j], recv_sems.at[j], p) for j, p in enumerate(peers)]
        for cp in sends:
            cp.start()
        for j, (px, py, pc) in enumerate(peers):
            slot = out_ref.at[4 * px + 2 * py + pc]
            _rcopy(slot, slot, send_sems.at[j], recv_sems.at[j], (px, py, pc)).wait_recv()
        for cp in sends:
            cp.wait_send()
        acc = out_ref[0]
        for d in range(1, N_DEV):
            acc = acc + out_ref[d]
        sum_ref[...] = acc

    return _pallas(body, in_specs=[VMEM], out_specs=[VMEM, VMEM],
                   out_shape=[SDS((N_DEV, R, W), v.dtype), SDS((R, W), v.dtype)],
                   scratch_shapes=[pltpu.SemaphoreType.DMA((N_DEV - 1,)), pltpu.SemaphoreType.DMA((N_DEV - 1,))],
                   name=name)(v)


def _row_tile(R, cap=512):
    best = None
    for t in range(32, cap + 1, 32):
        if R % t == 0:
            best = t
    assert best is not None, R
    return best


def add_halves(p, recv, c_idx, *, name):
    _, n, Rh, W = p.shape
    tr = _row_tile(Rh)

    def body(c_ref, p_ref, r_ref, o_ref):
        o_ref[...] = (p_ref[0].astype(F32) + r_ref[...].astype(F32)).astype(o_ref.dtype)

    gs = pltpu.PrefetchScalarGridSpec(
        num_scalar_prefetch=1, grid=(n, Rh // tr),
        in_specs=[BS((1, 1, tr, W), lambda s, i, c_ref: (c_ref[0], s, i, 0)), BS((1, tr, W), lambda s, i, c_ref: (s, i, 0))],
        out_specs=BS((1, tr, W), lambda s, i, c_ref: (s, i, 0)))
    return _pallas(body, grid_spec=gs, out_shape=SDS((n, Rh, W), p.dtype),
                   compiler_params=_params("parallel", "parallel"), name=name)(c_idx, p, recv)


def sum_slots(s1, got, kc_idx, *, name):
    n, Rh, W = got.shape
    tr = _row_tile(Rh)

    def body(kc_ref, s_ref, g1, g2, g3, o_ref):
        o_ref[0] = s_ref[0].astype(F32) + g1[0].astype(F32) + g2[0].astype(F32) + g3[0].astype(F32)

    slot = lambda d: BS((1, tr, W), lambda i, kc: ((kc[0] + d) % n, i, 0))
    gs = pltpu.PrefetchScalarGridSpec(
        num_scalar_prefetch=1, grid=(Rh // tr,), in_specs=[slot(0), slot(1), slot(2), slot(3)],
        out_specs=BS((1, tr, W), lambda i, kc: (kc[1], i, 0)))
    return _pallas(body, grid_spec=gs, out_shape=SDS((2, Rh, W), F32),
                   compiler_params=_params("parallel"), name=name)(kc_idx, s1, got, got, got)


BIG = [("ffn1_w_gu", True), ("ffn1_w_down", False), ("ffn2_w_gu", True), ("ffn2_w_down", False),
       ("sb_w_qkv", True), ("sb_w_o", False), ("ssd_w_in", True), ("ssd_w_out", False),
       ("sc_w_in", True), ("sc_w_out", False)]
PACK_ALIGN = 64


def _big_layout(shards):
    groups = {}
    for name, cols in BIG:
        L, K, n = shards[name].shape
        items, off = groups.setdefault(n, ([], 0))
        items.append((name, cols, (L, K, n), off, L * K))
        groups[n] = (items, off + L * K)
    return {w: (items, -(-rows // PACK_ALIGN) * PACK_ALIGN) for w, (items, rows) in groups.items()}


def _pack_rows(pieces, total, axis):
    used = sum(p.shape[axis] for p in pieces)
    if total > used:
        pad_shape = list(pieces[0].shape)
        pad_shape[axis] = total - used
        pieces = pieces + [jnp.zeros(pad_shape, pieces[0].dtype)]
    return jnp.concatenate(pieces, axis=axis) if len(pieces) > 1 else pieces[0]


def _gather_big(shards, tag):
    layout = _big_layout(shards)
    packs = [_pack_rows([shards[n].astype(CDT).reshape(-1, w) for n, *_ in items], total, 0).reshape(2, total // 2, w)
             for w, (items, total) in layout.items()]
    fulls = allgather_weights(packs, name=f"{tag}_allgather")
    k_chip = 2 * lax.axis_index("x") + lax.axis_index("y")
    out = {}
    for (w, (items, total)), pack, full in zip(layout.items(), packs, fulls):
        full = lax.dynamic_update_slice(full, pack[None], (k_chip, 0, 0, 0)).reshape(N_CHIPS, total, w)
        for name, cols, (L, K, n), off, rows in items:
            piece = full[:, off:off + rows].reshape(N_CHIPS, L, K, n)
            if cols:
                out[name] = piece.transpose(1, 2, 0, 3).reshape(L, K, N_CHIPS * n)
            else:
                out[name] = piece.transpose(1, 0, 2, 3).reshape(L, N_CHIPS * K, n)
    return out


def _reduce_big(grads, shards, tag):
    layout = _big_layout(shards)
    ps = []
    for w, (items, total) in layout.items():
        pieces = []
        for name, cols, (L, K, n), off, rows in items:
            for g in grads[name]:
                if cols:
                    g = g.reshape(K, N_CHIPS, n).transpose(1, 0, 2)
                else:
                    g = g.reshape(N_CHIPS, K, n)
                pieces.append(g.astype(CDT))
        pack = _pack_rows(pieces, total, 1)
        ps.append(pack.reshape(N_CHIPS, 2, total // 2, w).transpose(1, 0, 2, 3))
    c_core = lax.axis_index("c").astype(jnp.int32)
    k_chip = (2 * lax.axis_index("x") + lax.axis_index("y")).astype(jnp.int32)
    recvs = sibling_swap_half(ps, name=f"{tag}_sibling_swap")
    s1s = [add_halves(p, r, c_core.reshape(1), name=f"{tag}_add_halves_{p.shape[-1]}") for p, r in zip(ps, recvs)]
    gots = chip_scatter(s1s, name=f"{tag}_chip_scatter")
    r2s = [sum_slots(s1, got, jnp.stack([k_chip, c_core]), name=f"{tag}_sum_slots_{s1.shape[-1]}")
           for s1, got in zip(s1s, gots)]
    reds = sibling_pair(r2s, name=f"{tag}_sibling_pair")
    out = {}
    for (w, (items, total)), red in zip(layout.items(), reds):
        red = red.reshape(total, w)
        for name, cols, shape, off, rows in items:
            out[name] = red[off:off + rows].reshape(shape)
    return out


def _pad_lanes(v, width=LANES):
    return jnp.pad(v, ((0, 0), (0, width - v.shape[1])))


def _small_pack(vecs):
    flat, meta, off = [], [], 0
    for v in vecs:
        flat.append(v.reshape(-1))
        meta.append((off, v.size, v.shape))
        off += v.size
    total = -(-off // (8 * LANES)) * 8 * LANES
    flat.append(jnp.zeros((total - off,), F32))
    return jnp.concatenate(flat).reshape(-1, LANES), meta


def _small_unpack(pack, meta):
    flat = pack.reshape(-1)
    return [flat[off:off + size].reshape(shape) for off, size, shape in meta]


def _ffn_fwd(x, g, wgu, wd, tag):
    h = rms_fwd(x, g, name=f"{tag}_norm")
    gate, up, a = ffn_up(h, wgu, name=f"{tag}_gu")
    return matmul(a, wd, res=x, scale=0.5, name=f"{tag}_down"), (x, h, gate, up, a)


def _ffn_bwd(dout, saved, g, wgu, wd, tag):
    x, h, gate, up, a = saved
    F = wd.shape[0]
    dwd = matmul(a, dout, ta=True, out_dtype=CDT, scale=0.5, name=f"{tag}_dwd")
    dgate, dup = ffn_dact(dout, wd, gate, up, 0.5, name=f"{tag}_dact")
    dwgu = jnp.concatenate([matmul(h, dgate, ta=True, out_dtype=CDT, name=f"{tag}_dwg"),
                            matmul(h, dup, ta=True, out_dtype=CDT, name=f"{tag}_dwu")], axis=1)
    dh = matmul(dgate, wgu, tb=True, name=f"{tag}_dh_g")
    dh = matmul(dup, wgu, tb=True, b_k_start=F, res=dh, name=f"{tag}_dh_u")
    dx, dg = rms_bwd(dh, x, g, dout, name=f"{tag}_dnorm")
    return dx, dg, dwgu, dwd


def _sb_mixer_fwd(x, g, wqkv, wo, cfg, tag):
    h = rms_fwd(x, g, name=f"{tag}_norm")
    qkv = matmul(h, wqkv, out_dtype=CDT, name=f"{tag}_qkv")
    o, tab = sb_fwd(qkv, cfg.sb_heads, name=f"{tag}_attn")
    return matmul(o, wo, res=x, name=f"{tag}_out"), (x, h, qkv, tab, o)


def _sb_mixer_bwd(dout, saved, g, wqkv, wo, cfg, tag):
    x, h, qkv, tab, o = saved
    dwo = matmul(o, dout, ta=True, out_dtype=CDT, name=f"{tag}_dwo")
    do = matmul(dout, wo, tb=True, out_dtype=CDT, name=f"{tag}_do")
    dqkv = jnp.concatenate(sb_bwd(qkv, do, tab, cfg.sb_heads, name=f"{tag}_dattn"), axis=1)
    dwqkv = matmul(h, dqkv, ta=True, out_dtype=CDT, name=f"{tag}_dwqkv")
    dh = matmul(dqkv, wqkv, tb=True, name=f"{tag}_dh")
    dx, dg = rms_bwd(dh, x, g, dout, name=f"{tag}_dnorm")
    return dx, dg, dwqkv, dwo


def _sc_mixer_fwd(x, g, win, convw, wout, tag):
    h = rms_fwd(x, g, name=f"{tag}_norm")
    proj = matmul(h, win, name=f"{tag}_in")
    m = sc_fwd(proj, convw, name=f"{tag}_conv")
    return matmul(m, wout, res=x, name=f"{tag}_out"), (x, h, proj, m)


def _sc_mixer_bwd(dout, saved, g, win, convw, wout, tag):
    x, h, proj, m = saved
    dwout = matmul(m, dout, ta=True, out_dtype=CDT, name=f"{tag}_dwout")
    dm = matmul(dout, wout, tb=True, name=f"{tag}_dm")
    dproj, dconvw = sc_bwd(dm, proj, convw, name=f"{tag}_dconv")
    dwin = matmul(h, dproj, ta=True, out_dtype=CDT, name=f"{tag}_dwin")
    dh = matmul(dproj, win, tb=True, name=f"{tag}_dh")
    dx, dg = rms_bwd(dh, x, g, dout, name=f"{tag}_dnorm")
    return dx, dg, dwin, dconvw[:convw.shape[0]], dwout


def _ssd_split_w(win, cfg):
    DI, CV = cfg.DI, cfg.conv_dim
    return win[:, :DI], win[:, DI:DI + CV], _pad_lanes(win[:, DI + CV:])


def _ssd_mixer_fwd(x, g, win, convw, convb, dtb, alog, dskip, ng, wout, cfg, tag):
    wz, wx, wdt = _ssd_split_w(win, cfg)
    h = rms_fwd(x, g, name=f"{tag}_norm")
    z = matmul(h, wz, name=f"{tag}_z")
    xpre = matmul(h, wx, name=f"{tag}_xbc")
    dtpre = matmul(h, wdt, name=f"{tag}_dt")
    pre, xbc = ssd_conv_fwd(xpre, convw, convb, name=f"{tag}_conv")
    dt, ac, dtT, acT = ssd_prep_fwd(dtpre, dtb, alog, name=f"{tag}_prep")
    y, yn, hp = ssd_chunk_fwd(xbc, z, dt, ac, dtT, acT, dskip, ng, cfg, name=f"{tag}_scan")
    out = matmul(yn, wout, res=x, name=f"{tag}_out")
    return out, (x, h, z, xpre, dtpre, pre, xbc, dt, ac, dtT, acT, y, yn, hp)


def _ssd_mixer_bwd(dout, saved, g, win, convw, convb, dtb, alog, dskip, ng, wout, cfg, tag):
    x, h, z, xpre, dtpre, pre, xbc, dt, ac, dtT, acT, y, yn, hp = saved
    T = x.shape[0]
    NH, G = cfg.NH, cfg.G
    wz, wx, wdt = _ssd_split_w(win, cfg)
    dwout = matmul(yn, dout, ta=True, out_dtype=CDT, name=f"{tag}_dwout")
    dyn = matmul(dout, wout, tb=True, name=f"{tag}_dyn")
    dxs, db, dc, dz, colp, rowp, dd, dng = ssd_chunk_bwd(dyn, xbc, z, y, hp, dt, ac, dtT, acT, dskip, ng, cfg,
                                                        name=f"{tag}_dscan")
    col = lambda lo: _pad_lanes(colp[:, :, lo:lo + SSD_HPG].transpose(1, 0, 2).reshape(T, NH))
    row = lambda lo: jnp.pad(rowp[:, lo:lo + SSD_HPG, :].reshape(NH, T), ((0, LANES - NH), (0, 0)))
    ddtpre, dalog, ddtb = ssd_prep_bwd(col(0), col(SSD_HPG), row(0), row(SSD_HPG), dtpre, dtb, alog, name=f"{tag}_dprep")
    dxpre, dconvw, dconvb = ssd_conv_bwd(dxs, db, dc, pre, xpre, convw, name=f"{tag}_dconv")
    dwz = matmul(h, dz, ta=True, out_dtype=CDT, name=f"{tag}_dwz")
    dwx = matmul(h, dxpre, ta=True, out_dtype=CDT, name=f"{tag}_dwx")
    dwdt = matmul(h, ddtpre, ta=True, out_dtype=CDT, name=f"{tag}_dwdt")
    dh = matmul(dz, wz, tb=True, name=f"{tag}_dh_z")
    dh = matmul(dxpre, wx, tb=True, res=dh, name=f"{tag}_dh_x")
    dh = matmul(ddtpre, wdt, tb=True, res=dh, name=f"{tag}_dh_dt")
    dx, dg = rms_bwd(dh, x, g, dout, name=f"{tag}_dnorm")
    dwin = jnp.concatenate([dwz, dwx, dwdt[:, :NH]], axis=1)
    small = dict(conv_w=dconvw[:convw.shape[0]], conv_b=dconvb, dt_bias=ddtb[:, :NH], a_log=dalog[:, :NH],
                 d=dd[:, 0, :SSD_HPG].reshape(1, NH), norm=dng)
    return dx, dg, dwin, dwout, small


INPUT_NAMES = ['x', 'ffn1_norm', 'ffn1_w_gu', 'ffn1_w_down', 'mix_norm', 'ffn2_norm', 'ffn2_w_gu', 'ffn2_w_down',
               'sb_w_qkv', 'sb_w_o', 'ssd_w_in', 'ssd_conv_w', 'ssd_conv_b', 'ssd_dt_bias', 'ssd_a_log', 'ssd_d',
               'ssd_norm', 'ssd_w_out', 'sc_w_in', 'sc_conv_w', 'sc_w_out', 'final_norm']
WEIGHT_NAMES = INPUT_NAMES[1:]
SHARDED_SMALL = ("ssd_conv_w", "sc_conv_w")


def train_step(a, cfg):
    x = a["x"][0]
    tgt = a["loss_target"][0]
    big_names = [n for n, _ in BIG]
    k_chip = 2 * lax.axis_index("x") + lax.axis_index("y")

    w = _gather_big({n: a[n] for n in big_names}, "w")
    conv_pack, conv_meta = _small_pack([a[n] for n in SHARDED_SMALL])
    conv_all, _ = allgather_small(conv_pack, name="conv_w_allgather")
    conv_full = {}
    for i, n in enumerate(SHARDED_SMALL):
        per_chip = [_small_unpack(conv_all[2 * kk], conv_meta)[i] for kk in range(N_CHIPS)]
        conv_full[n] = jnp.concatenate(per_chip, axis=-1)

    def vec(name, i):
        return a[name][i:i + 1]

    ssd_small = lambda j: (a["ssd_conv_b"][j:j + 1], _pad_lanes(a["ssd_dt_bias"][j:j + 1]),
                           _pad_lanes(a["ssd_a_log"][j:j + 1]), _pad_lanes(a["ssd_d"][j:j + 1]), a["ssd_norm"][j:j + 1])

    saved = []
    for i in range(cfg.depth):
        kind, j = i % 3, i // 3
        x, s1 = _ffn_fwd(x, vec("ffn1_norm", i), w["ffn1_w_gu"][i], w["ffn1_w_down"][i], f"l{i}_ffn1")
        if kind == 0:
            x, s2 = _sb_mixer_fwd(x, vec("mix_norm", i), w["sb_w_qkv"][j], w["sb_w_o"][j], cfg, f"l{i}_sb")
        elif kind == 1:
            x, s2 = _ssd_mixer_fwd(x, vec("mix_norm", i), w["ssd_w_in"][j], conv_full["ssd_conv_w"][j], *ssd_small(j),
                                   w["ssd_w_out"][j], cfg, f"l{i}_ssd")
        else:
            x, s2 = _sc_mixer_fwd(x, vec("mix_norm", i), w["sc_w_in"][j], conv_full["sc_conv_w"][j], w["sc_w_out"][j],
                                  f"l{i}_sc")
        x, s3 = _ffn_fwd(x, vec("ffn2_norm", i), w["ffn2_w_gu"][i], w["ffn2_w_down"][i], f"l{i}_ffn2")
        saved.append((s1, s2, s3))
    loss_part, dx, d_final = loss_head(x, a["final_norm"].reshape(1, -1), tgt, name="loss_head")

    gbig = {n: [None] * a[n].shape[0] for n in big_names}
    gsmall = {n: [None] * a[n].shape[0] for n in ("ffn1_norm", "mix_norm", "ffn2_norm", "ssd_conv_w", "ssd_conv_b",
                                                   "ssd_dt_bias", "ssd_a_log", "ssd_d", "ssd_norm", "sc_conv_w")}
    for i in reversed(range(cfg.depth)):
        kind, j = i % 3, i // 3
        s1, s2, s3 = saved[i]
        dx, gsmall["ffn2_norm"][i], gbig["ffn2_w_gu"][i], gbig["ffn2_w_down"][i] = _ffn_bwd(
            dx, s3, vec("ffn2_norm", i), w["ffn2_w_gu"][i], w["ffn2_w_down"][i], f"l{i}_ffn2")
        if kind == 0:
            dx, gsmall["mix_norm"][i], gbig["sb_w_qkv"][j], gbig["sb_w_o"][j] = _sb_mixer_bwd(
                dx, s2, vec("mix_norm", i), w["sb_w_qkv"][j], w["sb_w_o"][j], cfg, f"l{i}_sb")
        elif kind == 1:
            dx, gsmall["mix_norm"][i], gbig["ssd_w_in"][j], gbig["ssd_w_out"][j], sm = _ssd_mixer_bwd(
                dx, s2, vec("mix_norm", i), w["ssd_w_in"][j], conv_full["ssd_conv_w"][j], *ssd_small(j),
                w["ssd_w_out"][j], cfg, f"l{i}_ssd")
            for key, val in sm.items():
                gsmall["ssd_" + key][j] = val
        else:
            dx, gsmall["mix_norm"][i], gbig["sc_w_in"][j], gsmall["sc_conv_w"][j], gbig["sc_w_out"][j] = _sc_mixer_bwd(
                dx, s2, vec("mix_norm", i), w["sc_w_in"][j], conv_full["sc_conv_w"][j], w["sc_w_out"][j], f"l{i}_sc")
        dx, gsmall["ffn1_norm"][i], gbig["ffn1_w_gu"][i], gbig["ffn1_w_down"][i] = _ffn_bwd(
            dx, s1, vec("ffn1_norm", i), w["ffn1_w_gu"][i], w["ffn1_w_down"][i], f"l{i}_ffn1")
    grad_x = dx[None]

    red_big = _reduce_big(gbig, {n: a[n] for n in big_names}, "g")
    small_names = list(gsmall) + ["final_norm"]
    small_vals = [jnp.concatenate([p.reshape((1,) + a[n].shape[1:-1] + (-1,)) for p in gsmall[n]], axis=0)
                  for n in gsmall] + [d_final.reshape(-1)]
    gpack, gmeta = _small_pack(small_vals + [loss_part[:, :1]])
    _, gsum = allgather_small(gpack, name="small_allreduce")
    *small_red, loss = _small_unpack(gsum, gmeta)
    red = dict(red_big)
    for n, v in zip(small_names, small_red):
        if n in SHARDED_SMALL:
            width = a[n].shape[-1]
            v = lax.dynamic_slice_in_dim(v, k_chip * width, width, axis=v.ndim - 1)
        red[n] = v.reshape(a[n].shape)

    delta, new_m, new_v = {}, {}, {}
    two_d = lambda t: t.reshape(-1, t.shape[-1])
    for n in big_names:
        d_, m_, v_ = adamw(two_d(a[n]), two_d(red[n]), two_d(a["m_" + n]), two_d(a["v_" + n]), name=f"adamw_{n}")
        delta[n], new_m[n], new_v[n] = (t.reshape(a[n].shape) for t in (d_, m_, v_))
    rest = [n for n in WEIGHT_NAMES if n not in big_names]
    packs = [_small_pack([src[n] for n in rest]) for src in
             (a, red, {n: a["m_" + n] for n in rest}, {n: a["v_" + n] for n in rest})]
    outs = adamw(*[p for p, _ in packs], name="adamw_small")
    for dst, o in zip((delta, new_m, new_v), outs):
        for n, t in zip(rest, _small_unpack(o, packs[0][1])):
            dst[n] = t
    return (loss.reshape(()), grad_x, *[red[n] for n in WEIGHT_NAMES], *[delta[n] for n in WEIGHT_NAMES],
            *[new_m[n] for n in WEIGHT_NAMES], *[new_v[n] for n in WEIGHT_NAMES])


def kernel(x, ffn1_norm, ffn1_w_gu, ffn1_w_down, mix_norm, ffn2_norm, ffn2_w_gu, ffn2_w_down, sb_w_qkv, sb_w_o, ssd_w_in, ssd_conv_w, ssd_conv_b, ssd_dt_bias, ssd_a_log, ssd_d, ssd_norm, ssd_w_out, sc_w_in, sc_conv_w, sc_w_out, final_norm, loss_target, m_ffn1_norm, m_ffn1_w_gu, m_ffn1_w_down, m_mix_norm, m_ffn2_norm, m_ffn2_w_gu, m_ffn2_w_down, m_sb_w_qkv, m_sb_w_o, m_ssd_w_in, m_ssd_conv_w, m_ssd_conv_b, m_ssd_dt_bias, m_ssd_a_log, m_ssd_d, m_ssd_norm, m_ssd_w_out, m_sc_w_in, m_sc_conv_w, m_sc_w_out, m_final_norm, v_ffn1_norm, v_ffn1_w_gu, v_ffn1_w_down, v_mix_norm, v_ffn2_norm, v_ffn2_w_gu, v_ffn2_w_down, v_sb_w_qkv, v_sb_w_o, v_ssd_w_in, v_ssd_conv_w, v_ssd_conv_b, v_ssd_dt_bias, v_ssd_a_log, v_ssd_d, v_ssd_norm, v_ssd_w_out, v_sc_w_in, v_sc_conv_w, v_sc_w_out, v_final_norm):
    args = dict(locals())
    return train_step(args, Cfg())
```

```python
import functools
import math

import jax
import jax.numpy as jnp
from jax import lax
from jax.experimental import pallas as pl
from jax.experimental.pallas import tpu as pltpu

F32 = jnp.float32
CDT = jnp.bfloat16
HI = lax.Precision.HIGHEST
BS = pl.BlockSpec
SDS = jax.ShapeDtypeStruct
MESH = pl.DeviceIdType.MESH

RMS_EPS = 1e-6
ADAM_LR, ADAM_B1, ADAM_B2, ADAM_EPS, ADAM_WD, ADAM_STEP = 0.001, 0.9, 0.999, 1e-08, 0.01, 10
V7X_VMEM_LIMIT_BYTES = 56 * 1024 * 1024
LANES = 128
CHUNK = 128
HEAD_DIM = 64
SSD_HPG = 4
N_CHIPS = 4
N_DEV = 8


class Cfg:
    def __init__(self, d_model=1024, d_ff=2816, depth=4, sb_heads=16, ssd_groups=8, ssd_state=128,
                 ssd_conv=4, sc_width=3):
        self.D, self.F, self.depth = d_model, d_ff, depth
        self.sb_heads = sb_heads
        self.DI = 2 * d_model
        self.G = ssd_groups
        self.NH = self.DI // HEAD_DIM
        assert self.NH == SSD_HPG * self.G and ssd_state == LANES and sb_heads * HEAD_DIM == d_model
        self.conv_dim = self.DI + 2 * self.G * ssd_state
        self.in_dim = self.DI + self.conv_dim + self.NH
        self.ssd_conv, self.sc_width = ssd_conv, sc_width
        self.n_sb, self.n_ssd, self.n_sc = (depth + 2) // 3, (depth + 1) // 3, depth // 3


def _pallas(body, **kw):
    return pl.pallas_call(body, **kw)


def _params(*sem):
    return pltpu.CompilerParams(dimension_semantics=sem, vmem_limit_bytes=V7X_VMEM_LIMIT_BYTES)


def _tile(n, cap):
    if n <= cap:
        return n
    best = None
    for m in range(1, n // LANES + 1):
        t = m * LANES
        if n % t == 0 and t <= cap:
            best = t
    assert best is not None, (n, cap)
    return best


def _sigmoid(x):
    return 1.0 / (1.0 + jnp.exp(-x))


def _softplus_neg_abs(x):
    return jnp.log(1.0 + jnp.exp(-jnp.abs(x)))


def _colsel(blk, idx):
    lane = lax.broadcasted_iota(jnp.int32, blk.shape, 1)
    return jnp.sum(jnp.where(lane == idx, blk, 0.0), axis=1, keepdims=True)


def _rowsel(blk, idx):
    sub = lax.broadcasted_iota(jnp.int32, blk.shape, 0)
    return jnp.sum(jnp.where(sub == idx, blk, 0.0), axis=0, keepdims=True)


def _nt(a, b):
    return lax.dot_general(a, b, (((1,), (1,)), ((), ())), preferred_element_type=F32)


def _tn(a, b):
    return lax.dot_general(a, b, (((0,), (0,)), ((), ())), preferred_element_type=F32)


def _nn(a, b):
    return jnp.dot(a, b, preferred_element_type=F32)


def matmul(a, b, *, name, ta=False, tb=False, out_dtype=F32, res=None, scale=1.0, tm=1408, tn=1408, tk=1408,
           b_k_start=0):
    M, K = (a.shape[1], a.shape[0]) if ta else a.shape
    N, K2 = b.shape if tb else (b.shape[1], b.shape[0])
    assert K == K2 or (tb and b_k_start + K <= K2), (a.shape, b.shape, ta, tb)
    tm, tn, tk = _tile(M, tm), _tile(N, tn), _tile(K, tk)
    nk = K // tk
    assert b_k_start % tk == 0
    kb0 = b_k_start // tk
    dn = (((0 if ta else 1,), (1 if tb else 0,)), ((), ()))

    def body(*refs):
        a_ref, b_ref = refs[:2]
        r_ref = refs[2] if res is not None else None
        o_ref = refs[3 if res is not None else 2]
        part = lax.dot_general(a_ref[...].astype(CDT), b_ref[...].astype(CDT), dn, preferred_element_type=F32)

        def finish(total):
            o = total * scale
            if res is not None:
                o = o + r_ref[...].astype(F32)
            o_ref[...] = o.astype(o_ref.dtype)

        if nk == 1:
            finish(part)
            return
        acc = refs[-1]
        k = pl.program_id(2)

        @pl.when(k == 0)
        def _():
            acc[...] = part

        @pl.when(jnp.logical_and(k > 0, k < nk - 1))
        def _():
            acc[...] += part

        @pl.when(k == nk - 1)
        def _():
            finish(acc[...] + part)

    a_spec = BS((tk, tm), lambda i, j, k: (k, i)) if ta else BS((tm, tk), lambda i, j, k: (i, k))
    b_spec = BS((tn, tk), lambda i, j, k: (j, k + kb0)) if tb else BS((tk, tn), lambda i, j, k: (k, j))
    o_spec = BS((tm, tn), lambda i, j, k: (i, j))
    ins, specs = [a, b], [a_spec, b_spec]
    if res is not None:
        ins.append(res)
        specs.append(o_spec)
    return _pallas(body, grid=(M // tm, N // tn, nk), in_specs=specs, out_specs=o_spec,
                   out_shape=SDS((M, N), out_dtype),
                   scratch_shapes=[pltpu.VMEM((tm, tn), F32)] if nk > 1 else [],
                   compiler_params=_params("parallel", "parallel", "arbitrary"), name=name)(*ins)


def rms_fwd(x, g, *, name):
    T, D = x.shape
    tm = _tile(T, 512)

    def body(x_ref, g_ref, o_ref):
        xv = x_ref[...]
        r = lax.rsqrt(jnp.mean(xv * xv, axis=-1, keepdims=True) + RMS_EPS)
        o_ref[...] = (xv * r * g_ref[...]).astype(o_ref.dtype)

    return _pallas(body, grid=(T // tm,), in_specs=[BS((tm, D), lambda i: (i, 0)), BS((1, D), lambda i: (0, 0))],
                   out_specs=BS((tm, D), lambda i: (i, 0)), out_shape=SDS((T, D), CDT),
                   compiler_params=_params("parallel"), name=name)(x, g)


def rms_bwd(dh, x, g, dres, *, name):
    T, D = x.shape
    tm = _tile(T, 512)

    def body(dh_ref, x_ref, g_ref, r_ref, dx_ref, dg_ref):
        i = pl.program_id(0)
        xv = x_ref[...]
        r = lax.rsqrt(jnp.mean(xv * xv, axis=-1, keepdims=True) + RMS_EPS)
        xhat = xv * r
        dhv = dh_ref[...].astype(F32)
        dxh = dhv * g_ref[...]
        m = jnp.mean(dxh * xhat, axis=-1, keepdims=True)
        dx_ref[...] = r * (dxh - xhat * m) + r_ref[...]

        @pl.when(i == 0)
        def _():
            dg_ref[...] = jnp.zeros_like(dg_ref)

        dg_ref[...] += jnp.sum(dhv * xhat, axis=0, keepdims=True)

    row = BS((tm, D), lambda i: (i, 0))
    vec = BS((1, D), lambda i: (0, 0))
    return _pallas(body, grid=(T // tm,), in_specs=[row, row, vec, row], out_specs=[row, vec],
                   out_shape=[SDS((T, D), F32), SDS((1, D), F32)],
                   compiler_params=_params("arbitrary"), name=name)(dh, x, g, dres)


def ffn_up(h, wgu, *, name):
    T, D = h.shape
    F = wgu.shape[1] // 2
    tm, tn = _tile(T, 512), _tile(F, 1408)
    nj = F // tn

    def body(h_ref, wg_ref, wu_ref, g_ref, u_ref, a_ref):
        hv = h_ref[...].astype(CDT)
        g = jnp.dot(hv, wg_ref[...].astype(CDT), preferred_element_type=F32)
        u = jnp.dot(hv, wu_ref[...].astype(CDT), preferred_element_type=F32)
        g_ref[...] = g.astype(g_ref.dtype)
        u_ref[...] = u.astype(u_ref.dtype)
        a_ref[...] = (g * _sigmoid(g) * u).astype(a_ref.dtype)

    out = BS((tm, tn), lambda j, i: (i, j))
    return _pallas(body, grid=(nj, T // tm),
                   in_specs=[BS((tm, D), lambda j, i: (i, 0)), BS((D, tn), lambda j, i: (0, j)),
                             BS((D, tn), lambda j, i: (0, j + nj))],
                   out_specs=[out, out, out], out_shape=[SDS((T, F), CDT)] * 3,
                   compiler_params=_params("parallel", "parallel"), name=name)(h, wgu, wgu)


def ffn_dact(dout, wd, g, u, scale, *, name):
    T, D = dout.shape
    F = wd.shape[0]
    tm, tn = _tile(T, 512), _tile(F, 1408)

    def body(do_ref, wd_ref, g_ref, u_ref, dg_ref, du_ref):
        da = scale * _nt(do_ref[...].astype(CDT), wd_ref[...].astype(CDT))
        gv = g_ref[...].astype(F32)
        uv = u_ref[...].astype(F32)
        sg = _sigmoid(gv)
        dg_ref[...] = (da * uv * (sg * (1.0 + gv * (1.0 - sg)))).astype(dg_ref.dtype)
        du_ref[...] = (da * (gv * sg)).astype(du_ref.dtype)

    blk = BS((tm, tn), lambda j, i: (i, j))
    return _pallas(body, grid=(F // tn, T // tm),
                   in_specs=[BS((tm, D), lambda j, i: (i, 0)), BS((tn, D), lambda j, i: (j, 0)), blk, blk],
                   out_specs=[blk, blk], out_shape=[SDS((T, F), CDT)] * 2,
                   compiler_params=_params("parallel", "parallel"), name=name)(dout, wd, g, u)


def loss_head(x, g, tgt, *, name):
    T, D = x.shape
    tm = _tile(T, 512)

    def body(x_ref, g_ref, t_ref, l_ref, dx_ref, dg_ref):
        i = pl.program_id(0)
        xv = x_ref[...]
        gv = g_ref[...]
        r = lax.rsqrt(jnp.mean(xv * xv, axis=-1, keepdims=True) + RMS_EPS)
        xhat = xv * r
        err = xhat * gv - t_ref[...]
        part = 0.5 * jnp.sum(jnp.mean(err * err, axis=-1, keepdims=True), axis=0, keepdims=True)
        dy = err * (1.0 / D)
        dxh = dy * gv
        m = jnp.mean(dxh * xhat, axis=-1, keepdims=True)
        dx_ref[...] = r * (dxh - xhat * m)

        @pl.when(i == 0)
        def _():
            dg_ref[...] = jnp.zeros_like(dg_ref)
            l_ref[...] = jnp.zeros_like(l_ref)

        dg_ref[...] += jnp.sum(dy * xhat, axis=0, keepdims=True)
        l_ref[...] += jnp.broadcast_to(part, l_ref.shape)

    row = BS((tm, D), lambda i: (i, 0))
    vec = BS((1, D), lambda i: (0, 0))
    return _pallas(body, grid=(T // tm,), in_specs=[row, vec, row],
                   out_specs=[BS((1, LANES), lambda i: (0, 0)), row, vec],
                   out_shape=[SDS((1, LANES), F32), SDS((T, D), F32), SDS((1, D), F32)],
                   compiler_params=_params("arbitrary"), name=name)(x, g, tgt)


def adamw(w, g, m, v, *, name):
    R, C = w.shape
    tr = R
    if R * C * 4 > (1 << 20):
        for cand in range(8, R + 1, 8):
            if R % cand == 0 and cand * C * 4 <= (1 << 20):
                tr = cand

    def body(w_ref, g_ref, m_ref, v_ref, d_ref, nm_ref, nv_ref):
        gv = g_ref[...]
        nm = ADAM_B1 * m_ref[...] + (1.0 - ADAM_B1) * gv
        nv = ADAM_B2 * v_ref[...] + (1.0 - ADAM_B2) * (gv * gv)
        m_hat = nm / (1.0 - ADAM_B1 ** ADAM_STEP)
        v_hat = nv / (1.0 - ADAM_B2 ** ADAM_STEP)
        d_ref[...] = -ADAM_LR * (m_hat / (jnp.sqrt(v_hat) + ADAM_EPS) + ADAM_WD * w_ref[...])
        nm_ref[...] = nm
        nv_ref[...] = nv

    blk = BS((tr, C), lambda i: (i, 0))
    return _pallas(body, grid=(R // tr,), in_specs=[blk] * 4, out_specs=[blk] * 3,
                   out_shape=[SDS((R, C), F32)] * 3, compiler_params=_params("parallel"), name=name)(w, g, m, v)


def _sb_tri():
    row = lax.broadcasted_iota(jnp.int32, (CHUNK, CHUNK), 0)
    col = lax.broadcasted_iota(jnp.int32, (CHUNK, CHUNK), 1)
    return row, col


def _tri_stack(keep):
    t = keep.astype(jnp.bfloat16)
    return jnp.concatenate([t, t], axis=0)


def _split_dot(x, tri_stack):
    hi = x.astype(jnp.bfloat16)
    mid = (x - hi.astype(F32)).astype(jnp.bfloat16)
    return jnp.dot(jnp.concatenate([hi, mid], axis=1), tri_stack, preferred_element_type=F32)


def _sb_heads_per_step(H, want):
    hb = min(H, want)
    assert H % hb == 0, (H, hb)
    return hb


SB_FWD_HEADS = 8
SB_BWD_HEADS = 8


def _head_pair_masked(blk, lo):
    zero = jnp.zeros_like(blk)
    return jnp.where(lo, blk, zero), jnp.where(lo, zero, blk)


def sb_fwd(qkv, n_heads, *, name):
    T, D3 = qkv.shape
    D = D3 // 3
    nq = T // CHUNK
    assert nq <= LANES
    scale = HEAD_DIM ** -0.5
    hb = _sb_heads_per_step(n_heads, SB_FWD_HEADS)
    assert hb % 2 == 0
    LW = hb * HEAD_DIM
    ngrp = D // LW
    heads, pairs = range(hb), range(hb // 2)
    lanes = lambda p: slice(p * LANES, (p + 1) * LANES)

    def body(q_ref, k_ref, v_ref, o_ref, tab_ref, acc_scr, cs_scr, qm_scr):
        qi = pl.program_id(1)
        row, col = _sb_tri()
        lo = col < HEAD_DIM
        upper_stack = _tri_stack(row > col)
        acc_scr[...] = jnp.zeros_like(acc_scr)
        cs_scr[...] = jnp.zeros_like(cs_scr)
        tab_ref[...] = jnp.zeros_like(tab_ref)
        for p in pairs:
            qm_scr[2 * p], qm_scr[2 * p + 1] = _head_pair_masked(q_ref[:, lanes(p)] * scale, lo)

        def tile(j, diag):
            sl = pl.ds(pl.multiple_of(j * CHUNK, CHUNK), CHUNK)
            keep = (lambda t: jnp.where(col < row, t, 0.0)) if diag else (lambda t: t)
            kbs = [k_ref[sl, lanes(p)] for p in pairs]
            zs = [_nt(qm_scr[h], kbs[h // 2]) for h in heads]
            lszs = [jnp.minimum(z, 0.0) - _softplus_neg_abs(z) for z in zs]
            lks = [keep(lsz - z) for lsz, z in zip(lszs, zs)]
            tls = [_split_dot(lk, upper_stack) for lk in lks]
            css = [cs_scr[h] for h in heads]
            logits = [lsz + tl + cs for lsz, tl, cs in zip(lszs, tls, css)]
            atts = [keep(jnp.exp(lg)).astype(CDT) for lg in logits]
            vms = [jnp.concatenate(_head_pair_masked(v_ref[sl, lanes(p)], lo), axis=0) for p in pairs]
            pvs = [_nn(jnp.concatenate([atts[2 * p], atts[2 * p + 1]], axis=1), vms[p]) for p in pairs]
            for p in pairs:
                acc_scr[:, lanes(p)] += pvs[p]
            for h in heads:
                tab_ref[h, 0] = jnp.where(col == j, css[h], tab_ref[h, 0])
                if diag:
                    cs_scr[h] = css[h] + (tls[h][:, :1] + lks[h][:, :1])
                else:
                    cs_scr[h] = logits[h][:, :1] - zs[h][:, :1]

        tile(qi, True)

        def step(jj, carry):
            tile(qi - jj, False)
            return carry

        lax.fori_loop(1, qi + 1, step, 0)
        o_ref[...] = acc_scr[...].astype(o_ref.dtype)

    once = pl.Buffered(1)
    qs = BS((CHUNK, LW), lambda g, i: (i, g))
    return _pallas(body, grid=(ngrp, nq),
                   in_specs=[qs, BS((T, LW), lambda g, i: (0, ngrp + g), pipeline_mode=once),
                             BS((T, LW), lambda g, i: (0, 2 * ngrp + g), pipeline_mode=once)],
                   out_specs=[qs, BS((hb, 1, CHUNK, CHUNK), lambda g, i: (g, i, 0, 0))],
                   out_shape=[SDS((T, D), CDT), SDS((n_heads, nq, CHUNK, CHUNK), F32)],
                   scratch_shapes=[pltpu.VMEM((CHUNK, LW), F32), pltpu.VMEM((hb, CHUNK, 1), F32),
                                   pltpu.VMEM((hb, CHUNK, LANES), CDT)],
                   compiler_params=_params("parallel", "arbitrary"), name=name)(qkv, qkv, qkv)


def sb_bwd(qkv, do, tab, n_heads, *, name):
    T, D3 = qkv.shape
    D = D3 // 3
    nq = T // CHUNK
    scale = HEAD_DIM ** -0.5
    hb = _sb_heads_per_step(n_heads, SB_BWD_HEADS)
    assert hb % 2 == 0
    LW = hb * HEAD_DIM
    ngrp = D // LW
    heads, pairs = range(hb), range(hb // 2)
    lanes = lambda p: slice(p * LANES, (p + 1) * LANES)

    def body(q_ref, k_ref, v_ref, do_ref, tab_ref, dq_ref, dk_ref, dv_ref,
             dk_acc, dv_acc, dq_scr, pfx_scr, qm_scr, dom_scr):
        qi = pl.program_id(1)

        @pl.when(qi == 0)
        def _():
            dk_acc[...] = jnp.zeros_like(dk_acc)
            dv_acc[...] = jnp.zeros_like(dv_acc)

        dq_scr[...] = jnp.zeros_like(dq_scr)
        pfx_scr[...] = jnp.zeros_like(pfx_scr)
        row, col = _sb_tri()
        lo = col < HEAD_DIM
        upper_stack = _tri_stack(row > col)
        before_stack = _tri_stack(row < col)
        for p in pairs:
            qm_scr[2 * p], qm_scr[2 * p + 1] = _head_pair_masked(q_ref[:, lanes(p)] * scale, lo)
            dom_scr[2 * p], dom_scr[2 * p + 1] = _head_pair_masked(do_ref[:, lanes(p)], lo)

        def tile(j, diag):
            sl = pl.ds(pl.multiple_of(j * CHUNK, CHUNK), CHUNK)
            keep = (lambda t: jnp.where(col < row, t, 0.0)) if diag else (lambda t: t)
            kbs = [k_ref[sl, lanes(p)] for p in pairs]
            vbs = [v_ref[sl, lanes(p)] for p in pairs]
            zs = [_nt(qm_scr[h], kbs[h // 2]) for h in heads]
            das = [_nt(dom_scr[h], vbs[h // 2]) for h in heads]
            lszs = [jnp.minimum(z, 0.0) - _softplus_neg_abs(z) for z in zs]
            lks = [keep(lsz - z) for lsz, z in zip(lszs, zs)]
            tls = [_split_dot(lk, upper_stack) for lk in lks]
            css = [jnp.sum(jnp.where(col == j, tab_ref[h, 0], 0.0), axis=1, keepdims=True) for h in heads]
            atts = [keep(jnp.exp(lsz + tl + cs)) for lsz, tl, cs in zip(lszs, tls, css)]
            dls = [att * da for att, da in zip(atts, das)]
            pres = [_split_dot(dl, before_stack) for dl in dls]
            pfxs = [pfx_scr[h] for h in heads]
            sigs = [jnp.exp(lsz) for lsz in lszs]
            befores = [pre + pfx for pre, pfx in zip(pres, pfxs)]
            dzs = [(dl * (1.0 - sig) - keep(sig * bf)).astype(CDT) for dl, sig, bf in zip(dls, sigs, befores)]
            attc = [att.astype(CDT) for att in atts]
            kms = [jnp.concatenate(_head_pair_masked(kb, lo), axis=0) for kb in kbs]
            dqs = [_nn(jnp.concatenate([dzs[2 * p], dzs[2 * p + 1]], axis=1), kms[p]) for p in pairs]
            both = lambda xs, p: jnp.concatenate([xs[2 * p], xs[2 * p + 1]], axis=0)
            dks = [_tn(both(dzs, p), jnp.concatenate([qm_scr[2 * p], qm_scr[2 * p + 1]], axis=0)) for p in pairs]
            dvs = [_tn(both(attc, p), jnp.concatenate([dom_scr[2 * p], dom_scr[2 * p + 1]], axis=0)) for p in pairs]
            last = CHUNK - 1
            for p in pairs:
                dq_scr[:, lanes(p)] += dqs[p]
                dk_acc[sl, lanes(p)] += dks[p]
                dv_acc[sl, lanes(p)] += dvs[p]
            for h in heads:
                pfx_scr[h] = befores[h][:, last:] + dls[h][:, last:]

        def step(j, carry):
            tile(j, False)
            return carry

        lax.fori_loop(0, qi, step, 0)
        tile(qi, True)
        dq_ref[...] = (dq_scr[...] * scale).astype(dq_ref.dtype)

        @pl.when(qi == nq - 1)
        def _():
            dk_ref[...] = dk_acc[...].astype(dk_ref.dtype)
            dv_ref[...] = dv_acc[...].astype(dv_ref.dtype)

    once = pl.Buffered(1)
    qs = BS((CHUNK, LW), lambda g, i: (i, g))
    ks = BS((T, LW), lambda g, i: (0, g))
    return _pallas(body, grid=(ngrp, nq),
                   in_specs=[qs, BS((T, LW), lambda g, i: (0, ngrp + g), pipeline_mode=once),
                             BS((T, LW), lambda g, i: (0, 2 * ngrp + g), pipeline_mode=once), qs,
                             BS((hb, 1, CHUNK, CHUNK), lambda g, i: (g, i, 0, 0))],
                   out_specs=[qs, ks, ks],
                   out_shape=[SDS((T, D), CDT)] * 3,
                   scratch_shapes=[pltpu.VMEM((T, LW), F32), pltpu.VMEM((T, LW), F32),
                                   pltpu.VMEM((CHUNK, LW), F32), pltpu.VMEM((hb, CHUNK, 1), F32),
                                   pltpu.VMEM((hb, CHUNK, LANES), CDT), pltpu.VMEM((hb, CHUNK, LANES), CDT)],
                   compiler_params=_params("parallel", "arbitrary"), name=name)(qkv, qkv, qkv, do, tab)


HALO = 8


def _shift_down(cur, prev8, s):
    if s == 0:
        return cur
    tt = cur.shape[0]
    row = lax.broadcasted_iota(jnp.int32, cur.shape, 0)
    fix = jnp.concatenate([pltpu.roll(prev8, s, 0), jnp.zeros((tt - HALO, cur.shape[1]), cur.dtype)], axis=0)
    return jnp.where(row < s, fix, pltpu.roll(cur, s, 0))


def _shift_up(cur, next8, s):
    if s == 0:
        return cur
    tt = cur.shape[0]
    row = lax.broadcasted_iota(jnp.int32, cur.shape, 0)
    fix = jnp.concatenate([jnp.zeros((tt - HALO, cur.shape[1]), cur.dtype), pltpu.roll(next8, HALO - s, 0)], axis=0)
    return jnp.where(row >= tt - s, fix, pltpu.roll(cur, tt - s, 0))


def _halo_specs(tt, width, T):
    per = tt // HALO
    last = T // HALO - 1
    cur = BS((tt, width), lambda i: (i, 0))
    prev = BS((HALO, width), lambda i: (jnp.maximum(i * per - 1, 0), 0))
    nxt = BS((HALO, width), lambda i: (jnp.minimum((i + 1) * per, last), 0))
    return cur, prev, nxt


def sc_fwd(proj, w, *, name):
    T, D3 = proj.shape
    D = D3 // 3
    K = w.shape[0]
    tt = _tile(T, 256)
    cur, prev, _ = _halo_specs(tt, D3, T)

    def body(p_ref, pp_ref, w_ref, o_ref):
        i = pl.program_id(0)
        pc = p_ref[:, D:2 * D] * p_ref[:, 2 * D:]
        pp = jnp.where(i > 0, pp_ref[:, D:2 * D] * pp_ref[:, 2 * D:], 0.0)
        u = jnp.zeros((tt, D), F32)
        for kk in range(K):
            u = u + w_ref[kk:kk + 1, :] * _shift_down(pc, pp, K - 1 - kk)
        o_ref[...] = (p_ref[:, :D] * u).astype(o_ref.dtype)

    return _pallas(body, grid=(T // tt,), in_specs=[cur, prev, BS((K, D), lambda i: (0, 0))],
                   out_specs=BS((tt, D), lambda i: (i, 0)), out_shape=SDS((T, D), CDT),
                   compiler_params=_params("parallel"), name=name)(proj, proj, w)


def sc_bwd(dm, proj, w, *, name):
    T, D3 = proj.shape
    D = D3 // 3
    K = w.shape[0]
    tt = _tile(T, 256)
    cur, prev, nxt = _halo_specs(tt, D3, T)
    dcur, _, dnxt = _halo_specs(tt, D, T)
    nsteps = T // tt

    def body(dm_ref, dmn_ref, p_ref, pp_ref, pn_ref, w_ref, o_ref, dw_ref):
        i = pl.program_id(0)
        b = p_ref[:, :D]
        c = p_ref[:, D:2 * D]
        h = p_ref[:, 2 * D:]
        pc = c * h
        pp = jnp.where(i > 0, pp_ref[:, D:2 * D] * pp_ref[:, 2 * D:], 0.0)
        dmv = dm_ref[...]
        du = dmv * b
        dun = jnp.where(i < nsteps - 1, dmn_ref[...] * pn_ref[:, :D], 0.0)
        u = jnp.zeros((tt, D), F32)
        dp = jnp.zeros((tt, D), F32)
        rows = []
        for kk in range(K):
            sh = _shift_down(pc, pp, K - 1 - kk)
            u = u + w_ref[kk:kk + 1, :] * sh
            dp = dp + w_ref[kk:kk + 1, :] * _shift_up(du, dun, K - 1 - kk)
            rows.append(jnp.sum(du * sh, axis=0, keepdims=True))
        rows.append(jnp.zeros((HALO - K, D), F32))
        o_ref[:, :D] = (dmv * u).astype(o_ref.dtype)
        o_ref[:, D:2 * D] = (dp * h).astype(o_ref.dtype)
        o_ref[:, 2 * D:] = (dp * c).astype(o_ref.dtype)

        @pl.when(i == 0)
        def _():
            dw_ref[...] = jnp.zeros_like(dw_ref)

        dw_ref[...] += jnp.concatenate(rows, axis=0)

    return _pallas(body, grid=(nsteps,), in_specs=[dcur, dnxt, cur, prev, nxt, BS((K, D), lambda i: (0, 0))],
                   out_specs=[BS((tt, D3), lambda i: (i, 0)), BS((HALO, D), lambda i: (0, 0))],
                   out_shape=[SDS((T, D3), CDT), SDS((HALO, D), F32)],
                   compiler_params=_params("arbitrary"), name=name)(dm, dm, proj, proj, proj, w)


def ssd_conv_fwd(xpre, w, b, *, name):
    T, C = xpre.shape
    K = w.shape[0]
    tt = _tile(T, 256)
    cur, prev, _ = _halo_specs(tt, C, T)

    def body(x_ref, xp_ref, w_ref, b_ref, pre_ref, o_ref):
        i = pl.program_id(0)
        xv = x_ref[...]
        xp = jnp.where(i > 0, xp_ref[...], 0.0)
        acc = jnp.broadcast_to(b_ref[...], (tt, C))
        for kk in range(K):
            acc = acc + w_ref[kk:kk + 1, :] * _shift_down(xv, xp, K - 1 - kk)
        pre_ref[...] = acc
        o_ref[...] = acc * _sigmoid(acc)

    row = BS((tt, C), lambda i: (i, 0))
    return _pallas(body, grid=(T // tt,), in_specs=[cur, prev, BS((K, C), lambda i: (0, 0)), BS((1, C), lambda i: (0, 0))],
                   out_specs=[row, row], out_shape=[SDS((T, C), F32)] * 2,
                   compiler_params=_params("parallel"), name=name)(xpre, xpre, w, b)


def ssd_conv_bwd(dxs, db_, dc_, pre, xpre, w, *, name):
    T, C = xpre.shape
    K = w.shape[0]
    tt = _tile(T, 256)
    cur, prev, nxt = _halo_specs(tt, C, T)
    widths = (dxs.shape[1], db_.shape[1], dc_.shape[1])
    gspecs = []
    for wd in widths:
        c_, _, n_ = _halo_specs(tt, wd, T)
        gspecs += [c_, n_]
    nsteps = T // tt

    def dsilu(p):
        s = _sigmoid(p)
        return s * (1.0 + p * (1.0 - s))

    def body(g0, g0n, g1, g1n, g2, g2n, pre_ref, pren_ref, x_ref, xp_ref, w_ref, dx_ref, dw_ref, dbias_ref):
        i = pl.program_id(0)
        gcur = jnp.concatenate([g0[...], g1[...], g2[...]], axis=1)
        gnxt = jnp.concatenate([g0n[...], g1n[...], g2n[...]], axis=1)
        dpre = gcur * dsilu(pre_ref[...])
        dpren = jnp.where(i < nsteps - 1, gnxt * dsilu(pren_ref[...]), 0.0)
        xv = x_ref[...]
        xp = jnp.where(i > 0, xp_ref[...], 0.0)
        dx = jnp.zeros((tt, C), F32)
        rows = []
        for kk in range(K):
            dx = dx + w_ref[kk:kk + 1, :] * _shift_up(dpre, dpren, K - 1 - kk)
            rows.append(jnp.sum(dpre * _shift_down(xv, xp, K - 1 - kk), axis=0, keepdims=True))
        rows.append(jnp.zeros((HALO - K, C), F32))
        dx_ref[...] = dx.astype(dx_ref.dtype)

        @pl.when(i == 0)
        def _():
            dw_ref[...] = jnp.zeros_like(dw_ref)
            dbias_ref[...] = jnp.zeros_like(dbias_ref)

        dw_ref[...] += jnp.concatenate(rows, axis=0)
        dbias_ref[...] += jnp.sum(dpre, axis=0, keepdims=True)

    return _pallas(body, grid=(nsteps,),
                   in_specs=gspecs + [cur, nxt, cur, prev, BS((K, C), lambda i: (0, 0))],
                   out_specs=[BS((tt, C), lambda i: (i, 0)), BS((HALO, C), lambda i: (0, 0)), BS((1, C), lambda i: (0, 0))],
                   out_shape=[SDS((T, C), CDT), SDS((HALO, C), F32), SDS((1, C), F32)],
                   compiler_params=_params("arbitrary"), name=name)(dxs, dxs, db_, db_, dc_, dc_, pre, pre, xpre, xpre, w)


def ssd_prep_fwd(dtpre, bias, alog, *, name):
    T = dtpre.shape[0]
    nc = T // CHUNK

    def body(p_ref, b_ref, a_ref, dt_ref, ac_ref, dtT_ref, acT_ref):
        xv = p_ref[...] + b_ref[...]
        dt = jnp.maximum(xv, 0.0) + _softplus_neg_abs(xv)
        row, col = _sb_tri()
        lower = (col <= row).astype(F32)
        ac = jnp.dot(lower, dt * -jnp.exp(a_ref[...]), precision=HI, preferred_element_type=F32)
        dt_ref[...] = dt
        ac_ref[...] = ac
        dtT_ref[...] = dt.T
        acT_ref[...] = ac.T

    blk = BS((CHUNK, LANES), lambda c: (c, 0))
    blkT = BS((LANES, CHUNK), lambda c: (0, c))
    vec = BS((1, LANES), lambda c: (0, 0))
    return _pallas(body, grid=(nc,), in_specs=[blk, vec, vec], out_specs=[blk, blk, blkT, blkT],
                   out_shape=[SDS((T, LANES), F32)] * 2 + [SDS((LANES, T), F32)] * 2,
                   compiler_params=_params("parallel"), name=name)(dtpre, bias, alog)


def ssd_prep_bwd(dac_c, ddt_c, dac_r, ddt_r, dtpre, bias, alog, *, name):
    T = dtpre.shape[0]
    nc = T // CHUNK

    def body(dac_ref, ddt_ref, dacr_ref, ddtr_ref, p_ref, b_ref, a_ref, o_ref, da_ref, db_ref):
        c = pl.program_id(0)
        xv = p_ref[...] + b_ref[...]
        dt = jnp.maximum(xv, 0.0) + _softplus_neg_abs(xv)
        av = -jnp.exp(a_ref[...])
        row, col = _sb_tri()
        after = (col >= row).astype(F32)
        dda = jnp.dot(after, dac_ref[...] + dacr_ref[...].T, precision=HI, preferred_element_type=F32)
        dpre = (ddt_ref[...] + ddtr_ref[...].T + av * dda) * _sigmoid(xv)
        o_ref[...] = dpre

        @pl.when(c == 0)
        def _():
            da_ref[...] = jnp.zeros_like(da_ref)
            db_ref[...] = jnp.zeros_like(db_ref)

        da_ref[...] += jnp.sum(dt * dda, axis=0, keepdims=True) * av
        db_ref[...] += jnp.sum(dpre, axis=0, keepdims=True)

    blk = BS((CHUNK, LANES), lambda c: (c, 0))
    blkT = BS((LANES, CHUNK), lambda c: (0, c))
    vec = BS((1, LANES), lambda c: (0, 0))
    return _pallas(body, grid=(nc,), in_specs=[blk, blk, blkT, blkT, blk, vec, vec], out_specs=[blk, vec, vec],
                   out_shape=[SDS((T, LANES), F32), SDS((1, LANES), F32), SDS((1, LANES), F32)],
                   compiler_params=_params("arbitrary"), name=name)(dac_c, ddt_c, dac_r, ddt_r, dtpre, bias, alog)


def _ssd_specs(cfg, T, rev):
    nc = T // CHUNK
    GW = SSD_HPG * HEAD_DIM
    bo = cfg.DI // LANES
    co = (cfg.DI + cfg.G * LANES) // LANES
    ci = (lambda c: nc - 1 - c) if rev else (lambda c: c)
    return dict(
        x=BS((CHUNK, GW), lambda g, c: (ci(c), g)),
        b=BS((CHUNK, LANES), lambda g, c: (ci(c), bo + g)),
        c=BS((CHUNK, LANES), lambda g, c: (ci(c), co + g)),
        col=BS((CHUNK, LANES), lambda g, c: (ci(c), 0)),
        row=BS((LANES, CHUNK), lambda g, c: (0, ci(c))),
        vec=BS((1, LANES), lambda g, c: (0, 0)),
        ng=BS((1, GW), lambda g, c: (0, g)),
        hp=BS((1, 1, SSD_HPG, HEAD_DIM, LANES), lambda g, c: (g, ci(c), 0, 0, 0)),
    )


def ssd_chunk_fwd(xbc, z, dt, ac, dtT, acT, dskip, ng, cfg, *, name):
    T = xbc.shape[0]
    nc = T // CHUNK
    G, DI = cfg.G, cfg.DI
    GW = SSD_HPG * HEAD_DIM
    sp = _ssd_specs(cfg, T, False)

    def body(x_ref, b_ref, c_ref, z_ref, dt_ref, ac_ref, dtT_ref, acT_ref, d_ref, ng_ref,
             y_ref, yn_ref, hp_ref, h_scr):
        g = pl.program_id(0)
        c = pl.program_id(1)

        @pl.when(c == 0)
        def _():
            h_scr[...] = jnp.zeros_like(h_scr)

        row, col = _sb_tri()
        causal = row >= col
        Bm = b_ref[...].astype(CDT)
        Cm = c_ref[...].astype(CDT)
        Gm = _nt(Cm, Bm)
        dtb, acb, dtTb, acTb, dv = dt_ref[...], ac_ref[...], dtT_ref[...], acT_ref[...], d_ref[...]
        heads = range(SSD_HPG)
        hhs = [g * SSD_HPG + r for r in heads]
        ac_cs = [_colsel(acb, hh) for hh in hhs]
        dt_cs = [_colsel(dtb, hh) for hh in hhs]
        Xs = [x_ref[:, r * HEAD_DIM:(r + 1) * HEAD_DIM] for r in heads]
        Hps = [h_scr[r] for r in heads]
        Ws = [(Gm * jnp.where(causal, jnp.exp(ac_c - _rowsel(acTb, hh)), 0.0) * _rowsel(dtTb, hh)).astype(CDT)
              for ac_c, hh in zip(ac_cs, hhs)]
        Qs = [_nt(Cm, Hp.astype(CDT)) for Hp in Hps]
        Yds = [_nn(Wm, Xr.astype(CDT)) for Wm, Xr in zip(Ws, Xs)]
        aLs = [_rowsel(ac_c, CHUNK - 1) for ac_c in ac_cs]
        Xes = [(Xr * (jnp.exp(aL - ac_c) * dt_c)).astype(CDT) for Xr, aL, ac_c, dt_c in zip(Xs, aLs, ac_cs, dt_cs)]
        Sts = [_tn(Xe, Bm) for Xe in Xes]
        for r in heads:
            hp_ref[0, 0, r] = Hps[r]
            h_scr[r] = jnp.exp(aLs[r]) * Hps[r] + Sts[r]
            y_ref[:, r * HEAD_DIM:(r + 1) * HEAD_DIM] = (Yds[r] + jnp.exp(ac_cs[r]) * Qs[r]
                                                         + _colsel(dv, hhs[r]) * Xs[r])
        zz = z_ref[...]
        yz = y_ref[...] * (zz * _sigmoid(zz))
        rstd = lax.rsqrt(jnp.mean(yz * yz, axis=-1, keepdims=True) + RMS_EPS)
        yn_ref[...] = (yz * rstd * ng_ref[...]).astype(yn_ref.dtype)

    return _pallas(body, grid=(G, nc),
                   in_specs=[sp["x"], sp["b"], sp["c"], sp["x"], sp["col"], sp["col"], sp["row"], sp["row"], sp["vec"], sp["ng"]],
                   out_specs=[sp["x"], sp["x"], sp["hp"]],
                   out_shape=[SDS((T, DI), F32), SDS((T, DI), CDT), SDS((G, nc, SSD_HPG, HEAD_DIM, LANES), F32)],
                   scratch_shapes=[pltpu.VMEM((SSD_HPG, HEAD_DIM, LANES), F32)],
                   compiler_params=_params("parallel", "arbitrary"), name=name)(xbc, xbc, xbc, z, dt, ac, dtT, acT, dskip, ng)


def ssd_chunk_bwd(dyn, xbc, z, y, hp, dt, ac, dtT, acT, dskip, ng, cfg, *, name):
    T = xbc.shape[0]
    nc = T // CHUNK
    G, DI = cfg.G, cfg.DI
    GW = SSD_HPG * HEAD_DIM
    sp = _ssd_specs(cfg, T, True)

    def body(dyn_ref, x_ref, b_ref, c_ref, z_ref, y_ref, hp_ref, dt_ref, ac_ref, dtT_ref, acT_ref, d_ref, ng_ref,
             dx_ref, db_ref, dc_ref, dz_ref, colp_ref, rowp_ref, dd_ref, dng_ref, dh_scr):
        g = pl.program_id(0)
        c = pl.program_id(1)

        @pl.when(c == 0)
        def _():
            dh_scr[...] = jnp.zeros_like(dh_scr)
            dd_ref[...] = jnp.zeros_like(dd_ref)
            dng_ref[...] = jnp.zeros_like(dng_ref)

        zz = z_ref[...]
        sg = _sigmoid(zz)
        gate = zz * sg
        yv = y_ref[...]
        yz = yv * gate
        rstd = lax.rsqrt(jnp.mean(yz * yz, axis=-1, keepdims=True) + RMS_EPS)
        nhat = yz * rstd
        dynv = dyn_ref[...].astype(F32)
        dng_ref[...] += jnp.sum(dynv * nhat, axis=0, keepdims=True)
        dnh = dynv * ng_ref[...]
        dyz = rstd * (dnh - nhat * jnp.mean(dnh * nhat, axis=-1, keepdims=True))
        dz_ref[...] = (dyz * yv * (sg * (1.0 + zz * (1.0 - sg)))).astype(dz_ref.dtype)
        dy_all = dyz * gate

        row, col = _sb_tri()
        causal = row >= col
        lane1 = lax.broadcasted_iota(jnp.int32, (1, LANES), 1)
        sub8 = lax.broadcasted_iota(jnp.int32, (HALO, CHUNK), 0)
        subc = lax.broadcasted_iota(jnp.int32, (CHUNK, 1), 0)
        Bf = b_ref[...]
        Cf = c_ref[...]
        Bm = Bf.astype(CDT)
        Cm = Cf.astype(CDT)
        Gm = _nt(Cm, Bm)
        dtb, acb, dtTb, acTb, dv = dt_ref[...], ac_ref[...], dtT_ref[...], acT_ref[...], d_ref[...]
        dG = jnp.zeros((CHUNK, CHUNK), F32)
        dBm = jnp.zeros((CHUNK, LANES), F32)
        dCm = jnp.zeros((CHUNK, LANES), F32)
        colp = jnp.zeros((CHUNK, LANES), F32)
        rowp = jnp.zeros((HALO, CHUNK), F32)
        ddv = jnp.zeros((1, LANES), F32)
        heads = range(SSD_HPG)
        hhs = [g * SSD_HPG + r for r in heads]
        ac_cs = [_colsel(acb, hh) for hh in hhs]
        dt_cs = [_colsel(dtb, hh) for hh in hhs]
        dt_rs = [_rowsel(dtTb, hh) for hh in hhs]
        Ss = [jnp.where(causal, jnp.exp(ac_c - _rowsel(acTb, hh)), 0.0) for ac_c, hh in zip(ac_cs, hhs)]
        GSs = [Gm * S for S in Ss]
        Ws = [GS * dt_r for GS, dt_r in zip(GSs, dt_rs)]
        Xs = [x_ref[:, r * HEAD_DIM:(r + 1) * HEAD_DIM] for r in heads]
        Xcs = [X.astype(CDT) for X in Xs]
        Hps = [hp_ref[0, 0, r] for r in heads]
        Hcs = [Hp.astype(CDT) for Hp in Hps]
        dYs = [dy_all[:, r * HEAD_DIM:(r + 1) * HEAD_DIM] for r in heads]
        dYcs = [dY.astype(CDT) for dY in dYs]
        Es = [jnp.exp(ac_c) for ac_c in ac_cs]
        dQs = [(dY * E).astype(CDT) for dY, E in zip(dYs, Es)]
        dHns = [dh_scr[r] for r in heads]
        dHcs = [dHn.astype(CDT) for dHn in dHns]
        aLs = [_rowsel(ac_c, CHUNK - 1) for ac_c in ac_cs]
        exs = [jnp.exp(aL - ac_c) for aL, ac_c in zip(aLs, ac_cs)]
        es = [ex * dt_c for ex, dt_c in zip(exs, dt_cs)]
        Xes = [(X * e).astype(CDT) for X, e in zip(Xs, es)]
        WtdYs = [_tn(W.astype(CDT), dYc) for W, dYc in zip(Ws, dYcs)]
        dWms = [_nt(dYc, Xc) for dYc, Xc in zip(dYcs, Xcs)]
        Qs = [_nt(Cm, Hc) for Hc in Hcs]
        dCps = [_nn(dQ, Hc) for dQ, Hc in zip(dQs, Hcs)]
        dHps = [_tn(dQ, Cm) for dQ in dQs]
        T2s = [_nt(Bm, dHc) for dHc in dHcs]
        dBps = [_nn(Xe, dHc) for Xe, dHc in zip(Xes, dHcs)]
        for r in heads:
            dY, Xr, Wm, dWm, E, ex, e, aL = dYs[r], Xs[r], Ws[r], dWms[r], Es[r], exs[r], es[r], aLs[r]
            ddv = ddv + jnp.where(lane1 == r, jnp.sum(jnp.sum(dY * Xr, axis=1, keepdims=True), axis=0, keepdims=True), 0.0)
            dG = dG + dWm * Ss[r] * dt_rs[r]
            Mm = dWm * Wm
            dac_r = -jnp.sum(Mm, axis=0, keepdims=True)
            ddt_r = jnp.sum(dWm * GSs[r], axis=0, keepdims=True)
            dCm = dCm + dCps[r]
            dBm = dBm + dBps[r]
            eaL = jnp.exp(aL)
            daL = eaL * jnp.sum(jnp.sum(dHns[r] * Hps[r], axis=1, keepdims=True), axis=0, keepdims=True)
            de = jnp.sum(Xr * T2s[r], axis=1, keepdims=True)
            dee = de * e
            daL = daL + jnp.sum(dee, axis=0, keepdims=True)
            dac_c = (jnp.sum(Mm, axis=1, keepdims=True) + jnp.sum(dY * Qs[r], axis=1, keepdims=True) * E
                     - dee + jnp.where(subc == CHUNK - 1, daL, 0.0))
            dh_scr[r] = dHps[r] + eaL * dHns[r]
            dx_ref[:, r * HEAD_DIM:(r + 1) * HEAD_DIM] = dY * _colsel(dv, hhs[r]) + WtdYs[r] + T2s[r] * e
            colp = jnp.where(col == r, dac_c, colp)
            colp = jnp.where(col == SSD_HPG + r, de * ex, colp)
            rowp = jnp.where(sub8 == r, dac_r, rowp)
            rowp = jnp.where(sub8 == SSD_HPG + r, ddt_r, rowp)
        dGc = dG.astype(CDT)
        db_ref[...] = dBm + _tn(dGc, Cm)
        dc_ref[...] = dCm + _nn(dGc, Bm)
        colp_ref[0] = colp
        rowp_ref[0] = rowp
        dd_ref[0] += ddv

    f32 = lambda shape: SDS(shape, F32)
    return _pallas(body, grid=(G, nc),
                   in_specs=[sp["x"], sp["x"], sp["b"], sp["c"], sp["x"], sp["x"], sp["hp"], sp["col"], sp["col"],
                             sp["row"], sp["row"], sp["vec"], sp["ng"]],
                   out_specs=[sp["x"],
                              BS((CHUNK, LANES), lambda g, c: (nc - 1 - c, g)),
                              BS((CHUNK, LANES), lambda g, c: (nc - 1 - c, g)),
                              sp["x"],
                              BS((1, CHUNK, LANES), lambda g, c: (g, nc - 1 - c, 0)),
                              BS((1, HALO, CHUNK), lambda g, c: (g, 0, nc - 1 - c)),
                              BS((1, 1, LANES), lambda g, c: (g, 0, 0)),
                              sp["ng"]],
                   out_shape=[f32((T, DI)), f32((T, G * LANES)), f32((T, G * LANES)), SDS((T, DI), CDT),
                              f32((G, T, LANES)), f32((G, HALO, T)), f32((G, 1, LANES)), f32((1, DI))],
                   scratch_shapes=[pltpu.VMEM((SSD_HPG, HEAD_DIM, LANES), F32)],
                   compiler_params=_params("parallel", "arbitrary"), name=name)(
                       dyn, xbc, xbc, xbc, z, y, hp, dt, ac, dtT, acT, dskip, ng)


HBM = pl.BlockSpec(memory_space=pltpu.HBM)
VMEM = pl.BlockSpec(memory_space=pltpu.VMEM)


def _me():
    return lax.axis_index("x"), lax.axis_index("y"), lax.axis_index("c")


def _other_chips(x, y):
    return [(1 - x, y), (x, 1 - y), (1 - x, 1 - y)]


def _rcopy(src, dst, send_sem, recv_sem, to):
    return pltpu.make_async_remote_copy(src_ref=src, dst_ref=dst, send_sem=send_sem, recv_sem=recv_sem,
                                        device_id=to, device_id_type=MESH)


def allgather_weights(packs, *, name):
    n = len(packs)

    def body(*refs):
        srcs, outs, (send_sems, recv_sems) = refs[:n], refs[n:2 * n], refs[2 * n:]
        x, y, c = _me()
        k = 2 * x + y
        sibling = (x, y, 1 - c)
        chips = _other_chips(x, y)
        sends, passed = [], []
        for a, (src, out) in enumerate(zip(srcs, outs)):
            for j, (cx, cy) in enumerate(chips):
                s = 6 * a + j
                sends.append(_rcopy(src.at[c], out.at[k, c], send_sems.at[s], recv_sems.at[s], (cx, cy, c)))
        for cp in sends:
            cp.start()
        for a, out in enumerate(outs):
            for j, (cx, cy) in enumerate(chips):
                s = 6 * a + j
                landed = out.at[2 * cx + cy, c]
                _rcopy(landed, landed, send_sems.at[s], recv_sems.at[s], (cx, cy, c)).wait_recv()
                fw = _rcopy(landed, landed, send_sems.at[s + 3], recv_sems.at[s + 3], sibling)
                fw.start()
                passed.append(fw)
        for a, out in enumerate(outs):
            for j, (cx, cy) in enumerate(chips):
                s = 6 * a + j + 3
                theirs = out.at[2 * cx + cy, 1 - c]
                _rcopy(theirs, theirs, send_sems.at[s], recv_sems.at[s], sibling).wait_recv()
        for cp in sends + passed:
            cp.wait_send()

    return _pallas(body, in_specs=[HBM] * n, out_specs=[HBM] * n,
                   out_shape=[SDS((N_CHIPS,) + p.shape, p.dtype) for p in packs],
                   scratch_shapes=[pltpu.SemaphoreType.DMA((6 * n,)), pltpu.SemaphoreType.DMA((6 * n,))],
                   name=name)(*packs)


def sibling_swap_half(ps, *, name):
    n = len(ps)

    def body(*refs):
        srcs, outs, (send_sems, recv_sems) = refs[:n], refs[n:2 * n], refs[2 * n:]
        x, y, c = _me()
        cps = [_rcopy(src.at[1 - c], out, send_sems.at[a], recv_sems.at[a], (x, y, 1 - c))
               for a, (src, out) in enumerate(zip(srcs, outs))]
        for cp in cps:
            cp.start()
        for cp in cps:
            cp.wait()

    return _pallas(body, in_specs=[HBM] * n, out_specs=[HBM] * n,
                   out_shape=[SDS(p.shape[1:], p.dtype) for p in ps],
                   scratch_shapes=[pltpu.SemaphoreType.DMA((n,)), pltpu.SemaphoreType.DMA((n,))], name=name)(*ps)


def chip_scatter(s1s, *, name):
    n = len(s1s)

    def body(*refs):
        srcs, outs, (send_sems, recv_sems) = refs[:n], refs[n:2 * n], refs[2 * n:]
        x, y, c = _me()
        k = 2 * x + y
        chips = _other_chips(x, y)
        sends = [_rcopy(src.at[2 * cx + cy], out.at[k], send_sems.at[3 * a + j], recv_sems.at[3 * a + j], (cx, cy, c))
                 for a, (src, out) in enumerate(zip(srcs, outs)) for j, (cx, cy) in enumerate(chips)]
        for cp in sends:
            cp.start()
        for a, out in enumerate(outs):
            for j, (cx, cy) in enumerate(chips):
                slot = out.at[2 * cx + cy]
                _rcopy(slot, slot, send_sems.at[3 * a + j], recv_sems.at[3 * a + j], (cx, cy, c)).wait_recv()
        for cp in sends:
            cp.wait_send()

    return _pallas(body, in_specs=[HBM] * n, out_specs=[HBM] * n, out_shape=[SDS(p.shape, p.dtype) for p in s1s],
                   scratch_shapes=[pltpu.SemaphoreType.DMA((3 * n,)), pltpu.SemaphoreType.DMA((3 * n,))],
                   name=name)(*s1s)


def sibling_pair(r2s, *, name):
    n = len(r2s)

    def body(*refs):
        srcs, outs, (send_sems, recv_sems) = refs[:n], refs[n:2 * n], refs[2 * n:]
        x, y, c = _me()
        cps = [_rcopy(src.at[c], out.at[c], send_sems.at[a], recv_sems.at[a], (x, y, 1 - c))
               for a, (src, out) in enumerate(zip(srcs, outs))]
        for cp in cps:
            cp.start()
        for a, out in enumerate(outs):
            theirs = out.at[1 - c]
            _rcopy(theirs, theirs, send_sems.at[a], recv_sems.at[a], (x, y, 1 - c)).wait_recv()
        for cp in cps:
            cp.wait_send()

    return _pallas(body, in_specs=[HBM] * n, out_specs=[HBM] * n, out_shape=[SDS(p.shape, p.dtype) for p in r2s],
                   input_output_aliases={a: a for a in range(n)},
                   scratch_shapes=[pltpu.SemaphoreType.DMA((n,)), pltpu.SemaphoreType.DMA((n,))], name=name)(*r2s)


def allgather_small(v, *, name):
    R, W = v.shape

    def body(v_ref, out_ref, sum_ref, send_sems, recv_sems):
        x, y, c = _me()
        me = 4 * x + 2 * y + c
        out_ref[me] = v_ref[...]
        peers = []
        for m in range(1, N_DEV):
            px = 1 - x if m & 4 else x
            py = 1 - y if m & 2 else y
            pc = 1 - c if m & 1 else c
            peers.append((px, py, pc))
        sends = [_rcopy(v_ref, out_ref.at[me], send_sems.at[j], recv_sems.at[j], p) for j, p in enumerate(peers)]
        for cp in sends:
            cp.start()
        for j, (px, py, pc) in enumerate(peers):
            slot = out_ref.at[4 * px + 2 * py + pc]
            _rcopy(slot, slot, send_sems.at[j], recv_sems.at[j], (px, py, pc)).wait_recv()
        for cp in sends:
            cp.wait_send()
        acc = out_ref[0]
        for d in range(1, N_DEV):
            acc = acc + out_ref[d]
        sum_ref[...] = acc

    return _pallas(body, in_specs=[VMEM], out_specs=[VMEM, VMEM],
                   out_shape=[SDS((N_DEV, R, W), v.dtype), SDS((R, W), v.dtype)],
                   scratch_shapes=[pltpu.SemaphoreType.DMA((N_DEV - 1,)), pltpu.SemaphoreType.DMA((N_DEV - 1,))],
                   name=name)(v)


def _row_tile(R, cap=512):
    best = None
    for t in range(32, cap + 1, 32):
        if R % t == 0:
            best = t
    assert best is not None, R
    return best


def add_halves(p, recv, c_idx, *, name):
    _, n, Rh, W = p.shape
    tr = _row_tile(Rh)

    def body(c_ref, p_ref, r_ref, o_ref):
        o_ref[...] = (p_ref[0].astype(F32) + r_ref[...].astype(F32)).astype(o_ref.dtype)

    gs = pltpu.PrefetchScalarGridSpec(
        num_scalar_prefetch=1, grid=(n, Rh // tr),
        in_specs=[BS((1, 1, tr, W), lambda s, i, c_ref: (c_ref[0], s, i, 0)), BS((1, tr, W), lambda s, i, c_ref: (s, i, 0))],
        out_specs=BS((1, tr, W), lambda s, i, c_ref: (s, i, 0)))
    return _pallas(body, grid_spec=gs, out_shape=SDS((n, Rh, W), p.dtype),
                   compiler_params=_params("parallel", "parallel"), name=name)(c_idx, p, recv)


def sum_slots(s1, got, kc_idx, *, name):
    n, Rh, W = got.shape
    tr = _row_tile(Rh)

    def body(kc_ref, s_ref, g1, g2, g3, o_ref):
        o_ref[0] = s_ref[0].astype(F32) + g1[0].astype(F32) + g2[0].astype(F32) + g3[0].astype(F32)

    slot = lambda d: BS((1, tr, W), lambda i, kc: ((kc[0] + d) % n, i, 0))
    gs = pltpu.PrefetchScalarGridSpec(
        num_scalar_prefetch=1, grid=(Rh // tr,), in_specs=[slot(0), slot(1), slot(2), slot(3)],
        out_specs=BS((1, tr, W), lambda i, kc: (kc[1], i, 0)))
    return _pallas(body, grid_spec=gs, out_shape=SDS((2, Rh, W), F32),
                   compiler_params=_params("parallel"), name=name)(kc_idx, s1, got, got, got)


BIG = [("ffn1_w_gu", True), ("ffn1_w_down", False), ("ffn2_w_gu", True), ("ffn2_w_down", False),
       ("sb_w_qkv", True), ("sb_w_o", False), ("ssd_w_in", True), ("ssd_w_out", False),
       ("sc_w_in", True), ("sc_w_out", False)]
PACK_ALIGN = 64


def _big_layout(shards):
    groups = {}
    for name, cols in BIG:
        L, K, n = shards[name].shape
        items, off = groups.setdefault(n, ([], 0))
        items.append((name, cols, (L, K, n), off, L * K))
        groups[n] = (items, off + L * K)
    return {w: (items, -(-rows // PACK_ALIGN) * PACK_ALIGN) for w, (items, rows) in groups.items()}


def _pack_rows(pieces, total, axis):
    used = sum(p.shape[axis] for p in pieces)
    if total > used:
        pad_shape = list(pieces[0].shape)
        pad_shape[axis] = total - used
        pieces = pieces + [jnp.zeros(pad_shape, pieces[0].dtype)]
    return jnp.concatenate(pieces, axis=axis) if len(pieces) > 1 else pieces[0]


def _gather_big(shards, tag):
    layout = _big_layout(shards)
    packs = [_pack_rows([shards[n].astype(CDT).reshape(-1, w) for n, *_ in items], total, 0).reshape(2, total // 2, w)
             for w, (items, total) in layout.items()]
    fulls = allgather_weights(packs, name=f"{tag}_allgather")
    k_chip = 2 * lax.axis_index("x") + lax.axis_index("y")
    out = {}
    for (w, (items, total)), pack, full in zip(layout.items(), packs, fulls):
        full = lax.dynamic_update_slice(full, pack[None], (k_chip, 0, 0, 0)).reshape(N_CHIPS, total, w)
        for name, cols, (L, K, n), off, rows in items:
            layers = [full[:, off + l * K:off + (l + 1) * K] for l in range(L)]
            if cols:
                out[name] = [p.transpose(1, 0, 2).reshape(K, N_CHIPS * n) for p in layers]
            else:
                out[name] = [p.reshape(N_CHIPS * K, n) for p in layers]
    return out


def _reduce_big(grads, shards, tag):
    layout = _big_layout(shards)
    ps = []
    for w, (items, total) in layout.items():
        pieces = []
        for name, cols, (L, K, n), off, rows in items:
            for g in grads[name]:
                if cols:
                    g = g.reshape(K, N_CHIPS, n).transpose(1, 0, 2)
                else:
                    g = g.reshape(N_CHIPS, K, n)
                pieces.append(g.astype(CDT))
        pack = _pack_rows(pieces, total, 1)
        ps.append(pack.reshape(N_CHIPS, 2, total // 2, w).transpose(1, 0, 2, 3))
    c_core = lax.axis_index("c").astype(jnp.int32)
    k_chip = (2 * lax.axis_index("x") + lax.axis_index("y")).astype(jnp.int32)
    recvs = sibling_swap_half(ps, name=f"{tag}_sibling_swap")
    s1s = [add_halves(p, r, c_core.reshape(1), name=f"{tag}_add_halves_{p.shape[-1]}") for p, r in zip(ps, recvs)]
    gots = chip_scatter(s1s, name=f"{tag}_chip_scatter")
    r2s = [sum_slots(s1, got, jnp.stack([k_chip, c_core]), name=f"{tag}_sum_slots_{s1.shape[-1]}")
           for s1, got in zip(s1s, gots)]
    reds = sibling_pair(r2s, name=f"{tag}_sibling_pair")
    out = {}
    for (w, (items, total)), red in zip(layout.items(), reds):
        red = red.reshape(total, w)
        for name, cols, shape, off, rows in items:
            out[name] = red[off:off + rows].reshape(shape)
    return out


def _pad_lanes(v, width=LANES):
    return jnp.pad(v, ((0, 0), (0, width - v.shape[1])))


def _small_pack(vecs):
    flat, meta, off = [], [], 0
    for v in vecs:
        flat.append(v.reshape(-1))
        meta.append((off, v.size, v.shape))
        off += v.size
    total = -(-off // (8 * LANES)) * 8 * LANES
    flat.append(jnp.zeros((total - off,), F32))
    return jnp.concatenate(flat).reshape(-1, LANES), meta


def _small_unpack(pack, meta):
    flat = pack.reshape(-1)
    return [flat[off:off + size].reshape(shape) for off, size, shape in meta]


def _ffn_fwd(x, g, wgu, wd, tag):
    h = rms_fwd(x, g, name=f"{tag}_norm")
    gate, up, a = ffn_up(h, wgu, name=f"{tag}_gu")
    return matmul(a, wd, res=x, scale=0.5, name=f"{tag}_down"), (x, h, gate, up, a)


def _ffn_bwd(dout, saved, g, wgu, wd, tag):
    x, h, gate, up, a = saved
    F = wd.shape[0]
    dwd = matmul(a, dout, ta=True, out_dtype=CDT, scale=0.5, name=f"{tag}_dwd")
    dgate, dup = ffn_dact(dout, wd, gate, up, 0.5, name=f"{tag}_dact")
    dwgu = jnp.concatenate([matmul(h, dgate, ta=True, out_dtype=CDT, name=f"{tag}_dwg"),
                            matmul(h, dup, ta=True, out_dtype=CDT, name=f"{tag}_dwu")], axis=1)
    dh = matmul(dgate, wgu, tb=True, name=f"{tag}_dh_g")
    dh = matmul(dup, wgu, tb=True, b_k_start=F, res=dh, name=f"{tag}_dh_u")
    dx, dg = rms_bwd(dh, x, g, dout, name=f"{tag}_dnorm")
    return dx, dg, dwgu, dwd


def _sb_mixer_fwd(x, g, wqkv, wo, cfg, tag):
    h = rms_fwd(x, g, name=f"{tag}_norm")
    qkv = matmul(h, wqkv, out_dtype=CDT, name=f"{tag}_qkv")
    o, tab = sb_fwd(qkv, cfg.sb_heads, name=f"{tag}_attn")
    return matmul(o, wo, res=x, name=f"{tag}_out"), (x, h, qkv, tab, o)


def _sb_mixer_bwd(dout, saved, g, wqkv, wo, cfg, tag):
    x, h, qkv, tab, o = saved
    dwo = matmul(o, dout, ta=True, out_dtype=CDT, name=f"{tag}_dwo")
    do = matmul(dout, wo, tb=True, out_dtype=CDT, name=f"{tag}_do")
    dqkv = jnp.concatenate(sb_bwd(qkv, do, tab, cfg.sb_heads, name=f"{tag}_dattn"), axis=1)
    dwqkv = matmul(h, dqkv, ta=True, out_dtype=CDT, name=f"{tag}_dwqkv")
    dh = matmul(dqkv, wqkv, tb=True, name=f"{tag}_dh")
    dx, dg = rms_bwd(dh, x, g, dout, name=f"{tag}_dnorm")
    return dx, dg, dwqkv, dwo


def _sc_mixer_fwd(x, g, win, convw, wout, tag):
    h = rms_fwd(x, g, name=f"{tag}_norm")
    proj = matmul(h, win, name=f"{tag}_in")
    m = sc_fwd(proj, convw, name=f"{tag}_conv")
    return matmul(m, wout, res=x, name=f"{tag}_out"), (x, h, proj, m)


def _sc_mixer_bwd(dout, saved, g, win, convw, wout, tag):
    x, h, proj, m = saved
    dwout = matmul(m, dout, ta=True, out_dtype=CDT, name=f"{tag}_dwout")
    dm = matmul(dout, wout, tb=True, name=f"{tag}_dm")
    dproj, dconvw = sc_bwd(dm, proj, convw, name=f"{tag}_dconv")
    dwin = matmul(h, dproj, ta=True, out_dtype=CDT, name=f"{tag}_dwin")
    dh = matmul(dproj, win, tb=True, name=f"{tag}_dh")
    dx, dg = rms_bwd(dh, x, g, dout, name=f"{tag}_dnorm")
    return dx, dg, dwin, dconvw[:convw.shape[0]], dwout


def _ssd_split_w(win, cfg):
    DI, CV = cfg.DI, cfg.conv_dim
    return win[:, :DI], win[:, DI:DI + CV], _pad_lanes(win[:, DI + CV:])


def _ssd_mixer_fwd(x, g, win, convw, convb, dtb, alog, dskip, ng, wout, cfg, tag):
    wz, wx, wdt = _ssd_split_w(win, cfg)
    h = rms_fwd(x, g, name=f"{tag}_norm")
    z = matmul(h, wz, name=f"{tag}_z")
    xpre = matmul(h, wx, name=f"{tag}_xbc")
    dtpre = matmul(h, wdt, name=f"{tag}_dt")
    pre, xbc = ssd_conv_fwd(xpre, convw, convb, name=f"{tag}_conv")
    dt, ac, dtT, acT = ssd_prep_fwd(dtpre, dtb, alog, name=f"{tag}_prep")
    y, yn, hp = ssd_chunk_fwd(xbc, z, dt, ac, dtT, acT, dskip, ng, cfg, name=f"{tag}_scan")
    out = matmul(yn, wout, res=x, name=f"{tag}_out")
    return out, (x, h, z, xpre, dtpre, pre, xbc, dt, ac, dtT, acT, y, yn, hp)


def _ssd_mixer_bwd(dout, saved, g, win, convw, convb, dtb, alog, dskip, ng, wout, cfg, tag):
    x, h, z, xpre, dtpre, pre, xbc, dt, ac, dtT, acT, y, yn, hp = saved
    T = x.shape[0]
    NH, G = cfg.NH, cfg.G
    wz, wx, wdt = _ssd_split_w(win, cfg)
    dwout = matmul(yn, dout, ta=True, out_dtype=CDT, name=f"{tag}_dwout")
    dyn = matmul(dout, wout, tb=True, name=f"{tag}_dyn")
    dxs, db, dc, dz, colp, rowp, dd, dng = ssd_chunk_bwd(dyn, xbc, z, y, hp, dt, ac, dtT, acT, dskip, ng, cfg,
                                                        name=f"{tag}_dscan")
    col = lambda lo: _pad_lanes(colp[:, :, lo:lo + SSD_HPG].transpose(1, 0, 2).reshape(T, NH))
    row = lambda lo: jnp.pad(rowp[:, lo:lo + SSD_HPG, :].reshape(NH, T), ((0, LANES - NH), (0, 0)))
    ddtpre, dalog, ddtb = ssd_prep_bwd(col(0), col(SSD_HPG), row(0), row(SSD_HPG), dtpre, dtb, alog, name=f"{tag}_dprep")
    dxpre, dconvw, dconvb = ssd_conv_bwd(dxs, db, dc, pre, xpre, convw, name=f"{tag}_dconv")
    dwz = matmul(h, dz, ta=True, out_dtype=CDT, name=f"{tag}_dwz")
    dwx = matmul(h, dxpre, ta=True, out_dtype=CDT, name=f"{tag}_dwx")
    dwdt = matmul(h, ddtpre, ta=True, out_dtype=CDT, name=f"{tag}_dwdt")
    dh = matmul(dz, wz, tb=True, name=f"{tag}_dh_z")
    dh = matmul(dxpre, wx, tb=True, res=dh, name=f"{tag}_dh_x")
    dh = matmul(ddtpre, wdt, tb=True, res=dh, name=f"{tag}_dh_dt")
    dx, dg = rms_bwd(dh, x, g, dout, name=f"{tag}_dnorm")
    dwin = jnp.concatenate([dwz, dwx, dwdt[:, :NH]], axis=1)
    small = dict(conv_w=dconvw[:convw.shape[0]], conv_b=dconvb, dt_bias=ddtb[:, :NH], a_log=dalog[:, :NH],
                 d=dd[:, 0, :SSD_HPG].reshape(1, NH), norm=dng)
    return dx, dg, dwin, dwout, small


INPUT_NAMES = ['x', 'ffn1_norm', 'ffn1_w_gu', 'ffn1_w_down', 'mix_norm', 'ffn2_norm', 'ffn2_w_gu', 'ffn2_w_down',
               'sb_w_qkv', 'sb_w_o', 'ssd_w_in', 'ssd_conv_w', 'ssd_conv_b', 'ssd_dt_bias', 'ssd_a_log', 'ssd_d',
               'ssd_norm', 'ssd_w_out', 'sc_w_in', 'sc_conv_w', 'sc_w_out', 'final_norm']
WEIGHT_NAMES = INPUT_NAMES[1:]
SHARDED_SMALL = ("ssd_conv_w", "sc_conv_w")


def train_step(a, cfg):
    x = a["x"][0]
    tgt = a["loss_target"][0]
    big_names = [n for n, _ in BIG]
    k_chip = 2 * lax.axis_index("x") + lax.axis_index("y")

    w = _gather_big({n: a[n] for n in big_names}, "w")
    conv_pack, conv_meta = _small_pack([a[n] for n in SHARDED_SMALL])
    conv_all, _ = allgather_small(conv_pack, name="conv_w_allgather")
    conv_full = {}
    for i, n in enumerate(SHARDED_SMALL):
        per_chip = [_small_unpack(conv_all[2 * kk], conv_meta)[i] for kk in range(N_CHIPS)]
        conv_full[n] = jnp.concatenate(per_chip, axis=-1)

    def vec(name, i):
        return a[name][i:i + 1]

    ssd_small = lambda j: (a["ssd_conv_b"][j:j + 1], _pad_lanes(a["ssd_dt_bias"][j:j + 1]),
                           _pad_lanes(a["ssd_a_log"][j:j + 1]), _pad_lanes(a["ssd_d"][j:j + 1]), a["ssd_norm"][j:j + 1])

    saved = []
    for i in range(cfg.depth):
        kind, j = i % 3, i // 3
        x, s1 = _ffn_fwd(x, vec("ffn1_norm", i), w["ffn1_w_gu"][i], w["ffn1_w_down"][i], f"l{i}_ffn1")
        if kind == 0:
            x, s2 = _sb_mixer_fwd(x, vec("mix_norm", i), w["sb_w_qkv"][j], w["sb_w_o"][j], cfg, f"l{i}_sb")
        elif kind == 1:
            x, s2 = _ssd_mixer_fwd(x, vec("mix_norm", i), w["ssd_w_in"][j], conv_full["ssd_conv_w"][j], *ssd_small(j),
                                   w["ssd_w_out"][j], cfg, f"l{i}_ssd")
        else:
            x, s2 = _sc_mixer_fwd(x, vec("mix_norm", i), w["sc_w_in"][j], conv_full["sc_conv_w"][j], w["sc_w_out"][j],
                                  f"l{i}_sc")
        x, s3 = _ffn_fwd(x, vec("ffn2_norm", i), w["ffn2_w_gu"][i], w["ffn2_w_down"][i], f"l{i}_ffn2")
        saved.append((s1, s2, s3))
    loss_part, dx, d_final = loss_head(x, a["final_norm"].reshape(1, -1), tgt, name="loss_head")

    gbig = {n: [None] * a[n].shape[0] for n in big_names}
    gsmall = {n: [None] * a[n].shape[0] for n in ("ffn1_norm", "mix_norm", "ffn2_norm", "ssd_conv_w", "ssd_conv_b",
                                                   "ssd_dt_bias", "ssd_a_log", "ssd_d", "ssd_norm", "sc_conv_w")}
    for i in reversed(range(cfg.depth)):
        kind, j = i % 3, i // 3
        s1, s2, s3 = saved[i]
        dx, gsmall["ffn2_norm"][i], gbig["ffn2_w_gu"][i], gbig["ffn2_w_down"][i] = _ffn_bwd(
            dx, s3, vec("ffn2_norm", i), w["ffn2_w_gu"][i], w["ffn2_w_down"][i], f"l{i}_ffn2")
        if kind == 0:
            dx, gsmall["mix_norm"][i], gbig["sb_w_qkv"][j], gbig["sb_w_o"][j] = _sb_mixer_bwd(
                dx, s2, vec("mix_norm", i), w["sb_w_qkv"][j], w["sb_w_o"][j], cfg, f"l{i}_sb")
        elif kind == 1:
            dx, gsmall["mix_norm"][i], gbig["ssd_w_in"][j], gbig["ssd_w_out"][j], sm = _ssd_mixer_bwd(
                dx, s2, vec("mix_norm", i), w["ssd_w_in"][j], conv_full["ssd_conv_w"][j], *ssd_small(j),
                w["ssd_w_out"][j], cfg, f"l{i}_ssd")
            for key, val in sm.items():
                gsmall["ssd_" + key][j] = val
        else:
            dx, gsmall["mix_norm"][i], gbig["sc_w_in"][j], gsmall["sc_conv_w"][j], gbig["sc_w_out"][j] = _sc_mixer_bwd(
                dx, s2, vec("mix_norm", i), w["sc_w_in"][j], conv_full["sc_conv_w"][j], w["sc_w_out"][j], f"l{i}_sc")
        dx, gsmall["ffn1_norm"][i], gbig["ffn1_w_gu"][i], gbig["ffn1_w_down"][i] = _ffn_bwd(
            dx, s1, vec("ffn1_norm", i), w["ffn1_w_gu"][i], w["ffn1_w_down"][i], f"l{i}_ffn1")
    grad_x = dx[None]

    red_big = _reduce_big(gbig, {n: a[n] for n in big_names}, "g")
    small_names = list(gsmall) + ["final_norm"]
    small_vals = [jnp.concatenate([p.reshape((1,) + a[n].shape[1:-1] + (-1,)) for p in gsmall[n]], axis=0)
                  for n in gsmall] + [d_final.reshape(-1)]
    gpack, gmeta = _small_pack(small_vals + [loss_part[:, :1]])
    _, gsum = allgather_small(gpack, name="small_allreduce")
    *small_red, loss = _small_unpack(gsum, gmeta)
    red = dict(red_big)
    for n, v in zip(small_names, small_red):
        if n in SHARDED_SMALL:
            width = a[n].shape[-1]
            v = lax.dynamic_slice_in_dim(v, k_chip * width, width, axis=v.ndim - 1)
        red[n] = v.reshape(a[n].shape)

    delta, new_m, new_v = {}, {}, {}
    two_d = lambda t: t.reshape(-1, t.shape[-1])
    for n in big_names:
        d_, m_, v_ = adamw(two_d(a[n]), two_d(red[n]), two_d(a["m_" + n]), two_d(a["v_" + n]), name=f"adamw_{n}")
        delta[n], new_m[n], new_v[n] = (t.reshape(a[n].shape) for t in (d_, m_, v_))
    rest = [n for n in WEIGHT_NAMES if n not in big_names]
    packs = [_small_pack([src[n] for n in rest]) for src in
             (a, red, {n: a["m_" + n] for n in rest}, {n: a["v_" + n] for n in rest})]
    outs = adamw(*[p for p, _ in packs], name="adamw_small")
    for dst, o in zip((delta, new_m, new_v), outs):
        for n, t in zip(rest, _small_unpack(o, packs[0][1])):
            dst[n] = t
    return (loss.reshape(()), grad_x, *[red[n] for n in WEIGHT_NAMES], *[delta[n] for n in WEIGHT_NAMES],
            *[new_m[n] for n in WEIGHT_NAMES], *[new_v[n] for n in WEIGHT_NAMES])


def kernel(x, ffn1_norm, ffn1_w_gu, ffn1_w_down, mix_norm, ffn2_norm, ffn2_w_gu, ffn2_w_down, sb_w_qkv, sb_w_o, ssd_w_in, ssd_conv_w, ssd_conv_b, ssd_dt_bias, ssd_a_log, ssd_d, ssd_norm, ssd_w_out, sc_w_in, sc_conv_w, sc_w_out, final_norm, loss_target, m_ffn1_norm, m_ffn1_w_gu, m_ffn1_w_down, m_mix_norm, m_ffn2_norm, m_ffn2_w_gu, m_ffn2_w_down, m_sb_w_qkv, m_sb_w_o, m_ssd_w_in, m_ssd_conv_w, m_ssd_conv_b, m_ssd_dt_bias, m_ssd_a_log, m_ssd_d, m_ssd_norm, m_ssd_w_out, m_sc_w_in, m_sc_conv_w, m_sc_w_out, m_final_norm, v_ffn1_norm, v_ffn1_w_gu, v_ffn1_w_down, v_mix_norm, v_ffn2_norm, v_ffn2_w_gu, v_ffn2_w_down, v_sb_w_qkv, v_sb_w_o, v_ssd_w_in, v_ssd_conv_w, v_ssd_conv_b, v_ssd_dt_bias, v_ssd_a_log, v_ssd_d, v_ssd_norm, v_ssd_w_out, v_sc_w_in, v_sc_conv_w, v_sc_w_out, v_final_norm):
    args = dict(locals())
    return train_step(args, Cfg())
```

```python
import functools
import math

import jax
import jax.numpy as jnp
from jax import lax
from jax.experimental import pallas as pl
from jax.experimental.pallas import tpu as pltpu

F32 = jnp.float32
CDT = jnp.bfloat16
HI = lax.Precision.HIGHEST
BS = pl.BlockSpec
SDS = jax.ShapeDtypeStruct
MESH = pl.DeviceIdType.MESH

RMS_EPS = 1e-6
ADAM_LR, ADAM_B1, ADAM_B2, ADAM_EPS, ADAM_WD, ADAM_STEP = 0.001, 0.9, 0.999, 1e-08, 0.01, 10
V7X_VMEM_LIMIT_BYTES = 56 * 1024 * 1024
LANES = 128
CHUNK = 128
HEAD_DIM = 64
SSD_HPG = 4
N_CHIPS = 4
N_DEV = 8


class Cfg:
    def __init__(self, d_model=1024, d_ff=2816, depth=4, sb_heads=16, ssd_groups=8, ssd_state=128,
                 ssd_conv=4, sc_width=3):
        self.D, self.F, self.depth = d_model, d_ff, depth
        self.sb_heads = sb_heads
        self.DI = 2 * d_model
        self.G = ssd_groups
        self.NH = self.DI // HEAD_DIM
        assert self.NH == SSD_HPG * self.G and ssd_state == LANES and sb_heads * HEAD_DIM == d_model
        self.conv_dim = self.DI + 2 * self.G * ssd_state
        self.in_dim = self.DI + self.conv_dim + self.NH
        self.ssd_conv, self.sc_width = ssd_conv, sc_width
        self.n_sb, self.n_ssd, self.n_sc = (depth + 2) // 3, (depth + 1) // 3, depth // 3


def _pallas(body, **kw):
    return pl.pallas_call(body, **kw)


def _params(*sem):
    return pltpu.CompilerParams(dimension_semantics=sem, vmem_limit_bytes=V7X_VMEM_LIMIT_BYTES)


def _tile(n, cap):
    if n <= cap:
        return n
    best = None
    for m in range(1, n // LANES + 1):
        t = m * LANES
        if n % t == 0 and t <= cap:
            best = t
    assert best is not None, (n, cap)
    return best


def _sigmoid(x):
    return 1.0 / (1.0 + jnp.exp(-x))


def _softplus_neg_abs(x):
    return jnp.log(1.0 + jnp.exp(-jnp.abs(x)))


def _colsel(blk, idx):
    lane = lax.broadcasted_iota(jnp.int32, blk.shape, 1)
    return jnp.sum(jnp.where(lane == idx, blk, 0.0), axis=1, keepdims=True)


def _rowsel(blk, idx):
    sub = lax.broadcasted_iota(jnp.int32, blk.shape, 0)
    return jnp.sum(jnp.where(sub == idx, blk, 0.0), axis=0, keepdims=True)


def _nt(a, b):
    return lax.dot_general(a, b, (((1,), (1,)), ((), ())), preferred_element_type=F32)


def _tn(a, b):
    return lax.dot_general(a, b, (((0,), (0,)), ((), ())), preferred_element_type=F32)


def _nn(a, b):
    return jnp.dot(a, b, preferred_element_type=F32)


def matmul(a, b, *, name, ta=False, tb=False, out_dtype=F32, res=None, scale=1.0, tm=1408, tn=1408, tk=1408,
           b_k_start=0):
    M, K = (a.shape[1], a.shape[0]) if ta else a.shape
    N, K2 = b.shape if tb else (b.shape[1], b.shape[0])
    assert K == K2 or (tb and b_k_start + K <= K2), (a.shape, b.shape, ta, tb)
    tm, tn, tk = _tile(M, tm), _tile(N, tn), _tile(K, tk)
    nk = K // tk
    assert b_k_start % tk == 0
    kb0 = b_k_start // tk
    dn = (((0 if ta else 1,), (1 if tb else 0,)), ((), ()))

    def body(*refs):
        a_ref, b_ref = refs[:2]
        r_ref = refs[2] if res is not None else None
        o_ref = refs[3 if res is not None else 2]
        part = lax.dot_general(a_ref[...].astype(CDT), b_ref[...].astype(CDT), dn, preferred_element_type=F32)

        def finish(total):
            o = total * scale
            if res is not None:
                o = o + r_ref[...].astype(F32)
            o_ref[...] = o.astype(o_ref.dtype)

        if nk == 1:
            finish(part)
            return
        acc = refs[-1]
        k = pl.program_id(2)

        @pl.when(k == 0)
        def _():
            acc[...] = part

        @pl.when(jnp.logical_and(k > 0, k < nk - 1))
        def _():
            acc[...] += part

        @pl.when(k == nk - 1)
        def _():
            finish(acc[...] + part)

    a_spec = BS((tk, tm), lambda i, j, k: (k, i)) if ta else BS((tm, tk), lambda i, j, k: (i, k))
    b_spec = BS((tn, tk), lambda i, j, k: (j, k + kb0)) if tb else BS((tk, tn), lambda i, j, k: (k, j))
    o_spec = BS((tm, tn), lambda i, j, k: (i, j))
    ins, specs = [a, b], [a_spec, b_spec]
    if res is not None:
        ins.append(res)
        specs.append(o_spec)
    return _pallas(body, grid=(M // tm, N // tn, nk), in_specs=specs, out_specs=o_spec,
                   out_shape=SDS((M, N), out_dtype),
                   scratch_shapes=[pltpu.VMEM((tm, tn), F32)] if nk > 1 else [],
                   compiler_params=_params("parallel", "parallel", "arbitrary"), name=name)(*ins)


def rms_fwd(x, g, *, name):
    T, D = x.shape
    tm = _tile(T, 512)

    def body(x_ref, g_ref, o_ref):
        xv = x_ref[...]
        r = lax.rsqrt(jnp.mean(xv * xv, axis=-1, keepdims=True) + RMS_EPS)
        o_ref[...] = (xv * r * g_ref[...]).astype(o_ref.dtype)

    return _pallas(body, grid=(T // tm,), in_specs=[BS((tm, D), lambda i: (i, 0)), BS((1, D), lambda i: (0, 0))],
                   out_specs=BS((tm, D), lambda i: (i, 0)), out_shape=SDS((T, D), CDT),
                   compiler_params=_params("parallel"), name=name)(x, g)


def rms_bwd(dh, x, g, dres, *, name):
    T, D = x.shape
    tm = _tile(T, 512)

    def body(dh_ref, x_ref, g_ref, r_ref, dx_ref, dg_ref):
        i = pl.program_id(0)
        xv = x_ref[...]
        r = lax.rsqrt(jnp.mean(xv * xv, axis=-1, keepdims=True) + RMS_EPS)
        xhat = xv * r
        dhv = dh_ref[...].astype(F32)
        dxh = dhv * g_ref[...]
        m = jnp.mean(dxh * xhat, axis=-1, keepdims=True)
        dx_ref[...] = r * (dxh - xhat * m) + r_ref[...]

        @pl.when(i == 0)
        def _():
            dg_ref[...] = jnp.zeros_like(dg_ref)

        dg_ref[...] += jnp.sum(dhv * xhat, axis=0, keepdims=True)

    row = BS((tm, D), lambda i: (i, 0))
    vec = BS((1, D), lambda i: (0, 0))
    return _pallas(body, grid=(T // tm,), in_specs=[row, row, vec, row], out_specs=[row, vec],
                   out_shape=[SDS((T, D), F32), SDS((1, D), F32)],
                   compiler_params=_params("arbitrary"), name=name)(dh, x, g, dres)


def ffn_up(h, wgu, *, name):
    T, D = h.shape
    F = wgu.shape[1] // 2
    tm, tn = _tile(T, 512), _tile(F, 1408)
    nj = F // tn

    def body(h_ref, wg_ref, wu_ref, g_ref, u_ref, a_ref):
        hv = h_ref[...].astype(CDT)
        g = jnp.dot(hv, wg_ref[...].astype(CDT), preferred_element_type=F32)
        u = jnp.dot(hv, wu_ref[...].astype(CDT), preferred_element_type=F32)
        g_ref[...] = g.astype(g_ref.dtype)
        u_ref[...] = u.astype(u_ref.dtype)
        a_ref[...] = (g * _sigmoid(g) * u).astype(a_ref.dtype)

    out = BS((tm, tn), lambda j, i: (i, j))
    return _pallas(body, grid=(nj, T // tm),
                   in_specs=[BS((tm, D), lambda j, i: (i, 0)), BS((D, tn), lambda j, i: (0, j)),
                             BS((D, tn), lambda j, i: (0, j + nj))],
                   out_specs=[out, out, out], out_shape=[SDS((T, F), CDT)] * 3,
                   compiler_params=_params("parallel", "parallel"), name=name)(h, wgu, wgu)


def ffn_dact(dout, wd, g, u, scale, *, name):
    T, D = dout.shape
    F = wd.shape[0]
    tm, tn = _tile(T, 512), _tile(F, 1408)

    def body(do_ref, wd_ref, g_ref, u_ref, dg_ref, du_ref):
        da = scale * _nt(do_ref[...].astype(CDT), wd_ref[...].astype(CDT))
        gv = g_ref[...].astype(F32)
        uv = u_ref[...].astype(F32)
        sg = _sigmoid(gv)
        dg_ref[...] = (da * uv * (sg * (1.0 + gv * (1.0 - sg)))).astype(dg_ref.dtype)
        du_ref[...] = (da * (gv * sg)).astype(du_ref.dtype)

    blk = BS((tm, tn), lambda j, i: (i, j))
    return _pallas(body, grid=(F // tn, T // tm),
                   in_specs=[BS((tm, D), lambda j, i: (i, 0)), BS((tn, D), lambda j, i: (j, 0)), blk, blk],
                   out_specs=[blk, blk], out_shape=[SDS((T, F), CDT)] * 2,
                   compiler_params=_params("parallel", "parallel"), name=name)(dout, wd, g, u)


def loss_head(x, g, tgt, *, name):
    T, D = x.shape
    tm = _tile(T, 512)

    def body(x_ref, g_ref, t_ref, l_ref, dx_ref, dg_ref):
        i = pl.program_id(0)
        xv = x_ref[...]
        gv = g_ref[...]
        r = lax.rsqrt(jnp.mean(xv * xv, axis=-1, keepdims=True) + RMS_EPS)
        xhat = xv * r
        err = xhat * gv - t_ref[...]
        part = 0.5 * jnp.sum(jnp.mean(err * err, axis=-1, keepdims=True), axis=0, keepdims=True)
        dy = err * (1.0 / D)
        dxh = dy * gv
        m = jnp.mean(dxh * xhat, axis=-1, keepdims=True)
        dx_ref[...] = r * (dxh - xhat * m)

        @pl.when(i == 0)
        def _():
            dg_ref[...] = jnp.zeros_like(dg_ref)
            l_ref[...] = jnp.zeros_like(l_ref)

        dg_ref[...] += jnp.sum(dy * xhat, axis=0, keepdims=True)
        l_ref[...] += jnp.broadcast_to(part, l_ref.shape)

    row = BS((tm, D), lambda i: (i, 0))
    vec = BS((1, D), lambda i: (0, 0))
    return _pallas(body, grid=(T // tm,), in_specs=[row, vec, row],
                   out_specs=[BS((1, LANES), lambda i: (0, 0)), row, vec],
                   out_shape=[SDS((1, LANES), F32), SDS((T, D), F32), SDS((1, D), F32)],
                   compiler_params=_params("arbitrary"), name=name)(x, g, tgt)


def adamw(w, g, m, v, *, name):
    R, C = w.shape
    tr = R
    if R * C * 4 > (1 << 20):
        for cand in range(8, R + 1, 8):
            if R % cand == 0 and cand * C * 4 <= (1 << 20):
                tr = cand

    def body(w_ref, g_ref, m_ref, v_ref, d_ref, nm_ref, nv_ref):
        gv = g_ref[...]
        nm = ADAM_B1 * m_ref[...] + (1.0 - ADAM_B1) * gv
        nv = ADAM_B2 * v_ref[...] + (1.0 - ADAM_B2) * (gv * gv)
        m_hat = nm / (1.0 - ADAM_B1 ** ADAM_STEP)
        v_hat = nv / (1.0 - ADAM_B2 ** ADAM_STEP)
        d_ref[...] = -ADAM_LR * (m_hat / (jnp.sqrt(v_hat) + ADAM_EPS) + ADAM_WD * w_ref[...])
        nm_ref[...] = nm
        nv_ref[...] = nv

    blk = BS((tr, C), lambda i: (i, 0))
    return _pallas(body, grid=(R // tr,), in_specs=[blk] * 4, out_specs=[blk] * 3,
                   out_shape=[SDS((R, C), F32)] * 3, compiler_params=_params("parallel"), name=name)(w, g, m, v)


def _sb_tri():
    row = lax.broadcasted_iota(jnp.int32, (CHUNK, CHUNK), 0)
    col = lax.broadcasted_iota(jnp.int32, (CHUNK, CHUNK), 1)
    return row, col


def _tri_stack(keep):
    t = keep.astype(jnp.bfloat16)
    return jnp.concatenate([t, t], axis=0)


def _split_dot(x, tri_stack):
    hi = x.astype(jnp.bfloat16)
    mid = (x - hi.astype(F32)).astype(jnp.bfloat16)
    return jnp.dot(jnp.concatenate([hi, mid], axis=1), tri_stack, preferred_element_type=F32)


def _sb_heads_per_step(H, want):
    hb = min(H, want)
    assert H % hb == 0, (H, hb)
    return hb


SB_FWD_HEADS = 8
SB_BWD_HEADS = 8


def _head_pair_masked(blk, lo):
    zero = jnp.zeros_like(blk)
    return jnp.where(lo, blk, zero), jnp.where(lo, zero, blk)


def sb_fwd(qkv, n_heads, *, name):
    T, D3 = qkv.shape
    D = D3 // 3
    nq = T // CHUNK
    assert nq <= LANES
    scale = HEAD_DIM ** -0.5
    hb = _sb_heads_per_step(n_heads, SB_FWD_HEADS)
    assert hb % 2 == 0
    LW = hb * HEAD_DIM
    ngrp = D // LW
    heads, pairs = range(hb), range(hb // 2)
    lanes = lambda p: slice(p * LANES, (p + 1) * LANES)

    def body(q_ref, k_ref, v_ref, o_ref, tab_ref, acc_scr, cs_scr, qm_scr):
        qi = pl.program_id(1)
        row, col = _sb_tri()
        lo = col < HEAD_DIM
        upper_stack = _tri_stack(row > col)
        acc_scr[...] = jnp.zeros_like(acc_scr)
        cs_scr[...] = jnp.zeros_like(cs_scr)
        tab_ref[...] = jnp.zeros_like(tab_ref)
        for p in pairs:
            qm_scr[2 * p], qm_scr[2 * p + 1] = _head_pair_masked(q_ref[:, lanes(p)] * scale, lo)

        def tile(j, diag):
            sl = pl.ds(pl.multiple_of(j * CHUNK, CHUNK), CHUNK)
            keep = (lambda t: jnp.where(col < row, t, 0.0)) if diag else (lambda t: t)
            kbs = [k_ref[sl, lanes(p)] for p in pairs]
            zs = [_nt(qm_scr[h], kbs[h // 2]) for h in heads]
            lszs = [jnp.minimum(z, 0.0) - _softplus_neg_abs(z) for z in zs]
            lks = [keep(lsz - z) for lsz, z in zip(lszs, zs)]
            tls = [_split_dot(lk, upper_stack) for lk in lks]
            css = [cs_scr[h] for h in heads]
            logits = [lsz + tl + cs for lsz, tl, cs in zip(lszs, tls, css)]
            atts = [keep(jnp.exp(lg)).astype(CDT) for lg in logits]
            vms = [jnp.concatenate(_head_pair_masked(v_ref[sl, lanes(p)], lo), axis=0) for p in pairs]
            pvs = [_nn(jnp.concatenate([atts[2 * p], atts[2 * p + 1]], axis=1), vms[p]) for p in pairs]
            for p in pairs:
                acc_scr[:, lanes(p)] += pvs[p]
            for h in heads:
                tab_ref[h, 0] = jnp.where(col == j, css[h], tab_ref[h, 0])
                if diag:
                    cs_scr[h] = css[h] + (tls[h][:, :1] + lks[h][:, :1])
                else:
                    cs_scr[h] = logits[h][:, :1] - zs[h][:, :1]

        tile(qi, True)

        def step(jj, carry):
            tile(qi - jj, False)
            return carry

        lax.fori_loop(1, qi + 1, step, 0)
        o_ref[...] = acc_scr[...].astype(o_ref.dtype)

    once = pl.Buffered(1)
    qs = BS((CHUNK, LW), lambda g, i: (i, g))
    return _pallas(body, grid=(ngrp, nq),
                   in_specs=[qs, BS((T, LW), lambda g, i: (0, ngrp + g), pipeline_mode=once),
                             BS((T, LW), lambda g, i: (0, 2 * ngrp + g), pipeline_mode=once)],
                   out_specs=[qs, BS((hb, 1, CHUNK, CHUNK), lambda g, i: (g, i, 0, 0))],
                   out_shape=[SDS((T, D), CDT), SDS((n_heads, nq, CHUNK, CHUNK), F32)],
                   scratch_shapes=[pltpu.VMEM((CHUNK, LW), F32), pltpu.VMEM((hb, CHUNK, 1), F32),
                                   pltpu.VMEM((hb, CHUNK, LANES), CDT)],
                   compiler_params=_params("parallel", "arbitrary"), name=name)(qkv, qkv, qkv)


def sb_bwd(qkv, do, tab, n_heads, *, name):
    T, D3 = qkv.shape
    D = D3 // 3
    nq = T // CHUNK
    scale = HEAD_DIM ** -0.5
    hb = _sb_heads_per_step(n_heads, SB_BWD_HEADS)
    assert hb % 2 == 0
    LW = hb * HEAD_DIM
    ngrp = D // LW
    heads, pairs = range(hb), range(hb // 2)
    lanes = lambda p: slice(p * LANES, (p + 1) * LANES)

    def body(q_ref, k_ref, v_ref, do_ref, tab_ref, dq_ref, dk_ref, dv_ref,
             dk_acc, dv_acc, dq_scr, pfx_scr, qm_scr, dom_scr):
        qi = pl.program_id(1)

        @pl.when(qi == 0)
        def _():
            dk_acc[...] = jnp.zeros_like(dk_acc)
            dv_acc[...] = jnp.zeros_like(dv_acc)

        dq_scr[...] = jnp.zeros_like(dq_scr)
        pfx_scr[...] = jnp.zeros_like(pfx_scr)
        row, col = _sb_tri()
        lo = col < HEAD_DIM
        upper_stack = _tri_stack(row > col)
        before_stack = _tri_stack(row < col)
        for p in pairs:
            qm_scr[2 * p], qm_scr[2 * p + 1] = _head_pair_masked(q_ref[:, lanes(p)] * scale, lo)
            dom_scr[2 * p], dom_scr[2 * p + 1] = _head_pair_masked(do_ref[:, lanes(p)], lo)

        def tile(j, diag):
            sl = pl.ds(pl.multiple_of(j * CHUNK, CHUNK), CHUNK)
            keep = (lambda t: jnp.where(col < row, t, 0.0)) if diag else (lambda t: t)
            kbs = [k_ref[sl, lanes(p)] for p in pairs]
            vbs = [v_ref[sl, lanes(p)] for p in pairs]
            zs = [_nt(qm_scr[h], kbs[h // 2]) for h in heads]
            das = [_nt(dom_scr[h], vbs[h // 2]) for h in heads]
            lszs = [jnp.minimum(z, 0.0) - _softplus_neg_abs(z) for z in zs]
            lks = [keep(lsz - z) for lsz, z in zip(lszs, zs)]
            tls = [_split_dot(lk, upper_stack) for lk in lks]
            css = [jnp.sum(jnp.where(col == j, tab_ref[h, 0], 0.0), axis=1, keepdims=True) for h in heads]
            atts = [keep(jnp.exp(lsz + tl + cs)) for lsz, tl, cs in zip(lszs, tls, css)]
            dls = [att * da for att, da in zip(atts, das)]
            pres = [_split_dot(dl, before_stack) for dl in dls]
            pfxs = [pfx_scr[h] for h in heads]
            sigs = [jnp.exp(lsz) for lsz in lszs]
            befores = [pre + pfx for pre, pfx in zip(pres, pfxs)]
            dzs = [(dl * (1.0 - sig) - keep(sig * bf)).astype(CDT) for dl, sig, bf in zip(dls, sigs, befores)]
            attc = [att.astype(CDT) for att in atts]
            kms = [jnp.concatenate(_head_pair_masked(kb, lo), axis=0) for kb in kbs]
            dqs = [_nn(jnp.concatenate([dzs[2 * p], dzs[2 * p + 1]], axis=1), kms[p]) for p in pairs]
            both = lambda xs, p: jnp.concatenate([xs[2 * p], xs[2 * p + 1]], axis=0)
            dks = [_tn(both(dzs, p), jnp.concatenate([qm_scr[2 * p], qm_scr[2 * p + 1]], axis=0)) for p in pairs]
            dvs = [_tn(both(attc, p), jnp.concatenate([dom_scr[2 * p], dom_scr[2 * p + 1]], axis=0)) for p in pairs]
            last = CHUNK - 1
            for p in pairs:
                dq_scr[:, lanes(p)] += dqs[p]
                dk_acc[sl, lanes(p)] += dks[p]
                dv_acc[sl, lanes(p)] += dvs[p]
            for h in heads:
                pfx_scr[h] = befores[h][:, last:] + dls[h][:, last:]

        def step(j, carry):
            tile(j, False)
            return carry

        lax.fori_loop(0, qi, step, 0)
        tile(qi, True)
        dq_ref[...] = (dq_scr[...] * scale).astype(dq_ref.dtype)

        @pl.when(qi == nq - 1)
        def _():
            dk_ref[...] = dk_acc[...].astype(dk_ref.dtype)
            dv_ref[...] = dv_acc[...].astype(dv_ref.dtype)

    once = pl.Buffered(1)
    qs = BS((CHUNK, LW), lambda g, i: (i, g))
    ks = BS((T, LW), lambda g, i: (0, g))
    return _pallas(body, grid=(ngrp, nq),
                   in_specs=[qs, BS((T, LW), lambda g, i: (0, ngrp + g), pipeline_mode=once),
                             BS((T, LW), lambda g, i: (0, 2 * ngrp + g), pipeline_mode=once), qs,
                             BS((hb, 1, CHUNK, CHUNK), lambda g, i: (g, i, 0, 0))],
                   out_specs=[qs, ks, ks],
                   out_shape=[SDS((T, D), CDT)] * 3,
                   scratch_shapes=[pltpu.VMEM((T, LW), F32), pltpu.VMEM((T, LW), F32),
                                   pltpu.VMEM((CHUNK, LW), F32), pltpu.VMEM((hb, CHUNK, 1), F32),
                                   pltpu.VMEM((hb, CHUNK, LANES), CDT), pltpu.VMEM((hb, CHUNK, LANES), CDT)],
                   compiler_params=_params("parallel", "arbitrary"), name=name)(qkv, qkv, qkv, do, tab)


HALO = 8


def _shift_down(cur, prev8, s):
    if s == 0:
        return cur
    tt = cur.shape[0]
    row = lax.broadcasted_iota(jnp.int32, cur.shape, 0)
    fix = jnp.concatenate([pltpu.roll(prev8, s, 0), jnp.zeros((tt - HALO, cur.shape[1]), cur.dtype)], axis=0)
    return jnp.where(row < s, fix, pltpu.roll(cur, s, 0))


def _shift_up(cur, next8, s):
    if s == 0:
        return cur
    tt = cur.shape[0]
    row = lax.broadcasted_iota(jnp.int32, cur.shape, 0)
    fix = jnp.concatenate([jnp.zeros((tt - HALO, cur.shape[1]), cur.dtype), pltpu.roll(next8, HALO - s, 0)], axis=0)
    return jnp.where(row >= tt - s, fix, pltpu.roll(cur, tt - s, 0))


def _halo_specs(tt, width, T):
    per = tt // HALO
    last = T // HALO - 1
    cur = BS((tt, width), lambda i: (i, 0))
    prev = BS((HALO, width), lambda i: (jnp.maximum(i * per - 1, 0), 0))
    nxt = BS((HALO, width), lambda i: (jnp.minimum((i + 1) * per, last), 0))
    return cur, prev, nxt


def sc_fwd(proj, w, *, name):
    T, D3 = proj.shape
    D = D3 // 3
    K = w.shape[0]
    tt = _tile(T, 256)
    cur, prev, _ = _halo_specs(tt, D3, T)

    def body(p_ref, pp_ref, w_ref, o_ref):
        i = pl.program_id(0)
        pc = p_ref[:, D:2 * D] * p_ref[:, 2 * D:]
        pp = jnp.where(i > 0, pp_ref[:, D:2 * D] * pp_ref[:, 2 * D:], 0.0)
        u = jnp.zeros((tt, D), F32)
        for kk in range(K):
            u = u + w_ref[kk:kk + 1, :] * _shift_down(pc, pp, K - 1 - kk)
        o_ref[...] = (p_ref[:, :D] * u).astype(o_ref.dtype)

    return _pallas(body, grid=(T // tt,), in_specs=[cur, prev, BS((K, D), lambda i: (0, 0))],
                   out_specs=BS((tt, D), lambda i: (i, 0)), out_shape=SDS((T, D), CDT),
                   compiler_params=_params("parallel"), name=name)(proj, proj, w)


def sc_bwd(dm, proj, w, *, name):
    T, D3 = proj.shape
    D = D3 // 3
    K = w.shape[0]
    tt = _tile(T, 256)
    cur, prev, nxt = _halo_specs(tt, D3, T)
    dcur, _, dnxt = _halo_specs(tt, D, T)
    nsteps = T // tt

    def body(dm_ref, dmn_ref, p_ref, pp_ref, pn_ref, w_ref, o_ref, dw_ref):
        i = pl.program_id(0)
        b = p_ref[:, :D]
        c = p_ref[:, D:2 * D]
        h = p_ref[:, 2 * D:]
        pc = c * h
        pp = jnp.where(i > 0, pp_ref[:, D:2 * D] * pp_ref[:, 2 * D:], 0.0)
        dmv = dm_ref[...]
        du = dmv * b
        dun = jnp.where(i < nsteps - 1, dmn_ref[...] * pn_ref[:, :D], 0.0)
        u = jnp.zeros((tt, D), F32)
        dp = jnp.zeros((tt, D), F32)
        rows = []
        for kk in range(K):
            sh = _shift_down(pc, pp, K - 1 - kk)
            u = u + w_ref[kk:kk + 1, :] * sh
            dp = dp + w_ref[kk:kk + 1, :] * _shift_up(du, dun, K - 1 - kk)
            rows.append(jnp.sum(du * sh, axis=0, keepdims=True))
        rows.append(jnp.zeros((HALO - K, D), F32))
        o_ref[:, :D] = (dmv * u).astype(o_ref.dtype)
        o_ref[:, D:2 * D] = (dp * h).astype(o_ref.dtype)
        o_ref[:, 2 * D:] = (dp * c).astype(o_ref.dtype)

        @pl.when(i == 0)
        def _():
            dw_ref[...] = jnp.zeros_like(dw_ref)

        dw_ref[...] += jnp.concatenate(rows, axis=0)

    return _pallas(body, grid=(nsteps,), in_specs=[dcur, dnxt, cur, prev, nxt, BS((K, D), lambda i: (0, 0))],
                   out_specs=[BS((tt, D3), lambda i: (i, 0)), BS((HALO, D), lambda i: (0, 0))],
                   out_shape=[SDS((T, D3), CDT), SDS((HALO, D), F32)],
                   compiler_params=_params("arbitrary"), name=name)(dm, dm, proj, proj, proj, w)


def ssd_conv_fwd(xpre, w, b, *, name):
    T, C = xpre.shape
    K = w.shape[0]
    tt = _tile(T, 256)
    cur, prev, _ = _halo_specs(tt, C, T)

    def body(x_ref, xp_ref, w_ref, b_ref, pre_ref, o_ref):
        i = pl.program_id(0)
        xv = x_ref[...]
        xp = jnp.where(i > 0, xp_ref[...], 0.0)
        acc = jnp.broadcast_to(b_ref[...], (tt, C))
        for kk in range(K):
            acc = acc + w_ref[kk:kk + 1, :] * _shift_down(xv, xp, K - 1 - kk)
        pre_ref[...] = acc
        o_ref[...] = acc * _sigmoid(acc)

    row = BS((tt, C), lambda i: (i, 0))
    return _pallas(body, grid=(T // tt,), in_specs=[cur, prev, BS((K, C), lambda i: (0, 0)), BS((1, C), lambda i: (0, 0))],
                   out_specs=[row, row], out_shape=[SDS((T, C), F32)] * 2,
                   compiler_params=_params("parallel"), name=name)(xpre, xpre, w, b)


def ssd_conv_bwd(dxs, db_, dc_, pre, xpre, w, *, name):
    T, C = xpre.shape
    K = w.shape[0]
    tt = _tile(T, 256)
    cur, prev, nxt = _halo_specs(tt, C, T)
    widths = (dxs.shape[1], db_.shape[1], dc_.shape[1])
    gspecs = []
    for wd in widths:
        c_, _, n_ = _halo_specs(tt, wd, T)
        gspecs += [c_, n_]
    nsteps = T // tt

    def dsilu(p):
        s = _sigmoid(p)
        return s * (1.0 + p * (1.0 - s))

    def body(g0, g0n, g1, g1n, g2, g2n, pre_ref, pren_ref, x_ref, xp_ref, w_ref, dx_ref, dw_ref, dbias_ref):
        i = pl.program_id(0)
        gcur = jnp.concatenate([g0[...], g1[...], g2[...]], axis=1)
        gnxt = jnp.concatenate([g0n[...], g1n[...], g2n[...]], axis=1)
        dpre = gcur * dsilu(pre_ref[...])
        dpren = jnp.where(i < nsteps - 1, gnxt * dsilu(pren_ref[...]), 0.0)
        xv = x_ref[...]
        xp = jnp.where(i > 0, xp_ref[...], 0.0)
        dx = jnp.zeros((tt, C), F32)
        rows = []
        for kk in range(K):
            dx = dx + w_ref[kk:kk + 1, :] * _shift_up(dpre, dpren, K - 1 - kk)
            rows.append(jnp.sum(dpre * _shift_down(xv, xp, K - 1 - kk), axis=0, keepdims=True))
        rows.append(jnp.zeros((HALO - K, C), F32))
        dx_ref[...] = dx.astype(dx_ref.dtype)

        @pl.when(i == 0)
        def _():
            dw_ref[...] = jnp.zeros_like(dw_ref)
            dbias_ref[...] = jnp.zeros_like(dbias_ref)

        dw_ref[...] += jnp.concatenate(rows, axis=0)
        dbias_ref[...] += jnp.sum(dpre, axis=0, keepdims=True)

    return _pallas(body, grid=(nsteps,),
                   in_specs=gspecs + [cur, nxt, cur, prev, BS((K, C), lambda i: (0, 0))],
                   out_specs=[BS((tt, C), lambda i: (i, 0)), BS((HALO, C), lambda i: (0, 0)), BS((1, C), lambda i: (0, 0))],
                   out_shape=[SDS((T, C), CDT), SDS((HALO, C), F32), SDS((1, C), F32)],
                   compiler_params=_params("arbitrary"), name=name)(dxs, dxs, db_, db_, dc_, dc_, pre, pre, xpre, xpre, w)


def ssd_prep_fwd(dtpre, bias, alog, *, name):
    T = dtpre.shape[0]
    nc = T // CHUNK

    def body(p_ref, b_ref, a_ref, dt_ref, ac_ref, dtT_ref, acT_ref):
        xv = p_ref[...] + b_ref[...]
        dt = jnp.maximum(xv, 0.0) + _softplus_neg_abs(xv)
        row, col = _sb_tri()
        lower = (col <= row).astype(F32)
        ac = jnp.dot(lower, dt * -jnp.exp(a_ref[...]), precision=HI, preferred_element_type=F32)
        dt_ref[...] = dt
        ac_ref[...] = ac
        dtT_ref[...] = dt.T
        acT_ref[...] = ac.T

    blk = BS((CHUNK, LANES), lambda c: (c, 0))
    blkT = BS((LANES, CHUNK), lambda c: (0, c))
    vec = BS((1, LANES), lambda c: (0, 0))
    return _pallas(body, grid=(nc,), in_specs=[blk, vec, vec], out_specs=[blk, blk, blkT, blkT],
                   out_shape=[SDS((T, LANES), F32)] * 2 + [SDS((LANES, T), F32)] * 2,
                   compiler_params=_params("parallel"), name=name)(dtpre, bias, alog)


def ssd_prep_bwd(dac_c, ddt_c, dac_r, ddt_r, dtpre, bias, alog, *, name):
    T = dtpre.shape[0]
    nc = T // CHUNK

    def body(dac_ref, ddt_ref, dacr_ref, ddtr_ref, p_ref, b_ref, a_ref, o_ref, da_ref, db_ref):
        c = pl.program_id(0)
        xv = p_ref[...] + b_ref[...]
        dt = jnp.maximum(xv, 0.0) + _softplus_neg_abs(xv)
        av = -jnp.exp(a_ref[...])
        row, col = _sb_tri()
        after = (col >= row).astype(F32)
        dda = jnp.dot(after, dac_ref[...] + dacr_ref[...].T, precision=HI, preferred_element_type=F32)
        dpre = (ddt_ref[...] + ddtr_ref[...].T + av * dda) * _sigmoid(xv)
        o_ref[...] = dpre

        @pl.when(c == 0)
        def _():
            da_ref[...] = jnp.zeros_like(da_ref)
            db_ref[...] = jnp.zeros_like(db_ref)

        da_ref[...] += jnp.sum(dt * dda, axis=0, keepdims=True) * av
        db_ref[...] += jnp.sum(dpre, axis=0, keepdims=True)

    blk = BS((CHUNK, LANES), lambda c: (c, 0))
    blkT = BS((LANES, CHUNK), lambda c: (0, c))
    vec = BS((1, LANES), lambda c: (0, 0))
    return _pallas(body, grid=(nc,), in_specs=[blk, blk, blkT, blkT, blk, vec, vec], out_specs=[blk, vec, vec],
                   out_shape=[SDS((T, LANES), F32), SDS((1, LANES), F32), SDS((1, LANES), F32)],
                   compiler_params=_params("arbitrary"), name=name)(dac_c, ddt_c, dac_r, ddt_r, dtpre, bias, alog)


def _ssd_specs(cfg, T, rev):
    nc = T // CHUNK
    GW = SSD_HPG * HEAD_DIM
    bo = cfg.DI // LANES
    co = (cfg.DI + cfg.G * LANES) // LANES
    ci = (lambda c: nc - 1 - c) if rev else (lambda c: c)
    return dict(
        x=BS((CHUNK, GW), lambda g, c: (ci(c), g)),
        b=BS((CHUNK, LANES), lambda g, c: (ci(c), bo + g)),
        c=BS((CHUNK, LANES), lambda g, c: (ci(c), co + g)),
        col=BS((CHUNK, LANES), lambda g, c: (ci(c), 0)),
        row=BS((LANES, CHUNK), lambda g, c: (0, ci(c))),
        vec=BS((1, LANES), lambda g, c: (0, 0)),
        ng=BS((1, GW), lambda g, c: (0, g)),
        hp=BS((1, 1, SSD_HPG, HEAD_DIM, LANES), lambda g, c: (g, ci(c), 0, 0, 0)),
    )


def ssd_chunk_fwd(xbc, z, dt, ac, dtT, acT, dskip, ng, cfg, *, name):
    T = xbc.shape[0]
    nc = T // CHUNK
    G, DI = cfg.G, cfg.DI
    GW = SSD_HPG * HEAD_DIM
    sp = _ssd_specs(cfg, T, False)

    def body(x_ref, b_ref, c_ref, z_ref, dt_ref, ac_ref, dtT_ref, acT_ref, d_ref, ng_ref,
             y_ref, yn_ref, hp_ref, h_scr):
        g = pl.program_id(0)
        c = pl.program_id(1)

        @pl.when(c == 0)
        def _():
            h_scr[...] = jnp.zeros_like(h_scr)

        row, col = _sb_tri()
        causal = row >= col
        Bm = b_ref[...].astype(CDT)
        Cm = c_ref[...].astype(CDT)
        Gm = _nt(Cm, Bm)
        dtb, acb, dtTb, acTb, dv = dt_ref[...], ac_ref[...], dtT_ref[...], acT_ref[...], d_ref[...]
        heads = range(SSD_HPG)
        hhs = [g * SSD_HPG + r for r in heads]
        ac_cs = [_colsel(acb, hh) for hh in hhs]
        dt_cs = [_colsel(dtb, hh) for hh in hhs]
        Xs = [x_ref[:, r * HEAD_DIM:(r + 1) * HEAD_DIM] for r in heads]
        Hps = [h_scr[r] for r in heads]
        Ws = [(Gm * jnp.where(causal, jnp.exp(ac_c - _rowsel(acTb, hh)), 0.0) * _rowsel(dtTb, hh)).astype(CDT)
              for ac_c, hh in zip(ac_cs, hhs)]
        Qs = [_nt(Cm, Hp.astype(CDT)) for Hp in Hps]
        Yds = [_nn(Wm, Xr.astype(CDT)) for Wm, Xr in zip(Ws, Xs)]
        aLs = [_rowsel(ac_c, CHUNK - 1) for ac_c in ac_cs]
        Xes = [(Xr * (jnp.exp(aL - ac_c) * dt_c)).astype(CDT) for Xr, aL, ac_c, dt_c in zip(Xs, aLs, ac_cs, dt_cs)]
        Sts = [_tn(Xe, Bm) for Xe in Xes]
        for r in heads:
            hp_ref[0, 0, r] = Hps[r]
            h_scr[r] = jnp.exp(aLs[r]) * Hps[r] + Sts[r]
            y_ref[:, r * HEAD_DIM:(r + 1) * HEAD_DIM] = (Yds[r] + jnp.exp(ac_cs[r]) * Qs[r]
                                                         + _colsel(dv, hhs[r]) * Xs[r])
        zz = z_ref[...]
        yz = y_ref[...] * (zz * _sigmoid(zz))
        rstd = lax.rsqrt(jnp.mean(yz * yz, axis=-1, keepdims=True) + RMS_EPS)
        yn_ref[...] = (yz * rstd * ng_ref[...]).astype(yn_ref.dtype)

    return _pallas(body, grid=(G, nc),
                   in_specs=[sp["x"], sp["b"], sp["c"], sp["x"], sp["col"], sp["col"], sp["row"], sp["row"], sp["vec"], sp["ng"]],
                   out_specs=[sp["x"], sp["x"], sp["hp"]],
                   out_shape=[SDS((T, DI), F32), SDS((T, DI), CDT), SDS((G, nc, SSD_HPG, HEAD_DIM, LANES), F32)],
                   scratch_shapes=[pltpu.VMEM((SSD_HPG, HEAD_DIM, LANES), F32)],
                   compiler_params=_params("parallel", "arbitrary"), name=name)(xbc, xbc, xbc, z, dt, ac, dtT, acT, dskip, ng)


def ssd_chunk_bwd(dyn, xbc, z, y, hp, dt, ac, dtT, acT, dskip, ng, cfg, *, name):
    T = xbc.shape[0]
    nc = T // CHUNK
    G, DI = cfg.G, cfg.DI
    GW = SSD_HPG * HEAD_DIM
    sp = _ssd_specs(cfg, T, True)

    def body(dyn_ref, x_ref, b_ref, c_ref, z_ref, y_ref, hp_ref, dt_ref, ac_ref, dtT_ref, acT_ref, d_ref, ng_ref,
             dx_ref, db_ref, dc_ref, dz_ref, colp_ref, rowp_ref, dd_ref, dng_ref, dh_scr):
        g = pl.program_id(0)
        c = pl.program_id(1)

        @pl.when(c == 0)
        def _():
            dh_scr[...] = jnp.zeros_like(dh_scr)
            dd_ref[...] = jnp.zeros_like(dd_ref)
            dng_ref[...] = jnp.zeros_like(dng_ref)

        zz = z_ref[...]
        sg = _sigmoid(zz)
        gate = zz * sg
        yv = y_ref[...]
        yz = yv * gate
        rstd = lax.rsqrt(jnp.mean(yz * yz, axis=-1, keepdims=True) + RMS_EPS)
        nhat = yz * rstd
        dynv = dyn_ref[...].astype(F32)
        dng_ref[...] += jnp.sum(dynv * nhat, axis=0, keepdims=True)
        dnh = dynv * ng_ref[...]
        dyz = rstd * (dnh - nhat * jnp.mean(dnh * nhat, axis=-1, keepdims=True))
        dz_ref[...] = (dyz * yv * (sg * (1.0 + zz * (1.0 - sg)))).astype(dz_ref.dtype)
        dy_all = dyz * gate

        row, col = _sb_tri()
        causal = row >= col
        lane1 = lax.broadcasted_iota(jnp.int32, (1, LANES), 1)
        sub8 = lax.broadcasted_iota(jnp.int32, (HALO, CHUNK), 0)
        subc = lax.broadcasted_iota(jnp.int32, (CHUNK, 1), 0)
        Bf = b_ref[...]
        Cf = c_ref[...]
        Bm = Bf.astype(CDT)
        Cm = Cf.astype(CDT)
        Gm = _nt(Cm, Bm)
        dtb, acb, dtTb, acTb, dv = dt_ref[...], ac_ref[...], dtT_ref[...], acT_ref[...], d_ref[...]
        dG = jnp.zeros((CHUNK, CHUNK), F32)
        dBm = jnp.zeros((CHUNK, LANES), F32)
        dCm = jnp.zeros((CHUNK, LANES), F32)
        colp = jnp.zeros((CHUNK, LANES), F32)
        rowp = jnp.zeros((HALO, CHUNK), F32)
        ddv = jnp.zeros((1, LANES), F32)
        heads = range(SSD_HPG)
        hhs = [g * SSD_HPG + r for r in heads]
        ac_cs = [_colsel(acb, hh) for hh in hhs]
        dt_cs = [_colsel(dtb, hh) for hh in hhs]
        dt_rs = [_rowsel(dtTb, hh) for hh in hhs]
        Ss = [jnp.where(causal, jnp.exp(ac_c - _rowsel(acTb, hh)), 0.0) for ac_c, hh in zip(ac_cs, hhs)]
        GSs = [Gm * S for S in Ss]
        Ws = [GS * dt_r for GS, dt_r in zip(GSs, dt_rs)]
        Xs = [x_ref[:, r * HEAD_DIM:(r + 1) * HEAD_DIM] for r in heads]
        Xcs = [X.astype(CDT) for X in Xs]
        Hps = [hp_ref[0, 0, r] for r in heads]
        Hcs = [Hp.astype(CDT) for Hp in Hps]
        dYs = [dy_all[:, r * HEAD_DIM:(r + 1) * HEAD_DIM] for r in heads]
        dYcs = [dY.astype(CDT) for dY in dYs]
        Es = [jnp.exp(ac_c) for ac_c in ac_cs]
        dQs = [(dY * E).astype(CDT) for dY, E in zip(dYs, Es)]
        dHns = [dh_scr[r] for r in heads]
        dHcs = [dHn.astype(CDT) for dHn in dHns]
        aLs = [_rowsel(ac_c, CHUNK - 1) for ac_c in ac_cs]
        exs = [jnp.exp(aL - ac_c) for aL, ac_c in zip(aLs, ac_cs)]
        es = [ex * dt_c for ex, dt_c in zip(exs, dt_cs)]
        Xes = [(X * e).astype(CDT) for X, e in zip(Xs, es)]
        WtdYs = [_tn(W.astype(CDT), dYc) for W, dYc in zip(Ws, dYcs)]
        dWms = [_nt(dYc, Xc) for dYc, Xc in zip(dYcs, Xcs)]
        Qs = [_nt(Cm, Hc) for Hc in Hcs]
        dCps = [_nn(dQ, Hc) for dQ, Hc in zip(dQs, Hcs)]
        dHps = [_tn(dQ, Cm) for dQ in dQs]
        T2s = [_nt(Bm, dHc) for dHc in dHcs]
        dBps = [_nn(Xe, dHc) for Xe, dHc in zip(Xes, dHcs)]
        for r in heads:
            dY, Xr, Wm, dWm, E, ex, e, aL = dYs[r], Xs[r], Ws[r], dWms[r], Es[r], exs[r], es[r], aLs[r]
            ddv = ddv + jnp.where(lane1 == r, jnp.sum(jnp.sum(dY * Xr, axis=1, keepdims=True), axis=0, keepdims=True), 0.0)
            dG = dG + dWm * Ss[r] * dt_rs[r]
            Mm = dWm * Wm
            dac_r = -jnp.sum(Mm, axis=0, keepdims=True)
            ddt_r = jnp.sum(dWm * GSs[r], axis=0, keepdims=True)
            dCm = dCm + dCps[r]
            dBm = dBm + dBps[r]
            eaL = jnp.exp(aL)
            daL = eaL * jnp.sum(jnp.sum(dHns[r] * Hps[r], axis=1, keepdims=True), axis=0, keepdims=True)
            de = jnp.sum(Xr * T2s[r], axis=1, keepdims=True)
            dee = de * e
            daL = daL + jnp.sum(dee, axis=0, keepdims=True)
            dac_c = (jnp.sum(Mm, axis=1, keepdims=True) + jnp.sum(dY * Qs[r], axis=1, keepdims=True) * E
                     - dee + jnp.where(subc == CHUNK - 1, daL, 0.0))
            dh_scr[r] = dHps[r] + eaL * dHns[r]
            dx_ref[:, r * HEAD_DIM:(r + 1) * HEAD_DIM] = dY * _colsel(dv, hhs[r]) + WtdYs[r] + T2s[r] * e
            colp = jnp.where(col == r, dac_c, colp)
            colp = jnp.where(col == SSD_HPG + r, de * ex, colp)
            rowp = jnp.where(sub8 == r, dac_r, rowp)
            rowp = jnp.where(sub8 == SSD_HPG + r, ddt_r, rowp)
        dGc = dG.astype(CDT)
        db_ref[...] = dBm + _tn(dGc, Cm)
        dc_ref[...] = dCm + _nn(dGc, Bm)
        colp_ref[0] = colp
        rowp_ref[0] = rowp
        dd_ref[0] += ddv

    f32 = lambda shape: SDS(shape, F32)
    return _pallas(body, grid=(G, nc),
                   in_specs=[sp["x"], sp["x"], sp["b"], sp["c"], sp["x"], sp["x"], sp["hp"], sp["col"], sp["col"],
                             sp["row"], sp["row"], sp["vec"], sp["ng"]],
                   out_specs=[sp["x"],
                              BS((CHUNK, LANES), lambda g, c: (nc - 1 - c, g)),
                              BS((CHUNK, LANES), lambda g, c: (nc - 1 - c, g)),
                              sp["x"],
                              BS((1, CHUNK, LANES), lambda g, c: (g, nc - 1 - c, 0)),
                              BS((1, HALO, CHUNK), lambda g, c: (g, 0, nc - 1 - c)),
                              BS((1, 1, LANES), lambda g, c: (g, 0, 0)),
                              sp["ng"]],
                   out_shape=[f32((T, DI)), f32((T, G * LANES)), f32((T, G * LANES)), SDS((T, DI), CDT),
                              f32((G, T, LANES)), f32((G, HALO, T)), f32((G, 1, LANES)), f32((1, DI))],
                   scratch_shapes=[pltpu.VMEM((SSD_HPG, HEAD_DIM, LANES), F32)],
                   compiler_params=_params("parallel", "arbitrary"), name=name)(
                       dyn, xbc, xbc, xbc, z, y, hp, dt, ac, dtT, acT, dskip, ng)


HBM = pl.BlockSpec(memory_space=pltpu.HBM)
VMEM = pl.BlockSpec(memory_space=pltpu.VMEM)


def _me():
    return lax.axis_index("x"), lax.axis_index("y"), lax.axis_index("c")


def _other_chips(x, y):
    return [(1 - x, y), (x, 1 - y), (1 - x, 1 - y)]


def _rcopy(src, dst, send_sem, recv_sem, to):
    return pltpu.make_async_remote_copy(src_ref=src, dst_ref=dst, send_sem=send_sem, recv_sem=recv_sem,
                                        device_id=to, device_id_type=MESH)


AG_SEMS = 13


def allgather_weights(packs, *, name):
    n = len(packs)

    def body(*refs):
        srcs, outs, (send_sems, recv_sems) = refs[:n], refs[n:2 * n], refs[2 * n:]
        x, y, c = _me()
        k = 2 * x + y
        sibling = (x, y, 1 - c)
        nx, ny = (1 - x, y, c), (x, 1 - y, c)
        kx, ky, kd = 2 * (1 - x) + y, 2 * x + (1 - y), 2 * (1 - x) + (1 - y)
        started = []

        def go(cp):
            cp.start()
            started.append(cp)

        def piece(ref, chip, core, r, Rq):
            return ref.at[chip, core, pl.ds(r * Rq, Rq)]

        for a, (src, out) in enumerate(zip(srcs, outs)):
            Rq = src.shape[1] // 2
            sem = lambda i: (send_sems.at[AG_SEMS * a + i], recv_sems.at[AG_SEMS * a + i])
            for i, (to, r) in enumerate([(nx, 0), (nx, 1), (ny, 0), (ny, 1)]):
                go(_rcopy(src.at[c, pl.ds(r * Rq, Rq)], piece(out, k, c, r, Rq), *sem(i), to))
            go(_rcopy(src, out.at[k], *sem(12), sibling))
        for a, out in enumerate(outs):
            Rq = out.shape[2] // 2
            sem = lambda i: (send_sems.at[AG_SEMS * a + i], recv_sems.at[AG_SEMS * a + i])

            def landed(i, chip, r, frm, relay=None):
                ref = piece(out, chip, c, r, Rq)
                _rcopy(ref, ref, *sem(i), frm).wait_recv()
                if relay is not None:
                    go(_rcopy(ref, ref, *sem(relay[0]), relay[1]))
                go(_rcopy(ref, ref, *sem(6 + i), sibling))

            landed(0, kx, 0, nx, relay=(4, ny))
            landed(3, ky, 1, ny, relay=(5, nx))
            landed(1, kx, 1, nx)
            landed(2, ky, 0, ny)
            landed(4, kd, 0, ny)
            landed(5, kd, 1, nx)
        for a, out in enumerate(outs):
            Rq = out.shape[2] // 2
            sem = lambda i: (send_sems.at[AG_SEMS * a + i], recv_sems.at[AG_SEMS * a + i])
            for i, (chip, r) in enumerate([(kx, 0), (kx, 1), (ky, 0), (ky, 1), (kd, 0), (kd, 1)]):
                ref = piece(out, chip, 1 - c, r, Rq)
                _rcopy(ref, ref, *sem(6 + i), sibling).wait_recv()
            mine = out.at[k]
            _rcopy(mine, mine, *sem(12), sibling).wait_recv()
        for cp in started:
            cp.wait_send()

    return _pallas(body, in_specs=[HBM] * n, out_specs=[HBM] * n,
                   out_shape=[SDS((N_CHIPS,) + p.shape, p.dtype) for p in packs],
                   scratch_shapes=[pltpu.SemaphoreType.DMA((AG_SEMS * n,)), pltpu.SemaphoreType.DMA((AG_SEMS * n,))],
                   name=name)(*packs)


def sibling_swap_half(ps, *, name):
    n = len(ps)

    def body(*refs):
        srcs, outs, (send_sems, recv_sems) = refs[:n], refs[n:2 * n], refs[2 * n:]
        x, y, c = _me()
        cps = [_rcopy(src.at[1 - c], out, send_sems.at[a], recv_sems.at[a], (x, y, 1 - c))
               for a, (src, out) in enumerate(zip(srcs, outs))]
        for cp in cps:
            cp.start()
        for cp in cps:
            cp.wait()

    return _pallas(body, in_specs=[HBM] * n, out_specs=[HBM] * n,
                   out_shape=[SDS(p.shape[1:], p.dtype) for p in ps],
                   scratch_shapes=[pltpu.SemaphoreType.DMA((n,)), pltpu.SemaphoreType.DMA((n,))], name=name)(*ps)


def chip_scatter(s1s, *, name):
    n = len(s1s)

    def body(*refs):
        srcs, outs, (send_sems, recv_sems) = refs[:n], refs[n:2 * n], refs[2 * n:]
        x, y, c = _me()
        k = 2 * x + y
        chips = _other_chips(x, y)
        sends = [_rcopy(src.at[2 * cx + cy], out.at[k], send_sems.at[3 * a + j], recv_sems.at[3 * a + j], (cx, cy, c))
                 for a, (src, out) in enumerate(zip(srcs, outs)) for j, (cx, cy) in enumerate(chips)]
        for cp in sends:
            cp.start()
        for a, out in enumerate(outs):
            for j, (cx, cy) in enumerate(chips):
                slot = out.at[2 * cx + cy]
                _rcopy(slot, slot, send_sems.at[3 * a + j], recv_sems.at[3 * a + j], (cx, cy, c)).wait_recv()
        for cp in sends:
            cp.wait_send()

    return _pallas(body, in_specs=[HBM] * n, out_specs=[HBM] * n, out_shape=[SDS(p.shape, p.dtype) for p in s1s],
                   scratch_shapes=[pltpu.SemaphoreType.DMA((3 * n,)), pltpu.SemaphoreType.DMA((3 * n,))],
                   name=name)(*s1s)


def sibling_pair(r2s, *, name):
    n = len(r2s)

    def body(*refs):
        srcs, outs, (send_sems, recv_sems) = refs[:n], refs[n:2 * n], refs[2 * n:]
        x, y, c = _me()
        cps = [_rcopy(src.at[c], out.at[c], send_sems.at[a], recv_sems.at[a], (x, y, 1 - c))
               for a, (src, out) in enumerate(zip(srcs, outs))]
        for cp in cps:
            cp.start()
        for a, out in enumerate(outs):
            theirs = out.at[1 - c]
            _rcopy(theirs, theirs, send_sems.at[a], recv_sems.at[a], (x, y, 1 - c)).wait_recv()
        for cp in cps:
            cp.wait_send()

    return _pallas(body, in_specs=[HBM] * n, out_specs=[HBM] * n, out_shape=[SDS(p.shape, p.dtype) for p in r2s],
                   input_output_aliases={a: a for a in range(n)},
                   scratch_shapes=[pltpu.SemaphoreType.DMA((n,)), pltpu.SemaphoreType.DMA((n,))], name=name)(*r2s)


def allgather_small(v, *, name):
    R, W = v.shape

    def body(v_ref, out_ref, sum_ref, send_sems, recv_sems):
        x, y, c = _me()
        me = 4 * x + 2 * y + c
        out_ref[me] = v_ref[...]
        peers = []
        for m in range(1, N_DEV):
            px = 1 - x if m & 4 else x
            py = 1 - y if m & 2 else y
            pc = 1 - c if m & 1 else c
            peers.append((px, py, pc))
        sends = [_rcopy(v_ref, out_ref.at[me], send_sems.at[j], recv_sems.at[j], p) for j, p in enumerate(peers)]
        for cp in sends:
            cp.start()
        for j, (px, py, pc) in enumerate(peers):
            slot = out_ref.at[4 * px + 2 * py + pc]
            _rcopy(slot, slot, send_sems.at[j], recv_sems.at[j], (px, py, pc)).wait_recv()
        for cp in sends:
            cp.wait_send()
        acc = out_ref[0]
        for d in range(1, N_DEV):
            acc = acc + out_ref[d]
        sum_ref[...] = acc

    return _pallas(body, in_specs=[VMEM], out_specs=[VMEM, VMEM],
                   out_shape=[SDS((N_DEV, R, W), v.dtype), SDS((R, W), v.dtype)],
                   scratch_shapes=[pltpu.SemaphoreType.DMA((N_DEV - 1,)), pltpu.SemaphoreType.DMA((N_DEV - 1,))],
                   name=name)(v)


def _row_tile(R, cap=512):
    best = None
    for t in range(32, cap + 1, 32):
        if R % t == 0:
            best = t
    assert best is not None, R
    return best


def add_halves(p, recv, c_idx, *, name):
    _, n, Rh, W = p.shape
    tr = _row_tile(Rh)

    def body(c_ref, p_ref, r_ref, o_ref):
        o_ref[...] = (p_ref[0].astype(F32) + r_ref[...].astype(F32)).astype(o_ref.dtype)

    gs = pltpu.PrefetchScalarGridSpec(
        num_scalar_prefetch=1, grid=(n, Rh // tr),
        in_specs=[BS((1, 1, tr, W), lambda s, i, c_ref: (c_ref[0], s, i, 0)), BS((1, tr, W), lambda s, i, c_ref: (s, i, 0))],
        out_specs=BS((1, tr, W), lambda s, i, c_ref: (s, i, 0)))
    return _pallas(body, grid_spec=gs, out_shape=SDS((n, Rh, W), p.dtype),
                   compiler_params=_params("parallel", "parallel"), name=name)(c_idx, p, recv)


def sum_slots(s1, got, kc_idx, *, name):
    n, Rh, W = got.shape
    tr = _row_tile(Rh)

    def body(kc_ref, s_ref, g1, g2, g3, o_ref):
        o_ref[0] = s_ref[0].astype(F32) + g1[0].astype(F32) + g2[0].astype(F32) + g3[0].astype(F32)

    slot = lambda d: BS((1, tr, W), lambda i, kc: ((kc[0] + d) % n, i, 0))
    gs = pltpu.PrefetchScalarGridSpec(
        num_scalar_prefetch=1, grid=(Rh // tr,), in_specs=[slot(0), slot(1), slot(2), slot(3)],
        out_specs=BS((1, tr, W), lambda i, kc: (kc[1], i, 0)))
    return _pallas(body, grid_spec=gs, out_shape=SDS((2, Rh, W), F32),
                   compiler_params=_params("parallel"), name=name)(kc_idx, s1, got, got, got)


BIG = [("ffn1_w_gu", True), ("ffn1_w_down", False), ("ffn2_w_gu", True), ("ffn2_w_down", False),
       ("sb_w_qkv", True), ("sb_w_o", False), ("ssd_w_in", True), ("ssd_w_out", False),
       ("sc_w_in", True), ("sc_w_out", False)]
PACK_ALIGN = 64


def _big_layout(shards):
    groups = {}
    for name, cols in BIG:
        L, K, n = shards[name].shape
        items, off = groups.setdefault(n, ([], 0))
        items.append((name, cols, (L, K, n), off, L * K))
        groups[n] = (items, off + L * K)
    return {w: (items, -(-rows // PACK_ALIGN) * PACK_ALIGN) for w, (items, rows) in groups.items()}


def _pack_rows(pieces, total, axis):
    used = sum(p.shape[axis] for p in pieces)
    if total > used:
        pad_shape = list(pieces[0].shape)
        pad_shape[axis] = total - used
        pieces = pieces + [jnp.zeros(pad_shape, pieces[0].dtype)]
    return jnp.concatenate(pieces, axis=axis) if len(pieces) > 1 else pieces[0]


def _gather_big(shards, tag):
    layout = _big_layout(shards)
    packs = [_pack_rows([shards[n].astype(CDT).reshape(-1, w) for n, *_ in items], total, 0).reshape(2, total // 2, w)
             for w, (items, total) in layout.items()]
    fulls = allgather_weights(packs, name=f"{tag}_allgather")
    out = {}
    for (w, (items, total)), full in zip(layout.items(), fulls):
        full = full.reshape(N_CHIPS, total, w)
        for name, cols, (L, K, n), off, rows in items:
            layers = [full[:, off + l * K:off + (l + 1) * K] for l in range(L)]
            if cols:
                out[name] = [p.transpose(1, 0, 2).reshape(K, N_CHIPS * n) for p in layers]
            else:
                out[name] = [p.reshape(N_CHIPS * K, n) for p in layers]
    return out


def _reduce_big(grads, shards, tag):
    layout = _big_layout(shards)
    ps = []
    for w, (items, total) in layout.items():
        pieces = []
        for name, cols, (L, K, n), off, rows in items:
            for g in grads[name]:
                if cols:
                    g = g.reshape(K, N_CHIPS, n).transpose(1, 0, 2)
                else:
                    g = g.reshape(N_CHIPS, K, n)
                pieces.append(g.astype(CDT))
        pack = _pack_rows(pieces, total, 1)
        ps.append(pack.reshape(N_CHIPS, 2, total // 2, w).transpose(1, 0, 2, 3))
    c_core = lax.axis_index("c").astype(jnp.int32)
    k_chip = (2 * lax.axis_index("x") + lax.axis_index("y")).astype(jnp.int32)
    recvs = sibling_swap_half(ps, name=f"{tag}_sibling_swap")
    s1s = [add_halves(p, r, c_core.reshape(1), name=f"{tag}_add_halves_{p.shape[-1]}") for p, r in zip(ps, recvs)]
    gots = chip_scatter(s1s, name=f"{tag}_chip_scatter")
    r2s = [sum_slots(s1, got, jnp.stack([k_chip, c_core]), name=f"{tag}_sum_slots_{s1.shape[-1]}")
           for s1, got in zip(s1s, gots)]
    reds = sibling_pair(r2s, name=f"{tag}_sibling_pair")
    out = {}
    for (w, (items, total)), red in zip(layout.items(), reds):
        red = red.reshape(total, w)
        for name, cols, shape, off, rows in items:
            out[name] = red[off:off + rows].reshape(shape)
    return out


def _pad_lanes(v, width=LANES):
    return jnp.pad(v, ((0, 0), (0, width - v.shape[1])))


def _small_pack(vecs):
    flat, meta, off = [], [], 0
    for v in vecs:
        flat.append(v.reshape(-1))
        meta.append((off, v.size, v.shape))
        off += v.size
    total = -(-off // (8 * LANES)) * 8 * LANES
    flat.append(jnp.zeros((total - off,), F32))
    return jnp.concatenate(flat).reshape(-1, LANES), meta


def _small_unpack(pack, meta):
    flat = pack.reshape(-1)
    return [flat[off:off + size].reshape(shape) for off, size, shape in meta]


def _ffn_fwd(x, g, wgu, wd, tag):
    h = rms_fwd(x, g, name=f"{tag}_norm")
    gate, up, a = ffn_up(h, wgu, name=f"{tag}_gu")
    return matmul(a, wd, res=x, scale=0.5, name=f"{tag}_down"), (x, h, gate, up, a)


def _ffn_bwd(dout, saved, g, wgu, wd, tag):
    x, h, gate, up, a = saved
    F = wd.shape[0]
    dwd = matmul(a, dout, ta=True, out_dtype=CDT, scale=0.5, name=f"{tag}_dwd")
    dgate, dup = ffn_dact(dout, wd, gate, up, 0.5, name=f"{tag}_dact")
    dwgu = jnp.concatenate([matmul(h, dgate, ta=True, out_dtype=CDT, name=f"{tag}_dwg"),
                            matmul(h, dup, ta=True, out_dtype=CDT, name=f"{tag}_dwu")], axis=1)
    dh = matmul(dgate, wgu, tb=True, name=f"{tag}_dh_g")
    dh = matmul(dup, wgu, tb=True, b_k_start=F, res=dh, name=f"{tag}_dh_u")
    dx, dg = rms_bwd(dh, x, g, dout, name=f"{tag}_dnorm")
    return dx, dg, dwgu, dwd


def _sb_mixer_fwd(x, g, wqkv, wo, cfg, tag):
    h = rms_fwd(x, g, name=f"{tag}_norm")
    qkv = matmul(h, wqkv, out_dtype=CDT, name=f"{tag}_qkv")
    o, tab = sb_fwd(qkv, cfg.sb_heads, name=f"{tag}_attn")
    return matmul(o, wo, res=x, name=f"{tag}_out"), (x, h, qkv, tab, o)


def _sb_mixer_bwd(dout, saved, g, wqkv, wo, cfg, tag):
    x, h, qkv, tab, o = saved
    dwo = matmul(o, dout, ta=True, out_dtype=CDT, name=f"{tag}_dwo")
    do = matmul(dout, wo, tb=True, out_dtype=CDT, name=f"{tag}_do")
    dqkv = jnp.concatenate(sb_bwd(qkv, do, tab, cfg.sb_heads, name=f"{tag}_dattn"), axis=1)
    dwqkv = matmul(h, dqkv, ta=True, out_dtype=CDT, name=f"{tag}_dwqkv")
    dh = matmul(dqkv, wqkv, tb=True, name=f"{tag}_dh")
    dx, dg = rms_bwd(dh, x, g, dout, name=f"{tag}_dnorm")
    return dx, dg, dwqkv, dwo


def _sc_mixer_fwd(x, g, win, convw, wout, tag):
    h = rms_fwd(x, g, name=f"{tag}_norm")
    proj = matmul(h, win, name=f"{tag}_in")
    m = sc_fwd(proj, convw, name=f"{tag}_conv")
    return matmul(m, wout, res=x, name=f"{tag}_out"), (x, h, proj, m)


def _sc_mixer_bwd(dout, saved, g, win, convw, wout, tag):
    x, h, proj, m = saved
    dwout = matmul(m, dout, ta=True, out_dtype=CDT, name=f"{tag}_dwout")
    dm = matmul(dout, wout, tb=True, name=f"{tag}_dm")
    dproj, dconvw = sc_bwd(dm, proj, convw, name=f"{tag}_dconv")
    dwin = matmul(h, dproj, ta=True, out_dtype=CDT, name=f"{tag}_dwin")
    dh = matmul(dproj, win, tb=True, name=f"{tag}_dh")
    dx, dg = rms_bwd(dh, x, g, dout, name=f"{tag}_dnorm")
    return dx, dg, dwin, dconvw[:convw.shape[0]], dwout


def _ssd_split_w(win, cfg):
    DI, CV = cfg.DI, cfg.conv_dim
    return win[:, :DI], win[:, DI:DI + CV], _pad_lanes(win[:, DI + CV:])


def _ssd_mixer_fwd(x, g, win, convw, convb, dtb, alog, dskip, ng, wout, cfg, tag):
    wz, wx, wdt = _ssd_split_w(win, cfg)
    h = rms_fwd(x, g, name=f"{tag}_norm")
    z = matmul(h, wz, name=f"{tag}_z")
    xpre = matmul(h, wx, name=f"{tag}_xbc")
    dtpre = matmul(h, wdt, name=f"{tag}_dt")
    pre, xbc = ssd_conv_fwd(xpre, convw, convb, name=f"{tag}_conv")
    dt, ac, dtT, acT = ssd_prep_fwd(dtpre, dtb, alog, name=f"{tag}_prep")
    y, yn, hp = ssd_chunk_fwd(xbc, z, dt, ac, dtT, acT, dskip, ng, cfg, name=f"{tag}_scan")
    out = matmul(yn, wout, res=x, name=f"{tag}_out")
    return out, (x, h, z, xpre, dtpre, pre, xbc, dt, ac, dtT, acT, y, yn, hp)


def _ssd_mixer_bwd(dout, saved, g, win, convw, convb, dtb, alog, dskip, ng, wout, cfg, tag):
    x, h, z, xpre, dtpre, pre, xbc, dt, ac, dtT, acT, y, yn, hp = saved
    T = x.shape[0]
    NH, G = cfg.NH, cfg.G
    wz, wx, wdt = _ssd_split_w(win, cfg)
    dwout = matmul(yn, dout, ta=True, out_dtype=CDT, name=f"{tag}_dwout")
    dyn = matmul(dout, wout, tb=True, name=f"{tag}_dyn")
    dxs, db, dc, dz, colp, rowp, dd, dng = ssd_chunk_bwd(dyn, xbc, z, y, hp, dt, ac, dtT, acT, dskip, ng, cfg,
                                                        name=f"{tag}_dscan")
    col = lambda lo: _pad_lanes(colp[:, :, lo:lo + SSD_HPG].transpose(1, 0, 2).reshape(T, NH))
    row = lambda lo: jnp.pad(rowp[:, lo:lo + SSD_HPG, :].reshape(NH, T), ((0, LANES - NH), (0, 0)))
    ddtpre, dalog, ddtb = ssd_prep_bwd(col(0), col(SSD_HPG), row(0), row(SSD_HPG), dtpre, dtb, alog, name=f"{tag}_dprep")
    dxpre, dconvw, dconvb = ssd_conv_bwd(dxs, db, dc, pre, xpre, convw, name=f"{tag}_dconv")
    dwz = matmul(h, dz, ta=True, out_dtype=CDT, name=f"{tag}_dwz")
    dwx = matmul(h, dxpre, ta=True, out_dtype=CDT, name=f"{tag}_dwx")
    dwdt = matmul(h, ddtpre, ta=True, out_dtype=CDT, name=f"{tag}_dwdt")
    dh = matmul(dz, wz, tb=True, name=f"{tag}_dh_z")
    dh = matmul(dxpre, wx, tb=True, res=dh, name=f"{tag}_dh_x")
    dh = matmul(ddtpre, wdt, tb=True, res=dh, name=f"{tag}_dh_dt")
    dx, dg = rms_bwd(dh, x, g, dout, name=f"{tag}_dnorm")
    dwin = jnp.concatenate([dwz, dwx, dwdt[:, :NH]], axis=1)
    small = dict(conv_w=dconvw[:convw.shape[0]], conv_b=dconvb, dt_bias=ddtb[:, :NH], a_log=dalog[:, :NH],
                 d=dd[:, 0, :SSD_HPG].reshape(1, NH), norm=dng)
    return dx, dg, dwin, dwout, small


INPUT_NAMES = ['x', 'ffn1_norm', 'ffn1_w_gu', 'ffn1_w_down', 'mix_norm', 'ffn2_norm', 'ffn2_w_gu', 'ffn2_w_down',
               'sb_w_qkv', 'sb_w_o', 'ssd_w_in', 'ssd_conv_w', 'ssd_conv_b', 'ssd_dt_bias', 'ssd_a_log', 'ssd_d',
               'ssd_norm', 'ssd_w_out', 'sc_w_in', 'sc_conv_w', 'sc_w_out', 'final_norm']
WEIGHT_NAMES = INPUT_NAMES[1:]
SHARDED_SMALL = ("ssd_conv_w", "sc_conv_w")


def train_step(a, cfg):
    x = a["x"][0]
    tgt = a["loss_target"][0]
    big_names = [n for n, _ in BIG]
    k_chip = 2 * lax.axis_index("x") + lax.axis_index("y")

    w = _gather_big({n: a[n] for n in big_names}, "w")
    conv_pack, conv_meta = _small_pack([a[n] for n in SHARDED_SMALL])
    conv_all, _ = allgather_small(conv_pack, name="conv_w_allgather")
    conv_full = {}
    for i, n in enumerate(SHARDED_SMALL):
        per_chip = [_small_unpack(conv_all[2 * kk], conv_meta)[i] for kk in range(N_CHIPS)]
        conv_full[n] = jnp.concatenate(per_chip, axis=-1)

    def vec(name, i):
        return a[name][i:i + 1]

    ssd_small = lambda j: (a["ssd_conv_b"][j:j + 1], _pad_lanes(a["ssd_dt_bias"][j:j + 1]),
                           _pad_lanes(a["ssd_a_log"][j:j + 1]), _pad_lanes(a["ssd_d"][j:j + 1]), a["ssd_norm"][j:j + 1])

    saved = []
    for i in range(cfg.depth):
        kind, j = i % 3, i // 3
        x, s1 = _ffn_fwd(x, vec("ffn1_norm", i), w["ffn1_w_gu"][i], w["ffn1_w_down"][i], f"l{i}_ffn1")
        if kind == 0:
            x, s2 = _sb_mixer_fwd(x, vec("mix_norm", i), w["sb_w_qkv"][j], w["sb_w_o"][j], cfg, f"l{i}_sb")
        elif kind == 1:
            x, s2 = _ssd_mixer_fwd(x, vec("mix_norm", i), w["ssd_w_in"][j], conv_full["ssd_conv_w"][j], *ssd_small(j),
                                   w["ssd_w_out"][j], cfg, f"l{i}_ssd")
        else:
            x, s2 = _sc_mixer_fwd(x, vec("mix_norm", i), w["sc_w_in"][j], conv_full["sc_conv_w"][j], w["sc_w_out"][j],
                                  f"l{i}_sc")
        x, s3 = _ffn_fwd(x, vec("ffn2_norm", i), w["ffn2_w_gu"][i], w["ffn2_w_down"][i], f"l{i}_ffn2")
        saved.append((s1, s2, s3))
    loss_part, dx, d_final = loss_head(x, a["final_norm"].reshape(1, -1), tgt, name="loss_head")

    gbig = {n: [None] * a[n].shape[0] for n in big_names}
    gsmall = {n: [None] * a[n].shape[0] for n in ("ffn1_norm", "mix_norm", "ffn2_norm", "ssd_conv_w", "ssd_conv_b",
                                                   "ssd_dt_bias", "ssd_a_log", "ssd_d", "ssd_norm", "sc_conv_w")}
    for i in reversed(range(cfg.depth)):
        kind, j = i % 3, i // 3
        s1, s2, s3 = saved[i]
        dx, gsmall["ffn2_norm"][i], gbig["ffn2_w_gu"][i], gbig["ffn2_w_down"][i] = _ffn_bwd(
            dx, s3, vec("ffn2_norm", i), w["ffn2_w_gu"][i], w["ffn2_w_down"][i], f"l{i}_ffn2")
        if kind == 0:
            dx, gsmall["mix_norm"][i], gbig["sb_w_qkv"][j], gbig["sb_w_o"][j] = _sb_mixer_bwd(
                dx, s2, vec("mix_norm", i), w["sb_w_qkv"][j], w["sb_w_o"][j], cfg, f"l{i}_sb")
        elif kind == 1:
            dx, gsmall["mix_norm"][i], gbig["ssd_w_in"][j], gbig["ssd_w_out"][j], sm = _ssd_mixer_bwd(
                dx, s2, vec("mix_norm", i), w["ssd_w_in"][j], conv_full["ssd_conv_w"][j], *ssd_small(j),
                w["ssd_w_out"][j], cfg, f"l{i}_ssd")
            for key, val in sm.items():
                gsmall["ssd_" + key][j] = val
        else:
            dx, gsmall["mix_norm"][i], gbig["sc_w_in"][j], gsmall["sc_conv_w"][j], gbig["sc_w_out"][j] = _sc_mixer_bwd(
                dx, s2, vec("mix_norm", i), w["sc_w_in"][j], conv_full["sc_conv_w"][j], w["sc_w_out"][j], f"l{i}_sc")
        dx, gsmall["ffn1_norm"][i], gbig["ffn1_w_gu"][i], gbig["ffn1_w_down"][i] = _ffn_bwd(
            dx, s1, vec("ffn1_norm", i), w["ffn1_w_gu"][i], w["ffn1_w_down"][i], f"l{i}_ffn1")
    grad_x = dx[None]

    red_big = _reduce_big(gbig, {n: a[n] for n in big_names}, "g")
    small_names = list(gsmall) + ["final_norm"]
    small_vals = [jnp.concatenate([p.reshape((1,) + a[n].shape[1:-1] + (-1,)) for p in gsmall[n]], axis=0)
                  for n in gsmall] + [d_final.reshape(-1)]
    gpack, gmeta = _small_pack(small_vals + [loss_part[:, :1]])
    _, gsum = allgather_small(gpack, name="small_allreduce")
    *small_red, loss = _small_unpack(gsum, gmeta)
    red = dict(red_big)
    for n, v in zip(small_names, small_red):
        if n in SHARDED_SMALL:
            width = a[n].shape[-1]
            v = lax.dynamic_slice_in_dim(v, k_chip * width, width, axis=v.ndim - 1)
        red[n] = v.reshape(a[n].shape)

    delta, new_m, new_v = {}, {}, {}
    two_d = lambda t: t.reshape(-1, t.shape[-1])
    for n in big_names:
        d_, m_, v_ = adamw(two_d(a[n]), two_d(red[n]), two_d(a["m_" + n]), two_d(a["v_" + n]), name=f"adamw_{n}")
        delta[n], new_m[n], new_v[n] = (t.reshape(a[n].shape) for t in (d_, m_, v_))
    rest = [n for n in WEIGHT_NAMES if n not in big_names]
    packs = [_small_pack([src[n] for n in rest]) for src in
             (a, red, {n: a["m_" + n] for n in rest}, {n: a["v_" + n] for n in rest})]
    outs = adamw(*[p for p, _ in packs], name="adamw_small")
    for dst, o in zip((delta, new_m, new_v), outs):
        for n, t in zip(rest, _small_unpack(o, packs[0][1])):
            dst[n] = t
    return (loss.reshape(()), grad_x, *[red[n] for n in WEIGHT_NAMES], *[delta[n] for n in WEIGHT_NAMES],
            *[new_m[n] for n in WEIGHT_NAMES], *[new_v[n] for n in WEIGHT_NAMES])


def kernel(x, ffn1_norm, ffn1_w_gu, ffn1_w_down, mix_norm, ffn2_norm, ffn2_w_gu, ffn2_w_down, sb_w_qkv, sb_w_o, ssd_w_in, ssd_conv_w, ssd_conv_b, ssd_dt_bias, ssd_a_log, ssd_d, ssd_norm, ssd_w_out, sc_w_in, sc_conv_w, sc_w_out, final_norm, loss_target, m_ffn1_norm, m_ffn1_w_gu, m_ffn1_w_down, m_mix_norm, m_ffn2_norm, m_ffn2_w_gu, m_ffn2_w_down, m_sb_w_qkv, m_sb_w_o, m_ssd_w_in, m_ssd_conv_w, m_ssd_conv_b, m_ssd_dt_bias, m_ssd_a_log, m_ssd_d, m_ssd_norm, m_ssd_w_out, m_sc_w_in, m_sc_conv_w, m_sc_w_out, m_final_norm, v_ffn1_norm, v_ffn1_w_gu, v_ffn1_w_down, v_mix_norm, v_ffn2_norm, v_ffn2_w_gu, v_ffn2_w_down, v_sb_w_qkv, v_sb_w_o, v_ssd_w_in, v_ssd_conv_w, v_ssd_conv_b, v_ssd_dt_bias, v_ssd_a_log, v_ssd_d, v_ssd_norm, v_ssd_w_out, v_sc_w_in, v_sc_conv_w, v_sc_w_out, v_final_norm):
    args = dict(locals())
    return train_step(args, Cfg())
```

```python
import functools
import math

import jax
import jax.numpy as jnp
from jax import lax
from jax.experimental import pallas as pl
from jax.experimental.pallas import tpu as pltpu

F32 = jnp.float32
CDT = jnp.bfloat16
HI = lax.Precision.HIGHEST
BS = pl.BlockSpec
SDS = jax.ShapeDtypeStruct
MESH = pl.DeviceIdType.MESH

RMS_EPS = 1e-6
ADAM_LR, ADAM_B1, ADAM_B2, ADAM_EPS, ADAM_WD, ADAM_STEP = 0.001, 0.9, 0.999, 1e-08, 0.01, 10
V7X_VMEM_LIMIT_BYTES = 56 * 1024 * 1024
LANES = 128
CHUNK = 128
HEAD_DIM = 64
SSD_HPG = 4
N_CHIPS = 4
N_DEV = 8


class Cfg:
    def __init__(self, d_model=1024, d_ff=2816, depth=4, sb_heads=16, ssd_groups=8, ssd_state=128,
                 ssd_conv=4, sc_width=3):
        self.D, self.F, self.depth = d_model, d_ff, depth
        self.sb_heads = sb_heads
        self.DI = 2 * d_model
        self.G = ssd_groups
        self.NH = self.DI // HEAD_DIM
        assert self.NH == SSD_HPG * self.G and ssd_state == LANES and sb_heads * HEAD_DIM == d_model
        self.conv_dim = self.DI + 2 * self.G * ssd_state
        self.in_dim = self.DI + self.conv_dim + self.NH
        self.ssd_conv, self.sc_width = ssd_conv, sc_width
        self.n_sb, self.n_ssd, self.n_sc = (depth + 2) // 3, (depth + 1) // 3, depth // 3


def _pallas(body, **kw):
    return pl.pallas_call(body, **kw)


def _params(*sem):
    return pltpu.CompilerParams(dimension_semantics=sem, vmem_limit_bytes=V7X_VMEM_LIMIT_BYTES)


def _tile(n, cap):
    if n <= cap:
        return n
    best = None
    for m in range(1, n // LANES + 1):
        t = m * LANES
        if n % t == 0 and t <= cap:
            best = t
    assert best is not None, (n, cap)
    return best


def _sigmoid(x):
    return 1.0 / (1.0 + jnp.exp(-x))


def _softplus_neg_abs(x):
    return jnp.log(1.0 + jnp.exp(-jnp.abs(x)))


def _colsel(blk, idx):
    lane = lax.broadcasted_iota(jnp.int32, blk.shape, 1)
    return jnp.sum(jnp.where(lane == idx, blk, 0.0), axis=1, keepdims=True)


def _rowsel(blk, idx):
    sub = lax.broadcasted_iota(jnp.int32, blk.shape, 0)
    return jnp.sum(jnp.where(sub == idx, blk, 0.0), axis=0, keepdims=True)


def _nt(a, b):
    return lax.dot_general(a, b, (((1,), (1,)), ((), ())), preferred_element_type=F32)


def _tn(a, b):
    return lax.dot_general(a, b, (((0,), (0,)), ((), ())), preferred_element_type=F32)


def _nn(a, b):
    return jnp.dot(a, b, preferred_element_type=F32)


def matmul(a, b, *, name, ta=False, tb=False, out_dtype=F32, res=None, scale=1.0, tm=1408, tn=1408, tk=1408,
           b_k_start=0):
    M, K = (a.shape[1], a.shape[0]) if ta else a.shape
    N, K2 = b.shape if tb else (b.shape[1], b.shape[0])
    assert K == K2 or (tb and b_k_start + K <= K2), (a.shape, b.shape, ta, tb)
    tm, tn, tk = _tile(M, tm), _tile(N, tn), _tile(K, tk)
    nk = K // tk
    assert b_k_start % tk == 0
    kb0 = b_k_start // tk
    dn = (((0 if ta else 1,), (1 if tb else 0,)), ((), ()))

    def body(*refs):
        a_ref, b_ref = refs[:2]
        r_ref = refs[2] if res is not None else None
        o_ref = refs[3 if res is not None else 2]
        part = lax.dot_general(a_ref[...].astype(CDT), b_ref[...].astype(CDT), dn, preferred_element_type=F32)

        def finish(total):
            o = total * scale
            if res is not None:
                o = o + r_ref[...].astype(F32)
            o_ref[...] = o.astype(o_ref.dtype)

        if nk == 1:
            finish(part)
            return
        acc = refs[-1]
        k = pl.program_id(2)

        @pl.when(k == 0)
        def _():
            acc[...] = part

        @pl.when(jnp.logical_and(k > 0, k < nk - 1))
        def _():
            acc[...] += part

        @pl.when(k == nk - 1)
        def _():
            finish(acc[...] + part)

    a_spec = BS((tk, tm), lambda i, j, k: (k, i)) if ta else BS((tm, tk), lambda i, j, k: (i, k))
    b_spec = BS((tn, tk), lambda i, j, k: (j, k + kb0)) if tb else BS((tk, tn), lambda i, j, k: (k, j))
    o_spec = BS((tm, tn), lambda i, j, k: (i, j))
    ins, specs = [a, b], [a_spec, b_spec]
    if res is not None:
        ins.append(res)
        specs.append(o_spec)
    return _pallas(body, grid=(M // tm, N // tn, nk), in_specs=specs, out_specs=o_spec,
                   out_shape=SDS((M, N), out_dtype),
                   scratch_shapes=[pltpu.VMEM((tm, tn), F32)] if nk > 1 else [],
                   compiler_params=_params("parallel", "parallel", "arbitrary"), name=name)(*ins)


def rms_fwd(x, g, *, name):
    T, D = x.shape
    tm = _tile(T, 512)

    def body(x_ref, g_ref, o_ref):
        xv = x_ref[...]
        r = lax.rsqrt(jnp.mean(xv * xv, axis=-1, keepdims=True) + RMS_EPS)
        o_ref[...] = (xv * r * g_ref[...]).astype(o_ref.dtype)

    return _pallas(body, grid=(T // tm,), in_specs=[BS((tm, D), lambda i: (i, 0)), BS((1, D), lambda i: (0, 0))],
                   out_specs=BS((tm, D), lambda i: (i, 0)), out_shape=SDS((T, D), CDT),
                   compiler_params=_params("parallel"), name=name)(x, g)


def rms_bwd(dh, x, g, dres, *, name):
    T, D = x.shape
    tm = _tile(T, 512)

    def body(dh_ref, x_ref, g_ref, r_ref, dx_ref, dg_ref):
        i = pl.program_id(0)
        xv = x_ref[...]
        r = lax.rsqrt(jnp.mean(xv * xv, axis=-1, keepdims=True) + RMS_EPS)
        xhat = xv * r
        dhv = dh_ref[...].astype(F32)
        dxh = dhv * g_ref[...]
        m = jnp.mean(dxh * xhat, axis=-1, keepdims=True)
        dx_ref[...] = r * (dxh - xhat * m) + r_ref[...]

        @pl.when(i == 0)
        def _():
            dg_ref[...] = jnp.zeros_like(dg_ref)

        dg_ref[...] += jnp.sum(dhv * xhat, axis=0, keepdims=True)

    row = BS((tm, D), lambda i: (i, 0))
    vec = BS((1, D), lambda i: (0, 0))
    return _pallas(body, grid=(T // tm,), in_specs=[row, row, vec, row], out_specs=[row, vec],
                   out_shape=[SDS((T, D), F32), SDS((1, D), F32)],
                   compiler_params=_params("arbitrary"), name=name)(dh, x, g, dres)


def ffn_up(h, wgu, *, name):
    T, D = h.shape
    F = wgu.shape[1] // 2
    tm, tn = _tile(T, 512), _tile(F, 1408)
    nj = F // tn

    def body(h_ref, wg_ref, wu_ref, g_ref, u_ref, a_ref):
        hv = h_ref[...].astype(CDT)
        g = jnp.dot(hv, wg_ref[...].astype(CDT), preferred_element_type=F32)
        u = jnp.dot(hv, wu_ref[...].astype(CDT), preferred_element_type=F32)
        g_ref[...] = g.astype(g_ref.dtype)
        u_ref[...] = u.astype(u_ref.dtype)
        a_ref[...] = (g * _sigmoid(g) * u).astype(a_ref.dtype)

    out = BS((tm, tn), lambda j, i: (i, j))
    return _pallas(body, grid=(nj, T // tm),
                   in_specs=[BS((tm, D), lambda j, i: (i, 0)), BS((D, tn), lambda j, i: (0, j)),
                             BS((D, tn), lambda j, i: (0, j + nj))],
                   out_specs=[out, out, out], out_shape=[SDS((T, F), CDT)] * 3,
                   compiler_params=_params("parallel", "parallel"), name=name)(h, wgu, wgu)


def ffn_dact(dout, wd, g, u, scale, *, name):
    T, D = dout.shape
    F = wd.shape[0]
    tm, tn = _tile(T, 512), _tile(F, 1408)

    def body(do_ref, wd_ref, g_ref, u_ref, dg_ref, du_ref):
        da = scale * _nt(do_ref[...].astype(CDT), wd_ref[...].astype(CDT))
        gv = g_ref[...].astype(F32)
        uv = u_ref[...].astype(F32)
        sg = _sigmoid(gv)
        dg_ref[...] = (da * uv * (sg * (1.0 + gv * (1.0 - sg)))).astype(dg_ref.dtype)
        du_ref[...] = (da * (gv * sg)).astype(du_ref.dtype)

    blk = BS((tm, tn), lambda j, i: (i, j))
    return _pallas(body, grid=(F // tn, T // tm),
                   in_specs=[BS((tm, D), lambda j, i: (i, 0)), BS((tn, D), lambda j, i: (j, 0)), blk, blk],
                   out_specs=[blk, blk], out_shape=[SDS((T, F), CDT)] * 2,
                   compiler_params=_params("parallel", "parallel"), name=name)(dout, wd, g, u)


def loss_head(x, g, tgt, *, name):
    T, D = x.shape
    tm = _tile(T, 512)

    def body(x_ref, g_ref, t_ref, l_ref, dx_ref, dg_ref):
        i = pl.program_id(0)
        xv = x_ref[...]
        gv = g_ref[...]
        r = lax.rsqrt(jnp.mean(xv * xv, axis=-1, keepdims=True) + RMS_EPS)
        xhat = xv * r
        err = xhat * gv - t_ref[...]
        part = 0.5 * jnp.sum(jnp.mean(err * err, axis=-1, keepdims=True), axis=0, keepdims=True)
        dy = err * (1.0 / D)
        dxh = dy * gv
        m = jnp.mean(dxh * xhat, axis=-1, keepdims=True)
        dx_ref[...] = r * (dxh - xhat * m)

        @pl.when(i == 0)
        def _():
            dg_ref[...] = jnp.zeros_like(dg_ref)
            l_ref[...] = jnp.zeros_like(l_ref)

        dg_ref[...] += jnp.sum(dy * xhat, axis=0, keepdims=True)
        l_ref[...] += jnp.broadcast_to(part, l_ref.shape)

    row = BS((tm, D), lambda i: (i, 0))
    vec = BS((1, D), lambda i: (0, 0))
    return _pallas(body, grid=(T // tm,), in_specs=[row, vec, row],
                   out_specs=[BS((1, LANES), lambda i: (0, 0)), row, vec],
                   out_shape=[SDS((1, LANES), F32), SDS((T, D), F32), SDS((1, D), F32)],
                   compiler_params=_params("arbitrary"), name=name)(x, g, tgt)


def adamw(w, g, m, v, *, name):
    R, C = w.shape
    tr = R
    if R * C * 4 > (1 << 20):
        for cand in range(8, R + 1, 8):
            if R % cand == 0 and cand * C * 4 <= (1 << 20):
                tr = cand

    def body(w_ref, g_ref, m_ref, v_ref, d_ref, nm_ref, nv_ref):
        gv = g_ref[...]
        nm = ADAM_B1 * m_ref[...] + (1.0 - ADAM_B1) * gv
        nv = ADAM_B2 * v_ref[...] + (1.0 - ADAM_B2) * (gv * gv)
        m_hat = nm / (1.0 - ADAM_B1 ** ADAM_STEP)
        v_hat = nv / (1.0 - ADAM_B2 ** ADAM_STEP)
        d_ref[...] = -ADAM_LR * (m_hat / (jnp.sqrt(v_hat) + ADAM_EPS) + ADAM_WD * w_ref[...])
        nm_ref[...] = nm
        nv_ref[...] = nv

    blk = BS((tr, C), lambda i: (i, 0))
    return _pallas(body, grid=(R // tr,), in_specs=[blk] * 4, out_specs=[blk] * 3,
                   out_shape=[SDS((R, C), F32)] * 3, compiler_params=_params("parallel"), name=name)(w, g, m, v)


def _sb_tri():
    row = lax.broadcasted_iota(jnp.int32, (CHUNK, CHUNK), 0)
    col = lax.broadcasted_iota(jnp.int32, (CHUNK, CHUNK), 1)
    return row, col


def _tri_stack(keep):
    t = keep.astype(jnp.bfloat16)
    return jnp.concatenate([t, t], axis=0)


def _split_dot(x, tri_stack):
    hi = x.astype(jnp.bfloat16)
    mid = (x - hi.astype(F32)).astype(jnp.bfloat16)
    return jnp.dot(jnp.concatenate([hi, mid], axis=1), tri_stack, preferred_element_type=F32)


def _sb_heads_per_step(H, want):
    hb = min(H, want)
    assert H % hb == 0, (H, hb)
    return hb


SB_FWD_HEADS = 8
SB_BWD_HEADS = 8


def _head_pair_masked(blk, lo):
    zero = jnp.zeros_like(blk)
    return jnp.where(lo, blk, zero), jnp.where(lo, zero, blk)


def sb_fwd(qkv, n_heads, *, name):
    T, D3 = qkv.shape
    D = D3 // 3
    nq = T // CHUNK
    assert nq <= LANES
    scale = HEAD_DIM ** -0.5
    hb = _sb_heads_per_step(n_heads, SB_FWD_HEADS)
    assert hb % 2 == 0
    LW = hb * HEAD_DIM
    ngrp = D // LW
    heads, pairs = range(hb), range(hb // 2)
    lanes = lambda p: slice(p * LANES, (p + 1) * LANES)

    def body(q_ref, k_ref, v_ref, o_ref, tab_ref, acc_scr, cs_scr, qm_scr):
        qi = pl.program_id(1)
        row, col = _sb_tri()
        lo = col < HEAD_DIM
        upper_stack = _tri_stack(row > col)
        acc_scr[...] = jnp.zeros_like(acc_scr)
        cs_scr[...] = jnp.zeros_like(cs_scr)
        tab_ref[...] = jnp.zeros_like(tab_ref)
        for p in pairs:
            qm_scr[2 * p], qm_scr[2 * p + 1] = _head_pair_masked(q_ref[:, lanes(p)] * scale, lo)

        def tile(j, diag):
            sl = pl.ds(pl.multiple_of(j * CHUNK, CHUNK), CHUNK)
            keep = (lambda t: jnp.where(col < row, t, 0.0)) if diag else (lambda t: t)
            kbs = [k_ref[sl, lanes(p)] for p in pairs]
            zs = [_nt(qm_scr[h], kbs[h // 2]) for h in heads]
            lszs = [jnp.minimum(z, 0.0) - _softplus_neg_abs(z) for z in zs]
            lks = [keep(lsz - z) for lsz, z in zip(lszs, zs)]
            tls = [_split_dot(lk, upper_stack) for lk in lks]
            css = [cs_scr[h] for h in heads]
            logits = [lsz + tl + cs for lsz, tl, cs in zip(lszs, tls, css)]
            atts = [keep(jnp.exp(lg)).astype(CDT) for lg in logits]
            vms = [jnp.concatenate(_head_pair_masked(v_ref[sl, lanes(p)], lo), axis=0) for p in pairs]
            pvs = [_nn(jnp.concatenate([atts[2 * p], atts[2 * p + 1]], axis=1), vms[p]) for p in pairs]
            for p in pairs:
                acc_scr[:, lanes(p)] += pvs[p]
            for h in heads:
                tab_ref[h, 0] = jnp.where(col == j, css[h], tab_ref[h, 0])
                if diag:
                    cs_scr[h] = css[h] + (tls[h][:, :1] + lks[h][:, :1])
                else:
                    cs_scr[h] = logits[h][:, :1] - zs[h][:, :1]

        tile(qi, True)

        def step(jj, carry):
            tile(qi - jj, False)
            return carry

        lax.fori_loop(1, qi + 1, step, 0)
        o_ref[...] = acc_scr[...].astype(o_ref.dtype)

    once = pl.Buffered(1)
    qs = BS((CHUNK, LW), lambda g, i: (i, g))
    return _pallas(body, grid=(ngrp, nq),
                   in_specs=[qs, BS((T, LW), lambda g, i: (0, ngrp + g), pipeline_mode=once),
                             BS((T, LW), lambda g, i: (0, 2 * ngrp + g), pipeline_mode=once)],
                   out_specs=[qs, BS((hb, 1, CHUNK, CHUNK), lambda g, i: (g, i, 0, 0))],
                   out_shape=[SDS((T, D), CDT), SDS((n_heads, nq, CHUNK, CHUNK), F32)],
                   scratch_shapes=[pltpu.VMEM((CHUNK, LW), F32), pltpu.VMEM((hb, CHUNK, 1), F32),
                                   pltpu.VMEM((hb, CHUNK, LANES), CDT)],
                   compiler_params=_params("parallel", "arbitrary"), name=name)(qkv, qkv, qkv)


def sb_bwd(qkv, do, tab, n_heads, *, name):
    T, D3 = qkv.shape
    D = D3 // 3
    nq = T // CHUNK
    scale = HEAD_DIM ** -0.5
    hb = _sb_heads_per_step(n_heads, SB_BWD_HEADS)
    assert hb % 2 == 0
    LW = hb * HEAD_DIM
    ngrp = D // LW
    heads, pairs = range(hb), range(hb // 2)
    lanes = lambda p: slice(p * LANES, (p + 1) * LANES)

    def body(q_ref, k_ref, v_ref, do_ref, tab_ref, dq_ref, dk_ref, dv_ref,
             dk_acc, dv_acc, dq_scr, pfx_scr, qm_scr, dom_scr):
        qi = pl.program_id(1)

        @pl.when(qi == 0)
        def _():
            dk_acc[...] = jnp.zeros_like(dk_acc)
            dv_acc[...] = jnp.zeros_like(dv_acc)

        dq_scr[...] = jnp.zeros_like(dq_scr)
        pfx_scr[...] = jnp.zeros_like(pfx_scr)
        row, col = _sb_tri()
        lo = col < HEAD_DIM
        upper_stack = _tri_stack(row > col)
        before_stack = _tri_stack(row < col)
        for p in pairs:
            qm_scr[2 * p], qm_scr[2 * p + 1] = _head_pair_masked(q_ref[:, lanes(p)] * scale, lo)
            dom_scr[2 * p], dom_scr[2 * p + 1] = _head_pair_masked(do_ref[:, lanes(p)], lo)

        def tile(j, diag):
            sl = pl.ds(pl.multiple_of(j * CHUNK, CHUNK), CHUNK)
            keep = (lambda t: jnp.where(col < row, t, 0.0)) if diag else (lambda t: t)
            kbs = [k_ref[sl, lanes(p)] for p in pairs]
            vbs = [v_ref[sl, lanes(p)] for p in pairs]
            zs = [_nt(qm_scr[h], kbs[h // 2]) for h in heads]
            das = [_nt(dom_scr[h], vbs[h // 2]) for h in heads]
            lszs = [jnp.minimum(z, 0.0) - _softplus_neg_abs(z) for z in zs]
            lks = [keep(lsz - z) for lsz, z in zip(lszs, zs)]
            tls = [_split_dot(lk, upper_stack) for lk in lks]
            css = [jnp.sum(jnp.where(col == j, tab_ref[h, 0], 0.0), axis=1, keepdims=True) for h in heads]
            atts = [keep(jnp.exp(lsz + tl + cs)) for lsz, tl, cs in zip(lszs, tls, css)]
            dls = [att * da for att, da in zip(atts, das)]
            pres = [_split_dot(dl, before_stack) for dl in dls]
            pfxs = [pfx_scr[h] for h in heads]
            sigs = [jnp.exp(lsz) for lsz in lszs]
            befores = [pre + pfx for pre, pfx in zip(pres, pfxs)]
            dzs = [(dl * (1.0 - sig) - keep(sig * bf)).astype(CDT) for dl, sig, bf in zip(dls, sigs, befores)]
            attc = [att.astype(CDT) for att in atts]
            kms = [jnp.concatenate(_head_pair_masked(kb, lo), axis=0) for kb in kbs]
            dqs = [_nn(jnp.concatenate([dzs[2 * p], dzs[2 * p + 1]], axis=1), kms[p]) for p in pairs]
            both = lambda xs, p: jnp.concatenate([xs[2 * p], xs[2 * p + 1]], axis=0)
            dks = [_tn(both(dzs, p), jnp.concatenate([qm_scr[2 * p], qm_scr[2 * p + 1]], axis=0)) for p in pairs]
            dvs = [_tn(both(attc, p), jnp.concatenate([dom_scr[2 * p], dom_scr[2 * p + 1]], axis=0)) for p in pairs]
            last = CHUNK - 1
            for p in pairs:
                dq_scr[:, lanes(p)] += dqs[p]
                dk_acc[sl, lanes(p)] += dks[p]
                dv_acc[sl, lanes(p)] += dvs[p]
            for h in heads:
                pfx_scr[h] = befores[h][:, last:] + dls[h][:, last:]

        def step(j, carry):
            tile(j, False)
            return carry

        lax.fori_loop(0, qi, step, 0)
        tile(qi, True)
        dq_ref[...] = (dq_scr[...] * scale).astype(dq_ref.dtype)

        @pl.when(qi == nq - 1)
        def _():
            dk_ref[...] = dk_acc[...].astype(dk_ref.dtype)
            dv_ref[...] = dv_acc[...].astype(dv_ref.dtype)

    once = pl.Buffered(1)
    qs = BS((CHUNK, LW), lambda g, i: (i, g))
    ks = BS((T, LW), lambda g, i: (0, g))
    return _pallas(body, grid=(ngrp, nq),
                   in_specs=[qs, BS((T, LW), lambda g, i: (0, ngrp + g), pipeline_mode=once),
                             BS((T, LW), lambda g, i: (0, 2 * ngrp + g), pipeline_mode=once), qs,
                             BS((hb, 1, CHUNK, CHUNK), lambda g, i: (g, i, 0, 0))],
                   out_specs=[qs, ks, ks],
                   out_shape=[SDS((T, D), CDT)] * 3,
                   scratch_shapes=[pltpu.VMEM((T, LW), F32), pltpu.VMEM((T, LW), F32),
                                   pltpu.VMEM((CHUNK, LW), F32), pltpu.VMEM((hb, CHUNK, 1), F32),
                                   pltpu.VMEM((hb, CHUNK, LANES), CDT), pltpu.VMEM((hb, CHUNK, LANES), CDT)],
                   compiler_params=_params("parallel", "arbitrary"), name=name)(qkv, qkv, qkv, do, tab)


HALO = 8


def _shift_down(cur, prev8, s):
    if s == 0:
        return cur
    tt = cur.shape[0]
    row = lax.broadcasted_iota(jnp.int32, cur.shape, 0)
    fix = jnp.concatenate([pltpu.roll(prev8, s, 0), jnp.zeros((tt - HALO, cur.shape[1]), cur.dtype)], axis=0)
    return jnp.where(row < s, fix, pltpu.roll(cur, s, 0))


def _shift_up(cur, next8, s):
    if s == 0:
        return cur
    tt = cur.shape[0]
    row = lax.broadcasted_iota(jnp.int32, cur.shape, 0)
    fix = jnp.concatenate([jnp.zeros((tt - HALO, cur.shape[1]), cur.dtype), pltpu.roll(next8, HALO - s, 0)], axis=0)
    return jnp.where(row >= tt - s, fix, pltpu.roll(cur, tt - s, 0))


def _halo_specs(tt, width, T):
    per = tt // HALO
    last = T // HALO - 1
    cur = BS((tt, width), lambda i: (i, 0))
    prev = BS((HALO, width), lambda i: (jnp.maximum(i * per - 1, 0), 0))
    nxt = BS((HALO, width), lambda i: (jnp.minimum((i + 1) * per, last), 0))
    return cur, prev, nxt


def sc_fwd(proj, w, *, name):
    T, D3 = proj.shape
    D = D3 // 3
    K = w.shape[0]
    tt = _tile(T, 256)
    cur, prev, _ = _halo_specs(tt, D3, T)

    def body(p_ref, pp_ref, w_ref, o_ref):
        i = pl.program_id(0)
        pc = p_ref[:, D:2 * D] * p_ref[:, 2 * D:]
        pp = jnp.where(i > 0, pp_ref[:, D:2 * D] * pp_ref[:, 2 * D:], 0.0)
        u = jnp.zeros((tt, D), F32)
        for kk in range(K):
            u = u + w_ref[kk:kk + 1, :] * _shift_down(pc, pp, K - 1 - kk)
        o_ref[...] = (p_ref[:, :D] * u).astype(o_ref.dtype)

    return _pallas(body, grid=(T // tt,), in_specs=[cur, prev, BS((K, D), lambda i: (0, 0))],
                   out_specs=BS((tt, D), lambda i: (i, 0)), out_shape=SDS((T, D), CDT),
                   compiler_params=_params("parallel"), name=name)(proj, proj, w)


def sc_bwd(dm, proj, w, *, name):
    T, D3 = proj.shape
    D = D3 // 3
    K = w.shape[0]
    tt = _tile(T, 256)
    cur, prev, nxt = _halo_specs(tt, D3, T)
    dcur, _, dnxt = _halo_specs(tt, D, T)
    nsteps = T // tt

    def body(dm_ref, dmn_ref, p_ref, pp_ref, pn_ref, w_ref, o_ref, dw_ref):
        i = pl.program_id(0)
        b = p_ref[:, :D]
        c = p_ref[:, D:2 * D]
        h = p_ref[:, 2 * D:]
        pc = c * h
        pp = jnp.where(i > 0, pp_ref[:, D:2 * D] * pp_ref[:, 2 * D:], 0.0)
        dmv = dm_ref[...]
        du = dmv * b
        dun = jnp.where(i < nsteps - 1, dmn_ref[...] * pn_ref[:, :D], 0.0)
        u = jnp.zeros((tt, D), F32)
        dp = jnp.zeros((tt, D), F32)
        rows = []
        for kk in range(K):
            sh = _shift_down(pc, pp, K - 1 - kk)
            u = u + w_ref[kk:kk + 1, :] * sh
            dp = dp + w_ref[kk:kk + 1, :] * _shift_up(du, dun, K - 1 - kk)
            rows.append(jnp.sum(du * sh, axis=0, keepdims=True))
        rows.append(jnp.zeros((HALO - K, D), F32))
        o_ref[:, :D] = (dmv * u).astype(o_ref.dtype)
        o_ref[:, D:2 * D] = (dp * h).astype(o_ref.dtype)
        o_ref[:, 2 * D:] = (dp * c).astype(o_ref.dtype)

        @pl.when(i == 0)
        def _():
            dw_ref[...] = jnp.zeros_like(dw_ref)

        dw_ref[...] += jnp.concatenate(rows, axis=0)

    return _pallas(body, grid=(nsteps,), in_specs=[dcur, dnxt, cur, prev, nxt, BS((K, D), lambda i: (0, 0))],
                   out_specs=[BS((tt, D3), lambda i: (i, 0)), BS((HALO, D), lambda i: (0, 0))],
                   out_shape=[SDS((T, D3), CDT), SDS((HALO, D), F32)],
                   compiler_params=_params("arbitrary"), name=name)(dm, dm, proj, proj, proj, w)


def ssd_conv_fwd(xpre, w, b, *, name):
    T, C = xpre.shape
    K = w.shape[0]
    tt = _tile(T, 256)
    cur, prev, _ = _halo_specs(tt, C, T)

    def body(x_ref, xp_ref, w_ref, b_ref, pre_ref, o_ref):
        i = pl.program_id(0)
        xv = x_ref[...]
        xp = jnp.where(i > 0, xp_ref[...], 0.0)
        acc = jnp.broadcast_to(b_ref[...], (tt, C))
        for kk in range(K):
            acc = acc + w_ref[kk:kk + 1, :] * _shift_down(xv, xp, K - 1 - kk)
        pre_ref[...] = acc
        o_ref[...] = acc * _sigmoid(acc)

    row = BS((tt, C), lambda i: (i, 0))
    return _pallas(body, grid=(T // tt,), in_specs=[cur, prev, BS((K, C), lambda i: (0, 0)), BS((1, C), lambda i: (0, 0))],
                   out_specs=[row, row], out_shape=[SDS((T, C), F32)] * 2,
                   compiler_params=_params("parallel"), name=name)(xpre, xpre, w, b)


def ssd_conv_bwd(dxs, db_, dc_, pre, xpre, w, *, name):
    T, C = xpre.shape
    K = w.shape[0]
    tt = _tile(T, 256)
    cur, prev, nxt = _halo_specs(tt, C, T)
    widths = (dxs.shape[1], db_.shape[1], dc_.shape[1])
    gspecs = []
    for wd in widths:
        c_, _, n_ = _halo_specs(tt, wd, T)
        gspecs += [c_, n_]
    nsteps = T // tt

    def dsilu(p):
        s = _sigmoid(p)
        return s * (1.0 + p * (1.0 - s))

    def body(g0, g0n, g1, g1n, g2, g2n, pre_ref, pren_ref, x_ref, xp_ref, w_ref, dx_ref, dw_ref, dbias_ref):
        i = pl.program_id(0)
        gcur = jnp.concatenate([g0[...], g1[...], g2[...]], axis=1)
        gnxt = jnp.concatenate([g0n[...], g1n[...], g2n[...]], axis=1)
        dpre = gcur * dsilu(pre_ref[...])
        dpren = jnp.where(i < nsteps - 1, gnxt * dsilu(pren_ref[...]), 0.0)
        xv = x_ref[...]
        xp = jnp.where(i > 0, xp_ref[...], 0.0)
        dx = jnp.zeros((tt, C), F32)
        rows = []
        for kk in range(K):
            dx = dx + w_ref[kk:kk + 1, :] * _shift_up(dpre, dpren, K - 1 - kk)
            rows.append(jnp.sum(dpre * _shift_down(xv, xp, K - 1 - kk), axis=0, keepdims=True))
        rows.append(jnp.zeros((HALO - K, C), F32))
        dx_ref[...] = dx.astype(dx_ref.dtype)

        @pl.when(i == 0)
        def _():
            dw_ref[...] = jnp.zeros_like(dw_ref)
            dbias_ref[...] = jnp.zeros_like(dbias_ref)

        dw_ref[...] += jnp.concatenate(rows, axis=0)
        dbias_ref[...] += jnp.sum(dpre, axis=0, keepdims=True)

    return _pallas(body, grid=(nsteps,),
                   in_specs=gspecs + [cur, nxt, cur, prev, BS((K, C), lambda i: (0, 0))],
                   out_specs=[BS((tt, C), lambda i: (i, 0)), BS((HALO, C), lambda i: (0, 0)), BS((1, C), lambda i: (0, 0))],
                   out_shape=[SDS((T, C), CDT), SDS((HALO, C), F32), SDS((1, C), F32)],
                   compiler_params=_params("arbitrary"), name=name)(dxs, dxs, db_, db_, dc_, dc_, pre, pre, xpre, xpre, w)


def ssd_prep_fwd(dtpre, bias, alog, *, name):
    T = dtpre.shape[0]
    nc = T // CHUNK

    def body(p_ref, b_ref, a_ref, dt_ref, ac_ref, dtT_ref, acT_ref):
        xv = p_ref[...] + b_ref[...]
        dt = jnp.maximum(xv, 0.0) + _softplus_neg_abs(xv)
        row, col = _sb_tri()
        lower = (col <= row).astype(F32)
        ac = jnp.dot(lower, dt * -jnp.exp(a_ref[...]), precision=HI, preferred_element_type=F32)
        dt_ref[...] = dt
        ac_ref[...] = ac
        dtT_ref[...] = dt.T
        acT_ref[...] = ac.T

    blk = BS((CHUNK, LANES), lambda c: (c, 0))
    blkT = BS((LANES, CHUNK), lambda c: (0, c))
    vec = BS((1, LANES), lambda c: (0, 0))
    return _pallas(body, grid=(nc,), in_specs=[blk, vec, vec], out_specs=[blk, blk, blkT, blkT],
                   out_shape=[SDS((T, LANES), F32)] * 2 + [SDS((LANES, T), F32)] * 2,
                   compiler_params=_params("parallel"), name=name)(dtpre, bias, alog)


def ssd_prep_bwd(dac_c, ddt_c, dac_r, ddt_r, dtpre, bias, alog, *, name):
    T = dtpre.shape[0]
    nc = T // CHUNK

    def body(dac_ref, ddt_ref, dacr_ref, ddtr_ref, p_ref, b_ref, a_ref, o_ref, da_ref, db_ref):
        c = pl.program_id(0)
        xv = p_ref[...] + b_ref[...]
        dt = jnp.maximum(xv, 0.0) + _softplus_neg_abs(xv)
        av = -jnp.exp(a_ref[...])
        row, col = _sb_tri()
        after = (col >= row).astype(F32)
        dda = jnp.dot(after, dac_ref[...] + dacr_ref[...].T, precision=HI, preferred_element_type=F32)
        dpre = (ddt_ref[...] + ddtr_ref[...].T + av * dda) * _sigmoid(xv)
        o_ref[...] = dpre

        @pl.when(c == 0)
        def _():
            da_ref[...] = jnp.zeros_like(da_ref)
            db_ref[...] = jnp.zeros_like(db_ref)

        da_ref[...] += jnp.sum(dt * dda, axis=0, keepdims=True) * av
        db_ref[...] += jnp.sum(dpre, axis=0, keepdims=True)

    blk = BS((CHUNK, LANES), lambda c: (c, 0))
    blkT = BS((LANES, CHUNK), lambda c: (0, c))
    vec = BS((1, LANES), lambda c: (0, 0))
    return _pallas(body, grid=(nc,), in_specs=[blk, blk, blkT, blkT, blk, vec, vec], out_specs=[blk, vec, vec],
                   out_shape=[SDS((T, LANES), F32), SDS((1, LANES), F32), SDS((1, LANES), F32)],
                   compiler_params=_params("arbitrary"), name=name)(dac_c, ddt_c, dac_r, ddt_r, dtpre, bias, alog)


def _ssd_specs(cfg, T, rev):
    nc = T // CHUNK
    GW = SSD_HPG * HEAD_DIM
    bo = cfg.DI // LANES
    co = (cfg.DI + cfg.G * LANES) // LANES
    ci = (lambda c: nc - 1 - c) if rev else (lambda c: c)
    return dict(
        x=BS((CHUNK, GW), lambda g, c: (ci(c), g)),
        b=BS((CHUNK, LANES), lambda g, c: (ci(c), bo + g)),
        c=BS((CHUNK, LANES), lambda g, c: (ci(c), co + g)),
        col=BS((CHUNK, LANES), lambda g, c: (ci(c), 0)),
        row=BS((LANES, CHUNK), lambda g, c: (0, ci(c))),
        vec=BS((1, LANES), lambda g, c: (0, 0)),
        ng=BS((1, GW), lambda g, c: (0, g)),
        hp=BS((1, 1, SSD_HPG, HEAD_DIM, LANES), lambda g, c: (g, ci(c), 0, 0, 0)),
    )


def ssd_chunk_fwd(xbc, z, dt, ac, dtT, acT, dskip, ng, cfg, *, name):
    T = xbc.shape[0]
    nc = T // CHUNK
    G, DI = cfg.G, cfg.DI
    GW = SSD_HPG * HEAD_DIM
    sp = _ssd_specs(cfg, T, False)

    def body(x_ref, b_ref, c_ref, z_ref, dt_ref, ac_ref, dtT_ref, acT_ref, d_ref, ng_ref,
             y_ref, yn_ref, hp_ref, h_scr):
        g = pl.program_id(0)
        c = pl.program_id(1)

        @pl.when(c == 0)
        def _():
            h_scr[...] = jnp.zeros_like(h_scr)

        row, col = _sb_tri()
        causal = row >= col
        Bm = b_ref[...].astype(CDT)
        Cm = c_ref[...].astype(CDT)
        Gm = _nt(Cm, Bm)
        dtb, acb, dtTb, acTb, dv = dt_ref[...], ac_ref[...], dtT_ref[...], acT_ref[...], d_ref[...]
        heads = range(SSD_HPG)
        hhs = [g * SSD_HPG + r for r in heads]
        ac_cs = [_colsel(acb, hh) for hh in hhs]
        dt_cs = [_colsel(dtb, hh) for hh in hhs]
        Xs = [x_ref[:, r * HEAD_DIM:(r + 1) * HEAD_DIM] for r in heads]
        Hps = [h_scr[r] for r in heads]
        Ws = [(Gm * jnp.where(causal, jnp.exp(ac_c - _rowsel(acTb, hh)), 0.0) * _rowsel(dtTb, hh)).astype(CDT)
              for ac_c, hh in zip(ac_cs, hhs)]
        Qs = [_nt(Cm, Hp.astype(CDT)) for Hp in Hps]
        Yds = [_nn(Wm, Xr.astype(CDT)) for Wm, Xr in zip(Ws, Xs)]
        aLs = [_rowsel(ac_c, CHUNK - 1) for ac_c in ac_cs]
        Xes = [(Xr * (jnp.exp(aL - ac_c) * dt_c)).astype(CDT) for Xr, aL, ac_c, dt_c in zip(Xs, aLs, ac_cs, dt_cs)]
        Sts = [_tn(Xe, Bm) for Xe in Xes]
        for r in heads:
            hp_ref[0, 0, r] = Hps[r]
            h_scr[r] = jnp.exp(aLs[r]) * Hps[r] + Sts[r]
            y_ref[:, r * HEAD_DIM:(r + 1) * HEAD_DIM] = (Yds[r] + jnp.exp(ac_cs[r]) * Qs[r]
                                                         + _colsel(dv, hhs[r]) * Xs[r])
        zz = z_ref[...]
        yz = y_ref[...] * (zz * _sigmoid(zz))
        rstd = lax.rsqrt(jnp.mean(yz * yz, axis=-1, keepdims=True) + RMS_EPS)
        yn_ref[...] = (yz * rstd * ng_ref[...]).astype(yn_ref.dtype)

    return _pallas(body, grid=(G, nc),
                   in_specs=[sp["x"], sp["b"], sp["c"], sp["x"], sp["col"], sp["col"], sp["row"], sp["row"], sp["vec"], sp["ng"]],
                   out_specs=[sp["x"], sp["x"], sp["hp"]],
                   out_shape=[SDS((T, DI), F32), SDS((T, DI), CDT), SDS((G, nc, SSD_HPG, HEAD_DIM, LANES), F32)],
                   scratch_shapes=[pltpu.VMEM((SSD_HPG, HEAD_DIM, LANES), F32)],
                   compiler_params=_params("parallel", "arbitrary"), name=name)(xbc, xbc, xbc, z, dt, ac, dtT, acT, dskip, ng)


def ssd_chunk_bwd(dyn, xbc, z, y, hp, dt, ac, dtT, acT, dskip, ng, cfg, *, name):
    T = xbc.shape[0]
    nc = T // CHUNK
    G, DI = cfg.G, cfg.DI
    GW = SSD_HPG * HEAD_DIM
    sp = _ssd_specs(cfg, T, True)

    def body(dyn_ref, x_ref, b_ref, c_ref, z_ref, y_ref, hp_ref, dt_ref, ac_ref, dtT_ref, acT_ref, d_ref, ng_ref,
             dx_ref, db_ref, dc_ref, dz_ref, colp_ref, rowp_ref, dd_ref, dng_ref, dh_scr):
        g = pl.program_id(0)
        c = pl.program_id(1)

        @pl.when(c == 0)
        def _():
            dh_scr[...] = jnp.zeros_like(dh_scr)
            dd_ref[...] = jnp.zeros_like(dd_ref)
            dng_ref[...] = jnp.zeros_like(dng_ref)

        zz = z_ref[...]
        sg = _sigmoid(zz)
        gate = zz * sg
        yv = y_ref[...]
        yz = yv * gate
        rstd = lax.rsqrt(jnp.mean(yz * yz, axis=-1, keepdims=True) + RMS_EPS)
        nhat = yz * rstd
        dynv = dyn_ref[...].astype(F32)
        dng_ref[...] += jnp.sum(dynv * nhat, axis=0, keepdims=True)
        dnh = dynv * ng_ref[...]
        dyz = rstd * (dnh - nhat * jnp.mean(dnh * nhat, axis=-1, keepdims=True))
        dz_ref[...] = (dyz * yv * (sg * (1.0 + zz * (1.0 - sg)))).astype(dz_ref.dtype)
        dy_all = dyz * gate

        row, col = _sb_tri()
        causal = row >= col
        lane1 = lax.broadcasted_iota(jnp.int32, (1, LANES), 1)
        sub8 = lax.broadcasted_iota(jnp.int32, (HALO, CHUNK), 0)
        subc = lax.broadcasted_iota(jnp.int32, (CHUNK, 1), 0)
        Bf = b_ref[...]
        Cf = c_ref[...]
        Bm = Bf.astype(CDT)
        Cm = Cf.astype(CDT)
        Gm = _nt(Cm, Bm)
        dtb, acb, dtTb, acTb, dv = dt_ref[...], ac_ref[...], dtT_ref[...], acT_ref[...], d_ref[...]
        dG = jnp.zeros((CHUNK, CHUNK), F32)
        dBm = jnp.zeros((CHUNK, LANES), F32)
        dCm = jnp.zeros((CHUNK, LANES), F32)
        colp = jnp.zeros((CHUNK, LANES), F32)
        rowp = jnp.zeros((HALO, CHUNK), F32)
        ddv = jnp.zeros((1, LANES), F32)
        heads = range(SSD_HPG)
        hhs = [g * SSD_HPG + r for r in heads]
        ac_cs = [_colsel(acb, hh) for hh in hhs]
        dt_cs = [_colsel(dtb, hh) for hh in hhs]
        dt_rs = [_rowsel(dtTb, hh) for hh in hhs]
        Ss = [jnp.where(causal, jnp.exp(ac_c - _rowsel(acTb, hh)), 0.0) for ac_c, hh in zip(ac_cs, hhs)]
        GSs = [Gm * S for S in Ss]
        Ws = [GS * dt_r for GS, dt_r in zip(GSs, dt_rs)]
        Xs = [x_ref[:, r * HEAD_DIM:(r + 1) * HEAD_DIM] for r in heads]
        Xcs = [X.astype(CDT) for X in Xs]
        Hps = [hp_ref[0, 0, r] for r in heads]
        Hcs = [Hp.astype(CDT) for Hp in Hps]
        dYs = [dy_all[:, r * HEAD_DIM:(r + 1) * HEAD_DIM] for r in heads]
        dYcs = [dY.astype(CDT) for dY in dYs]
        Es = [jnp.exp(ac_c) for ac_c in ac_cs]
        dQs = [(dY * E).astype(CDT) for dY, E in zip(dYs, Es)]
        dHns = [dh_scr[r] for r in heads]
        dHcs = [dHn.astype(CDT) for dHn in dHns]
        aLs = [_rowsel(ac_c, CHUNK - 1) for ac_c in ac_cs]
        exs = [jnp.exp(aL - ac_c) for aL, ac_c in zip(aLs, ac_cs)]
        es = [ex * dt_c for ex, dt_c in zip(exs, dt_cs)]
        Xes = [(X * e).astype(CDT) for X, e in zip(Xs, es)]
        WtdYs = [_tn(W.astype(CDT), dYc) for W, dYc in zip(Ws, dYcs)]
        dWms = [_nt(dYc, Xc) for dYc, Xc in zip(dYcs, Xcs)]
        Qs = [_nt(Cm, Hc) for Hc in Hcs]
        dCps = [_nn(dQ, Hc) for dQ, Hc in zip(dQs, Hcs)]
        dHps = [_tn(dQ, Cm) for dQ in dQs]
        T2s = [_nt(Bm, dHc) for dHc in dHcs]
        dBps = [_nn(Xe, dHc) for Xe, dHc in zip(Xes, dHcs)]
        for r in heads:
            dY, Xr, Wm, dWm, E, ex, e, aL = dYs[r], Xs[r], Ws[r], dWms[r], Es[r], exs[r], es[r], aLs[r]
            ddv = ddv + jnp.where(lane1 == r, jnp.sum(jnp.sum(dY * Xr, axis=1, keepdims=True), axis=0, keepdims=True), 0.0)
            dG = dG + dWm * Ss[r] * dt_rs[r]
            Mm = dWm * Wm
            dac_r = -jnp.sum(Mm, axis=0, keepdims=True)
            ddt_r = jnp.sum(dWm * GSs[r], axis=0, keepdims=True)
            dCm = dCm + dCps[r]
            dBm = dBm + dBps[r]
            eaL = jnp.exp(aL)
            daL = eaL * jnp.sum(jnp.sum(dHns[r] * Hps[r], axis=1, keepdims=True), axis=0, keepdims=True)
            de = jnp.sum(Xr * T2s[r], axis=1, keepdims=True)
            dee = de * e
            daL = daL + jnp.sum(dee, axis=0, keepdims=True)
            dac_c = (jnp.sum(Mm, axis=1, keepdims=True) + jnp.sum(dY * Qs[r], axis=1, keepdims=True) * E
                     - dee + jnp.where(subc == CHUNK - 1, daL, 0.0))
            dh_scr[r] = dHps[r] + eaL * dHns[r]
            dx_ref[:, r * HEAD_DIM:(r + 1) * HEAD_DIM] = dY * _colsel(dv, hhs[r]) + WtdYs[r] + T2s[r] * e
            colp = jnp.where(col == r, dac_c, colp)
            colp = jnp.where(col == SSD_HPG + r, de * ex, colp)
            rowp = jnp.where(sub8 == r, dac_r, rowp)
            rowp = jnp.where(sub8 == SSD_HPG + r, ddt_r, rowp)
        dGc = dG.astype(CDT)
        db_ref[...] = dBm + _tn(dGc, Cm)
        dc_ref[...] = dCm + _nn(dGc, Bm)
        colp_ref[0] = colp
        rowp_ref[0] = rowp
        dd_ref[0] += ddv

    f32 = lambda shape: SDS(shape, F32)
    return _pallas(body, grid=(G, nc),
                   in_specs=[sp["x"], sp["x"], sp["b"], sp["c"], sp["x"], sp["x"], sp["hp"], sp["col"], sp["col"],
                             sp["row"], sp["row"], sp["vec"], sp["ng"]],
                   out_specs=[sp["x"],
                              BS((CHUNK, LANES), lambda g, c: (nc - 1 - c, g)),
                              BS((CHUNK, LANES), lambda g, c: (nc - 1 - c, g)),
                              sp["x"],
                              BS((1, CHUNK, LANES), lambda g, c: (g, nc - 1 - c, 0)),
                              BS((1, HALO, CHUNK), lambda g, c: (g, 0, nc - 1 - c)),
                              BS((1, 1, LANES), lambda g, c: (g, 0, 0)),
                              sp["ng"]],
                   out_shape=[f32((T, DI)), f32((T, G * LANES)), f32((T, G * LANES)), SDS((T, DI), CDT),
                              f32((G, T, LANES)), f32((G, HALO, T)), f32((G, 1, LANES)), f32((1, DI))],
                   scratch_shapes=[pltpu.VMEM((SSD_HPG, HEAD_DIM, LANES), F32)],
                   compiler_params=_params("parallel", "arbitrary"), name=name)(
                       dyn, xbc, xbc, xbc, z, y, hp, dt, ac, dtT, acT, dskip, ng)


HBM = pl.BlockSpec(memory_space=pltpu.HBM)
VMEM = pl.BlockSpec(memory_space=pltpu.VMEM)


def _me():
    return lax.axis_index("x"), lax.axis_index("y"), lax.axis_index("c")


def _other_chips(x, y):
    return [(1 - x, y), (x, 1 - y), (1 - x, 1 - y)]


def _rcopy(src, dst, send_sem, recv_sem, to):
    return pltpu.make_async_remote_copy(src_ref=src, dst_ref=dst, send_sem=send_sem, recv_sem=recv_sem,
                                        device_id=to, device_id_type=MESH)


AG_SEMS = 13


def allgather_weights(packs, *, name):
    n = len(packs)

    def body(*refs):
        srcs, outs, (send_sems, recv_sems) = refs[:n], refs[n:2 * n], refs[2 * n:]
        x, y, c = _me()
        k = 2 * x + y
        sibling = (x, y, 1 - c)
        nx, ny = (1 - x, y, c), (x, 1 - y, c)
        kx, ky, kd = 2 * (1 - x) + y, 2 * x + (1 - y), 2 * (1 - x) + (1 - y)
        started = []

        def go(cp):
            cp.start()
            started.append(cp)

        def piece(ref, chip, core, r, Rq):
            return ref.at[chip, core, pl.ds(r * Rq, Rq)]

        for a, (src, out) in enumerate(zip(srcs, outs)):
            Rq = src.shape[1] // 2
            sem = lambda i: (send_sems.at[AG_SEMS * a + i], recv_sems.at[AG_SEMS * a + i])
            for i, (to, r) in enumerate([(nx, 0), (nx, 1), (ny, 0), (ny, 1)]):
                go(_rcopy(src.at[c, pl.ds(r * Rq, Rq)], piece(out, k, c, r, Rq), *sem(i), to))
            go(_rcopy(src, out.at[k], *sem(12), sibling))
        for a, out in enumerate(outs):
            Rq = out.shape[2] // 2
            sem = lambda i: (send_sems.at[AG_SEMS * a + i], recv_sems.at[AG_SEMS * a + i])

            def landed(i, chip, r, frm, relay=None):
                ref = piece(out, chip, c, r, Rq)
                _rcopy(ref, ref, *sem(i), frm).wait_recv()
                if relay is not None:
                    go(_rcopy(ref, ref, *sem(relay[0]), relay[1]))
                go(_rcopy(ref, ref, *sem(6 + i), sibling))

            landed(0, kx, 0, nx, relay=(4, ny))
            landed(3, ky, 1, ny, relay=(5, nx))
            landed(1, kx, 1, nx)
            landed(2, ky, 0, ny)
            landed(4, kd, 0, ny)
            landed(5, kd, 1, nx)
        for a, out in enumerate(outs):
            Rq = out.shape[2] // 2
            sem = lambda i: (send_sems.at[AG_SEMS * a + i], recv_sems.at[AG_SEMS * a + i])
            for i, (chip, r) in enumerate([(kx, 0), (kx, 1), (ky, 0), (ky, 1), (kd, 0), (kd, 1)]):
                ref = piece(out, chip, 1 - c, r, Rq)
                _rcopy(ref, ref, *sem(6 + i), sibling).wait_recv()
            mine = out.at[k]
            _rcopy(mine, mine, *sem(12), sibling).wait_recv()
        for cp in started:
            cp.wait_send()

    return _pallas(body, in_specs=[HBM] * n, out_specs=[HBM] * n,
                   out_shape=[SDS((N_CHIPS,) + p.shape, p.dtype) for p in packs],
                   scratch_shapes=[pltpu.SemaphoreType.DMA((AG_SEMS * n,)), pltpu.SemaphoreType.DMA((AG_SEMS * n,))],
                   name=name)(*packs)


def sibling_swap_half(ps, *, name):
    n = len(ps)

    def body(*refs):
        srcs, outs, (send_sems, recv_sems) = refs[:n], refs[n:2 * n], refs[2 * n:]
        x, y, c = _me()
        cps = [_rcopy(src.at[1 - c], out, send_sems.at[a], recv_sems.at[a], (x, y, 1 - c))
               for a, (src, out) in enumerate(zip(srcs, outs))]
        for cp in cps:
            cp.start()
        for cp in cps:
            cp.wait()

    return _pallas(body, in_specs=[HBM] * n, out_specs=[HBM] * n,
                   out_shape=[SDS(p.shape[1:], p.dtype) for p in ps],
                   scratch_shapes=[pltpu.SemaphoreType.DMA((n,)), pltpu.SemaphoreType.DMA((n,))], name=name)(*ps)


def scatter_relay(s1s, *, name):
    n = len(s1s)

    def body(*refs):
        srcs, outs, (send_sems, recv_sems) = refs[:n], refs[n:2 * n], refs[2 * n:]
        x, y, c = _me()
        kd = 2 * (1 - x) + (1 - y)
        peers = [(1 - x, y, c), (x, 1 - y, c)]
        sends = []
        for a, (src, out) in enumerate(zip(srcs, outs)):
            Rq = src.shape[1] // 2
            for r, to in enumerate(peers):
                sends.append(_rcopy(src.at[kd, pl.ds(r * Rq, Rq)], out.at[r], send_sems.at[2 * a + r],
                                    recv_sems.at[2 * a + r], to))
        for cp in sends:
            cp.start()
        for a, out in enumerate(outs):
            for r, frm in enumerate(peers):
                slot = out.at[r]
                _rcopy(slot, slot, send_sems.at[2 * a + r], recv_sems.at[2 * a + r], frm).wait_recv()
        for cp in sends:
            cp.wait_send()

    return _pallas(body, in_specs=[HBM] * n, out_specs=[HBM] * n,
                   out_shape=[SDS((2, p.shape[1] // 2, p.shape[2]), p.dtype) for p in s1s],
                   scratch_shapes=[pltpu.SemaphoreType.DMA((2 * n,)), pltpu.SemaphoreType.DMA((2 * n,))],
                   name=name)(*s1s)


def add_relayed(s1, relay, yx_idx, *, name):
    _, Rq, W = relay.shape
    tr = _row_tile(Rq)
    nb = Rq // tr

    def body(idx_ref, s_ref, t_ref, o_ref):
        o_ref[...] = (s_ref[...].astype(F32) + t_ref[...].astype(F32)).astype(o_ref.dtype)

    gs = pltpu.PrefetchScalarGridSpec(
        num_scalar_prefetch=1, grid=(2, nb),
        in_specs=[BS((1, tr, W), lambda p, i, idx: (idx[p], p * nb + i, 0)), BS((1, tr, W), lambda p, i, idx: (p, i, 0))],
        out_specs=BS((1, tr, W), lambda p, i, idx: (p, i, 0)))
    return _pallas(body, grid_spec=gs, out_shape=SDS(relay.shape, relay.dtype),
                   compiler_params=_params("parallel", "parallel"), name=name)(yx_idx, s1, relay)


def chip_scatter(s1s, us, *, name):
    n = len(s1s)

    def body(*refs):
        srcs, adds, outs, (send_sems, recv_sems) = refs[:n], refs[n:2 * n], refs[2 * n:3 * n], refs[3 * n:]
        x, y, c = _me()
        k = 2 * x + y
        kx, ky = 2 * (1 - x) + y, 2 * x + (1 - y)
        nx, ny = (1 - x, y, c), (x, 1 - y, c)
        sends = []
        for a, (src, u, out) in enumerate(zip(srcs, adds, outs)):
            Rq = src.shape[1] // 2
            lo, hi = pl.ds(0, Rq), pl.ds(Rq, Rq)
            sem = lambda i: (send_sems.at[4 * a + i], recv_sems.at[4 * a + i])
            sends += [_rcopy(src.at[kx, lo], out.at[k, lo], *sem(0), nx), _rcopy(u.at[1], out.at[k, hi], *sem(1), nx),
                      _rcopy(u.at[0], out.at[k, lo], *sem(2), ny), _rcopy(src.at[ky, hi], out.at[k, hi], *sem(3), ny)]
        for cp in sends:
            cp.start()
        for a, out in enumerate(outs):
            Rq = out.shape[1] // 2
            sem = lambda i: (send_sems.at[4 * a + i], recv_sems.at[4 * a + i])
            for i, (chip, rows, frm) in enumerate([(kx, pl.ds(0, Rq), nx), (kx, pl.ds(Rq, Rq), nx),
                                                   (ky, pl.ds(0, Rq), ny), (ky, pl.ds(Rq, Rq), ny)]):
                slot = out.at[chip, rows]
                _rcopy(slot, slot, *sem(i), frm).wait_recv()
        for cp in sends:
            cp.wait_send()

    return _pallas(body, in_specs=[HBM] * (2 * n), out_specs=[HBM] * n, out_shape=[SDS(p.shape, p.dtype) for p in s1s],
                   scratch_shapes=[pltpu.SemaphoreType.DMA((4 * n,)), pltpu.SemaphoreType.DMA((4 * n,))],
                   name=name)(*s1s, *us)


def sibling_pair(r2s, *, name):
    n = len(r2s)

    def body(*refs):
        srcs, outs, (send_sems, recv_sems) = refs[:n], refs[n:2 * n], refs[2 * n:]
        x, y, c = _me()
        cps = [_rcopy(src.at[c], out.at[c], send_sems.at[a], recv_sems.at[a], (x, y, 1 - c))
               for a, (src, out) in enumerate(zip(srcs, outs))]
        for cp in cps:
            cp.start()
        for a, out in enumerate(outs):
            theirs = out.at[1 - c]
            _rcopy(theirs, theirs, send_sems.at[a], recv_sems.at[a], (x, y, 1 - c)).wait_recv()
        for cp in cps:
            cp.wait_send()

    return _pallas(body, in_specs=[HBM] * n, out_specs=[HBM] * n, out_shape=[SDS(p.shape, p.dtype) for p in r2s],
                   input_output_aliases={a: a for a in range(n)},
                   scratch_shapes=[pltpu.SemaphoreType.DMA((n,)), pltpu.SemaphoreType.DMA((n,))], name=name)(*r2s)


def allgather_small(v, *, name):
    R, W = v.shape

    def body(v_ref, out_ref, sum_ref, send_sems, recv_sems):
        x, y, c = _me()
        me = 4 * x + 2 * y + c
        out_ref[me] = v_ref[...]
        peers = []
        for m in range(1, N_DEV):
            px = 1 - x if m & 4 else x
            py = 1 - y if m & 2 else y
            pc = 1 - c if m & 1 else c
            peers.append((px, py, pc))
        sends = [_rcopy(v_ref, out_ref.at[me], send_sems.at[j], recv_sems.at[j], p) for j, p in enumerate(peers)]
        for cp in sends:
            cp.start()
        for j, (px, py, pc) in enumerate(peers):
            slot = out_ref.at[4 * px + 2 * py + pc]
            _rcopy(slot, slot, send_sems.at[j], recv_sems.at[j], (px, py, pc)).wait_recv()
        for cp in sends:
            cp.wait_send()
        acc = out_ref[0]
        for d in range(1, N_DEV):
            acc = acc + out_ref[d]
        sum_ref[...] = acc

    return _pallas(body, in_specs=[VMEM], out_specs=[VMEM, VMEM],
                   out_shape=[SDS((N_DEV, R, W), v.dtype), SDS((R, W), v.dtype)],
                   scratch_shapes=[pltpu.SemaphoreType.DMA((N_DEV - 1,)), pltpu.SemaphoreType.DMA((N_DEV - 1,))],
                   name=name)(v)


def _row_tile(R, cap=512):
    for step in (32, 16, 8):
        fits = [t for t in range(step, cap + 1, step) if R % t == 0]
        if fits:
            return fits[-1]
    raise AssertionError(R)


def add_halves(p, recv, c_idx, *, name):
    _, n, Rh, W = p.shape
    tr = _row_tile(Rh)

    def body(c_ref, p_ref, r_ref, o_ref):
        o_ref[...] = (p_ref[0].astype(F32) + r_ref[...].astype(F32)).astype(o_ref.dtype)

    gs = pltpu.PrefetchScalarGridSpec(
        num_scalar_prefetch=1, grid=(n, Rh // tr),
        in_specs=[BS((1, 1, tr, W), lambda s, i, c_ref: (c_ref[0], s, i, 0)), BS((1, tr, W), lambda s, i, c_ref: (s, i, 0))],
        out_specs=BS((1, tr, W), lambda s, i, c_ref: (s, i, 0)))
    return _pallas(body, grid_spec=gs, out_shape=SDS((n, Rh, W), p.dtype),
                   compiler_params=_params("parallel", "parallel"), name=name)(c_idx, p, recv)


def sum_slots(s1, got, idx, *, name):
    n, Rh, W = got.shape
    tr = _row_tile(Rh)

    def body(idx_ref, s_ref, gx, gy, o_ref):
        o_ref[0] = s_ref[0].astype(F32) + gx[0].astype(F32) + gy[0].astype(F32)

    slot = lambda d: BS((1, tr, W), lambda i, idx: (idx[d], i, 0))
    gs = pltpu.PrefetchScalarGridSpec(
        num_scalar_prefetch=1, grid=(Rh // tr,), in_specs=[slot(0), slot(1), slot(2)],
        out_specs=BS((1, tr, W), lambda i, idx: (idx[3], i, 0)))
    return _pallas(body, grid_spec=gs, out_shape=SDS((2, Rh, W), F32),
                   compiler_params=_params("parallel"), name=name)(idx, s1, got, got)


BIG = [("ffn1_w_gu", True), ("ffn1_w_down", False), ("ffn2_w_gu", True), ("ffn2_w_down", False),
       ("sb_w_qkv", True), ("sb_w_o", False), ("ssd_w_in", True), ("ssd_w_out", False),
       ("sc_w_in", True), ("sc_w_out", False)]
PACK_ALIGN = 64


def _big_layout(shards):
    groups = {}
    for name, cols in BIG:
        L, K, n = shards[name].shape
        items, off = groups.setdefault(n, ([], 0))
        items.append((name, cols, (L, K, n), off, L * K))
        groups[n] = (items, off + L * K)
    return {w: (items, -(-rows // PACK_ALIGN) * PACK_ALIGN) for w, (items, rows) in groups.items()}


def _pack_rows(pieces, total, axis):
    used = sum(p.shape[axis] for p in pieces)
    if total > used:
        pad_shape = list(pieces[0].shape)
        pad_shape[axis] = total - used
        pieces = pieces + [jnp.zeros(pad_shape, pieces[0].dtype)]
    return jnp.concatenate(pieces, axis=axis) if len(pieces) > 1 else pieces[0]


def _gather_big(shards, tag):
    layout = _big_layout(shards)
    packs = [_pack_rows([shards[n].astype(CDT).reshape(-1, w) for n, *_ in items], total, 0).reshape(2, total // 2, w)
             for w, (items, total) in layout.items()]
    fulls = allgather_weights(packs, name=f"{tag}_allgather")
    out = {}
    for (w, (items, total)), full in zip(layout.items(), fulls):
        full = full.reshape(N_CHIPS, total, w)
        for name, cols, (L, K, n), off, rows in items:
            layers = [full[:, off + l * K:off + (l + 1) * K] for l in range(L)]
            if cols:
                out[name] = [p.transpose(1, 0, 2).reshape(K, N_CHIPS * n) for p in layers]
            else:
                out[name] = [p.reshape(N_CHIPS * K, n) for p in layers]
    return out


def _reduce_big(grads, shards, tag):
    layout = _big_layout(shards)
    ps = []
    for w, (items, total) in layout.items():
        pieces = []
        for name, cols, (L, K, n), off, rows in items:
            for g in grads[name]:
                if cols:
                    g = g.reshape(K, N_CHIPS, n).transpose(1, 0, 2)
                else:
                    g = g.reshape(N_CHIPS, K, n)
                pieces.append(g.astype(CDT))
        pack = _pack_rows(pieces, total, 1)
        ps.append(pack.reshape(N_CHIPS, 2, total // 2, w).transpose(1, 0, 2, 3))
    ix, iy = lax.axis_index("x").astype(jnp.int32), lax.axis_index("y").astype(jnp.int32)
    c_core = lax.axis_index("c").astype(jnp.int32)
    k_chip, k_x, k_y = 2 * ix + iy, 2 * (1 - ix) + iy, 2 * ix + (1 - iy)
    recvs = sibling_swap_half(ps, name=f"{tag}_sibling_swap")
    s1s = [add_halves(p, r, c_core.reshape(1), name=f"{tag}_add_halves_{p.shape[-1]}") for p, r in zip(ps, recvs)]
    relays = scatter_relay(s1s, name=f"{tag}_scatter_relay")
    us = [add_relayed(s1, t, jnp.stack([k_y, k_x]), name=f"{tag}_add_relayed_{s1.shape[-1]}") for s1, t in zip(s1s, relays)]
    gots = chip_scatter(s1s, us, name=f"{tag}_chip_scatter")
    r2s = [sum_slots(s1, got, jnp.stack([k_chip, k_x, k_y, c_core]), name=f"{tag}_sum_slots_{s1.shape[-1]}")
           for s1, got in zip(s1s, gots)]
    reds = sibling_pair(r2s, name=f"{tag}_sibling_pair")
    out = {}
    for (w, (items, total)), red in zip(layout.items(), reds):
        red = red.reshape(total, w)
        for name, cols, shape, off, rows in items:
            out[name] = red[off:off + rows].reshape(shape)
    return out


def _pad_lanes(v, width=LANES):
    return jnp.pad(v, ((0, 0), (0, width - v.shape[1])))


def _small_pack(vecs):
    flat, meta, off = [], [], 0
    for v in vecs:
        flat.append(v.reshape(-1))
        meta.append((off, v.size, v.shape))
        off += v.size
    total = -(-off // (8 * LANES)) * 8 * LANES
    flat.append(jnp.zeros((total - off,), F32))
    return jnp.concatenate(flat).reshape(-1, LANES), meta


def _small_unpack(pack, meta):
    flat = pack.reshape(-1)
    return [flat[off:off + size].reshape(shape) for off, size, shape in meta]


def _ffn_fwd(x, g, wgu, wd, tag):
    h = rms_fwd(x, g, name=f"{tag}_norm")
    gate, up, a = ffn_up(h, wgu, name=f"{tag}_gu")
    return matmul(a, wd, res=x, scale=0.5, name=f"{tag}_down"), (x, h, gate, up, a)


def _ffn_bwd(dout, saved, g, wgu, wd, tag):
    x, h, gate, up, a = saved
    F = wd.shape[0]
    dwd = matmul(a, dout, ta=True, out_dtype=CDT, scale=0.5, name=f"{tag}_dwd")
    dgate, dup = ffn_dact(dout, wd, gate, up, 0.5, name=f"{tag}_dact")
    dwgu = jnp.concatenate([matmul(h, dgate, ta=True, out_dtype=CDT, name=f"{tag}_dwg"),
                            matmul(h, dup, ta=True, out_dtype=CDT, name=f"{tag}_dwu")], axis=1)
    dh = matmul(dgate, wgu, tb=True, name=f"{tag}_dh_g")
    dh = matmul(dup, wgu, tb=True, b_k_start=F, res=dh, name=f"{tag}_dh_u")
    dx, dg = rms_bwd(dh, x, g, dout, name=f"{tag}_dnorm")
    return dx, dg, dwgu, dwd


def _sb_mixer_fwd(x, g, wqkv, wo, cfg, tag):
    h = rms_fwd(x, g, name=f"{tag}_norm")
    qkv = matmul(h, wqkv, out_dtype=CDT, name=f"{tag}_qkv")
    o, tab = sb_fwd(qkv, cfg.sb_heads, name=f"{tag}_attn")
    return matmul(o, wo, res=x, name=f"{tag}_out"), (x, h, qkv, tab, o)


def _sb_mixer_bwd(dout, saved, g, wqkv, wo, cfg, tag):
    x, h, qkv, tab, o = saved
    dwo = matmul(o, dout, ta=True, out_dtype=CDT, name=f"{tag}_dwo")
    do = matmul(dout, wo, tb=True, out_dtype=CDT, name=f"{tag}_do")
    dqkv = jnp.concatenate(sb_bwd(qkv, do, tab, cfg.sb_heads, name=f"{tag}_dattn"), axis=1)
    dwqkv = matmul(h, dqkv, ta=True, out_dtype=CDT, name=f"{tag}_dwqkv")
    dh = matmul(dqkv, wqkv, tb=True, name=f"{tag}_dh")
    dx, dg = rms_bwd(dh, x, g, dout, name=f"{tag}_dnorm")
    return dx, dg, dwqkv, dwo


def _sc_mixer_fwd(x, g, win, convw, wout, tag):
    h = rms_fwd(x, g, name=f"{tag}_norm")
    proj = matmul(h, win, name=f"{tag}_in")
    m = sc_fwd(proj, convw, name=f"{tag}_conv")
    return matmul(m, wout, res=x, name=f"{tag}_out"), (x, h, proj, m)


def _sc_mixer_bwd(dout, saved, g, win, convw, wout, tag):
    x, h, proj, m = saved
    dwout = matmul(m, dout, ta=True, out_dtype=CDT, name=f"{tag}_dwout")
    dm = matmul(dout, wout, tb=True, name=f"{tag}_dm")
    dproj, dconvw = sc_bwd(dm, proj, convw, name=f"{tag}_dconv")
    dwin = matmul(h, dproj, ta=True, out_dtype=CDT, name=f"{tag}_dwin")
    dh = matmul(dproj, win, tb=True, name=f"{tag}_dh")
    dx, dg = rms_bwd(dh, x, g, dout, name=f"{tag}_dnorm")
    return dx, dg, dwin, dconvw[:convw.shape[0]], dwout


def _ssd_split_w(win, cfg):
    DI, CV = cfg.DI, cfg.conv_dim
    return win[:, :DI], win[:, DI:DI + CV], _pad_lanes(win[:, DI + CV:])


def _ssd_mixer_fwd(x, g, win, convw, convb, dtb, alog, dskip, ng, wout, cfg, tag):
    wz, wx, wdt = _ssd_split_w(win, cfg)
    h = rms_fwd(x, g, name=f"{tag}_norm")
    z = matmul(h, wz, name=f"{tag}_z")
    xpre = matmul(h, wx, name=f"{tag}_xbc")
    dtpre = matmul(h, wdt, name=f"{tag}_dt")
    pre, xbc = ssd_conv_fwd(xpre, convw, convb, name=f"{tag}_conv")
    dt, ac, dtT, acT = ssd_prep_fwd(dtpre, dtb, alog, name=f"{tag}_prep")
    y, yn, hp = ssd_chunk_fwd(xbc, z, dt, ac, dtT, acT, dskip, ng, cfg, name=f"{tag}_scan")
    out = matmul(yn, wout, res=x, name=f"{tag}_out")
    return out, (x, h, z, xpre, dtpre, pre, xbc, dt, ac, dtT, acT, y, yn, hp)


def _ssd_mixer_bwd(dout, saved, g, win, convw, convb, dtb, alog, dskip, ng, wout, cfg, tag):
    x, h, z, xpre, dtpre, pre, xbc, dt, ac, dtT, acT, y, yn, hp = saved
    T = x.shape[0]
    NH, G = cfg.NH, cfg.G
    wz, wx, wdt = _ssd_split_w(win, cfg)
    dwout = matmul(yn, dout, ta=True, out_dtype=CDT, name=f"{tag}_dwout")
    dyn = matmul(dout, wout, tb=True, name=f"{tag}_dyn")
    dxs, db, dc, dz, colp, rowp, dd, dng = ssd_chunk_bwd(dyn, xbc, z, y, hp, dt, ac, dtT, acT, dskip, ng, cfg,
                                                        name=f"{tag}_dscan")
    col = lambda lo: _pad_lanes(colp[:, :, lo:lo + SSD_HPG].transpose(1, 0, 2).reshape(T, NH))
    row = lambda lo: jnp.pad(rowp[:, lo:lo + SSD_HPG, :].reshape(NH, T), ((0, LANES - NH), (0, 0)))
    ddtpre, dalog, ddtb = ssd_prep_bwd(col(0), col(SSD_HPG), row(0), row(SSD_HPG), dtpre, dtb, alog, name=f"{tag}_dprep")
    dxpre, dconvw, dconvb = ssd_conv_bwd(dxs, db, dc, pre, xpre, convw, name=f"{tag}_dconv")
    dwz = matmul(h, dz, ta=True, out_dtype=CDT, name=f"{tag}_dwz")
    dwx = matmul(h, dxpre, ta=True, out_dtype=CDT, name=f"{tag}_dwx")
    dwdt = matmul(h, ddtpre, ta=True, out_dtype=CDT, name=f"{tag}_dwdt")
    dh = matmul(dz, wz, tb=True, name=f"{tag}_dh_z")
    dh = matmul(dxpre, wx, tb=True, res=dh, name=f"{tag}_dh_x")
    dh = matmul(ddtpre, wdt, tb=True, res=dh, name=f"{tag}_dh_dt")
    dx, dg = rms_bwd(dh, x, g, dout, name=f"{tag}_dnorm")
    dwin = jnp.concatenate([dwz, dwx, dwdt[:, :NH]], axis=1)
    small = dict(conv_w=dconvw[:convw.shape[0]], conv_b=dconvb, dt_bias=ddtb[:, :NH], a_log=dalog[:, :NH],
                 d=dd[:, 0, :SSD_HPG].reshape(1, NH), norm=dng)
    return dx, dg, dwin, dwout, small


INPUT_NAMES = ['x', 'ffn1_norm', 'ffn1_w_gu', 'ffn1_w_down', 'mix_norm', 'ffn2_norm', 'ffn2_w_gu', 'ffn2_w_down',
               'sb_w_qkv', 'sb_w_o', 'ssd_w_in', 'ssd_conv_w', 'ssd_conv_b', 'ssd_dt_bias', 'ssd_a_log', 'ssd_d',
               'ssd_norm', 'ssd_w_out', 'sc_w_in', 'sc_conv_w', 'sc_w_out', 'final_norm']
WEIGHT_NAMES = INPUT_NAMES[1:]
SHARDED_SMALL = ("ssd_conv_w", "sc_conv_w")


def train_step(a, cfg):
    x = a["x"][0]
    tgt = a["loss_target"][0]
    big_names = [n for n, _ in BIG]
    k_chip = 2 * lax.axis_index("x") + lax.axis_index("y")

    w = _gather_big({n: a[n] for n in big_names}, "w")
    conv_pack, conv_meta = _small_pack([a[n] for n in SHARDED_SMALL])
    conv_all, _ = allgather_small(conv_pack, name="conv_w_allgather")
    conv_full = {}
    for i, n in enumerate(SHARDED_SMALL):
        per_chip = [_small_unpack(conv_all[2 * kk], conv_meta)[i] for kk in range(N_CHIPS)]
        conv_full[n] = jnp.concatenate(per_chip, axis=-1)

    def vec(name, i):
        return a[name][i:i + 1]

    ssd_small = lambda j: (a["ssd_conv_b"][j:j + 1], _pad_lanes(a["ssd_dt_bias"][j:j + 1]),
                           _pad_lanes(a["ssd_a_log"][j:j + 1]), _pad_lanes(a["ssd_d"][j:j + 1]), a["ssd_norm"][j:j + 1])

    saved = []
    for i in range(cfg.depth):
        kind, j = i % 3, i // 3
        x, s1 = _ffn_fwd(x, vec("ffn1_norm", i), w["ffn1_w_gu"][i], w["ffn1_w_down"][i], f"l{i}_ffn1")
        if kind == 0:
            x, s2 = _sb_mixer_fwd(x, vec("mix_norm", i), w["sb_w_qkv"][j], w["sb_w_o"][j], cfg, f"l{i}_sb")
        elif kind == 1:
            x, s2 = _ssd_mixer_fwd(x, vec("mix_norm", i), w["ssd_w_in"][j], conv_full["ssd_conv_w"][j], *ssd_small(j),
                                   w["ssd_w_out"][j], cfg, f"l{i}_ssd")
        else:
            x, s2 = _sc_mixer_fwd(x, vec("mix_norm", i), w["sc_w_in"][j], conv_full["sc_conv_w"][j], w["sc_w_out"][j],
                                  f"l{i}_sc")
        x, s3 = _ffn_fwd(x, vec("ffn2_norm", i), w["ffn2_w_gu"][i], w["ffn2_w_down"][i], f"l{i}_ffn2")
        saved.append((s1, s2, s3))
    loss_part, dx, d_final = loss_head(x, a["final_norm"].reshape(1, -1), tgt, name="loss_head")

    gbig = {n: [None] * a[n].shape[0] for n in big_names}
    gsmall = {n: [None] * a[n].shape[0] for n in ("ffn1_norm", "mix_norm", "ffn2_norm", "ssd_conv_w", "ssd_conv_b",
                                                   "ssd_dt_bias", "ssd_a_log", "ssd_d", "ssd_norm", "sc_conv_w")}
    for i in reversed(range(cfg.depth)):
        kind, j = i % 3, i // 3
        s1, s2, s3 = saved[i]
        dx, gsmall["ffn2_norm"][i], gbig["ffn2_w_gu"][i], gbig["ffn2_w_down"][i] = _ffn_bwd(
            dx, s3, vec("ffn2_norm", i), w["ffn2_w_gu"][i], w["ffn2_w_down"][i], f"l{i}_ffn2")
        if kind == 0:
            dx, gsmall["mix_norm"][i], gbig["sb_w_qkv"][j], gbig["sb_w_o"][j] = _sb_mixer_bwd(
                dx, s2, vec("mix_norm", i), w["sb_w_qkv"][j], w["sb_w_o"][j], cfg, f"l{i}_sb")
        elif kind == 1:
            dx, gsmall["mix_norm"][i], gbig["ssd_w_in"][j], gbig["ssd_w_out"][j], sm = _ssd_mixer_bwd(
                dx, s2, vec("mix_norm", i), w["ssd_w_in"][j], conv_full["ssd_conv_w"][j], *ssd_small(j),
                w["ssd_w_out"][j], cfg, f"l{i}_ssd")
            for key, val in sm.items():
                gsmall["ssd_" + key][j] = val
        else:
            dx, gsmall["mix_norm"][i], gbig["sc_w_in"][j], gsmall["sc_conv_w"][j], gbig["sc_w_out"][j] = _sc_mixer_bwd(
                dx, s2, vec("mix_norm", i), w["sc_w_in"][j], conv_full["sc_conv_w"][j], w["sc_w_out"][j], f"l{i}_sc")
        dx, gsmall["ffn1_norm"][i], gbig["ffn1_w_gu"][i], gbig["ffn1_w_down"][i] = _ffn_bwd(
            dx, s1, vec("ffn1_norm", i), w["ffn1_w_gu"][i], w["ffn1_w_down"][i], f"l{i}_ffn1")
    grad_x = dx[None]

    red_big = _reduce_big(gbig, {n: a[n] for n in big_names}, "g")
    small_names = list(gsmall) + ["final_norm"]
    small_vals = [jnp.concatenate([p.reshape((1,) + a[n].shape[1:-1] + (-1,)) for p in gsmall[n]], axis=0)
                  for n in gsmall] + [d_final.reshape(-1)]
    gpack, gmeta = _small_pack(small_vals + [loss_part[:, :1]])
    _, gsum = allgather_small(gpack, name="small_allreduce")
    *small_red, loss = _small_unpack(gsum, gmeta)
    red = dict(red_big)
    for n, v in zip(small_names, small_red):
        if n in SHARDED_SMALL:
            width = a[n].shape[-1]
            v = lax.dynamic_slice_in_dim(v, k_chip * width, width, axis=v.ndim - 1)
        red[n] = v.reshape(a[n].shape)

    delta, new_m, new_v = {}, {}, {}
    two_d = lambda t: t.reshape(-1, t.shape[-1])
    for n in big_names:
        d_, m_, v_ = adamw(two_d(a[n]), two_d(red[n]), two_d(a["m_" + n]), two_d(a["v_" + n]), name=f"adamw_{n}")
        delta[n], new_m[n], new_v[n] = (t.reshape(a[n].shape) for t in (d_, m_, v_))
    rest = [n for n in WEIGHT_NAMES if n not in big_names]
    packs = [_small_pack([src[n] for n in rest]) for src in
             (a, red, {n: a["m_" + n] for n in rest}, {n: a["v_" + n] for n in rest})]
    outs = adamw(*[p for p, _ in packs], name="adamw_small")
    for dst, o in zip((delta, new_m, new_v), outs):
        for n, t in zip(rest, _small_unpack(o, packs[0][1])):
            dst[n] = t
    return (loss.reshape(()), grad_x, *[red[n] for n in WEIGHT_NAMES], *[delta[n] for n in WEIGHT_NAMES],
            *[new_m[n] for n in WEIGHT_NAMES], *[new_v[n] for n in WEIGHT_NAMES])


def kernel(x, ffn1_norm, ffn1_w_gu, ffn1_w_down, mix_norm, ffn2_norm, ffn2_w_gu, ffn2_w_down, sb_w_qkv, sb_w_o, ssd_w_in, ssd_conv_w, ssd_conv_b, ssd_dt_bias, ssd_a_log, ssd_d, ssd_norm, ssd_w_out, sc_w_in, sc_conv_w, sc_w_out, final_norm, loss_target, m_ffn1_norm, m_ffn1_w_gu, m_ffn1_w_down, m_mix_norm, m_ffn2_norm, m_ffn2_w_gu, m_ffn2_w_down, m_sb_w_qkv, m_sb_w_o, m_ssd_w_in, m_ssd_conv_w, m_ssd_conv_b, m_ssd_dt_bias, m_ssd_a_log, m_ssd_d, m_ssd_norm, m_ssd_w_out, m_sc_w_in, m_sc_conv_w, m_sc_w_out, m_final_norm, v_ffn1_norm, v_ffn1_w_gu, v_ffn1_w_down, v_mix_norm, v_ffn2_norm, v_ffn2_w_gu, v_ffn2_w_down, v_sb_w_qkv, v_sb_w_o, v_ssd_w_in, v_ssd_conv_w, v_ssd_conv_b, v_ssd_dt_bias, v_ssd_a_log, v_ssd_d, v_ssd_norm, v_ssd_w_out, v_sc_w_in, v_sc_conv_w, v_sc_w_out, v_final_norm):
    args = dict(locals())
    return train_step(args, Cfg())
```

```python
import jax
import jax.numpy as jnp
from jax import lax
from jax.experimental import pallas as pl
from jax.experimental.pallas import tpu as pltpu

F32 = jnp.float32
CDT = jnp.bfloat16
HI = lax.Precision.HIGHEST
BS = pl.BlockSpec
SDS = jax.ShapeDtypeStruct
MESH = pl.DeviceIdType.MESH

RMS_EPS = 1e-6
ADAM_LR, ADAM_B1, ADAM_B2, ADAM_EPS, ADAM_WD, ADAM_STEP = 0.001, 0.9, 0.999, 1e-08, 0.01, 10
V7X_VMEM_LIMIT_BYTES = 56 * 1024 * 1024
LANES = 128
CHUNK = 128
HEAD_DIM = 64
SSD_HPG = 4
N_CHIPS = 4
N_DEV = 8


class Cfg:
    def __init__(self, d_model=1024, d_ff=2816, depth=4, sb_heads=16, ssd_groups=8, ssd_state=128,
                 ssd_conv=4, sc_width=3):
        self.D, self.F, self.depth = d_model, d_ff, depth
        self.sb_heads = sb_heads
        self.DI = 2 * d_model
        self.G = ssd_groups
        self.NH = self.DI // HEAD_DIM
        assert self.NH == SSD_HPG * self.G and ssd_state == LANES and sb_heads * HEAD_DIM == d_model
        self.conv_dim = self.DI + 2 * self.G * ssd_state
        self.in_dim = self.DI + self.conv_dim + self.NH
        self.ssd_conv, self.sc_width = ssd_conv, sc_width
        self.n_sb, self.n_ssd, self.n_sc = (depth + 2) // 3, (depth + 1) // 3, depth // 3


def _pallas(body, **kw):
    return pl.pallas_call(body, **kw)


def _params(*sem):
    return pltpu.CompilerParams(dimension_semantics=sem, vmem_limit_bytes=V7X_VMEM_LIMIT_BYTES)


def _tile(n, cap):
    if n <= cap:
        return n
    best = None
    for m in range(1, n // LANES + 1):
        t = m * LANES
        if n % t == 0 and t <= cap:
            best = t
    assert best is not None, (n, cap)
    return best


def _sigmoid(x):
    return 1.0 / (1.0 + jnp.exp(-x))


def _softplus_neg_abs(x):
    return jnp.log(1.0 + jnp.exp(-jnp.abs(x)))


def _colsel(blk, idx):
    lane = lax.broadcasted_iota(jnp.int32, blk.shape, 1)
    return jnp.sum(jnp.where(lane == idx, blk, 0.0), axis=1, keepdims=True)


def _rowsel(blk, idx):
    sub = lax.broadcasted_iota(jnp.int32, blk.shape, 0)
    return jnp.sum(jnp.where(sub == idx, blk, 0.0), axis=0, keepdims=True)


def _nt(a, b):
    return lax.dot_general(a, b, (((1,), (1,)), ((), ())), preferred_element_type=F32)


def _tn(a, b):
    return lax.dot_general(a, b, (((0,), (0,)), ((), ())), preferred_element_type=F32)


def _nn(a, b):
    return jnp.dot(a, b, preferred_element_type=F32)


def matmul(a, b, *, name, ta=False, tb=False, out_dtype=F32, res=None, scale=1.0, tm=1408, tn=1408, tk=1408,
           b_k_start=0):
    M, K = (a.shape[1], a.shape[0]) if ta else a.shape
    N, K2 = b.shape if tb else (b.shape[1], b.shape[0])
    assert K == K2 or (tb and b_k_start + K <= K2), (a.shape, b.shape, ta, tb)
    tm, tn, tk = _tile(M, tm), _tile(N, tn), _tile(K, tk)
    nk = K // tk
    assert b_k_start % tk == 0
    kb0 = b_k_start // tk
    dn = (((0 if ta else 1,), (1 if tb else 0,)), ((), ()))

    def body(*refs):
        a_ref, b_ref = refs[:2]
        r_ref = refs[2] if res is not None else None
        o_ref = refs[3 if res is not None else 2]
        part = lax.dot_general(a_ref[...].astype(CDT), b_ref[...].astype(CDT), dn, preferred_element_type=F32)

        def finish(total):
            o = total * scale
            if res is not None:
                o = o + r_ref[...].astype(F32)
            o_ref[...] = o.astype(o_ref.dtype)

        if nk == 1:
            finish(part)
            return
        acc = refs[-1]
        k = pl.program_id(2)

        @pl.when(k == 0)
        def _():
            acc[...] = part

        @pl.when(jnp.logical_and(k > 0, k < nk - 1))
        def _():
            acc[...] += part

        @pl.when(k == nk - 1)
        def _():
            finish(acc[...] + part)

    a_spec = BS((tk, tm), lambda i, j, k: (k, i)) if ta else BS((tm, tk), lambda i, j, k: (i, k))
    b_spec = BS((tn, tk), lambda i, j, k: (j, k + kb0)) if tb else BS((tk, tn), lambda i, j, k: (k, j))
    o_spec = BS((tm, tn), lambda i, j, k: (i, j))
    ins, specs = [a, b], [a_spec, b_spec]
    if res is not None:
        ins.append(res)
        specs.append(o_spec)
    return _pallas(body, grid=(M // tm, N // tn, nk), in_specs=specs, out_specs=o_spec,
                   out_shape=SDS((M, N), out_dtype),
                   scratch_shapes=[pltpu.VMEM((tm, tn), F32)] if nk > 1 else [],
                   compiler_params=_params("parallel", "parallel", "arbitrary"), name=name)(*ins)


def rms_fwd(x, g, *, name):
    T, D = x.shape
    tm = _tile(T, 512)

    def body(x_ref, g_ref, o_ref):
        xv = x_ref[...]
        r = lax.rsqrt(jnp.mean(xv * xv, axis=-1, keepdims=True) + RMS_EPS)
        o_ref[...] = (xv * r * g_ref[...]).astype(o_ref.dtype)

    return _pallas(body, grid=(T // tm,), in_specs=[BS((tm, D), lambda i: (i, 0)), BS((1, D), lambda i: (0, 0))],
                   out_specs=BS((tm, D), lambda i: (i, 0)), out_shape=SDS((T, D), CDT),
                   compiler_params=_params("parallel"), name=name)(x, g)


def rms_bwd(dh, x, g, dres, *, name):
    T, D = x.shape
    tm = _tile(T, 512)

    def body(dh_ref, x_ref, g_ref, r_ref, dx_ref, dg_ref):
        i = pl.program_id(0)
        xv = x_ref[...]
        r = lax.rsqrt(jnp.mean(xv * xv, axis=-1, keepdims=True) + RMS_EPS)
        xhat = xv * r
        dhv = dh_ref[...].astype(F32)
        dxh = dhv * g_ref[...]
        m = jnp.mean(dxh * xhat, axis=-1, keepdims=True)
        dx_ref[...] = r * (dxh - xhat * m) + r_ref[...]

        @pl.when(i == 0)
        def _():
            dg_ref[...] = jnp.zeros_like(dg_ref)

        dg_ref[...] += jnp.sum(dhv * xhat, axis=0, keepdims=True)

    row = BS((tm, D), lambda i: (i, 0))
    vec = BS((1, D), lambda i: (0, 0))
    return _pallas(body, grid=(T // tm,), in_specs=[row, row, vec, row], out_specs=[row, vec],
                   out_shape=[SDS((T, D), F32), SDS((1, D), F32)],
                   compiler_params=_params("arbitrary"), name=name)(dh, x, g, dres)


def ffn_up(h, wgu, *, name):
    T, D = h.shape
    F = wgu.shape[1] // 2
    tm, tn = _tile(T, 512), _tile(F, 1408)
    nj = F // tn

    def body(h_ref, wg_ref, wu_ref, g_ref, u_ref, a_ref):
        hv = h_ref[...].astype(CDT)
        g = jnp.dot(hv, wg_ref[...].astype(CDT), preferred_element_type=F32)
        u = jnp.dot(hv, wu_ref[...].astype(CDT), preferred_element_type=F32)
        g_ref[...] = g.astype(g_ref.dtype)
        u_ref[...] = u.astype(u_ref.dtype)
        a_ref[...] = (g * _sigmoid(g) * u).astype(a_ref.dtype)

    out = BS((tm, tn), lambda j, i: (i, j))
    return _pallas(body, grid=(nj, T // tm),
                   in_specs=[BS((tm, D), lambda j, i: (i, 0)), BS((D, tn), lambda j, i: (0, j)),
                             BS((D, tn), lambda j, i: (0, j + nj))],
                   out_specs=[out, out, out], out_shape=[SDS((T, F), CDT)] * 3,
                   compiler_params=_params("parallel", "parallel"), name=name)(h, wgu, wgu)


def ffn_dact(dout, wd, g, u, scale, *, name):
    T, D = dout.shape
    F = wd.shape[0]
    tm, tn = _tile(T, 512), _tile(F, 1408)

    def body(do_ref, wd_ref, g_ref, u_ref, dg_ref, du_ref):
        da = scale * _nt(do_ref[...].astype(CDT), wd_ref[...].astype(CDT))
        gv = g_ref[...].astype(F32)
        uv = u_ref[...].astype(F32)
        sg = _sigmoid(gv)
        dg_ref[...] = (da * uv * (sg * (1.0 + gv * (1.0 - sg)))).astype(dg_ref.dtype)
        du_ref[...] = (da * (gv * sg)).astype(du_ref.dtype)

    blk = BS((tm, tn), lambda j, i: (i, j))
    return _pallas(body, grid=(F // tn, T // tm),
                   in_specs=[BS((tm, D), lambda j, i: (i, 0)), BS((tn, D), lambda j, i: (j, 0)), blk, blk],
                   out_specs=[blk, blk], out_shape=[SDS((T, F), CDT)] * 2,
                   compiler_params=_params("parallel", "parallel"), name=name)(dout, wd, g, u)


def loss_head(x, g, tgt, *, name):
    T, D = x.shape
    tm = _tile(T, 512)

    def body(x_ref, g_ref, t_ref, l_ref, dx_ref, dg_ref):
        i = pl.program_id(0)
        xv = x_ref[...]
        gv = g_ref[...]
        r = lax.rsqrt(jnp.mean(xv * xv, axis=-1, keepdims=True) + RMS_EPS)
        xhat = xv * r
        err = xhat * gv - t_ref[...]
        part = 0.5 * jnp.sum(jnp.mean(err * err, axis=-1, keepdims=True), axis=0, keepdims=True)
        dy = err * (1.0 / D)
        dxh = dy * gv
        m = jnp.mean(dxh * xhat, axis=-1, keepdims=True)
        dx_ref[...] = r * (dxh - xhat * m)

        @pl.when(i == 0)
        def _():
            dg_ref[...] = jnp.zeros_like(dg_ref)
            l_ref[...] = jnp.zeros_like(l_ref)

        dg_ref[...] += jnp.sum(dy * xhat, axis=0, keepdims=True)
        l_ref[...] += jnp.broadcast_to(part, l_ref.shape)

    row = BS((tm, D), lambda i: (i, 0))
    vec = BS((1, D), lambda i: (0, 0))
    return _pallas(body, grid=(T // tm,), in_specs=[row, vec, row],
                   out_specs=[BS((1, LANES), lambda i: (0, 0)), row, vec],
                   out_shape=[SDS((1, LANES), F32), SDS((T, D), F32), SDS((1, D), F32)],
                   compiler_params=_params("arbitrary"), name=name)(x, g, tgt)


def adamw(w, g, m, v, *, name):
    R, C = w.shape
    tr = R
    if R * C * 4 > (1 << 20):
        for cand in range(8, R + 1, 8):
            if R % cand == 0 and cand * C * 4 <= (1 << 20):
                tr = cand

    def body(w_ref, g_ref, m_ref, v_ref, d_ref, nm_ref, nv_ref):
        gv = g_ref[...]
        nm = ADAM_B1 * m_ref[...] + (1.0 - ADAM_B1) * gv
        nv = ADAM_B2 * v_ref[...] + (1.0 - ADAM_B2) * (gv * gv)
        m_hat = nm / (1.0 - ADAM_B1 ** ADAM_STEP)
        v_hat = nv / (1.0 - ADAM_B2 ** ADAM_STEP)
        d_ref[...] = -ADAM_LR * (m_hat / (jnp.sqrt(v_hat) + ADAM_EPS) + ADAM_WD * w_ref[...])
        nm_ref[...] = nm
        nv_ref[...] = nv

    blk = BS((tr, C), lambda i: (i, 0))
    return _pallas(body, grid=(R // tr,), in_specs=[blk] * 4, out_specs=[blk] * 3,
                   out_shape=[SDS((R, C), F32)] * 3, compiler_params=_params("parallel"), name=name)(w, g, m, v)


def _sb_tri():
    row = lax.broadcasted_iota(jnp.int32, (CHUNK, CHUNK), 0)
    col = lax.broadcasted_iota(jnp.int32, (CHUNK, CHUNK), 1)
    return row, col


def _tri_stack(keep):
    t = keep.astype(jnp.bfloat16)
    return jnp.concatenate([t, t], axis=0)


def _split_dot(x, tri_stack):
    hi = x.astype(jnp.bfloat16)
    mid = (x - hi.astype(F32)).astype(jnp.bfloat16)
    return jnp.dot(jnp.concatenate([hi, mid], axis=1), tri_stack, preferred_element_type=F32)


def _sb_heads_per_step(H, want):
    hb = min(H, want)
    assert H % hb == 0, (H, hb)
    return hb


SB_FWD_HEADS = 16
SB_BWD_HEADS = 8


def _head_pair_masked(blk, lo):
    zero = jnp.zeros_like(blk)
    return jnp.where(lo, blk, zero), jnp.where(lo, zero, blk)


def sb_fwd(qkv, n_heads, *, name):
    T, D3 = qkv.shape
    D = D3 // 3
    nq = T // CHUNK
    assert nq <= LANES
    scale = HEAD_DIM ** -0.5
    hb = _sb_heads_per_step(n_heads, SB_FWD_HEADS)
    assert hb % 2 == 0
    LW = hb * HEAD_DIM
    ngrp = D // LW
    heads, pairs = range(hb), range(hb // 2)
    lanes = lambda p: slice(p * LANES, (p + 1) * LANES)

    def body(q_ref, k_ref, v_ref, o_ref, tab_ref, acc_scr, cs_scr, qm_scr):
        qi = pl.program_id(1)
        row, col = _sb_tri()
        lo = col < HEAD_DIM
        upper_stack = _tri_stack(row > col)
        acc_scr[...] = jnp.zeros_like(acc_scr)
        cs_scr[...] = jnp.zeros_like(cs_scr)
        tab_ref[...] = jnp.zeros_like(tab_ref)
        for p in pairs:
            qm_scr[2 * p], qm_scr[2 * p + 1] = _head_pair_masked(q_ref[:, lanes(p)] * scale, lo)

        def tile(j, diag):
            sl = pl.ds(pl.multiple_of(j * CHUNK, CHUNK), CHUNK)
            keep = (lambda t: jnp.where(col < row, t, 0.0)) if diag else (lambda t: t)
            kbs = [k_ref[sl, lanes(p)] for p in pairs]
            zs = [_nt(qm_scr[h], kbs[h // 2]) for h in heads]
            lszs = [jnp.minimum(z, 0.0) - _softplus_neg_abs(z) for z in zs]
            lks = [keep(lsz - z) for lsz, z in zip(lszs, zs)]
            tls = [_split_dot(lk, upper_stack) for lk in lks]
            css = [cs_scr[h] for h in heads]
            logits = [lsz + tl + cs for lsz, tl, cs in zip(lszs, tls, css)]
            atts = [keep(jnp.exp(lg)).astype(CDT) for lg in logits]
            vms = [jnp.concatenate(_head_pair_masked(v_ref[sl, lanes(p)], lo), axis=0) for p in pairs]
            pvs = [_nn(jnp.concatenate([atts[2 * p], atts[2 * p + 1]], axis=1), vms[p]) for p in pairs]
            for p in pairs:
                acc_scr[:, lanes(p)] += pvs[p]
            for h in heads:
                tab_ref[h, 0] = jnp.where(col == j, css[h], tab_ref[h, 0])
                if diag:
                    cs_scr[h] = css[h] + (tls[h][:, :1] + lks[h][:, :1])
                else:
                    cs_scr[h] = logits[h][:, :1] - zs[h][:, :1]

        tile(qi, True)

        def step(jj, carry):
            tile(qi - jj, False)
            return carry

        lax.fori_loop(1, qi + 1, step, 0)
        o_ref[...] = acc_scr[...].astype(o_ref.dtype)

    once = pl.Buffered(1)
    qs = BS((CHUNK, LW), lambda g, i: (i, g))
    return _pallas(body, grid=(ngrp, nq),
                   in_specs=[qs, BS((T, LW), lambda g, i: (0, ngrp + g), pipeline_mode=once),
                             BS((T, LW), lambda g, i: (0, 2 * ngrp + g), pipeline_mode=once)],
                   out_specs=[qs, BS((hb, 1, CHUNK, CHUNK), lambda g, i: (g, i, 0, 0))],
                   out_shape=[SDS((T, D), CDT), SDS((n_heads, nq, CHUNK, CHUNK), F32)],
                   scratch_shapes=[pltpu.VMEM((CHUNK, LW), F32), pltpu.VMEM((hb, CHUNK, 1), F32),
                                   pltpu.VMEM((hb, CHUNK, LANES), CDT)],
                   compiler_params=_params("parallel", "arbitrary"), name=name)(qkv, qkv, qkv)


def sb_bwd(qkv, do, tab, n_heads, *, name):
    T, D3 = qkv.shape
    D = D3 // 3
    nq = T // CHUNK
    scale = HEAD_DIM ** -0.5
    hb = _sb_heads_per_step(n_heads, SB_BWD_HEADS)
    assert hb % 2 == 0
    LW = hb * HEAD_DIM
    ngrp = D // LW
    heads, pairs = range(hb), range(hb // 2)
    lanes = lambda p: slice(p * LANES, (p + 1) * LANES)

    def body(q_ref, k_ref, v_ref, do_ref, tab_ref, dq_ref, dk_ref, dv_ref,
             dk_acc, dv_acc, dq_scr, pfx_scr, qm_scr, dom_scr):
        qi = pl.program_id(1)

        @pl.when(qi == 0)
        def _():
            dk_acc[...] = jnp.zeros_like(dk_acc)
            dv_acc[...] = jnp.zeros_like(dv_acc)

        dq_scr[...] = jnp.zeros_like(dq_scr)
        pfx_scr[...] = jnp.zeros_like(pfx_scr)
        row, col = _sb_tri()
        lo = col < HEAD_DIM
        upper_stack = _tri_stack(row > col)
        before_stack = _tri_stack(row < col)
        for p in pairs:
            qm_scr[2 * p], qm_scr[2 * p + 1] = _head_pair_masked(q_ref[:, lanes(p)] * scale, lo)
            dom_scr[2 * p], dom_scr[2 * p + 1] = _head_pair_masked(do_ref[:, lanes(p)], lo)

        def tile(j, diag):
            sl = pl.ds(pl.multiple_of(j * CHUNK, CHUNK), CHUNK)
            keep = (lambda t: jnp.where(col < row, t, 0.0)) if diag else (lambda t: t)
            kbs = [k_ref[sl, lanes(p)] for p in pairs]
            vbs = [v_ref[sl, lanes(p)] for p in pairs]
            zs = [_nt(qm_scr[h], kbs[h // 2]) for h in heads]
            das = [_nt(dom_scr[h], vbs[h // 2]) for h in heads]
            lszs = [jnp.minimum(z, 0.0) - _softplus_neg_abs(z) for z in zs]
            lks = [keep(lsz - z) for lsz, z in zip(lszs, zs)]
            tls = [_split_dot(lk, upper_stack) for lk in lks]
            css = [jnp.sum(jnp.where(col == j, tab_ref[h, 0], 0.0), axis=1, keepdims=True) for h in heads]
            atts = [keep(jnp.exp(lsz + tl + cs)) for lsz, tl, cs in zip(lszs, tls, css)]
            dls = [att * da for att, da in zip(atts, das)]
            pres = [_split_dot(dl, before_stack) for dl in dls]
            pfxs = [pfx_scr[h] for h in heads]
            sigs = [jnp.exp(lsz) for lsz in lszs]
            befores = [pre + pfx for pre, pfx in zip(pres, pfxs)]
            dzs = [(dl * (1.0 - sig) - keep(sig * bf)).astype(CDT) for dl, sig, bf in zip(dls, sigs, befores)]
            attc = [att.astype(CDT) for att in atts]
            kms = [jnp.concatenate(_head_pair_masked(kb, lo), axis=0) for kb in kbs]
            dqs = [_nn(jnp.concatenate([dzs[2 * p], dzs[2 * p + 1]], axis=1), kms[p]) for p in pairs]
            both = lambda xs, p: jnp.concatenate([xs[2 * p], xs[2 * p + 1]], axis=0)
            dks = [_tn(both(dzs, p), jnp.concatenate([qm_scr[2 * p], qm_scr[2 * p + 1]], axis=0)) for p in pairs]
            dvs = [_tn(both(attc, p), jnp.concatenate([dom_scr[2 * p], dom_scr[2 * p + 1]], axis=0)) for p in pairs]
            last = CHUNK - 1
            for p in pairs:
                dq_scr[:, lanes(p)] += dqs[p]
                dk_acc[sl, lanes(p)] += dks[p]
                dv_acc[sl, lanes(p)] += dvs[p]
            for h in heads:
                pfx_scr[h] = befores[h][:, last:] + dls[h][:, last:]

        def step(j, carry):
            tile(j, False)
            return carry

        lax.fori_loop(0, qi, step, 0)
        tile(qi, True)
        dq_ref[...] = (dq_scr[...] * scale).astype(dq_ref.dtype)

        @pl.when(qi == nq - 1)
        def _():
            dk_ref[...] = dk_acc[...].astype(dk_ref.dtype)
            dv_ref[...] = dv_acc[...].astype(dv_ref.dtype)

    once = pl.Buffered(1)
    qs = BS((CHUNK, LW), lambda g, i: (i, g))
    ks = BS((T, LW), lambda g, i: (0, g))
    return _pallas(body, grid=(ngrp, nq),
                   in_specs=[qs, BS((T, LW), lambda g, i: (0, ngrp + g), pipeline_mode=once),
                             BS((T, LW), lambda g, i: (0, 2 * ngrp + g), pipeline_mode=once), qs,
                             BS((hb, 1, CHUNK, CHUNK), lambda g, i: (g, i, 0, 0))],
                   out_specs=[qs, ks, ks],
                   out_shape=[SDS((T, D), CDT)] * 3,
                   scratch_shapes=[pltpu.VMEM((T, LW), F32), pltpu.VMEM((T, LW), F32),
                                   pltpu.VMEM((CHUNK, LW), F32), pltpu.VMEM((hb, CHUNK, 1), F32),
                                   pltpu.VMEM((hb, CHUNK, LANES), CDT), pltpu.VMEM((hb, CHUNK, LANES), CDT)],
                   compiler_params=_params("parallel", "arbitrary"), name=name)(qkv, qkv, qkv, do, tab)


HALO = 8


def _shift_down(cur, prev8, s):
    if s == 0:
        return cur
    tt = cur.shape[0]
    row = lax.broadcasted_iota(jnp.int32, cur.shape, 0)
    fix = jnp.concatenate([pltpu.roll(prev8, s, 0), jnp.zeros((tt - HALO, cur.shape[1]), cur.dtype)], axis=0)
    return jnp.where(row < s, fix, pltpu.roll(cur, s, 0))


def _shift_up(cur, next8, s):
    if s == 0:
        return cur
    tt = cur.shape[0]
    row = lax.broadcasted_iota(jnp.int32, cur.shape, 0)
    fix = jnp.concatenate([jnp.zeros((tt - HALO, cur.shape[1]), cur.dtype), pltpu.roll(next8, HALO - s, 0)], axis=0)
    return jnp.where(row >= tt - s, fix, pltpu.roll(cur, tt - s, 0))


def _halo_specs(tt, width, T):
    per = tt // HALO
    last = T // HALO - 1
    cur = BS((tt, width), lambda i: (i, 0))
    prev = BS((HALO, width), lambda i: (jnp.maximum(i * per - 1, 0), 0))
    nxt = BS((HALO, width), lambda i: (jnp.minimum((i + 1) * per, last), 0))
    return cur, prev, nxt


def sc_fwd(proj, w, *, name):
    T, D3 = proj.shape
    D = D3 // 3
    K = w.shape[0]
    tt = _tile(T, 256)
    cur, prev, _ = _halo_specs(tt, D3, T)

    def body(p_ref, pp_ref, w_ref, o_ref):
        i = pl.program_id(0)
        pc = p_ref[:, D:2 * D] * p_ref[:, 2 * D:]
        pp = jnp.where(i > 0, pp_ref[:, D:2 * D] * pp_ref[:, 2 * D:], 0.0)
        u = jnp.zeros((tt, D), F32)
        for kk in range(K):
            u = u + w_ref[kk:kk + 1, :] * _shift_down(pc, pp, K - 1 - kk)
        o_ref[...] = (p_ref[:, :D] * u).astype(o_ref.dtype)

    return _pallas(body, grid=(T // tt,), in_specs=[cur, prev, BS((K, D), lambda i: (0, 0))],
                   out_specs=BS((tt, D), lambda i: (i, 0)), out_shape=SDS((T, D), CDT),
                   compiler_params=_params("parallel"), name=name)(proj, proj, w)


def sc_bwd(dm, proj, w, *, name):
    T, D3 = proj.shape
    D = D3 // 3
    K = w.shape[0]
    tt = _tile(T, 256)
    cur, prev, nxt = _halo_specs(tt, D3, T)
    dcur, _, dnxt = _halo_specs(tt, D, T)
    nsteps = T // tt

    def body(dm_ref, dmn_ref, p_ref, pp_ref, pn_ref, w_ref, o_ref, dw_ref):
        i = pl.program_id(0)
        b = p_ref[:, :D]
        c = p_ref[:, D:2 * D]
        h = p_ref[:, 2 * D:]
        pc = c * h
        pp = jnp.where(i > 0, pp_ref[:, D:2 * D] * pp_ref[:, 2 * D:], 0.0)
        dmv = dm_ref[...]
        du = dmv * b
        dun = jnp.where(i < nsteps - 1, dmn_ref[...] * pn_ref[:, :D], 0.0)
        u = jnp.zeros((tt, D), F32)
        dp = jnp.zeros((tt, D), F32)
        rows = []
        for kk in range(K):
            sh = _shift_down(pc, pp, K - 1 - kk)
            u = u + w_ref[kk:kk + 1, :] * sh
            dp = dp + w_ref[kk:kk + 1, :] * _shift_up(du, dun, K - 1 - kk)
            rows.append(jnp.sum(du * sh, axis=0, keepdims=True))
        rows.append(jnp.zeros((HALO - K, D), F32))
        o_ref[:, :D] = (dmv * u).astype(o_ref.dtype)
        o_ref[:, D:2 * D] = (dp * h).astype(o_ref.dtype)
        o_ref[:, 2 * D:] = (dp * c).astype(o_ref.dtype)

        @pl.when(i == 0)
        def _():
            dw_ref[...] = jnp.zeros_like(dw_ref)

        dw_ref[...] += jnp.concatenate(rows, axis=0)

    return _pallas(body, grid=(nsteps,), in_specs=[dcur, dnxt, cur, prev, nxt, BS((K, D), lambda i: (0, 0))],
                   out_specs=[BS((tt, D3), lambda i: (i, 0)), BS((HALO, D), lambda i: (0, 0))],
                   out_shape=[SDS((T, D3), CDT), SDS((HALO, D), F32)],
                   compiler_params=_params("arbitrary"), name=name)(dm, dm, proj, proj, proj, w)


def ssd_conv_fwd(xpre, w, b, *, name):
    T, C = xpre.shape
    K = w.shape[0]
    tt = _tile(T, 256)
    cur, prev, _ = _halo_specs(tt, C, T)

    def body(x_ref, xp_ref, w_ref, b_ref, pre_ref, o_ref):
        i = pl.program_id(0)
        xv = x_ref[...]
        xp = jnp.where(i > 0, xp_ref[...], 0.0)
        acc = jnp.broadcast_to(b_ref[...], (tt, C))
        for kk in range(K):
            acc = acc + w_ref[kk:kk + 1, :] * _shift_down(xv, xp, K - 1 - kk)
        pre_ref[...] = acc
        o_ref[...] = acc * _sigmoid(acc)

    row = BS((tt, C), lambda i: (i, 0))
    return _pallas(body, grid=(T // tt,), in_specs=[cur, prev, BS((K, C), lambda i: (0, 0)), BS((1, C), lambda i: (0, 0))],
                   out_specs=[row, row], out_shape=[SDS((T, C), F32)] * 2,
                   compiler_params=_params("parallel"), name=name)(xpre, xpre, w, b)


def ssd_conv_bwd(dxs, db_, dc_, pre, xpre, w, *, name):
    T, C = xpre.shape
    K = w.shape[0]
    tt = _tile(T, 256)
    cur, prev, nxt = _halo_specs(tt, C, T)
    widths = (dxs.shape[1], db_.shape[1], dc_.shape[1])
    gspecs = []
    for wd in widths:
        c_, _, n_ = _halo_specs(tt, wd, T)
        gspecs += [c_, n_]
    nsteps = T // tt

    def dsilu(p):
        s = _sigmoid(p)
        return s * (1.0 + p * (1.0 - s))

    def body(g0, g0n, g1, g1n, g2, g2n, pre_ref, pren_ref, x_ref, xp_ref, w_ref, dx_ref, dw_ref, dbias_ref):
        i = pl.program_id(0)
        gcur = jnp.concatenate([g0[...], g1[...], g2[...]], axis=1)
        gnxt = jnp.concatenate([g0n[...], g1n[...], g2n[...]], axis=1)
        dpre = gcur * dsilu(pre_ref[...])
        dpren = jnp.where(i < nsteps - 1, gnxt * dsilu(pren_ref[...]), 0.0)
        xv = x_ref[...]
        xp = jnp.where(i > 0, xp_ref[...], 0.0)
        dx = jnp.zeros((tt, C), F32)
        rows = []
        for kk in range(K):
            dx = dx + w_ref[kk:kk + 1, :] * _shift_up(dpre, dpren, K - 1 - kk)
            rows.append(jnp.sum(dpre * _shift_down(xv, xp, K - 1 - kk), axis=0, keepdims=True))
        rows.append(jnp.zeros((HALO - K, C), F32))
        dx_ref[...] = dx.astype(dx_ref.dtype)

        @pl.when(i == 0)
        def _():
            dw_ref[...] = jnp.zeros_like(dw_ref)
            dbias_ref[...] = jnp.zeros_like(dbias_ref)

        dw_ref[...] += jnp.concatenate(rows, axis=0)
        dbias_ref[...] += jnp.sum(dpre, axis=0, keepdims=True)

    return _pallas(body, grid=(nsteps,),
                   in_specs=gspecs + [cur, nxt, cur, prev, BS((K, C), lambda i: (0, 0))],
                   out_specs=[BS((tt, C), lambda i: (i, 0)), BS((HALO, C), lambda i: (0, 0)), BS((1, C), lambda i: (0, 0))],
                   out_shape=[SDS((T, C), CDT), SDS((HALO, C), F32), SDS((1, C), F32)],
                   compiler_params=_params("arbitrary"), name=name)(dxs, dxs, db_, db_, dc_, dc_, pre, pre, xpre, xpre, w)


def ssd_prep_fwd(dtpre, bias, alog, *, name):
    T = dtpre.shape[0]
    nc = T // CHUNK

    def body(p_ref, b_ref, a_ref, dt_ref, ac_ref, dtT_ref, acT_ref):
        xv = p_ref[...] + b_ref[...]
        dt = jnp.maximum(xv, 0.0) + _softplus_neg_abs(xv)
        row, col = _sb_tri()
        lower = (col <= row).astype(F32)
        ac = jnp.dot(lower, dt * -jnp.exp(a_ref[...]), precision=HI, preferred_element_type=F32)
        dt_ref[...] = dt
        ac_ref[...] = ac
        dtT_ref[...] = dt.T
        acT_ref[...] = ac.T

    blk = BS((CHUNK, LANES), lambda c: (c, 0))
    blkT = BS((LANES, CHUNK), lambda c: (0, c))
    vec = BS((1, LANES), lambda c: (0, 0))
    return _pallas(body, grid=(nc,), in_specs=[blk, vec, vec], out_specs=[blk, blk, blkT, blkT],
                   out_shape=[SDS((T, LANES), F32)] * 2 + [SDS((LANES, T), F32)] * 2,
                   compiler_params=_params("parallel"), name=name)(dtpre, bias, alog)


def ssd_prep_bwd(dac_c, ddt_c, dac_r, ddt_r, dtpre, bias, alog, *, name):
    T = dtpre.shape[0]
    nc = T // CHUNK

    def body(dac_ref, ddt_ref, dacr_ref, ddtr_ref, p_ref, b_ref, a_ref, o_ref, da_ref, db_ref):
        c = pl.program_id(0)
        xv = p_ref[...] + b_ref[...]
        dt = jnp.maximum(xv, 0.0) + _softplus_neg_abs(xv)
        av = -jnp.exp(a_ref[...])
        row, col = _sb_tri()
        after = (col >= row).astype(F32)
        dda = jnp.dot(after, dac_ref[...] + dacr_ref[...].T, precision=HI, preferred_element_type=F32)
        dpre = (ddt_ref[...] + ddtr_ref[...].T + av * dda) * _sigmoid(xv)
        o_ref[...] = dpre

        @pl.when(c == 0)
        def _():
            da_ref[...] = jnp.zeros_like(da_ref)
            db_ref[...] = jnp.zeros_like(db_ref)

        da_ref[...] += jnp.sum(dt * dda, axis=0, keepdims=True) * av
        db_ref[...] += jnp.sum(dpre, axis=0, keepdims=True)

    blk = BS((CHUNK, LANES), lambda c: (c, 0))
    blkT = BS((LANES, CHUNK), lambda c: (0, c))
    vec = BS((1, LANES), lambda c: (0, 0))
    return _pallas(body, grid=(nc,), in_specs=[blk, blk, blkT, blkT, blk, vec, vec], out_specs=[blk, vec, vec],
                   out_shape=[SDS((T, LANES), F32), SDS((1, LANES), F32), SDS((1, LANES), F32)],
                   compiler_params=_params("arbitrary"), name=name)(dac_c, ddt_c, dac_r, ddt_r, dtpre, bias, alog)


def _ssd_specs(cfg, T, rev):
    nc = T // CHUNK
    GW = SSD_HPG * HEAD_DIM
    bo = cfg.DI // LANES
    co = (cfg.DI + cfg.G * LANES) // LANES
    ci = (lambda c: nc - 1 - c) if rev else (lambda c: c)
    return dict(
        x=BS((CHUNK, GW), lambda g, c: (ci(c), g)),
        b=BS((CHUNK, LANES), lambda g, c: (ci(c), bo + g)),
        c=BS((CHUNK, LANES), lambda g, c: (ci(c), co + g)),
        col=BS((CHUNK, LANES), lambda g, c: (ci(c), 0)),
        row=BS((LANES, CHUNK), lambda g, c: (0, ci(c))),
        vec=BS((1, LANES), lambda g, c: (0, 0)),
        ng=BS((1, GW), lambda g, c: (0, g)),
        hp=BS((1, 1, SSD_HPG, HEAD_DIM, LANES), lambda g, c: (g, ci(c), 0, 0, 0)),
    )


def ssd_chunk_fwd(xbc, z, dt, ac, dtT, acT, dskip, ng, cfg, *, name):
    T = xbc.shape[0]
    nc = T // CHUNK
    G, DI = cfg.G, cfg.DI
    GW = SSD_HPG * HEAD_DIM
    sp = _ssd_specs(cfg, T, False)

    def body(x_ref, b_ref, c_ref, z_ref, dt_ref, ac_ref, dtT_ref, acT_ref, d_ref, ng_ref,
             y_ref, yn_ref, hp_ref, h_scr):
        g = pl.program_id(0)
        c = pl.program_id(1)

        @pl.when(c == 0)
        def _():
            h_scr[...] = jnp.zeros_like(h_scr)

        row, col = _sb_tri()
        causal = row >= col
        Bm = b_ref[...].astype(CDT)
        Cm = c_ref[...].astype(CDT)
        Gm = _nt(Cm, Bm)
        dtb, acb, dtTb, acTb, dv = dt_ref[...], ac_ref[...], dtT_ref[...], acT_ref[...], d_ref[...]
        heads = range(SSD_HPG)
        hhs = [g * SSD_HPG + r for r in heads]
        ac_cs = [_colsel(acb, hh) for hh in hhs]
        dt_cs = [_colsel(dtb, hh) for hh in hhs]
        Xs = [x_ref[:, r * HEAD_DIM:(r + 1) * HEAD_DIM] for r in heads]
        Hps = [h_scr[r] for r in heads]
        Ws = [(Gm * jnp.where(causal, jnp.exp(ac_c - _rowsel(acTb, hh)), 0.0) * _rowsel(dtTb, hh)).astype(CDT)
              for ac_c, hh in zip(ac_cs, hhs)]
        Qs = [_nt(Cm, Hp.astype(CDT)) for Hp in Hps]
        Yds = [_nn(Wm, Xr.astype(CDT)) for Wm, Xr in zip(Ws, Xs)]
        aLs = [_rowsel(ac_c, CHUNK - 1) for ac_c in ac_cs]
        Xes = [(Xr * (jnp.exp(aL - ac_c) * dt_c)).astype(CDT) for Xr, aL, ac_c, dt_c in zip(Xs, aLs, ac_cs, dt_cs)]
        Sts = [_tn(Xe, Bm) for Xe in Xes]
        for r in heads:
            hp_ref[0, 0, r] = Hps[r]
            h_scr[r] = jnp.exp(aLs[r]) * Hps[r] + Sts[r]
            y_ref[:, r * HEAD_DIM:(r + 1) * HEAD_DIM] = (Yds[r] + jnp.exp(ac_cs[r]) * Qs[r]
                                                         + _colsel(dv, hhs[r]) * Xs[r])
        zz = z_ref[...]
        yz = y_ref[...] * (zz * _sigmoid(zz))
        rstd = lax.rsqrt(jnp.mean(yz * yz, axis=-1, keepdims=True) + RMS_EPS)
        yn_ref[...] = (yz * rstd * ng_ref[...]).astype(yn_ref.dtype)

    return _pallas(body, grid=(G, nc),
                   in_specs=[sp["x"], sp["b"], sp["c"], sp["x"], sp["col"], sp["col"], sp["row"], sp["row"], sp["vec"], sp["ng"]],
                   out_specs=[sp["x"], sp["x"], sp["hp"]],
                   out_shape=[SDS((T, DI), F32), SDS((T, DI), CDT), SDS((G, nc, SSD_HPG, HEAD_DIM, LANES), F32)],
                   scratch_shapes=[pltpu.VMEM((SSD_HPG, HEAD_DIM, LANES), F32)],
                   compiler_params=_params("parallel", "arbitrary"), name=name)(xbc, xbc, xbc, z, dt, ac, dtT, acT, dskip, ng)


def ssd_chunk_bwd(dyn, xbc, z, y, hp, dt, ac, dtT, acT, dskip, ng, cfg, *, name):
    T = xbc.shape[0]
    nc = T // CHUNK
    G, DI = cfg.G, cfg.DI
    GW = SSD_HPG * HEAD_DIM
    sp = _ssd_specs(cfg, T, True)

    def body(dyn_ref, x_ref, b_ref, c_ref, z_ref, y_ref, hp_ref, dt_ref, ac_ref, dtT_ref, acT_ref, d_ref, ng_ref,
             dx_ref, db_ref, dc_ref, dz_ref, colp_ref, rowp_ref, dd_ref, dng_ref, dh_scr):
        g = pl.program_id(0)
        c = pl.program_id(1)

        @pl.when(c == 0)
        def _():
            dh_scr[...] = jnp.zeros_like(dh_scr)
            dd_ref[...] = jnp.zeros_like(dd_ref)
            dng_ref[...] = jnp.zeros_like(dng_ref)

        zz = z_ref[...]
        sg = _sigmoid(zz)
        gate = zz * sg
        yv = y_ref[...]
        yz = yv * gate
        rstd = lax.rsqrt(jnp.mean(yz * yz, axis=-1, keepdims=True) + RMS_EPS)
        nhat = yz * rstd
        dynv = dyn_ref[...].astype(F32)
        dng_ref[...] += jnp.sum(dynv * nhat, axis=0, keepdims=True)
        dnh = dynv * ng_ref[...]
        dyz = rstd * (dnh - nhat * jnp.mean(dnh * nhat, axis=-1, keepdims=True))
        dz_ref[...] = (dyz * yv * (sg * (1.0 + zz * (1.0 - sg)))).astype(dz_ref.dtype)
        dy_all = dyz * gate

        row, col = _sb_tri()
        causal = row >= col
        lane1 = lax.broadcasted_iota(jnp.int32, (1, LANES), 1)
        sub8 = lax.broadcasted_iota(jnp.int32, (HALO, CHUNK), 0)
        subc = lax.broadcasted_iota(jnp.int32, (CHUNK, 1), 0)
        Bf = b_ref[...]
        Cf = c_ref[...]
        Bm = Bf.astype(CDT)
        Cm = Cf.astype(CDT)
        Gm = _nt(Cm, Bm)
        dtb, acb, dtTb, acTb, dv = dt_ref[...], ac_ref[...], dtT_ref[...], acT_ref[...], d_ref[...]
        dG = jnp.zeros((CHUNK, CHUNK), F32)
        dBm = jnp.zeros((CHUNK, LANES), F32)
        dCm = jnp.zeros((CHUNK, LANES), F32)
        colp = jnp.zeros((CHUNK, LANES), F32)
        rowp = jnp.zeros((HALO, CHUNK), F32)
        ddv = jnp.zeros((1, LANES), F32)
        heads = range(SSD_HPG)
        hhs = [g * SSD_HPG + r for r in heads]
        ac_cs = [_colsel(acb, hh) for hh in hhs]
        dt_cs = [_colsel(dtb, hh) for hh in hhs]
        dt_rs = [_rowsel(dtTb, hh) for hh in hhs]
        Ss = [jnp.where(causal, jnp.exp(ac_c - _rowsel(acTb, hh)), 0.0) for ac_c, hh in zip(ac_cs, hhs)]
        GSs = [Gm * S for S in Ss]
        Ws = [GS * dt_r for GS, dt_r in zip(GSs, dt_rs)]
        Xs = [x_ref[:, r * HEAD_DIM:(r + 1) * HEAD_DIM] for r in heads]
        Xcs = [X.astype(CDT) for X in Xs]
        Hps = [hp_ref[0, 0, r] for r in heads]
        Hcs = [Hp.astype(CDT) for Hp in Hps]
        dYs = [dy_all[:, r * HEAD_DIM:(r + 1) * HEAD_DIM] for r in heads]
        dYcs = [dY.astype(CDT) for dY in dYs]
        Es = [jnp.exp(ac_c) for ac_c in ac_cs]
        dQs = [(dY * E).astype(CDT) for dY, E in zip(dYs, Es)]
        dHns = [dh_scr[r] for r in heads]
        dHcs = [dHn.astype(CDT) for dHn in dHns]
        aLs = [_rowsel(ac_c, CHUNK - 1) for ac_c in ac_cs]
        exs = [jnp.exp(aL - ac_c) for aL, ac_c in zip(aLs, ac_cs)]
        es = [ex * dt_c for ex, dt_c in zip(exs, dt_cs)]
        Xes = [(X * e).astype(CDT) for X, e in zip(Xs, es)]
        WtdYs = [_tn(W.astype(CDT), dYc) for W, dYc in zip(Ws, dYcs)]
        dWms = [_nt(dYc, Xc) for dYc, Xc in zip(dYcs, Xcs)]
        Qs = [_nt(Cm, Hc) for Hc in Hcs]
        dCps = [_nn(dQ, Hc) for dQ, Hc in zip(dQs, Hcs)]
        dHps = [_tn(dQ, Cm) for dQ in dQs]
        T2s = [_nt(Bm, dHc) for dHc in dHcs]
        dBps = [_nn(Xe, dHc) for Xe, dHc in zip(Xes, dHcs)]
        for r in heads:
            dY, Xr, Wm, dWm, E, ex, e, aL = dYs[r], Xs[r], Ws[r], dWms[r], Es[r], exs[r], es[r], aLs[r]
            ddv = ddv + jnp.where(lane1 == r, jnp.sum(jnp.sum(dY * Xr, axis=1, keepdims=True), axis=0, keepdims=True), 0.0)
            dG = dG + dWm * Ss[r] * dt_rs[r]
            Mm = dWm * Wm
            dac_r = -jnp.sum(Mm, axis=0, keepdims=True)
            ddt_r = jnp.sum(dWm * GSs[r], axis=0, keepdims=True)
            dCm = dCm + dCps[r]
            dBm = dBm + dBps[r]
            eaL = jnp.exp(aL)
            daL = eaL * jnp.sum(jnp.sum(dHns[r] * Hps[r], axis=1, keepdims=True), axis=0, keepdims=True)
            de = jnp.sum(Xr * T2s[r], axis=1, keepdims=True)
            dee = de * e
            daL = daL + jnp.sum(dee, axis=0, keepdims=True)
            dac_c = (jnp.sum(Mm, axis=1, keepdims=True) + jnp.sum(dY * Qs[r], axis=1, keepdims=True) * E
                     - dee + jnp.where(subc == CHUNK - 1, daL, 0.0))
            dh_scr[r] = dHps[r] + eaL * dHns[r]
            dx_ref[:, r * HEAD_DIM:(r + 1) * HEAD_DIM] = dY * _colsel(dv, hhs[r]) + WtdYs[r] + T2s[r] * e
            colp = jnp.where(col == r, dac_c, colp)
            colp = jnp.where(col == SSD_HPG + r, de * ex, colp)
            rowp = jnp.where(sub8 == r, dac_r, rowp)
            rowp = jnp.where(sub8 == SSD_HPG + r, ddt_r, rowp)
        dGc = dG.astype(CDT)
        db_ref[...] = dBm + _tn(dGc, Cm)
        dc_ref[...] = dCm + _nn(dGc, Bm)
        colp_ref[0] = colp
        rowp_ref[0] = rowp
        dd_ref[0] += ddv

    f32 = lambda shape: SDS(shape, F32)
    return _pallas(body, grid=(G, nc),
                   in_specs=[sp["x"], sp["x"], sp["b"], sp["c"], sp["x"], sp["x"], sp["hp"], sp["col"], sp["col"],
                             sp["row"], sp["row"], sp["vec"], sp["ng"]],
                   out_specs=[sp["x"],
                              BS((CHUNK, LANES), lambda g, c: (nc - 1 - c, g)),
                              BS((CHUNK, LANES), lambda g, c: (nc - 1 - c, g)),
                              sp["x"],
                              BS((1, CHUNK, LANES), lambda g, c: (g, nc - 1 - c, 0)),
                              BS((1, HALO, CHUNK), lambda g, c: (g, 0, nc - 1 - c)),
                              BS((1, 1, LANES), lambda g, c: (g, 0, 0)),
                              sp["ng"]],
                   out_shape=[f32((T, DI)), f32((T, G * LANES)), f32((T, G * LANES)), SDS((T, DI), CDT),
                              f32((G, T, LANES)), f32((G, HALO, T)), f32((G, 1, LANES)), f32((1, DI))],
                   scratch_shapes=[pltpu.VMEM((SSD_HPG, HEAD_DIM, LANES), F32)],
                   compiler_params=_params("parallel", "arbitrary"), name=name)(
                       dyn, xbc, xbc, xbc, z, y, hp, dt, ac, dtT, acT, dskip, ng)


HBM = pl.BlockSpec(memory_space=pltpu.HBM)
VMEM = pl.BlockSpec(memory_space=pltpu.VMEM)


def _me():
    return lax.axis_index("x"), lax.axis_index("y"), lax.axis_index("c")


def _rcopy(src, dst, send_sem, recv_sem, to):
    return pltpu.make_async_remote_copy(src_ref=src, dst_ref=dst, send_sem=send_sem, recv_sem=recv_sem,
                                        device_id=to, device_id_type=MESH)


AG_SEMS = 13


def allgather_weights(packs, *, name):
    n = len(packs)

    def body(*refs):
        srcs, outs, (send_sems, recv_sems) = refs[:n], refs[n:2 * n], refs[2 * n:]
        x, y, c = _me()
        k = 2 * x + y
        sibling = (x, y, 1 - c)
        nx, ny = (1 - x, y, c), (x, 1 - y, c)
        kx, ky, kd = 2 * (1 - x) + y, 2 * x + (1 - y), 2 * (1 - x) + (1 - y)
        started = []

        def go(cp):
            cp.start()
            started.append(cp)

        def piece(ref, chip, core, r, Rq):
            return ref.at[chip, core, pl.ds(r * Rq, Rq)]

        for a, (src, out) in enumerate(zip(srcs, outs)):
            Rq = src.shape[1] // 2
            sem = lambda i: (send_sems.at[AG_SEMS * a + i], recv_sems.at[AG_SEMS * a + i])
            for i, (to, r) in enumerate([(nx, 0), (nx, 1), (ny, 0), (ny, 1)]):
                go(_rcopy(src.at[c, pl.ds(r * Rq, Rq)], piece(out, k, c, r, Rq), *sem(i), to))
            go(_rcopy(src, out.at[k], *sem(12), sibling))
        for a, out in enumerate(outs):
            Rq = out.shape[2] // 2
            sem = lambda i: (send_sems.at[AG_SEMS * a + i], recv_sems.at[AG_SEMS * a + i])

            def landed(i, chip, r, frm, relay=None):
                ref = piece(out, chip, c, r, Rq)
                _rcopy(ref, ref, *sem(i), frm).wait_recv()
                if relay is not None:
                    go(_rcopy(ref, ref, *sem(relay[0]), relay[1]))
                go(_rcopy(ref, ref, *sem(6 + i), sibling))

            landed(0, kx, 0, nx, relay=(4, ny))
            landed(3, ky, 1, ny, relay=(5, nx))
            landed(1, kx, 1, nx)
            landed(2, ky, 0, ny)
            landed(4, kd, 0, ny)
            landed(5, kd, 1, nx)
        for a, out in enumerate(outs):
            Rq = out.shape[2] // 2
            sem = lambda i: (send_sems.at[AG_SEMS * a + i], recv_sems.at[AG_SEMS * a + i])
            for i, (chip, r) in enumerate([(kx, 0), (kx, 1), (ky, 0), (ky, 1), (kd, 0), (kd, 1)]):
                ref = piece(out, chip, 1 - c, r, Rq)
                _rcopy(ref, ref, *sem(6 + i), sibling).wait_recv()
            mine = out.at[k]
            _rcopy(mine, mine, *sem(12), sibling).wait_recv()
        for cp in started:
            cp.wait_send()

    return _pallas(body, in_specs=[HBM] * n, out_specs=[HBM] * n,
                   out_shape=[SDS((N_CHIPS,) + p.shape, p.dtype) for p in packs],
                   scratch_shapes=[pltpu.SemaphoreType.DMA((AG_SEMS * n,)), pltpu.SemaphoreType.DMA((AG_SEMS * n,))],
                   name=name)(*packs)


def sibling_swap_half(ps, *, name):
    n = len(ps)

    def body(*refs):
        srcs, outs, (send_sems, recv_sems) = refs[:n], refs[n:2 * n], refs[2 * n:]
        x, y, c = _me()
        cps = [_rcopy(src.at[1 - c], out, send_sems.at[a], recv_sems.at[a], (x, y, 1 - c))
               for a, (src, out) in enumerate(zip(srcs, outs))]
        for cp in cps:
            cp.start()
        for cp in cps:
            cp.wait()

    return _pallas(body, in_specs=[HBM] * n, out_specs=[HBM] * n,
                   out_shape=[SDS(p.shape[1:], p.dtype) for p in ps],
                   scratch_shapes=[pltpu.SemaphoreType.DMA((n,)), pltpu.SemaphoreType.DMA((n,))], name=name)(*ps)


def scatter_relay(s1s, *, name):
    n = len(s1s)

    def body(*refs):
        srcs, outs, (send_sems, recv_sems) = refs[:n], refs[n:2 * n], refs[2 * n:]
        x, y, c = _me()
        kd = 2 * (1 - x) + (1 - y)
        peers = [(1 - x, y, c), (x, 1 - y, c)]
        sends = []
        for a, (src, out) in enumerate(zip(srcs, outs)):
            Rq = src.shape[1] // 2
            for r, to in enumerate(peers):
                sends.append(_rcopy(src.at[kd, pl.ds(r * Rq, Rq)], out.at[r], send_sems.at[2 * a + r],
                                    recv_sems.at[2 * a + r], to))
        for cp in sends:
            cp.start()
        for a, out in enumerate(outs):
            for r, frm in enumerate(peers):
                slot = out.at[r]
                _rcopy(slot, slot, send_sems.at[2 * a + r], recv_sems.at[2 * a + r], frm).wait_recv()
        for cp in sends:
            cp.wait_send()

    return _pallas(body, in_specs=[HBM] * n, out_specs=[HBM] * n,
                   out_shape=[SDS((2, p.shape[1] // 2, p.shape[2]), p.dtype) for p in s1s],
                   scratch_shapes=[pltpu.SemaphoreType.DMA((2 * n,)), pltpu.SemaphoreType.DMA((2 * n,))],
                   name=name)(*s1s)


def add_relayed(s1, relay, yx_idx, *, name):
    _, Rq, W = relay.shape
    tr = _row_tile(Rq)
    nb = Rq // tr

    def body(idx_ref, s_ref, t_ref, o_ref):
        o_ref[...] = (s_ref[...].astype(F32) + t_ref[...].astype(F32)).astype(o_ref.dtype)

    gs = pltpu.PrefetchScalarGridSpec(
        num_scalar_prefetch=1, grid=(2, nb),
        in_specs=[BS((1, tr, W), lambda p, i, idx: (idx[p], p * nb + i, 0)), BS((1, tr, W), lambda p, i, idx: (p, i, 0))],
        out_specs=BS((1, tr, W), lambda p, i, idx: (p, i, 0)))
    return _pallas(body, grid_spec=gs, out_shape=SDS(relay.shape, relay.dtype),
                   compiler_params=_params("parallel", "parallel"), name=name)(yx_idx, s1, relay)


def chip_scatter(s1s, us, *, name):
    n = len(s1s)

    def body(*refs):
        srcs, adds, outs, (send_sems, recv_sems) = refs[:n], refs[n:2 * n], refs[2 * n:3 * n], refs[3 * n:]
        x, y, c = _me()
        k = 2 * x + y
        kx, ky = 2 * (1 - x) + y, 2 * x + (1 - y)
        nx, ny = (1 - x, y, c), (x, 1 - y, c)
        sends = []
        for a, (src, u, out) in enumerate(zip(srcs, adds, outs)):
            Rq = src.shape[1] // 2
            lo, hi = pl.ds(0, Rq), pl.ds(Rq, Rq)
            sem = lambda i: (send_sems.at[4 * a + i], recv_sems.at[4 * a + i])
            sends += [_rcopy(src.at[kx, lo], out.at[k, lo], *sem(0), nx), _rcopy(u.at[1], out.at[k, hi], *sem(1), nx),
                      _rcopy(u.at[0], out.at[k, lo], *sem(2), ny), _rcopy(src.at[ky, hi], out.at[k, hi], *sem(3), ny)]
        for cp in sends:
            cp.start()
        for a, out in enumerate(outs):
            Rq = out.shape[1] // 2
            sem = lambda i: (send_sems.at[4 * a + i], recv_sems.at[4 * a + i])
            for i, (chip, rows, frm) in enumerate([(kx, pl.ds(0, Rq), nx), (kx, pl.ds(Rq, Rq), nx),
                                                   (ky, pl.ds(0, Rq), ny), (ky, pl.ds(Rq, Rq), ny)]):
                slot = out.at[chip, rows]
                _rcopy(slot, slot, *sem(i), frm).wait_recv()
        for cp in sends:
            cp.wait_send()

    return _pallas(body, in_specs=[HBM] * (2 * n), out_specs=[HBM] * n, out_shape=[SDS(p.shape, p.dtype) for p in s1s],
                   scratch_shapes=[pltpu.SemaphoreType.DMA((4 * n,)), pltpu.SemaphoreType.DMA((4 * n,))],
                   name=name)(*s1s, *us)


def sibling_pair(r2s, *, name):
    n = len(r2s)

    def body(*refs):
        srcs, outs, (send_sems, recv_sems) = refs[:n], refs[n:2 * n], refs[2 * n:]
        x, y, c = _me()
        cps = [_rcopy(src.at[c], out.at[c], send_sems.at[a], recv_sems.at[a], (x, y, 1 - c))
               for a, (src, out) in enumerate(zip(srcs, outs))]
        for cp in cps:
            cp.start()
        for a, out in enumerate(outs):
            theirs = out.at[1 - c]
            _rcopy(theirs, theirs, send_sems.at[a], recv_sems.at[a], (x, y, 1 - c)).wait_recv()
        for cp in cps:
            cp.wait_send()

    return _pallas(body, in_specs=[HBM] * n, out_specs=[HBM] * n, out_shape=[SDS(p.shape, p.dtype) for p in r2s],
                   input_output_aliases={a: a for a in range(n)},
                   scratch_shapes=[pltpu.SemaphoreType.DMA((n,)), pltpu.SemaphoreType.DMA((n,))], name=name)(*r2s)


def allgather_small(v, *, name):
    R, W = v.shape

    def body(v_ref, out_ref, sum_ref, send_sems, recv_sems):
        x, y, c = _me()
        me = 4 * x + 2 * y + c
        out_ref[me] = v_ref[...]
        peers = []
        for m in range(1, N_DEV):
            px = 1 - x if m & 4 else x
            py = 1 - y if m & 2 else y
            pc = 1 - c if m & 1 else c
            peers.append((px, py, pc))
        sends = [_rcopy(v_ref, out_ref.at[me], send_sems.at[j], recv_sems.at[j], p) for j, p in enumerate(peers)]
        for cp in sends:
            cp.start()
        for j, (px, py, pc) in enumerate(peers):
            slot = out_ref.at[4 * px + 2 * py + pc]
            _rcopy(slot, slot, send_sems.at[j], recv_sems.at[j], (px, py, pc)).wait_recv()
        for cp in sends:
            cp.wait_send()
        acc = out_ref[0]
        for d in range(1, N_DEV):
            acc = acc + out_ref[d]
        sum_ref[...] = acc

    return _pallas(body, in_specs=[VMEM], out_specs=[VMEM, VMEM],
                   out_shape=[SDS((N_DEV, R, W), v.dtype), SDS((R, W), v.dtype)],
                   scratch_shapes=[pltpu.SemaphoreType.DMA((N_DEV - 1,)), pltpu.SemaphoreType.DMA((N_DEV - 1,))],
                   name=name)(v)


def _row_tile(R, cap=512):
    for step in (32, 16, 8):
        fits = [t for t in range(step, cap + 1, step) if R % t == 0]
        if fits:
            return fits[-1]
    raise AssertionError(R)


def add_halves(p, recv, c_idx, *, name):
    _, n, Rh, W = p.shape
    tr = _row_tile(Rh)

    def body(c_ref, p_ref, r_ref, o_ref):
        o_ref[...] = (p_ref[0].astype(F32) + r_ref[...].astype(F32)).astype(o_ref.dtype)

    gs = pltpu.PrefetchScalarGridSpec(
        num_scalar_prefetch=1, grid=(n, Rh // tr),
        in_specs=[BS((1, 1, tr, W), lambda s, i, c_ref: (c_ref[0], s, i, 0)), BS((1, tr, W), lambda s, i, c_ref: (s, i, 0))],
        out_specs=BS((1, tr, W), lambda s, i, c_ref: (s, i, 0)))
    return _pallas(body, grid_spec=gs, out_shape=SDS((n, Rh, W), p.dtype),
                   compiler_params=_params("parallel", "parallel"), name=name)(c_idx, p, recv)


def sum_slots(s1, got, idx, *, name):
    n, Rh, W = got.shape
    tr = _row_tile(Rh)

    def body(idx_ref, s_ref, gx, gy, o_ref):
        o_ref[0] = s_ref[0].astype(F32) + gx[0].astype(F32) + gy[0].astype(F32)

    slot = lambda d: BS((1, tr, W), lambda i, idx: (idx[d], i, 0))
    gs = pltpu.PrefetchScalarGridSpec(
        num_scalar_prefetch=1, grid=(Rh // tr,), in_specs=[slot(0), slot(1), slot(2)],
        out_specs=BS((1, tr, W), lambda i, idx: (idx[3], i, 0)))
    return _pallas(body, grid_spec=gs, out_shape=SDS((2, Rh, W), F32),
                   compiler_params=_params("parallel"), name=name)(idx, s1, got, got)


BIG = [("ffn1_w_gu", True), ("ffn1_w_down", False), ("ffn2_w_gu", True), ("ffn2_w_down", False),
       ("sb_w_qkv", True), ("sb_w_o", False), ("ssd_w_in", True), ("ssd_w_out", False),
       ("sc_w_in", True), ("sc_w_out", False)]
PACK_ALIGN = 64


def _big_layout(shards):
    groups = {}
    for name, cols in BIG:
        L, K, n = shards[name].shape
        items, off = groups.setdefault(n, ([], 0))
        items.append((name, cols, (L, K, n), off, L * K))
        groups[n] = (items, off + L * K)
    return {w: (items, -(-rows // PACK_ALIGN) * PACK_ALIGN) for w, (items, rows) in groups.items()}


def _pack_rows(pieces, total, axis):
    used = sum(p.shape[axis] for p in pieces)
    if total > used:
        pad_shape = list(pieces[0].shape)
        pad_shape[axis] = total - used
        pieces = pieces + [jnp.zeros(pad_shape, pieces[0].dtype)]
    return jnp.concatenate(pieces, axis=axis) if len(pieces) > 1 else pieces[0]


def _gather_big(shards, tag):
    layout = _big_layout(shards)
    packs = [_pack_rows([shards[n].astype(CDT).reshape(-1, w) for n, *_ in items], total, 0).reshape(2, total // 2, w)
             for w, (items, total) in layout.items()]
    fulls = allgather_weights(packs, name=f"{tag}_allgather")
    out = {}
    for (w, (items, total)), full in zip(layout.items(), fulls):
        full = full.reshape(N_CHIPS, total, w)
        for name, cols, (L, K, n), off, rows in items:
            layers = [full[:, off + l * K:off + (l + 1) * K] for l in range(L)]
            if cols:
                out[name] = [p.transpose(1, 0, 2).reshape(K, N_CHIPS * n) for p in layers]
            else:
                out[name] = [p.reshape(N_CHIPS * K, n) for p in layers]
    return out


def _reduce_big(grads, shards, tag):
    layout = _big_layout(shards)
    ps = []
    for w, (items, total) in layout.items():
        pieces = []
        for name, cols, (L, K, n), off, rows in items:
            for g in grads[name]:
                if cols:
                    g = g.reshape(K, N_CHIPS, n).transpose(1, 0, 2)
                else:
                    g = g.reshape(N_CHIPS, K, n)
                pieces.append(g.astype(CDT))
        pack = _pack_rows(pieces, total, 1)
        ps.append(pack.reshape(N_CHIPS, 2, total // 2, w).transpose(1, 0, 2, 3))
    ix, iy = lax.axis_index("x").astype(jnp.int32), lax.axis_index("y").astype(jnp.int32)
    c_core = lax.axis_index("c").astype(jnp.int32)
    k_chip, k_x, k_y = 2 * ix + iy, 2 * (1 - ix) + iy, 2 * ix + (1 - iy)
    recvs = sibling_swap_half(ps, name=f"{tag}_sibling_swap")
    s1s = [add_halves(p, r, c_core.reshape(1), name=f"{tag}_add_halves_{p.shape[-1]}") for p, r in zip(ps, recvs)]
    relays = scatter_relay(s1s, name=f"{tag}_scatter_relay")
    us = [add_relayed(s1, t, jnp.stack([k_y, k_x]), name=f"{tag}_add_relayed_{s1.shape[-1]}") for s1, t in zip(s1s, relays)]
    gots = chip_scatter(s1s, us, name=f"{tag}_chip_scatter")
    r2s = [sum_slots(s1, got, jnp.stack([k_chip, k_x, k_y, c_core]), name=f"{tag}_sum_slots_{s1.shape[-1]}")
           for s1, got in zip(s1s, gots)]
    reds = sibling_pair(r2s, name=f"{tag}_sibling_pair")
    out = {}
    for (w, (items, total)), red in zip(layout.items(), reds):
        red = red.reshape(total, w)
        for name, cols, shape, off, rows in items:
            out[name] = red[off:off + rows].reshape(shape)
    return out


def _pad_lanes(v, width=LANES):
    return jnp.pad(v, ((0, 0), (0, width - v.shape[1])))


def _small_pack(vecs):
    flat, meta, off = [], [], 0
    for v in vecs:
        flat.append(v.reshape(-1))
        meta.append((off, v.size, v.shape))
        off += v.size
    total = -(-off // (8 * LANES)) * 8 * LANES
    flat.append(jnp.zeros((total - off,), F32))
    return jnp.concatenate(flat).reshape(-1, LANES), meta


def _small_unpack(pack, meta):
    flat = pack.reshape(-1)
    return [flat[off:off + size].reshape(shape) for off, size, shape in meta]


def _ffn_fwd(x, g, wgu, wd, tag):
    h = rms_fwd(x, g, name=f"{tag}_norm")
    gate, up, a = ffn_up(h, wgu, name=f"{tag}_gu")
    return matmul(a, wd, res=x, scale=0.5, name=f"{tag}_down"), (x, h, gate, up, a)


def _ffn_bwd(dout, saved, g, wgu, wd, tag):
    x, h, gate, up, a = saved
    F = wd.shape[0]
    dwd = matmul(a, dout, ta=True, out_dtype=CDT, scale=0.5, name=f"{tag}_dwd")
    dgate, dup = ffn_dact(dout, wd, gate, up, 0.5, name=f"{tag}_dact")
    dwgu = jnp.concatenate([matmul(h, dgate, ta=True, out_dtype=CDT, name=f"{tag}_dwg"),
                            matmul(h, dup, ta=True, out_dtype=CDT, name=f"{tag}_dwu")], axis=1)
    dh = matmul(dgate, wgu, tb=True, name=f"{tag}_dh_g")
    dh = matmul(dup, wgu, tb=True, b_k_start=F, res=dh, name=f"{tag}_dh_u")
    dx, dg = rms_bwd(dh, x, g, dout, name=f"{tag}_dnorm")
    return dx, dg, dwgu, dwd


def _sb_mixer_fwd(x, g, wqkv, wo, cfg, tag):
    h = rms_fwd(x, g, name=f"{tag}_norm")
    qkv = matmul(h, wqkv, out_dtype=CDT, name=f"{tag}_qkv")
    o, tab = sb_fwd(qkv, cfg.sb_heads, name=f"{tag}_attn")
    return matmul(o, wo, res=x, name=f"{tag}_out"), (x, h, qkv, tab, o)


def _sb_mixer_bwd(dout, saved, g, wqkv, wo, cfg, tag):
    x, h, qkv, tab, o = saved
    dwo = matmul(o, dout, ta=True, out_dtype=CDT, name=f"{tag}_dwo")
    do = matmul(dout, wo, tb=True, out_dtype=CDT, name=f"{tag}_do")
    dqkv = jnp.concatenate(sb_bwd(qkv, do, tab, cfg.sb_heads, name=f"{tag}_dattn"), axis=1)
    dwqkv = matmul(h, dqkv, ta=True, out_dtype=CDT, name=f"{tag}_dwqkv")
    dh = matmul(dqkv, wqkv, tb=True, name=f"{tag}_dh")
    dx, dg = rms_bwd(dh, x, g, dout, name=f"{tag}_dnorm")
    return dx, dg, dwqkv, dwo


def _sc_mixer_fwd(x, g, win, convw, wout, tag):
    h = rms_fwd(x, g, name=f"{tag}_norm")
    proj = matmul(h, win, name=f"{tag}_in")
    m = sc_fwd(proj, convw, name=f"{tag}_conv")
    return matmul(m, wout, res=x, name=f"{tag}_out"), (x, h, proj, m)


def _sc_mixer_bwd(dout, saved, g, win, convw, wout, tag):
    x, h, proj, m = saved
    dwout = matmul(m, dout, ta=True, out_dtype=CDT, name=f"{tag}_dwout")
    dm = matmul(dout, wout, tb=True, name=f"{tag}_dm")
    dproj, dconvw = sc_bwd(dm, proj, convw, name=f"{tag}_dconv")
    dwin = matmul(h, dproj, ta=True, out_dtype=CDT, name=f"{tag}_dwin")
    dh = matmul(dproj, win, tb=True, name=f"{tag}_dh")
    dx, dg = rms_bwd(dh, x, g, dout, name=f"{tag}_dnorm")
    return dx, dg, dwin, dconvw[:convw.shape[0]], dwout


def _ssd_split_w(win, cfg):
    DI, CV = cfg.DI, cfg.conv_dim
    return win[:, :DI], win[:, DI:DI + CV], _pad_lanes(win[:, DI + CV:])


def _ssd_mixer_fwd(x, g, win, convw, convb, dtb, alog, dskip, ng, wout, cfg, tag):
    wz, wx, wdt = _ssd_split_w(win, cfg)
    h = rms_fwd(x, g, name=f"{tag}_norm")
    z = matmul(h, wz, name=f"{tag}_z")
    xpre = matmul(h, wx, name=f"{tag}_xbc")
    dtpre = matmul(h, wdt, name=f"{tag}_dt")
    pre, xbc = ssd_conv_fwd(xpre, convw, convb, name=f"{tag}_conv")
    dt, ac, dtT, acT = ssd_prep_fwd(dtpre, dtb, alog, name=f"{tag}_prep")
    y, yn, hp = ssd_chunk_fwd(xbc, z, dt, ac, dtT, acT, dskip, ng, cfg, name=f"{tag}_scan")
    out = matmul(yn, wout, res=x, name=f"{tag}_out")
    return out, (x, h, z, xpre, dtpre, pre, xbc, dt, ac, dtT, acT, y, yn, hp)


def _ssd_mixer_bwd(dout, saved, g, win, convw, convb, dtb, alog, dskip, ng, wout, cfg, tag):
    x, h, z, xpre, dtpre, pre, xbc, dt, ac, dtT, acT, y, yn, hp = saved
    T = x.shape[0]
    NH, G = cfg.NH, cfg.G
    wz, wx, wdt = _ssd_split_w(win, cfg)
    dwout = matmul(yn, dout, ta=True, out_dtype=CDT, name=f"{tag}_dwout")
    dyn = matmul(dout, wout, tb=True, name=f"{tag}_dyn")
    dxs, db, dc, dz, colp, rowp, dd, dng = ssd_chunk_bwd(dyn, xbc, z, y, hp, dt, ac, dtT, acT, dskip, ng, cfg,
                                                        name=f"{tag}_dscan")
    col = lambda lo: _pad_lanes(colp[:, :, lo:lo + SSD_HPG].transpose(1, 0, 2).reshape(T, NH))
    row = lambda lo: jnp.pad(rowp[:, lo:lo + SSD_HPG, :].reshape(NH, T), ((0, LANES - NH), (0, 0)))
    ddtpre, dalog, ddtb = ssd_prep_bwd(col(0), col(SSD_HPG), row(0), row(SSD_HPG), dtpre, dtb, alog, name=f"{tag}_dprep")
    dxpre, dconvw, dconvb = ssd_conv_bwd(dxs, db, dc, pre, xpre, convw, name=f"{tag}_dconv")
    dwz = matmul(h, dz, ta=True, out_dtype=CDT, name=f"{tag}_dwz")
    dwx = matmul(h, dxpre, ta=True, out_dtype=CDT, name=f"{tag}_dwx")
    dwdt = matmul(h, ddtpre, ta=True, out_dtype=CDT, name=f"{tag}_dwdt")
    dh = matmul(dz, wz, tb=True, name=f"{tag}_dh_z")
    dh = matmul(dxpre, wx, tb=True, res=dh, name=f"{tag}_dh_x")
    dh = matmul(ddtpre, wdt, tb=True, res=dh, name=f"{tag}_dh_dt")
    dx, dg = rms_bwd(dh, x, g, dout, name=f"{tag}_dnorm")
    dwin = jnp.concatenate([dwz, dwx, dwdt[:, :NH]], axis=1)
    small = dict(conv_w=dconvw[:convw.shape[0]], conv_b=dconvb, dt_bias=ddtb[:, :NH], a_log=dalog[:, :NH],
                 d=dd[:, 0, :SSD_HPG].reshape(1, NH), norm=dng)
    return dx, dg, dwin, dwout, small


INPUT_NAMES = ['x', 'ffn1_norm', 'ffn1_w_gu', 'ffn1_w_down', 'mix_norm', 'ffn2_norm', 'ffn2_w_gu', 'ffn2_w_down',
               'sb_w_qkv', 'sb_w_o', 'ssd_w_in', 'ssd_conv_w', 'ssd_conv_b', 'ssd_dt_bias', 'ssd_a_log', 'ssd_d',
               'ssd_norm', 'ssd_w_out', 'sc_w_in', 'sc_conv_w', 'sc_w_out', 'final_norm']
WEIGHT_NAMES = INPUT_NAMES[1:]
SHARDED_SMALL = ("ssd_conv_w", "sc_conv_w")


def train_step(a, cfg):
    x = a["x"][0]
    tgt = a["loss_target"][0]
    big_names = [n for n, _ in BIG]
    k_chip = 2 * lax.axis_index("x") + lax.axis_index("y")

    w = _gather_big({n: a[n] for n in big_names}, "w")
    conv_pack, conv_meta = _small_pack([a[n] for n in SHARDED_SMALL])
    conv_all, _ = allgather_small(conv_pack, name="conv_w_allgather")
    conv_full = {}
    for i, n in enumerate(SHARDED_SMALL):
        per_chip = [_small_unpack(conv_all[2 * kk], conv_meta)[i] for kk in range(N_CHIPS)]
        conv_full[n] = jnp.concatenate(per_chip, axis=-1)

    def vec(name, i):
        return a[name][i:i + 1]

    ssd_small = lambda j: (a["ssd_conv_b"][j:j + 1], _pad_lanes(a["ssd_dt_bias"][j:j + 1]),
                           _pad_lanes(a["ssd_a_log"][j:j + 1]), _pad_lanes(a["ssd_d"][j:j + 1]), a["ssd_norm"][j:j + 1])

    saved = []
    for i in range(cfg.depth):
        kind, j = i % 3, i // 3
        x, s1 = _ffn_fwd(x, vec("ffn1_norm", i), w["ffn1_w_gu"][i], w["ffn1_w_down"][i], f"l{i}_ffn1")
        if kind == 0:
            x, s2 = _sb_mixer_fwd(x, vec("mix_norm", i), w["sb_w_qkv"][j], w["sb_w_o"][j], cfg, f"l{i}_sb")
        elif kind == 1:
            x, s2 = _ssd_mixer_fwd(x, vec("mix_norm", i), w["ssd_w_in"][j], conv_full["ssd_conv_w"][j], *ssd_small(j),
                                   w["ssd_w_out"][j], cfg, f"l{i}_ssd")
        else:
            x, s2 = _sc_mixer_fwd(x, vec("mix_norm", i), w["sc_w_in"][j], conv_full["sc_conv_w"][j], w["sc_w_out"][j],
                                  f"l{i}_sc")
        x, s3 = _ffn_fwd(x, vec("ffn2_norm", i), w["ffn2_w_gu"][i], w["ffn2_w_down"][i], f"l{i}_ffn2")
        saved.append((s1, s2, s3))
    loss_part, dx, d_final = loss_head(x, a["final_norm"].reshape(1, -1), tgt, name="loss_head")

    gbig = {n: [None] * a[n].shape[0] for n in big_names}
    gsmall = {n: [None] * a[n].shape[0] for n in ("ffn1_norm", "mix_norm", "ffn2_norm", "ssd_conv_w", "ssd_conv_b",
                                                   "ssd_dt_bias", "ssd_a_log", "ssd_d", "ssd_norm", "sc_conv_w")}
    for i in reversed(range(cfg.depth)):
        kind, j = i % 3, i // 3
        s1, s2, s3 = saved[i]
        dx, gsmall["ffn2_norm"][i], gbig["ffn2_w_gu"][i], gbig["ffn2_w_down"][i] = _ffn_bwd(
            dx, s3, vec("ffn2_norm", i), w["ffn2_w_gu"][i], w["ffn2_w_down"][i], f"l{i}_ffn2")
        if kind == 0:
            dx, gsmall["mix_norm"][i], gbig["sb_w_qkv"][j], gbig["sb_w_o"][j] = _sb_mixer_bwd(
                dx, s2, vec("mix_norm", i), w["sb_w_qkv"][j], w["sb_w_o"][j], cfg, f"l{i}_sb")
        elif kind == 1:
            dx, gsmall["mix_norm"][i], gbig["ssd_w_in"][j], gbig["ssd_w_out"][j], sm = _ssd_mixer_bwd(
                dx, s2, vec("mix_norm", i), w["ssd_w_in"][j], conv_full["ssd_conv_w"][j], *ssd_small(j),
                w["ssd_w_out"][j], cfg, f"l{i}_ssd")
            for key, val in sm.items():
                gsmall["ssd_" + key][j] = val
        else:
            dx, gsmall["mix_norm"][i], gbig["sc_w_in"][j], gsmall["sc_conv_w"][j], gbig["sc_w_out"][j] = _sc_mixer_bwd(
                dx, s2, vec("mix_norm", i), w["sc_w_in"][j], conv_full["sc_conv_w"][j], w["sc_w_out"][j], f"l{i}_sc")
        dx, gsmall["ffn1_norm"][i], gbig["ffn1_w_gu"][i], gbig["ffn1_w_down"][i] = _ffn_bwd(
            dx, s1, vec("ffn1_norm", i), w["ffn1_w_gu"][i], w["ffn1_w_down"][i], f"l{i}_ffn1")
    grad_x = dx[None]

    red_big = _reduce_big(gbig, {n: a[n] for n in big_names}, "g")
    small_names = list(gsmall) + ["final_norm"]
    small_vals = [jnp.concatenate([p.reshape((1,) + a[n].shape[1:-1] + (-1,)) for p in gsmall[n]], axis=0)
                  for n in gsmall] + [d_final.reshape(-1)]
    gpack, gmeta = _small_pack(small_vals + [loss_part[:, :1]])
    _, gsum = allgather_small(gpack, name="small_allreduce")
    *small_red, loss = _small_unpack(gsum, gmeta)
    red = dict(red_big)
    for n, v in zip(small_names, small_red):
        if n in SHARDED_SMALL:
            width = a[n].shape[-1]
            v = lax.dynamic_slice_in_dim(v, k_chip * width, width, axis=v.ndim - 1)
        red[n] = v.reshape(a[n].shape)

    delta, new_m, new_v = {}, {}, {}
    two_d = lambda t: t.reshape(-1, t.shape[-1])
    for n in big_names:
        d_, m_, v_ = adamw(two_d(a[n]), two_d(red[n]), two_d(a["m_" + n]), two_d(a["v_" + n]), name=f"adamw_{n}")
        delta[n], new_m[n], new_v[n] = (t.reshape(a[n].shape) for t in (d_, m_, v_))
    rest = [n for n in WEIGHT_NAMES if n not in big_names]
    packs = [_small_pack([src[n] for n in rest]) for src in
             (a, red, {n: a["m_" + n] for n in rest}, {n: a["v_" + n] for n in rest})]
    outs = adamw(*[p for p, _ in packs], name="adamw_small")
    for dst, o in zip((delta, new_m, new_v), outs):
        for n, t in zip(rest, _small_unpack(o, packs[0][1])):
            dst[n] = t
    return (loss.reshape(()), grad_x, *[red[n] for n in WEIGHT_NAMES], *[delta[n] for n in WEIGHT_NAMES],
            *[new_m[n] for n in WEIGHT_NAMES], *[new_v[n] for n in WEIGHT_NAMES])


def kernel(x, ffn1_norm, ffn1_w_gu, ffn1_w_down, mix_norm, ffn2_norm, ffn2_w_gu, ffn2_w_down, sb_w_qkv, sb_w_o, ssd_w_in, ssd_conv_w, ssd_conv_b, ssd_dt_bias, ssd_a_log, ssd_d, ssd_norm, ssd_w_out, sc_w_in, sc_conv_w, sc_w_out, final_norm, loss_target, m_ffn1_norm, m_ffn1_w_gu, m_ffn1_w_down, m_mix_norm, m_ffn2_norm, m_ffn2_w_gu, m_ffn2_w_down, m_sb_w_qkv, m_sb_w_o, m_ssd_w_in, m_ssd_conv_w, m_ssd_conv_b, m_ssd_dt_bias, m_ssd_a_log, m_ssd_d, m_ssd_norm, m_ssd_w_out, m_sc_w_in, m_sc_conv_w, m_sc_w_out, m_final_norm, v_ffn1_norm, v_ffn1_w_gu, v_ffn1_w_down, v_mix_norm, v_ffn2_norm, v_ffn2_w_gu, v_ffn2_w_down, v_sb_w_qkv, v_sb_w_o, v_ssd_w_in, v_ssd_conv_w, v_ssd_conv_b, v_ssd_dt_bias, v_ssd_a_log, v_ssd_d, v_ssd_norm, v_ssd_w_out, v_sc_w_in, v_sc_conv_w, v_sc_w_out, v_final_norm):
    args = dict(locals())
    return train_step(args, Cfg())
```

```python
import jax
import jax.numpy as jnp
from jax import lax
from jax.experimental import pallas as pl
from jax.experimental.pallas import tpu as pltpu

F32 = jnp.float32
CDT = jnp.bfloat16
HI = lax.Precision.HIGHEST
BS = pl.BlockSpec
SDS = jax.ShapeDtypeStruct
MESH = pl.DeviceIdType.MESH

RMS_EPS = 1e-6
ADAM_LR, ADAM_B1, ADAM_B2, ADAM_EPS, ADAM_WD, ADAM_STEP = 0.001, 0.9, 0.999, 1e-08, 0.01, 10
V7X_VMEM_LIMIT_BYTES = 56 * 1024 * 1024
LANES = 128
CHUNK = 128
HEAD_DIM = 64
SSD_HPG = 4
N_CHIPS = 4
N_DEV = 8


class Cfg:
    def __init__(self, d_model=1024, d_ff=2816, depth=4, sb_heads=16, ssd_groups=8, ssd_state=128,
                 ssd_conv=4, sc_width=3):
        self.D, self.F, self.depth = d_model, d_ff, depth
        self.sb_heads = sb_heads
        self.DI = 2 * d_model
        self.G = ssd_groups
        self.NH = self.DI // HEAD_DIM
        assert self.NH == SSD_HPG * self.G and ssd_state == LANES and sb_heads * HEAD_DIM == d_model
        self.conv_dim = self.DI + 2 * self.G * ssd_state
        self.in_dim = self.DI + self.conv_dim + self.NH
        self.ssd_conv, self.sc_width = ssd_conv, sc_width
        self.n_sb, self.n_ssd, self.n_sc = (depth + 2) // 3, (depth + 1) // 3, depth // 3


def _pallas(body, **kw):
    return pl.pallas_call(body, **kw)


def _params(*sem):
    return pltpu.CompilerParams(dimension_semantics=sem, vmem_limit_bytes=V7X_VMEM_LIMIT_BYTES)


def _tile(n, cap):
    if n <= cap:
        return n
    best = None
    for m in range(1, n // LANES + 1):
        t = m * LANES
        if n % t == 0 and t <= cap:
            best = t
    assert best is not None, (n, cap)
    return best


def _sigmoid(x):
    return 1.0 / (1.0 + jnp.exp(-x))


def _softplus_neg_abs(x):
    return jnp.log(1.0 + jnp.exp(-jnp.abs(x)))


def _colsel(blk, idx):
    lane = lax.broadcasted_iota(jnp.int32, blk.shape, 1)
    return jnp.sum(jnp.where(lane == idx, blk, 0.0), axis=1, keepdims=True)


def _rowsel(blk, idx):
    sub = lax.broadcasted_iota(jnp.int32, blk.shape, 0)
    return jnp.sum(jnp.where(sub == idx, blk, 0.0), axis=0, keepdims=True)


def _nt(a, b):
    return lax.dot_general(a, b, (((1,), (1,)), ((), ())), preferred_element_type=F32)


def _tn(a, b):
    return lax.dot_general(a, b, (((0,), (0,)), ((), ())), preferred_element_type=F32)


def _nn(a, b):
    return jnp.dot(a, b, preferred_element_type=F32)


def matmul(a, b, *, name, ta=False, tb=False, out_dtype=F32, res=None, scale=1.0, tm=1408, tn=1408, tk=2048,
           b_k_start=0):
    M, K = (a.shape[1], a.shape[0]) if ta else a.shape
    N, K2 = b.shape if tb else (b.shape[1], b.shape[0])
    assert K == K2 or (tb and b_k_start + K <= K2), (a.shape, b.shape, ta, tb)
    tm, tn, tk = _tile(M, tm), _tile(N, tn), _tile(K, tk)
    nk = K // tk
    assert b_k_start % tk == 0
    kb0 = b_k_start // tk
    dn = (((0 if ta else 1,), (1 if tb else 0,)), ((), ()))

    def body(*refs):
        a_ref, b_ref = refs[:2]
        r_ref = refs[2] if res is not None else None
        o_ref = refs[3 if res is not None else 2]
        part = lax.dot_general(a_ref[...].astype(CDT), b_ref[...].astype(CDT), dn, preferred_element_type=F32)

        def finish(total):
            o = total * scale
            if res is not None:
                o = o + r_ref[...].astype(F32)
            o_ref[...] = o.astype(o_ref.dtype)

        if nk == 1:
            finish(part)
            return
        acc = refs[-1]
        k = pl.program_id(2)

        @pl.when(k == 0)
        def _():
            acc[...] = part

        @pl.when(jnp.logical_and(k > 0, k < nk - 1))
        def _():
            acc[...] += part

        @pl.when(k == nk - 1)
        def _():
            finish(acc[...] + part)

    a_spec = BS((tk, tm), lambda i, j, k: (k, i)) if ta else BS((tm, tk), lambda i, j, k: (i, k))
    b_spec = BS((tn, tk), lambda i, j, k: (j, k + kb0)) if tb else BS((tk, tn), lambda i, j, k: (k, j))
    o_spec = BS((tm, tn), lambda i, j, k: (i, j))
    ins, specs = [a, b], [a_spec, b_spec]
    if res is not None:
        ins.append(res)
        specs.append(o_spec)
    return _pallas(body, grid=(M // tm, N // tn, nk), in_specs=specs, out_specs=o_spec,
                   out_shape=SDS((M, N), out_dtype),
                   scratch_shapes=[pltpu.VMEM((tm, tn), F32)] if nk > 1 else [],
                   compiler_params=_params("parallel", "parallel", "arbitrary"), name=name)(*ins)


def rms_fwd(x, g, *, name):
    T, D = x.shape
    tm = _tile(T, 512)

    def body(x_ref, g_ref, o_ref):
        xv = x_ref[...]
        r = lax.rsqrt(jnp.mean(xv * xv, axis=-1, keepdims=True) + RMS_EPS)
        o_ref[...] = (xv * r * g_ref[...]).astype(o_ref.dtype)

    return _pallas(body, grid=(T // tm,), in_specs=[BS((tm, D), lambda i: (i, 0)), BS((1, D), lambda i: (0, 0))],
                   out_specs=BS((tm, D), lambda i: (i, 0)), out_shape=SDS((T, D), CDT),
                   compiler_params=_params("parallel"), name=name)(x, g)


def rms_bwd(dh, x, g, dres, *, name):
    T, D = x.shape
    tm = _tile(T, 512)

    def body(dh_ref, x_ref, g_ref, r_ref, dx_ref, dg_ref):
        i = pl.program_id(0)
        xv = x_ref[...]
        r = lax.rsqrt(jnp.mean(xv * xv, axis=-1, keepdims=True) + RMS_EPS)
        xhat = xv * r
        dhv = dh_ref[...].astype(F32)
        dxh = dhv * g_ref[...]
        m = jnp.mean(dxh * xhat, axis=-1, keepdims=True)
        dx_ref[...] = r * (dxh - xhat * m) + r_ref[...]

        @pl.when(i == 0)
        def _():
            dg_ref[...] = jnp.zeros_like(dg_ref)

        dg_ref[...] += jnp.sum(dhv * xhat, axis=0, keepdims=True)

    row = BS((tm, D), lambda i: (i, 0))
    vec = BS((1, D), lambda i: (0, 0))
    return _pallas(body, grid=(T // tm,), in_specs=[row, row, vec, row], out_specs=[row, vec],
                   out_shape=[SDS((T, D), F32), SDS((1, D), F32)],
                   compiler_params=_params("arbitrary"), name=name)(dh, x, g, dres)


def ffn_up(h, wgu, *, name):
    T, D = h.shape
    F = wgu.shape[1] // 2
    tm, tn = _tile(T, 512), _tile(F, 1408)
    nj = F // tn

    def body(h_ref, wg_ref, wu_ref, g_ref, u_ref, a_ref):
        hv = h_ref[...].astype(CDT)
        g = jnp.dot(hv, wg_ref[...].astype(CDT), preferred_element_type=F32)
        u = jnp.dot(hv, wu_ref[...].astype(CDT), preferred_element_type=F32)
        g_ref[...] = g.astype(g_ref.dtype)
        u_ref[...] = u.astype(u_ref.dtype)
        a_ref[...] = (g * _sigmoid(g) * u).astype(a_ref.dtype)

    out = BS((tm, tn), lambda j, i: (i, j))
    return _pallas(body, grid=(nj, T // tm),
                   in_specs=[BS((tm, D), lambda j, i: (i, 0)), BS((D, tn), lambda j, i: (0, j)),
                             BS((D, tn), lambda j, i: (0, j + nj))],
                   out_specs=[out, out, out], out_shape=[SDS((T, F), CDT)] * 3,
                   compiler_params=_params("parallel", "parallel"), name=name)(h, wgu, wgu)


def ffn_dact(dout, wd, g, u, scale, *, name):
    T, D = dout.shape
    F = wd.shape[0]
    tm, tn = _tile(T, 512), _tile(F, 1408)

    def body(do_ref, wd_ref, g_ref, u_ref, dg_ref, du_ref):
        da = scale * _nt(do_ref[...].astype(CDT), wd_ref[...].astype(CDT))
        gv = g_ref[...].astype(F32)
        uv = u_ref[...].astype(F32)
        sg = _sigmoid(gv)
        dg_ref[...] = (da * uv * (sg * (1.0 + gv * (1.0 - sg)))).astype(dg_ref.dtype)
        du_ref[...] = (da * (gv * sg)).astype(du_ref.dtype)

    blk = BS((tm, tn), lambda j, i: (i, j))
    return _pallas(body, grid=(F // tn, T // tm),
                   in_specs=[BS((tm, D), lambda j, i: (i, 0)), BS((tn, D), lambda j, i: (j, 0)), blk, blk],
                   out_specs=[blk, blk], out_shape=[SDS((T, F), CDT)] * 2,
                   compiler_params=_params("parallel", "parallel"), name=name)(dout, wd, g, u)


def loss_head(x, g, tgt, *, name):
    T, D = x.shape
    tm = _tile(T, 512)

    def body(x_ref, g_ref, t_ref, l_ref, dx_ref, dg_ref):
        i = pl.program_id(0)
        xv = x_ref[...]
        gv = g_ref[...]
        r = lax.rsqrt(jnp.mean(xv * xv, axis=-1, keepdims=True) + RMS_EPS)
        xhat = xv * r
        err = xhat * gv - t_ref[...]
        part = 0.5 * jnp.sum(jnp.mean(err * err, axis=-1, keepdims=True), axis=0, keepdims=True)
        dy = err * (1.0 / D)
        dxh = dy * gv
        m = jnp.mean(dxh * xhat, axis=-1, keepdims=True)
        dx_ref[...] = r * (dxh - xhat * m)

        @pl.when(i == 0)
        def _():
            dg_ref[...] = jnp.zeros_like(dg_ref)
            l_ref[...] = jnp.zeros_like(l_ref)

        dg_ref[...] += jnp.sum(dy * xhat, axis=0, keepdims=True)
        l_ref[...] += jnp.broadcast_to(part, l_ref.shape)

    row = BS((tm, D), lambda i: (i, 0))
    vec = BS((1, D), lambda i: (0, 0))
    return _pallas(body, grid=(T // tm,), in_specs=[row, vec, row],
                   out_specs=[BS((1, LANES), lambda i: (0, 0)), row, vec],
                   out_shape=[SDS((1, LANES), F32), SDS((T, D), F32), SDS((1, D), F32)],
                   compiler_params=_params("arbitrary"), name=name)(x, g, tgt)


def adamw(w, g, m, v, *, name):
    R, C = w.shape
    tr = R
    if R * C * 4 > (1 << 20):
        for cand in range(8, R + 1, 8):
            if R % cand == 0 and cand * C * 4 <= (1 << 20):
                tr = cand

    def body(w_ref, g_ref, m_ref, v_ref, d_ref, nm_ref, nv_ref):
        gv = g_ref[...]
        nm = ADAM_B1 * m_ref[...] + (1.0 - ADAM_B1) * gv
        nv = ADAM_B2 * v_ref[...] + (1.0 - ADAM_B2) * (gv * gv)
        m_hat = nm / (1.0 - ADAM_B1 ** ADAM_STEP)
        v_hat = nv / (1.0 - ADAM_B2 ** ADAM_STEP)
        d_ref[...] = -ADAM_LR * (m_hat / (jnp.sqrt(v_hat) + ADAM_EPS) + ADAM_WD * w_ref[...])
        nm_ref[...] = nm
        nv_ref[...] = nv

    blk = BS((tr, C), lambda i: (i, 0))
    return _pallas(body, grid=(R // tr,), in_specs=[blk] * 4, out_specs=[blk] * 3,
                   out_shape=[SDS((R, C), F32)] * 3, compiler_params=_params("parallel"), name=name)(w, g, m, v)


def _sb_tri():
    row = lax.broadcasted_iota(jnp.int32, (CHUNK, CHUNK), 0)
    col = lax.broadcasted_iota(jnp.int32, (CHUNK, CHUNK), 1)
    return row, col


def _tri_stack(keep):
    t = keep.astype(jnp.bfloat16)
    return jnp.concatenate([t, t], axis=0)


def _split_dot(x, tri_stack):
    hi = x.astype(jnp.bfloat16)
    mid = (x - hi.astype(F32)).astype(jnp.bfloat16)
    return jnp.dot(jnp.concatenate([hi, mid], axis=1), tri_stack, preferred_element_type=F32)


def _sb_heads_per_step(H, want):
    hb = min(H, want)
    assert H % hb == 0, (H, hb)
    return hb


SB_FWD_HEADS = 16
SB_BWD_HEADS = 8


def _head_pair_masked(blk, lo):
    zero = jnp.zeros_like(blk)
    return jnp.where(lo, blk, zero), jnp.where(lo, zero, blk)


def sb_fwd(qkv, n_heads, *, name):
    T, D3 = qkv.shape
    D = D3 // 3
    nq = T // CHUNK
    assert nq <= LANES
    scale = HEAD_DIM ** -0.5
    hb = _sb_heads_per_step(n_heads, SB_FWD_HEADS)
    assert hb % 2 == 0
    LW = hb * HEAD_DIM
    ngrp = D // LW
    heads, pairs = range(hb), range(hb // 2)
    lanes = lambda p: slice(p * LANES, (p + 1) * LANES)

    def body(q_ref, k_ref, v_ref, o_ref, tab_ref, acc_scr, cs_scr, qm_scr):
        qi = pl.program_id(1)
        row, col = _sb_tri()
        lo = col < HEAD_DIM
        upper_stack = _tri_stack(row > col)
        acc_scr[...] = jnp.zeros_like(acc_scr)
        cs_scr[...] = jnp.zeros_like(cs_scr)
        tab_ref[...] = jnp.zeros_like(tab_ref)
        for p in pairs:
            qm_scr[2 * p], qm_scr[2 * p + 1] = _head_pair_masked(q_ref[:, lanes(p)] * scale, lo)

        def tile(j, diag):
            sl = pl.ds(pl.multiple_of(j * CHUNK, CHUNK), CHUNK)
            keep = (lambda t: jnp.where(col < row, t, 0.0)) if diag else (lambda t: t)
            kbs = [k_ref[sl, lanes(p)] for p in pairs]
            zs = [_nt(qm_scr[h], kbs[h // 2]) for h in heads]
            lszs = [jnp.minimum(z, 0.0) - _softplus_neg_abs(z) for z in zs]
            lks = [keep(lsz - z) for lsz, z in zip(lszs, zs)]
            tls = [_split_dot(lk, upper_stack) for lk in lks]
            css = [cs_scr[h] for h in heads]
            logits = [lsz + tl + cs for lsz, tl, cs in zip(lszs, tls, css)]
            atts = [keep(jnp.exp(lg)).astype(CDT) for lg in logits]
            vms = [jnp.concatenate(_head_pair_masked(v_ref[sl, lanes(p)], lo), axis=0) for p in pairs]
            pvs = [_nn(jnp.concatenate([atts[2 * p], atts[2 * p + 1]], axis=1), vms[p]) for p in pairs]
            for p in pairs:
                acc_scr[:, lanes(p)] += pvs[p]
            for h in heads:
                tab_ref[h, 0] = jnp.where(col == j, css[h], tab_ref[h, 0])
                if diag:
                    cs_scr[h] = css[h] + (tls[h][:, :1] + lks[h][:, :1])
                else:
                    cs_scr[h] = logits[h][:, :1] - zs[h][:, :1]

        tile(qi, True)

        def step(jj, carry):
            tile(qi - jj, False)
            return carry

        lax.fori_loop(1, qi + 1, step, 0)
        o_ref[...] = acc_scr[...].astype(o_ref.dtype)

    once = pl.Buffered(1)
    qs = BS((CHUNK, LW), lambda g, i: (i, g))
    return _pallas(body, grid=(ngrp, nq),
                   in_specs=[qs, BS((T, LW), lambda g, i: (0, ngrp + g), pipeline_mode=once),
                             BS((T, LW), lambda g, i: (0, 2 * ngrp + g), pipeline_mode=once)],
                   out_specs=[qs, BS((hb, 1, CHUNK, CHUNK), lambda g, i: (g, i, 0, 0))],
                   out_shape=[SDS((T, D), CDT), SDS((n_heads, nq, CHUNK, CHUNK), F32)],
                   scratch_shapes=[pltpu.VMEM((CHUNK, LW), F32), pltpu.VMEM((hb, CHUNK, 1), F32),
                                   pltpu.VMEM((hb, CHUNK, LANES), CDT)],
                   compiler_params=_params("parallel", "arbitrary"), name=name)(qkv, qkv, qkv)


def sb_bwd(qkv, do, tab, n_heads, *, name):
    T, D3 = qkv.shape
    D = D3 // 3
    nq = T // CHUNK
    scale = HEAD_DIM ** -0.5
    hb = _sb_heads_per_step(n_heads, SB_BWD_HEADS)
    assert hb % 2 == 0
    LW = hb * HEAD_DIM
    ngrp = D // LW
    heads, pairs = range(hb), range(hb // 2)
    lanes = lambda p: slice(p * LANES, (p + 1) * LANES)

    def body(q_ref, k_ref, v_ref, do_ref, tab_ref, dq_ref, dk_ref, dv_ref,
             dk_acc, dv_acc, dq_scr, pfx_scr, qm_scr, dom_scr):
        qi = pl.program_id(1)

        @pl.when(qi == 0)
        def _():
            dk_acc[...] = jnp.zeros_like(dk_acc)
            dv_acc[...] = jnp.zeros_like(dv_acc)

        dq_scr[...] = jnp.zeros_like(dq_scr)
        pfx_scr[...] = jnp.zeros_like(pfx_scr)
        row, col = _sb_tri()
        lo = col < HEAD_DIM
        upper_stack = _tri_stack(row > col)
        before_stack = _tri_stack(row < col)
        for p in pairs:
            qm_scr[2 * p], qm_scr[2 * p + 1] = _head_pair_masked(q_ref[:, lanes(p)] * scale, lo)
            dom_scr[2 * p], dom_scr[2 * p + 1] = _head_pair_masked(do_ref[:, lanes(p)], lo)

        def tile(j, diag):
            sl = pl.ds(pl.multiple_of(j * CHUNK, CHUNK), CHUNK)
            keep = (lambda t: jnp.where(col < row, t, 0.0)) if diag else (lambda t: t)
            kbs = [k_ref[sl, lanes(p)] for p in pairs]
            vbs = [v_ref[sl, lanes(p)] for p in pairs]
            zs = [_nt(qm_scr[h], kbs[h // 2]) for h in heads]
            das = [_nt(dom_scr[h], vbs[h // 2]) for h in heads]
            lszs = [jnp.minimum(z, 0.0) - _softplus_neg_abs(z) for z in zs]
            lks = [keep(lsz - z) for lsz, z in zip(lszs, zs)]
            tls = [_split_dot(lk, upper_stack) for lk in lks]
            css = [jnp.sum(jnp.where(col == j, tab_ref[h, 0], 0.0), axis=1, keepdims=True) for h in heads]
            atts = [keep(jnp.exp(lsz + tl + cs)) for lsz, tl, cs in zip(lszs, tls, css)]
            dls = [att * da for att, da in zip(atts, das)]
            pres = [_split_dot(dl, before_stack) for dl in dls]
            pfxs = [pfx_scr[h] for h in heads]
            sigs = [jnp.exp(lsz) for lsz in lszs]
            befores = [pre + pfx for pre, pfx in zip(pres, pfxs)]
            dzs = [(dl * (1.0 - sig) - keep(sig * bf)).astype(CDT) for dl, sig, bf in zip(dls, sigs, befores)]
            attc = [att.astype(CDT) for att in atts]
            kms = [jnp.concatenate(_head_pair_masked(kb, lo), axis=0) for kb in kbs]
            dqs = [_nn(jnp.concatenate([dzs[2 * p], dzs[2 * p + 1]], axis=1), kms[p]) for p in pairs]
            both = lambda xs, p: jnp.concatenate([xs[2 * p], xs[2 * p + 1]], axis=0)
            dks = [_tn(both(dzs, p), jnp.concatenate([qm_scr[2 * p], qm_scr[2 * p + 1]], axis=0)) for p in pairs]
            dvs = [_tn(both(attc, p), jnp.concatenate([dom_scr[2 * p], dom_scr[2 * p + 1]], axis=0)) for p in pairs]
            last = CHUNK - 1
            for p in pairs:
                dq_scr[:, lanes(p)] += dqs[p]
                dk_acc[sl, lanes(p)] += dks[p]
                dv_acc[sl, lanes(p)] += dvs[p]
            for h in heads:
                pfx_scr[h] = befores[h][:, last:] + dls[h][:, last:]

        def step(j, carry):
            tile(j, False)
            return carry

        lax.fori_loop(0, qi, step, 0)
        tile(qi, True)
        dq_ref[...] = (dq_scr[...] * scale).astype(dq_ref.dtype)

        @pl.when(qi == nq - 1)
        def _():
            dk_ref[...] = dk_acc[...].astype(dk_ref.dtype)
            dv_ref[...] = dv_acc[...].astype(dv_ref.dtype)

    once = pl.Buffered(1)
    qs = BS((CHUNK, LW), lambda g, i: (i, g))
    ks = BS((T, LW), lambda g, i: (0, g))
    return _pallas(body, grid=(ngrp, nq),
                   in_specs=[qs, BS((T, LW), lambda g, i: (0, ngrp + g), pipeline_mode=once),
                             BS((T, LW), lambda g, i: (0, 2 * ngrp + g), pipeline_mode=once), qs,
                             BS((hb, 1, CHUNK, CHUNK), lambda g, i: (g, i, 0, 0))],
                   out_specs=[qs, ks, ks],
                   out_shape=[SDS((T, D), CDT)] * 3,
                   scratch_shapes=[pltpu.VMEM((T, LW), F32), pltpu.VMEM((T, LW), F32),
                                   pltpu.VMEM((CHUNK, LW), F32), pltpu.VMEM((hb, CHUNK, 1), F32),
                                   pltpu.VMEM((hb, CHUNK, LANES), CDT), pltpu.VMEM((hb, CHUNK, LANES), CDT)],
                   compiler_params=_params("parallel", "arbitrary"), name=name)(qkv, qkv, qkv, do, tab)


HALO = 8


def _shift_down(cur, prev8, s):
    if s == 0:
        return cur
    tt = cur.shape[0]
    row = lax.broadcasted_iota(jnp.int32, cur.shape, 0)
    fix = jnp.concatenate([pltpu.roll(prev8, s, 0), jnp.zeros((tt - HALO, cur.shape[1]), cur.dtype)], axis=0)
    return jnp.where(row < s, fix, pltpu.roll(cur, s, 0))


def _shift_up(cur, next8, s):
    if s == 0:
        return cur
    tt = cur.shape[0]
    row = lax.broadcasted_iota(jnp.int32, cur.shape, 0)
    fix = jnp.concatenate([jnp.zeros((tt - HALO, cur.shape[1]), cur.dtype), pltpu.roll(next8, HALO - s, 0)], axis=0)
    return jnp.where(row >= tt - s, fix, pltpu.roll(cur, tt - s, 0))


def _halo_specs(tt, width, T):
    per = tt // HALO
    last = T // HALO - 1
    cur = BS((tt, width), lambda i: (i, 0))
    prev = BS((HALO, width), lambda i: (jnp.maximum(i * per - 1, 0), 0))
    nxt = BS((HALO, width), lambda i: (jnp.minimum((i + 1) * per, last), 0))
    return cur, prev, nxt


def sc_fwd(proj, w, *, name):
    T, D3 = proj.shape
    D = D3 // 3
    K = w.shape[0]
    tt = _tile(T, 256)
    cur, prev, _ = _halo_specs(tt, D3, T)

    def body(p_ref, pp_ref, w_ref, o_ref):
        i = pl.program_id(0)
        pc = p_ref[:, D:2 * D] * p_ref[:, 2 * D:]
        pp = jnp.where(i > 0, pp_ref[:, D:2 * D] * pp_ref[:, 2 * D:], 0.0)
        u = jnp.zeros((tt, D), F32)
        for kk in range(K):
            u = u + w_ref[kk:kk + 1, :] * _shift_down(pc, pp, K - 1 - kk)
        o_ref[...] = (p_ref[:, :D] * u).astype(o_ref.dtype)

    return _pallas(body, grid=(T // tt,), in_specs=[cur, prev, BS((K, D), lambda i: (0, 0))],
                   out_specs=BS((tt, D), lambda i: (i, 0)), out_shape=SDS((T, D), CDT),
                   compiler_params=_params("parallel"), name=name)(proj, proj, w)


def sc_bwd(dm, proj, w, *, name):
    T, D3 = proj.shape
    D = D3 // 3
    K = w.shape[0]
    tt = _tile(T, 256)
    cur, prev, nxt = _halo_specs(tt, D3, T)
    dcur, _, dnxt = _halo_specs(tt, D, T)
    nsteps = T // tt

    def body(dm_ref, dmn_ref, p_ref, pp_ref, pn_ref, w_ref, o_ref, dw_ref):
        i = pl.program_id(0)
        b = p_ref[:, :D]
        c = p_ref[:, D:2 * D]
        h = p_ref[:, 2 * D:]
        pc = c * h
        pp = jnp.where(i > 0, pp_ref[:, D:2 * D] * pp_ref[:, 2 * D:], 0.0)
        dmv = dm_ref[...]
        du = dmv * b
        dun = jnp.where(i < nsteps - 1, dmn_ref[...] * pn_ref[:, :D], 0.0)
        u = jnp.zeros((tt, D), F32)
        dp = jnp.zeros((tt, D), F32)
        rows = []
        for kk in range(K):
            sh = _shift_down(pc, pp, K - 1 - kk)
            u = u + w_ref[kk:kk + 1, :] * sh
            dp = dp + w_ref[kk:kk + 1, :] * _shift_up(du, dun, K - 1 - kk)
            rows.append(jnp.sum(du * sh, axis=0, keepdims=True))
        rows.append(jnp.zeros((HALO - K, D), F32))
        o_ref[:, :D] = (dmv * u).astype(o_ref.dtype)
        o_ref[:, D:2 * D] = (dp * h).astype(o_ref.dtype)
        o_ref[:, 2 * D:] = (dp * c).astype(o_ref.dtype)

        @pl.when(i == 0)
        def _():
            dw_ref[...] = jnp.zeros_like(dw_ref)

        dw_ref[...] += jnp.concatenate(rows, axis=0)

    return _pallas(body, grid=(nsteps,), in_specs=[dcur, dnxt, cur, prev, nxt, BS((K, D), lambda i: (0, 0))],
                   out_specs=[BS((tt, D3), lambda i: (i, 0)), BS((HALO, D), lambda i: (0, 0))],
                   out_shape=[SDS((T, D3), CDT), SDS((HALO, D), F32)],
                   compiler_params=_params("arbitrary"), name=name)(dm, dm, proj, proj, proj, w)


def ssd_conv_fwd(xpre, w, b, *, name):
    T, C = xpre.shape
    K = w.shape[0]
    tt = _tile(T, 256)
    cur, prev, _ = _halo_specs(tt, C, T)

    def body(x_ref, xp_ref, w_ref, b_ref, pre_ref, o_ref):
        i = pl.program_id(0)
        xv = x_ref[...]
        xp = jnp.where(i > 0, xp_ref[...], 0.0)
        acc = jnp.broadcast_to(b_ref[...], (tt, C))
        for kk in range(K):
            acc = acc + w_ref[kk:kk + 1, :] * _shift_down(xv, xp, K - 1 - kk)
        pre_ref[...] = acc
        o_ref[...] = acc * _sigmoid(acc)

    row = BS((tt, C), lambda i: (i, 0))
    return _pallas(body, grid=(T // tt,), in_specs=[cur, prev, BS((K, C), lambda i: (0, 0)), BS((1, C), lambda i: (0, 0))],
                   out_specs=[row, row], out_shape=[SDS((T, C), F32)] * 2,
                   compiler_params=_params("parallel"), name=name)(xpre, xpre, w, b)


def ssd_conv_bwd(dxs, db_, dc_, pre, xpre, w, *, name):
    T, C = xpre.shape
    K = w.shape[0]
    tt = _tile(T, 256)
    cur, prev, nxt = _halo_specs(tt, C, T)
    widths = (dxs.shape[1], db_.shape[1], dc_.shape[1])
    gspecs = []
    for wd in widths:
        c_, _, n_ = _halo_specs(tt, wd, T)
        gspecs += [c_, n_]
    nsteps = T // tt

    def dsilu(p):
        s = _sigmoid(p)
        return s * (1.0 + p * (1.0 - s))

    def body(g0, g0n, g1, g1n, g2, g2n, pre_ref, pren_ref, x_ref, xp_ref, w_ref, dx_ref, dw_ref, dbias_ref):
        i = pl.program_id(0)
        gcur = jnp.concatenate([g0[...], g1[...], g2[...]], axis=1)
        gnxt = jnp.concatenate([g0n[...], g1n[...], g2n[...]], axis=1)
        dpre = gcur * dsilu(pre_ref[...])
        dpren = jnp.where(i < nsteps - 1, gnxt * dsilu(pren_ref[...]), 0.0)
        xv = x_ref[...]
        xp = jnp.where(i > 0, xp_ref[...], 0.0)
        dx = jnp.zeros((tt, C), F32)
        rows = []
        for kk in range(K):
            dx = dx + w_ref[kk:kk + 1, :] * _shift_up(dpre, dpren, K - 1 - kk)
            rows.append(jnp.sum(dpre * _shift_down(xv, xp, K - 1 - kk), axis=0, keepdims=True))
        rows.append(jnp.zeros((HALO - K, C), F32))
        dx_ref[...] = dx.astype(dx_ref.dtype)

        @pl.when(i == 0)
        def _():
            dw_ref[...] = jnp.zeros_like(dw_ref)
            dbias_ref[...] = jnp.zeros_like(dbias_ref)

        dw_ref[...] += jnp.concatenate(rows, axis=0)
        dbias_ref[...] += jnp.sum(dpre, axis=0, keepdims=True)

    return _pallas(body, grid=(nsteps,),
                   in_specs=gspecs + [cur, nxt, cur, prev, BS((K, C), lambda i: (0, 0))],
                   out_specs=[BS((tt, C), lambda i: (i, 0)), BS((HALO, C), lambda i: (0, 0)), BS((1, C), lambda i: (0, 0))],
                   out_shape=[SDS((T, C), CDT), SDS((HALO, C), F32), SDS((1, C), F32)],
                   compiler_params=_params("arbitrary"), name=name)(dxs, dxs, db_, db_, dc_, dc_, pre, pre, xpre, xpre, w)


def ssd_prep_fwd(dtpre, bias, alog, *, name):
    T = dtpre.shape[0]
    nc = T // CHUNK

    def body(p_ref, b_ref, a_ref, dt_ref, ac_ref, dtT_ref, acT_ref):
        xv = p_ref[...] + b_ref[...]
        dt = jnp.maximum(xv, 0.0) + _softplus_neg_abs(xv)
        row, col = _sb_tri()
        lower = (col <= row).astype(F32)
        ac = jnp.dot(lower, dt * -jnp.exp(a_ref[...]), precision=HI, preferred_element_type=F32)
        dt_ref[...] = dt
        ac_ref[...] = ac
        dtT_ref[...] = dt.T
        acT_ref[...] = ac.T

    blk = BS((CHUNK, LANES), lambda c: (c, 0))
    blkT = BS((LANES, CHUNK), lambda c: (0, c))
    vec = BS((1, LANES), lambda c: (0, 0))
    return _pallas(body, grid=(nc,), in_specs=[blk, vec, vec], out_specs=[blk, blk, blkT, blkT],
                   out_shape=[SDS((T, LANES), F32)] * 2 + [SDS((LANES, T), F32)] * 2,
                   compiler_params=_params("parallel"), name=name)(dtpre, bias, alog)


def ssd_prep_bwd(dac_c, ddt_c, dac_r, ddt_r, dtpre, bias, alog, *, name):
    T = dtpre.shape[0]
    nc = T // CHUNK

    def body(dac_ref, ddt_ref, dacr_ref, ddtr_ref, p_ref, b_ref, a_ref, o_ref, da_ref, db_ref):
        c = pl.program_id(0)
        xv = p_ref[...] + b_ref[...]
        dt = jnp.maximum(xv, 0.0) + _softplus_neg_abs(xv)
        av = -jnp.exp(a_ref[...])
        row, col = _sb_tri()
        after = (col >= row).astype(F32)
        dda = jnp.dot(after, dac_ref[...] + dacr_ref[...].T, precision=HI, preferred_element_type=F32)
        dpre = (ddt_ref[...] + ddtr_ref[...].T + av * dda) * _sigmoid(xv)
        o_ref[...] = dpre

        @pl.when(c == 0)
        def _():
            da_ref[...] = jnp.zeros_like(da_ref)
            db_ref[...] = jnp.zeros_like(db_ref)

        da_ref[...] += jnp.sum(dt * dda, axis=0, keepdims=True) * av
        db_ref[...] += jnp.sum(dpre, axis=0, keepdims=True)

    blk = BS((CHUNK, LANES), lambda c: (c, 0))
    blkT = BS((LANES, CHUNK), lambda c: (0, c))
    vec = BS((1, LANES), lambda c: (0, 0))
    return _pallas(body, grid=(nc,), in_specs=[blk, blk, blkT, blkT, blk, vec, vec], out_specs=[blk, vec, vec],
                   out_shape=[SDS((T, LANES), F32), SDS((1, LANES), F32), SDS((1, LANES), F32)],
                   compiler_params=_params("arbitrary"), name=name)(dac_c, ddt_c, dac_r, ddt_r, dtpre, bias, alog)


def _ssd_specs(cfg, T, rev):
    nc = T // CHUNK
    GW = SSD_HPG * HEAD_DIM
    bo = cfg.DI // LANES
    co = (cfg.DI + cfg.G * LANES) // LANES
    ci = (lambda c: nc - 1 - c) if rev else (lambda c: c)
    return dict(
        x=BS((CHUNK, GW), lambda g, c: (ci(c), g)),
        b=BS((CHUNK, LANES), lambda g, c: (ci(c), bo + g)),
        c=BS((CHUNK, LANES), lambda g, c: (ci(c), co + g)),
        col=BS((CHUNK, LANES), lambda g, c: (ci(c), 0)),
        row=BS((LANES, CHUNK), lambda g, c: (0, ci(c))),
        vec=BS((1, LANES), lambda g, c: (0, 0)),
        ng=BS((1, GW), lambda g, c: (0, g)),
        hp=BS((1, 1, SSD_HPG, HEAD_DIM, LANES), lambda g, c: (g, ci(c), 0, 0, 0)),
    )


def ssd_chunk_fwd(xbc, z, dt, ac, dtT, acT, dskip, ng, cfg, *, name):
    T = xbc.shape[0]
    nc = T // CHUNK
    G, DI = cfg.G, cfg.DI
    GW = SSD_HPG * HEAD_DIM
    sp = _ssd_specs(cfg, T, False)

    def body(x_ref, b_ref, c_ref, z_ref, dt_ref, ac_ref, dtT_ref, acT_ref, d_ref, ng_ref,
             y_ref, yn_ref, hp_ref, h_scr):
        g = pl.program_id(0)
        c = pl.program_id(1)

        @pl.when(c == 0)
        def _():
            h_scr[...] = jnp.zeros_like(h_scr)

        row, col = _sb_tri()
        causal = row >= col
        Bm = b_ref[...].astype(CDT)
        Cm = c_ref[...].astype(CDT)
        Gm = _nt(Cm, Bm)
        dtb, acb, dtTb, acTb, dv = dt_ref[...], ac_ref[...], dtT_ref[...], acT_ref[...], d_ref[...]
        heads = range(SSD_HPG)
        hhs = [g * SSD_HPG + r for r in heads]
        ac_cs = [_colsel(acb, hh) for hh in hhs]
        dt_cs = [_colsel(dtb, hh) for hh in hhs]
        Xs = [x_ref[:, r * HEAD_DIM:(r + 1) * HEAD_DIM] for r in heads]
        Hps = [h_scr[r] for r in heads]
        Ws = [(Gm * jnp.where(causal, jnp.exp(ac_c - _rowsel(acTb, hh)), 0.0) * _rowsel(dtTb, hh)).astype(CDT)
              for ac_c, hh in zip(ac_cs, hhs)]
        Qs = [_nt(Cm, Hp.astype(CDT)) for Hp in Hps]
        Yds = [_nn(Wm, Xr.astype(CDT)) for Wm, Xr in zip(Ws, Xs)]
        aLs = [_rowsel(ac_c, CHUNK - 1) for ac_c in ac_cs]
        Xes = [(Xr * (jnp.exp(aL - ac_c) * dt_c)).astype(CDT) for Xr, aL, ac_c, dt_c in zip(Xs, aLs, ac_cs, dt_cs)]
        Sts = [_tn(Xe, Bm) for Xe in Xes]
        for r in heads:
            hp_ref[0, 0, r] = Hps[r]
            h_scr[r] = jnp.exp(aLs[r]) * Hps[r] + Sts[r]
            y_ref[:, r * HEAD_DIM:(r + 1) * HEAD_DIM] = (Yds[r] + jnp.exp(ac_cs[r]) * Qs[r]
                                                         + _colsel(dv, hhs[r]) * Xs[r])
        zz = z_ref[...]
        yz = y_ref[...] * (zz * _sigmoid(zz))
        rstd = lax.rsqrt(jnp.mean(yz * yz, axis=-1, keepdims=True) + RMS_EPS)
        yn_ref[...] = (yz * rstd * ng_ref[...]).astype(yn_ref.dtype)

    return _pallas(body, grid=(G, nc),
                   in_specs=[sp["x"], sp["b"], sp["c"], sp["x"], sp["col"], sp["col"], sp["row"], sp["row"], sp["vec"], sp["ng"]],
                   out_specs=[sp["x"], sp["x"], sp["hp"]],
                   out_shape=[SDS((T, DI), F32), SDS((T, DI), CDT), SDS((G, nc, SSD_HPG, HEAD_DIM, LANES), F32)],
                   scratch_shapes=[pltpu.VMEM((SSD_HPG, HEAD_DIM, LANES), F32)],
                   compiler_params=_params("parallel", "arbitrary"), name=name)(xbc, xbc, xbc, z, dt, ac, dtT, acT, dskip, ng)


def ssd_chunk_bwd(dyn, xbc, z, y, hp, dt, ac, dtT, acT, dskip, ng, cfg, *, name):
    T = xbc.shape[0]
    nc = T // CHUNK
    G, DI = cfg.G, cfg.DI
    GW = SSD_HPG * HEAD_DIM
    sp = _ssd_specs(cfg, T, True)

    def body(dyn_ref, x_ref, b_ref, c_ref, z_ref, y_ref, hp_ref, dt_ref, ac_ref, dtT_ref, acT_ref, d_ref, ng_ref,
             dx_ref, db_ref, dc_ref, dz_ref, colp_ref, rowp_ref, dd_ref, dng_ref, dh_scr):
        g = pl.program_id(0)
        c = pl.program_id(1)

        @pl.when(c == 0)
        def _():
            dh_scr[...] = jnp.zeros_like(dh_scr)
            dd_ref[...] = jnp.zeros_like(dd_ref)
            dng_ref[...] = jnp.zeros_like(dng_ref)

        zz = z_ref[...]
        sg = _sigmoid(zz)
        gate = zz * sg
        yv = y_ref[...]
        yz = yv * gate
        rstd = lax.rsqrt(jnp.mean(yz * yz, axis=-1, keepdims=True) + RMS_EPS)
        nhat = yz * rstd
        dynv = dyn_ref[...].astype(F32)
        dng_ref[...] += jnp.sum(dynv * nhat, axis=0, keepdims=True)
        dnh = dynv * ng_ref[...]
        dyz = rstd * (dnh - nhat * jnp.mean(dnh * nhat, axis=-1, keepdims=True))
        dz_ref[...] = (dyz * yv * (sg * (1.0 + zz * (1.0 - sg)))).astype(dz_ref.dtype)
        dy_all = dyz * gate

        row, col = _sb_tri()
        causal = row >= col
        lane1 = lax.broadcasted_iota(jnp.int32, (1, LANES), 1)
        sub8 = lax.broadcasted_iota(jnp.int32, (HALO, CHUNK), 0)
        subc = lax.broadcasted_iota(jnp.int32, (CHUNK, 1), 0)
        Bf = b_ref[...]
        Cf = c_ref[...]
        Bm = Bf.astype(CDT)
        Cm = Cf.astype(CDT)
        Gm = _nt(Cm, Bm)
        dtb, acb, dtTb, acTb, dv = dt_ref[...], ac_ref[...], dtT_ref[...], acT_ref[...], d_ref[...]
        dG = jnp.zeros((CHUNK, CHUNK), F32)
        dBm = jnp.zeros((CHUNK, LANES), F32)
        dCm = jnp.zeros((CHUNK, LANES), F32)
        colp = jnp.zeros((CHUNK, LANES), F32)
        rowp = jnp.zeros((HALO, CHUNK), F32)
        ddv = jnp.zeros((1, LANES), F32)
        heads = range(SSD_HPG)
        hhs = [g * SSD_HPG + r for r in heads]
        ac_cs = [_colsel(acb, hh) for hh in hhs]
        dt_cs = [_colsel(dtb, hh) for hh in hhs]
        dt_rs = [_rowsel(dtTb, hh) for hh in hhs]
        Ss = [jnp.where(causal, jnp.exp(ac_c - _rowsel(acTb, hh)), 0.0) for ac_c, hh in zip(ac_cs, hhs)]
        GSs = [Gm * S for S in Ss]
        Ws = [GS * dt_r for GS, dt_r in zip(GSs, dt_rs)]
        Xs = [x_ref[:, r * HEAD_DIM:(r + 1) * HEAD_DIM] for r in heads]
        Xcs = [X.astype(CDT) for X in Xs]
        Hps = [hp_ref[0, 0, r] for r in heads]
        Hcs = [Hp.astype(CDT) for Hp in Hps]
        dYs = [dy_all[:, r * HEAD_DIM:(r + 1) * HEAD_DIM] for r in heads]
        dYcs = [dY.astype(CDT) for dY in dYs]
        Es = [jnp.exp(ac_c) for ac_c in ac_cs]
        dQs = [(dY * E).astype(CDT) for dY, E in zip(dYs, Es)]
        dHns = [dh_scr[r] for r in heads]
        dHcs = [dHn.astype(CDT) for dHn in dHns]
        aLs = [_rowsel(ac_c, CHUNK - 1) for ac_c in ac_cs]
        exs = [jnp.exp(aL - ac_c) for aL, ac_c in zip(aLs, ac_cs)]
        es = [ex * dt_c for ex, dt_c in zip(exs, dt_cs)]
        Xes = [(X * e).astype(CDT) for X, e in zip(Xs, es)]
        WtdYs = [_tn(W.astype(CDT), dYc) for W, dYc in zip(Ws, dYcs)]
        dWms = [_nt(dYc, Xc) for dYc, Xc in zip(dYcs, Xcs)]
        Qs = [_nt(Cm, Hc) for Hc in Hcs]
        dCps = [_nn(dQ, Hc) for dQ, Hc in zip(dQs, Hcs)]
        dHps = [_tn(dQ, Cm) for dQ in dQs]
        T2s = [_nt(Bm, dHc) for dHc in dHcs]
        dBps = [_nn(Xe, dHc) for Xe, dHc in zip(Xes, dHcs)]
        for r in heads:
            dY, Xr, Wm, dWm, E, ex, e, aL = dYs[r], Xs[r], Ws[r], dWms[r], Es[r], exs[r], es[r], aLs[r]
            ddv = ddv + jnp.where(lane1 == r, jnp.sum(jnp.sum(dY * Xr, axis=1, keepdims=True), axis=0, keepdims=True), 0.0)
            dG = dG + dWm * Ss[r] * dt_rs[r]
            Mm = dWm * Wm
            dac_r = -jnp.sum(Mm, axis=0, keepdims=True)
            ddt_r = jnp.sum(dWm * GSs[r], axis=0, keepdims=True)
            dCm = dCm + dCps[r]
            dBm = dBm + dBps[r]
            eaL = jnp.exp(aL)
            daL = eaL * jnp.sum(jnp.sum(dHns[r] * Hps[r], axis=1, keepdims=True), axis=0, keepdims=True)
            de = jnp.sum(Xr * T2s[r], axis=1, keepdims=True)
            dee = de * e
            daL = daL + jnp.sum(dee, axis=0, keepdims=True)
            dac_c = (jnp.sum(Mm, axis=1, keepdims=True) + jnp.sum(dY * Qs[r], axis=1, keepdims=True) * E
                     - dee + jnp.where(subc == CHUNK - 1, daL, 0.0))
            dh_scr[r] = dHps[r] + eaL * dHns[r]
            dx_ref[:, r * HEAD_DIM:(r + 1) * HEAD_DIM] = dY * _colsel(dv, hhs[r]) + WtdYs[r] + T2s[r] * e
            colp = jnp.where(col == r, dac_c, colp)
            colp = jnp.where(col == SSD_HPG + r, de * ex, colp)
            rowp = jnp.where(sub8 == r, dac_r, rowp)
            rowp = jnp.where(sub8 == SSD_HPG + r, ddt_r, rowp)
        dGc = dG.astype(CDT)
        db_ref[...] = dBm + _tn(dGc, Cm)
        dc_ref[...] = dCm + _nn(dGc, Bm)
        colp_ref[0] = colp
        rowp_ref[0] = rowp
        dd_ref[0] += ddv

    f32 = lambda shape: SDS(shape, F32)
    return _pallas(body, grid=(G, nc),
                   in_specs=[sp["x"], sp["x"], sp["b"], sp["c"], sp["x"], sp["x"], sp["hp"], sp["col"], sp["col"],
                             sp["row"], sp["row"], sp["vec"], sp["ng"]],
                   out_specs=[sp["x"],
                              BS((CHUNK, LANES), lambda g, c: (nc - 1 - c, g)),
                              BS((CHUNK, LANES), lambda g, c: (nc - 1 - c, g)),
                              sp["x"],
                              BS((1, CHUNK, LANES), lambda g, c: (g, nc - 1 - c, 0)),
                              BS((1, HALO, CHUNK), lambda g, c: (g, 0, nc - 1 - c)),
                              BS((1, 1, LANES), lambda g, c: (g, 0, 0)),
                              sp["ng"]],
                   out_shape=[f32((T, DI)), f32((T, G * LANES)), f32((T, G * LANES)), SDS((T, DI), CDT),
                              f32((G, T, LANES)), f32((G, HALO, T)), f32((G, 1, LANES)), f32((1, DI))],
                   scratch_shapes=[pltpu.VMEM((SSD_HPG, HEAD_DIM, LANES), F32)],
                   compiler_params=_params("parallel", "arbitrary"), name=name)(
                       dyn, xbc, xbc, xbc, z, y, hp, dt, ac, dtT, acT, dskip, ng)


HBM = pl.BlockSpec(memory_space=pltpu.HBM)
VMEM = pl.BlockSpec(memory_space=pltpu.VMEM)


def _me():
    return lax.axis_index("x"), lax.axis_index("y"), lax.axis_index("c")


def _rcopy(src, dst, send_sem, recv_sem, to):
    return pltpu.make_async_remote_copy(src_ref=src, dst_ref=dst, send_sem=send_sem, recv_sem=recv_sem,
                                        device_id=to, device_id_type=MESH)


AG_SEMS = 13


def allgather_weights(packs, *, name):
    n = len(packs)

    def body(*refs):
        srcs, outs, (send_sems, recv_sems) = refs[:n], refs[n:2 * n], refs[2 * n:]
        x, y, c = _me()
        k = 2 * x + y
        sibling = (x, y, 1 - c)
        nx, ny = (1 - x, y, c), (x, 1 - y, c)
        kx, ky, kd = 2 * (1 - x) + y, 2 * x + (1 - y), 2 * (1 - x) + (1 - y)
        started = []

        def go(cp):
            cp.start()
            started.append(cp)

        def piece(ref, chip, core, r, Rq):
            return ref.at[chip, core, pl.ds(r * Rq, Rq)]

        for a, (src, out) in enumerate(zip(srcs, outs)):
            Rq = src.shape[1] // 2
            sem = lambda i: (send_sems.at[AG_SEMS * a + i], recv_sems.at[AG_SEMS * a + i])
            for i, (to, r) in enumerate([(nx, 0), (nx, 1), (ny, 0), (ny, 1)]):
                go(_rcopy(src.at[c, pl.ds(r * Rq, Rq)], piece(out, k, c, r, Rq), *sem(i), to))
            go(_rcopy(src, out.at[k], *sem(12), sibling))
        for a, out in enumerate(outs):
            Rq = out.shape[2] // 2
            sem = lambda i: (send_sems.at[AG_SEMS * a + i], recv_sems.at[AG_SEMS * a + i])

            def landed(i, chip, r, frm, relay=None):
                ref = piece(out, chip, c, r, Rq)
                _rcopy(ref, ref, *sem(i), frm).wait_recv()
                if relay is not None:
                    go(_rcopy(ref, ref, *sem(relay[0]), relay[1]))
                go(_rcopy(ref, ref, *sem(6 + i), sibling))

            landed(0, kx, 0, nx, relay=(4, ny))
            landed(3, ky, 1, ny, relay=(5, nx))
            landed(1, kx, 1, nx)
            landed(2, ky, 0, ny)
            landed(4, kd, 0, ny)
            landed(5, kd, 1, nx)
        for a, out in enumerate(outs):
            Rq = out.shape[2] // 2
            sem = lambda i: (send_sems.at[AG_SEMS * a + i], recv_sems.at[AG_SEMS * a + i])
            for i, (chip, r) in enumerate([(kx, 0), (kx, 1), (ky, 0), (ky, 1), (kd, 0), (kd, 1)]):
                ref = piece(out, chip, 1 - c, r, Rq)
                _rcopy(ref, ref, *sem(6 + i), sibling).wait_recv()
            mine = out.at[k]
            _rcopy(mine, mine, *sem(12), sibling).wait_recv()
        for cp in started:
            cp.wait_send()

    return _pallas(body, in_specs=[HBM] * n, out_specs=[HBM] * n,
                   out_shape=[SDS((N_CHIPS,) + p.shape, p.dtype) for p in packs],
                   scratch_shapes=[pltpu.SemaphoreType.DMA((AG_SEMS * n,)), pltpu.SemaphoreType.DMA((AG_SEMS * n,))],
                   name=name)(*packs)


def sibling_swap_half(ps, *, name):
    n = len(ps)

    def body(*refs):
        srcs, outs, (send_sems, recv_sems) = refs[:n], refs[n:2 * n], refs[2 * n:]
        x, y, c = _me()
        cps = [_rcopy(src.at[1 - c], out, send_sems.at[a], recv_sems.at[a], (x, y, 1 - c))
               for a, (src, out) in enumerate(zip(srcs, outs))]
        for cp in cps:
            cp.start()
        for cp in cps:
            cp.wait()

    return _pallas(body, in_specs=[HBM] * n, out_specs=[HBM] * n,
                   out_shape=[SDS(p.shape[1:], p.dtype) for p in ps],
                   scratch_shapes=[pltpu.SemaphoreType.DMA((n,)), pltpu.SemaphoreType.DMA((n,))], name=name)(*ps)


def scatter_relay(s1s, *, name):
    n = len(s1s)

    def body(*refs):
        srcs, outs, (send_sems, recv_sems) = refs[:n], refs[n:2 * n], refs[2 * n:]
        x, y, c = _me()
        kd = 2 * (1 - x) + (1 - y)
        peers = [(1 - x, y, c), (x, 1 - y, c)]
        sends = []
        for a, (src, out) in enumerate(zip(srcs, outs)):
            Rq = src.shape[1] // 2
            for r, to in enumerate(peers):
                sends.append(_rcopy(src.at[kd, pl.ds(r * Rq, Rq)], out.at[r], send_sems.at[2 * a + r],
                                    recv_sems.at[2 * a + r], to))
        for cp in sends:
            cp.start()
        for a, out in enumerate(outs):
            for r, frm in enumerate(peers):
                slot = out.at[r]
                _rcopy(slot, slot, send_sems.at[2 * a + r], recv_sems.at[2 * a + r], frm).wait_recv()
        for cp in sends:
            cp.wait_send()

    return _pallas(body, in_specs=[HBM] * n, out_specs=[HBM] * n,
                   out_shape=[SDS((2, p.shape[1] // 2, p.shape[2]), p.dtype) for p in s1s],
                   scratch_shapes=[pltpu.SemaphoreType.DMA((2 * n,)), pltpu.SemaphoreType.DMA((2 * n,))],
                   name=name)(*s1s)


def add_relayed(s1, relay, yx_idx, *, name):
    _, Rq, W = relay.shape
    tr = _row_tile(Rq)
    nb = Rq // tr

    def body(idx_ref, s_ref, t_ref, o_ref):
        o_ref[...] = (s_ref[...].astype(F32) + t_ref[...].astype(F32)).astype(o_ref.dtype)

    gs = pltpu.PrefetchScalarGridSpec(
        num_scalar_prefetch=1, grid=(2, nb),
        in_specs=[BS((1, tr, W), lambda p, i, idx: (idx[p], p * nb + i, 0)), BS((1, tr, W), lambda p, i, idx: (p, i, 0))],
        out_specs=BS((1, tr, W), lambda p, i, idx: (p, i, 0)))
    return _pallas(body, grid_spec=gs, out_shape=SDS(relay.shape, relay.dtype),
                   compiler_params=_params("parallel", "parallel"), name=name)(yx_idx, s1, relay)


def chip_scatter(s1s, us, *, name):
    n = len(s1s)

    def body(*refs):
        srcs, adds, outs, (send_sems, recv_sems) = refs[:n], refs[n:2 * n], refs[2 * n:3 * n], refs[3 * n:]
        x, y, c = _me()
        k = 2 * x + y
        kx, ky = 2 * (1 - x) + y, 2 * x + (1 - y)
        nx, ny = (1 - x, y, c), (x, 1 - y, c)
        sends = []
        for a, (src, u, out) in enumerate(zip(srcs, adds, outs)):
            Rq = src.shape[1] // 2
            lo, hi = pl.ds(0, Rq), pl.ds(Rq, Rq)
            sem = lambda i: (send_sems.at[4 * a + i], recv_sems.at[4 * a + i])
            sends += [_rcopy(src.at[kx, lo], out.at[k, lo], *sem(0), nx), _rcopy(u.at[1], out.at[k, hi], *sem(1), nx),
                      _rcopy(u.at[0], out.at[k, lo], *sem(2), ny), _rcopy(src.at[ky, hi], out.at[k, hi], *sem(3), ny)]
        for cp in sends:
            cp.start()
        for a, out in enumerate(outs):
            Rq = out.shape[1] // 2
            sem = lambda i: (send_sems.at[4 * a + i], recv_sems.at[4 * a + i])
            for i, (chip, rows, frm) in enumerate([(kx, pl.ds(0, Rq), nx), (kx, pl.ds(Rq, Rq), nx),
                                                   (ky, pl.ds(0, Rq), ny), (ky, pl.ds(Rq, Rq), ny)]):
                slot = out.at[chip, rows]
                _rcopy(slot, slot, *sem(i), frm).wait_recv()
        for cp in sends:
            cp.wait_send()

    return _pallas(body, in_specs=[HBM] * (2 * n), out_specs=[HBM] * n, out_shape=[SDS(p.shape, p.dtype) for p in s1s],
                   scratch_shapes=[pltpu.SemaphoreType.DMA((4 * n,)), pltpu.SemaphoreType.DMA((4 * n,))],
                   name=name)(*s1s, *us)


def sibling_pair(r2s, *, name):
    n = len(r2s)

    def body(*refs):
        srcs, outs, (send_sems, recv_sems) = refs[:n], refs[n:2 * n], refs[2 * n:]
        x, y, c = _me()
        cps = [_rcopy(src.at[c], out.at[c], send_sems.at[a], recv_sems.at[a], (x, y, 1 - c))
               for a, (src, out) in enumerate(zip(srcs, outs))]
        for cp in cps:
            cp.start()
        for a, out in enumerate(outs):
            theirs = out.at[1 - c]
            _rcopy(theirs, theirs, send_sems.at[a], recv_sems.at[a], (x, y, 1 - c)).wait_recv()
        for cp in cps:
            cp.wait_send()

    return _pallas(body, in_specs=[HBM] * n, out_specs=[HBM] * n, out_shape=[SDS(p.shape, p.dtype) for p in r2s],
                   input_output_aliases={a: a for a in range(n)},
                   scratch_shapes=[pltpu.SemaphoreType.DMA((n,)), pltpu.SemaphoreType.DMA((n,))], name=name)(*r2s)


def allgather_small(v, *, name):
    R, W = v.shape

    def body(v_ref, out_ref, sum_ref, send_sems, recv_sems):
        x, y, c = _me()
        me = 4 * x + 2 * y + c
        out_ref[me] = v_ref[...]
        peers = []
        for m in range(1, N_DEV):
            px = 1 - x if m & 4 else x
            py = 1 - y if m & 2 else y
            pc = 1 - c if m & 1 else c
            peers.append((px, py, pc))
        sends = [_rcopy(v_ref, out_ref.at[me], send_sems.at[j], recv_sems.at[j], p) for j, p in enumerate(peers)]
        for cp in sends:
            cp.start()
        for j, (px, py, pc) in enumerate(peers):
            slot = out_ref.at[4 * px + 2 * py + pc]
            _rcopy(slot, slot, send_sems.at[j], recv_sems.at[j], (px, py, pc)).wait_recv()
        for cp in sends:
            cp.wait_send()
        acc = out_ref[0]
        for d in range(1, N_DEV):
            acc = acc + out_ref[d]
        sum_ref[...] = acc

    return _pallas(body, in_specs=[VMEM], out_specs=[VMEM, VMEM],
                   out_shape=[SDS((N_DEV, R, W), v.dtype), SDS((R, W), v.dtype)],
                   scratch_shapes=[pltpu.SemaphoreType.DMA((N_DEV - 1,)), pltpu.SemaphoreType.DMA((N_DEV - 1,))],
                   name=name)(v)


def _row_tile(R, cap=512):
    for step in (32, 16, 8):
        fits = [t for t in range(step, cap + 1, step) if R % t == 0]
        if fits:
            return fits[-1]
    raise AssertionError(R)


def add_halves(p, recv, c_idx, *, name):
    _, n, Rh, W = p.shape
    tr = _row_tile(Rh)

    def body(c_ref, p_ref, r_ref, o_ref):
        o_ref[...] = (p_ref[0].astype(F32) + r_ref[...].astype(F32)).astype(o_ref.dtype)

    gs = pltpu.PrefetchScalarGridSpec(
        num_scalar_prefetch=1, grid=(n, Rh // tr),
        in_specs=[BS((1, 1, tr, W), lambda s, i, c_ref: (c_ref[0], s, i, 0)), BS((1, tr, W), lambda s, i, c_ref: (s, i, 0))],
        out_specs=BS((1, tr, W), lambda s, i, c_ref: (s, i, 0)))
    return _pallas(body, grid_spec=gs, out_shape=SDS((n, Rh, W), p.dtype),
                   compiler_params=_params("parallel", "parallel"), name=name)(c_idx, p, recv)


def sum_slots(s1, got, idx, *, name):
    n, Rh, W = got.shape
    tr = _row_tile(Rh)

    def body(idx_ref, s_ref, gx, gy, o_ref):
        o_ref[0] = s_ref[0].astype(F32) + gx[0].astype(F32) + gy[0].astype(F32)

    slot = lambda d: BS((1, tr, W), lambda i, idx: (idx[d], i, 0))
    gs = pltpu.PrefetchScalarGridSpec(
        num_scalar_prefetch=1, grid=(Rh // tr,), in_specs=[slot(0), slot(1), slot(2)],
        out_specs=BS((1, tr, W), lambda i, idx: (idx[3], i, 0)))
    return _pallas(body, grid_spec=gs, out_shape=SDS((2, Rh, W), F32),
                   compiler_params=_params("parallel"), name=name)(idx, s1, got, got)


BIG = [("ffn1_w_gu", True), ("ffn1_w_down", False), ("ffn2_w_gu", True), ("ffn2_w_down", False),
       ("sb_w_qkv", True), ("sb_w_o", False), ("ssd_w_in", True), ("ssd_w_out", False),
       ("sc_w_in", True), ("sc_w_out", False)]
PACK_ALIGN = 64


def _big_layout(shards):
    groups = {}
    for name, cols in BIG:
        L, K, n = shards[name].shape
        items, off = groups.setdefault(n, ([], 0))
        items.append((name, cols, (L, K, n), off, L * K))
        groups[n] = (items, off + L * K)
    return {w: (items, -(-rows // PACK_ALIGN) * PACK_ALIGN) for w, (items, rows) in groups.items()}


def _pack_rows(pieces, total, axis):
    used = sum(p.shape[axis] for p in pieces)
    if total > used:
        pad_shape = list(pieces[0].shape)
        pad_shape[axis] = total - used
        pieces = pieces + [jnp.zeros(pad_shape, pieces[0].dtype)]
    return jnp.concatenate(pieces, axis=axis) if len(pieces) > 1 else pieces[0]


def _gather_big(shards, tag):
    layout = _big_layout(shards)
    packs = [_pack_rows([shards[n].astype(CDT).reshape(-1, w) for n, *_ in items], total, 0).reshape(2, total // 2, w)
             for w, (items, total) in layout.items()]
    fulls = allgather_weights(packs, name=f"{tag}_allgather")
    out = {}
    for (w, (items, total)), full in zip(layout.items(), fulls):
        full = full.reshape(N_CHIPS, total, w)
        for name, cols, (L, K, n), off, rows in items:
            layers = [full[:, off + l * K:off + (l + 1) * K] for l in range(L)]
            if cols:
                out[name] = [p.transpose(1, 0, 2).reshape(K, N_CHIPS * n) for p in layers]
            else:
                out[name] = [p.reshape(N_CHIPS * K, n) for p in layers]
    return out


def _reduce_big(grads, shards, tag):
    layout = _big_layout(shards)
    ps = []
    for w, (items, total) in layout.items():
        pieces = []
        for name, cols, (L, K, n), off, rows in items:
            for g in grads[name]:
                if cols:
                    g = g.reshape(K, N_CHIPS, n).transpose(1, 0, 2)
                else:
                    g = g.reshape(N_CHIPS, K, n)
                pieces.append(g.astype(CDT))
        pack = _pack_rows(pieces, total, 1)
        ps.append(pack.reshape(N_CHIPS, 2, total // 2, w).transpose(1, 0, 2, 3))
    ix, iy = lax.axis_index("x").astype(jnp.int32), lax.axis_index("y").astype(jnp.int32)
    c_core = lax.axis_index("c").astype(jnp.int32)
    k_chip, k_x, k_y = 2 * ix + iy, 2 * (1 - ix) + iy, 2 * ix + (1 - iy)
    recvs = sibling_swap_half(ps, name=f"{tag}_sibling_swap")
    s1s = [add_halves(p, r, c_core.reshape(1), name=f"{tag}_add_halves_{p.shape[-1]}") for p, r in zip(ps, recvs)]
    relays = scatter_relay(s1s, name=f"{tag}_scatter_relay")
    us = [add_relayed(s1, t, jnp.stack([k_y, k_x]), name=f"{tag}_add_relayed_{s1.shape[-1]}") for s1, t in zip(s1s, relays)]
    gots = chip_scatter(s1s, us, name=f"{tag}_chip_scatter")
    r2s = [sum_slots(s1, got, jnp.stack([k_chip, k_x, k_y, c_core]), name=f"{tag}_sum_slots_{s1.shape[-1]}")
           for s1, got in zip(s1s, gots)]
    reds = sibling_pair(r2s, name=f"{tag}_sibling_pair")
    out = {}
    for (w, (items, total)), red in zip(layout.items(), reds):
        red = red.reshape(total, w)
        for name, cols, shape, off, rows in items:
            out[name] = red[off:off + rows].reshape(shape)
    return out


def _pad_lanes(v, width=LANES):
    return jnp.pad(v, ((0, 0), (0, width - v.shape[1])))


def _small_pack(vecs):
    flat, meta, off = [], [], 0
    for v in vecs:
        flat.append(v.reshape(-1))
        meta.append((off, v.size, v.shape))
        off += v.size
    total = -(-off // (8 * LANES)) * 8 * LANES
    flat.append(jnp.zeros((total - off,), F32))
    return jnp.concatenate(flat).reshape(-1, LANES), meta


def _small_unpack(pack, meta):
    flat = pack.reshape(-1)
    return [flat[off:off + size].reshape(shape) for off, size, shape in meta]


def _ffn_fwd(x, g, wgu, wd, tag):
    h = rms_fwd(x, g, name=f"{tag}_norm")
    gate, up, a = ffn_up(h, wgu, name=f"{tag}_gu")
    return matmul(a, wd, res=x, scale=0.5, name=f"{tag}_down"), (x, h, gate, up, a)


def _ffn_bwd(dout, saved, g, wgu, wd, tag):
    x, h, gate, up, a = saved
    F = wd.shape[0]
    dwd = matmul(a, dout, ta=True, out_dtype=CDT, scale=0.5, name=f"{tag}_dwd")
    dgate, dup = ffn_dact(dout, wd, gate, up, 0.5, name=f"{tag}_dact")
    dwgu = jnp.concatenate([matmul(h, dgate, ta=True, out_dtype=CDT, name=f"{tag}_dwg"),
                            matmul(h, dup, ta=True, out_dtype=CDT, name=f"{tag}_dwu")], axis=1)
    dh = matmul(dgate, wgu, tb=True, name=f"{tag}_dh_g")
    dh = matmul(dup, wgu, tb=True, b_k_start=F, res=dh, name=f"{tag}_dh_u")
    dx, dg = rms_bwd(dh, x, g, dout, name=f"{tag}_dnorm")
    return dx, dg, dwgu, dwd


def _sb_mixer_fwd(x, g, wqkv, wo, cfg, tag):
    h = rms_fwd(x, g, name=f"{tag}_norm")
    qkv = matmul(h, wqkv, out_dtype=CDT, name=f"{tag}_qkv")
    o, tab = sb_fwd(qkv, cfg.sb_heads, name=f"{tag}_attn")
    return matmul(o, wo, res=x, name=f"{tag}_out"), (x, h, qkv, tab, o)


def _sb_mixer_bwd(dout, saved, g, wqkv, wo, cfg, tag):
    x, h, qkv, tab, o = saved
    dwo = matmul(o, dout, ta=True, out_dtype=CDT, name=f"{tag}_dwo")
    do = matmul(dout, wo, tb=True, out_dtype=CDT, name=f"{tag}_do")
    dqkv = jnp.concatenate(sb_bwd(qkv, do, tab, cfg.sb_heads, name=f"{tag}_dattn"), axis=1)
    dwqkv = matmul(h, dqkv, ta=True, out_dtype=CDT, name=f"{tag}_dwqkv")
    dh = matmul(dqkv, wqkv, tb=True, name=f"{tag}_dh")
    dx, dg = rms_bwd(dh, x, g, dout, name=f"{tag}_dnorm")
    return dx, dg, dwqkv, dwo


def _sc_mixer_fwd(x, g, win, convw, wout, tag):
    h = rms_fwd(x, g, name=f"{tag}_norm")
    proj = matmul(h, win, name=f"{tag}_in")
    m = sc_fwd(proj, convw, name=f"{tag}_conv")
    return matmul(m, wout, res=x, name=f"{tag}_out"), (x, h, proj, m)


def _sc_mixer_bwd(dout, saved, g, win, convw, wout, tag):
    x, h, proj, m = saved
    dwout = matmul(m, dout, ta=True, out_dtype=CDT, name=f"{tag}_dwout")
    dm = matmul(dout, wout, tb=True, name=f"{tag}_dm")
    dproj, dconvw = sc_bwd(dm, proj, convw, name=f"{tag}_dconv")
    dwin = matmul(h, dproj, ta=True, out_dtype=CDT, name=f"{tag}_dwin")
    dh = matmul(dproj, win, tb=True, name=f"{tag}_dh")
    dx, dg = rms_bwd(dh, x, g, dout, name=f"{tag}_dnorm")
    return dx, dg, dwin, dconvw[:convw.shape[0]], dwout


def _ssd_split_w(win, cfg):
    DI, CV = cfg.DI, cfg.conv_dim
    return win[:, :DI], win[:, DI:DI + CV], _pad_lanes(win[:, DI + CV:])


def _ssd_mixer_fwd(x, g, win, convw, convb, dtb, alog, dskip, ng, wout, cfg, tag):
    wz, wx, wdt = _ssd_split_w(win, cfg)
    h = rms_fwd(x, g, name=f"{tag}_norm")
    z = matmul(h, wz, name=f"{tag}_z")
    xpre = matmul(h, wx, name=f"{tag}_xbc")
    dtpre = matmul(h, wdt, name=f"{tag}_dt")
    pre, xbc = ssd_conv_fwd(xpre, convw, convb, name=f"{tag}_conv")
    dt, ac, dtT, acT = ssd_prep_fwd(dtpre, dtb, alog, name=f"{tag}_prep")
    y, yn, hp = ssd_chunk_fwd(xbc, z, dt, ac, dtT, acT, dskip, ng, cfg, name=f"{tag}_scan")
    out = matmul(yn, wout, res=x, name=f"{tag}_out")
    return out, (x, h, z, xpre, dtpre, pre, xbc, dt, ac, dtT, acT, y, yn, hp)


def _ssd_mixer_bwd(dout, saved, g, win, convw, convb, dtb, alog, dskip, ng, wout, cfg, tag):
    x, h, z, xpre, dtpre, pre, xbc, dt, ac, dtT, acT, y, yn, hp = saved
    T = x.shape[0]
    NH, G = cfg.NH, cfg.G
    wz, wx, wdt = _ssd_split_w(win, cfg)
    dwout = matmul(yn, dout, ta=True, out_dtype=CDT, name=f"{tag}_dwout")
    dyn = matmul(dout, wout, tb=True, name=f"{tag}_dyn")
    dxs, db, dc, dz, colp, rowp, dd, dng = ssd_chunk_bwd(dyn, xbc, z, y, hp, dt, ac, dtT, acT, dskip, ng, cfg,
                                                        name=f"{tag}_dscan")
    col = lambda lo: _pad_lanes(colp[:, :, lo:lo + SSD_HPG].transpose(1, 0, 2).reshape(T, NH))
    row = lambda lo: jnp.pad(rowp[:, lo:lo + SSD_HPG, :].reshape(NH, T), ((0, LANES - NH), (0, 0)))
    ddtpre, dalog, ddtb = ssd_prep_bwd(col(0), col(SSD_HPG), row(0), row(SSD_HPG), dtpre, dtb, alog, name=f"{tag}_dprep")
    dxpre, dconvw, dconvb = ssd_conv_bwd(dxs, db, dc, pre, xpre, convw, name=f"{tag}_dconv")
    dwz = matmul(h, dz, ta=True, out_dtype=CDT, name=f"{tag}_dwz")
    dwx = matmul(h, dxpre, ta=True, out_dtype=CDT, name=f"{tag}_dwx")
    dwdt = matmul(h, ddtpre, ta=True, out_dtype=CDT, name=f"{tag}_dwdt")
    dh = matmul(dz, wz, tb=True, name=f"{tag}_dh_z")
    dh = matmul(dxpre, wx, tb=True, res=dh, name=f"{tag}_dh_x")
    dh = matmul(ddtpre, wdt, tb=True, res=dh, name=f"{tag}_dh_dt")
    dx, dg = rms_bwd(dh, x, g, dout, name=f"{tag}_dnorm")
    dwin = jnp.concatenate([dwz, dwx, dwdt[:, :NH]], axis=1)
    small = dict(conv_w=dconvw[:convw.shape[0]], conv_b=dconvb, dt_bias=ddtb[:, :NH], a_log=dalog[:, :NH],
                 d=dd[:, 0, :SSD_HPG].reshape(1, NH), norm=dng)
    return dx, dg, dwin, dwout, small


INPUT_NAMES = ['x', 'ffn1_norm', 'ffn1_w_gu', 'ffn1_w_down', 'mix_norm', 'ffn2_norm', 'ffn2_w_gu', 'ffn2_w_down',
               'sb_w_qkv', 'sb_w_o', 'ssd_w_in', 'ssd_conv_w', 'ssd_conv_b', 'ssd_dt_bias', 'ssd_a_log', 'ssd_d',
               'ssd_norm', 'ssd_w_out', 'sc_w_in', 'sc_conv_w', 'sc_w_out', 'final_norm']
WEIGHT_NAMES = INPUT_NAMES[1:]
SHARDED_SMALL = ("ssd_conv_w", "sc_conv_w")


def train_step(a, cfg):
    x = a["x"][0]
    tgt = a["loss_target"][0]
    big_names = [n for n, _ in BIG]
    k_chip = 2 * lax.axis_index("x") + lax.axis_index("y")

    w = _gather_big({n: a[n] for n in big_names}, "w")
    conv_pack, conv_meta = _small_pack([a[n] for n in SHARDED_SMALL])
    conv_all, _ = allgather_small(conv_pack, name="conv_w_allgather")
    conv_full = {}
    for i, n in enumerate(SHARDED_SMALL):
        per_chip = [_small_unpack(conv_all[2 * kk], conv_meta)[i] for kk in range(N_CHIPS)]
        conv_full[n] = jnp.concatenate(per_chip, axis=-1)

    def vec(name, i):
        return a[name][i:i + 1]

    ssd_small = lambda j: (a["ssd_conv_b"][j:j + 1], _pad_lanes(a["ssd_dt_bias"][j:j + 1]),
                           _pad_lanes(a["ssd_a_log"][j:j + 1]), _pad_lanes(a["ssd_d"][j:j + 1]), a["ssd_norm"][j:j + 1])

    saved = []
    for i in range(cfg.depth):
        kind, j = i % 3, i // 3
        x, s1 = _ffn_fwd(x, vec("ffn1_norm", i), w["ffn1_w_gu"][i], w["ffn1_w_down"][i], f"l{i}_ffn1")
        if kind == 0:
            x, s2 = _sb_mixer_fwd(x, vec("mix_norm", i), w["sb_w_qkv"][j], w["sb_w_o"][j], cfg, f"l{i}_sb")
        elif kind == 1:
            x, s2 = _ssd_mixer_fwd(x, vec("mix_norm", i), w["ssd_w_in"][j], conv_full["ssd_conv_w"][j], *ssd_small(j),
                                   w["ssd_w_out"][j], cfg, f"l{i}_ssd")
        else:
            x, s2 = _sc_mixer_fwd(x, vec("mix_norm", i), w["sc_w_in"][j], conv_full["sc_conv_w"][j], w["sc_w_out"][j],
                                  f"l{i}_sc")
        x, s3 = _ffn_fwd(x, vec("ffn2_norm", i), w["ffn2_w_gu"][i], w["ffn2_w_down"][i], f"l{i}_ffn2")
        saved.append((s1, s2, s3))
    loss_part, dx, d_final = loss_head(x, a["final_norm"].reshape(1, -1), tgt, name="loss_head")

    gbig = {n: [None] * a[n].shape[0] for n in big_names}
    gsmall = {n: [None] * a[n].shape[0] for n in ("ffn1_norm", "mix_norm", "ffn2_norm", "ssd_conv_w", "ssd_conv_b",
                                                   "ssd_dt_bias", "ssd_a_log", "ssd_d", "ssd_norm", "sc_conv_w")}
    for i in reversed(range(cfg.depth)):
        kind, j = i % 3, i // 3
        s1, s2, s3 = saved[i]
        dx, gsmall["ffn2_norm"][i], gbig["ffn2_w_gu"][i], gbig["ffn2_w_down"][i] = _ffn_bwd(
            dx, s3, vec("ffn2_norm", i), w["ffn2_w_gu"][i], w["ffn2_w_down"][i], f"l{i}_ffn2")
        if kind == 0:
            dx, gsmall["mix_norm"][i], gbig["sb_w_qkv"][j], gbig["sb_w_o"][j] = _sb_mixer_bwd(
                dx, s2, vec("mix_norm", i), w["sb_w_qkv"][j], w["sb_w_o"][j], cfg, f"l{i}_sb")
        elif kind == 1:
            dx, gsmall["mix_norm"][i], gbig["ssd_w_in"][j], gbig["ssd_w_out"][j], sm = _ssd_mixer_bwd(
                dx, s2, vec("mix_norm", i), w["ssd_w_in"][j], conv_full["ssd_conv_w"][j], *ssd_small(j),
                w["ssd_w_out"][j], cfg, f"l{i}_ssd")
            for key, val in sm.items():
                gsmall["ssd_" + key][j] = val
        else:
            dx, gsmall["mix_norm"][i], gbig["sc_w_in"][j], gsmall["sc_conv_w"][j], gbig["sc_w_out"][j] = _sc_mixer_bwd(
                dx, s2, vec("mix_norm", i), w["sc_w_in"][j], conv_full["sc_conv_w"][j], w["sc_w_out"][j], f"l{i}_sc")
        dx, gsmall["ffn1_norm"][i], gbig["ffn1_w_gu"][i], gbig["ffn1_w_down"][i] = _ffn_bwd(
            dx, s1, vec("ffn1_norm", i), w["ffn1_w_gu"][i], w["ffn1_w_down"][i], f"l{i}_ffn1")
    grad_x = dx[None]

    red_big = _reduce_big(gbig, {n: a[n] for n in big_names}, "g")
    small_names = list(gsmall) + ["final_norm"]
    small_vals = [jnp.concatenate([p.reshape((1,) + a[n].shape[1:-1] + (-1,)) for p in gsmall[n]], axis=0)
                  for n in gsmall] + [d_final.reshape(-1)]
    gpack, gmeta = _small_pack(small_vals + [loss_part[:, :1]])
    _, gsum = allgather_small(gpack, name="small_allreduce")
    *small_red, loss = _small_unpack(gsum, gmeta)
    red = dict(red_big)
    for n, v in zip(small_names, small_red):
        if n in SHARDED_SMALL:
            width = a[n].shape[-1]
            v = lax.dynamic_slice_in_dim(v, k_chip * width, width, axis=v.ndim - 1)
        red[n] = v.reshape(a[n].shape)

    delta, new_m, new_v = {}, {}, {}
    two_d = lambda t: t.reshape(-1, t.shape[-1])
    for n in big_names:
        d_, m_, v_ = adamw(two_d(a[n]), two_d(red[n]), two_d(a["m_" + n]), two_d(a["v_" + n]), name=f"adamw_{n}")
        delta[n], new_m[n], new_v[n] = (t.reshape(a[n].shape) for t in (d_, m_, v_))
    rest = [n for n in WEIGHT_NAMES if n not in big_names]
    packs = [_small_pack([src[n] for n in rest]) for src in
             (a, red, {n: a["m_" + n] for n in rest}, {n: a["v_" + n] for n in rest})]
    outs = adamw(*[p for p, _ in packs], name="adamw_small")
    for dst, o in zip((delta, new_m, new_v), outs):
        for n, t in zip(rest, _small_unpack(o, packs[0][1])):
            dst[n] = t
    return (loss.reshape(()), grad_x, *[red[n] for n in WEIGHT_NAMES], *[delta[n] for n in WEIGHT_NAMES],
            *[new_m[n] for n in WEIGHT_NAMES], *[new_v[n] for n in WEIGHT_NAMES])


def kernel(x, ffn1_norm, ffn1_w_gu, ffn1_w_down, mix_norm, ffn2_norm, ffn2_w_gu, ffn2_w_down, sb_w_qkv, sb_w_o, ssd_w_in, ssd_conv_w, ssd_conv_b, ssd_dt_bias, ssd_a_log, ssd_d, ssd_norm, ssd_w_out, sc_w_in, sc_conv_w, sc_w_out, final_norm, loss_target, m_ffn1_norm, m_ffn1_w_gu, m_ffn1_w_down, m_mix_norm, m_ffn2_norm, m_ffn2_w_gu, m_ffn2_w_down, m_sb_w_qkv, m_sb_w_o, m_ssd_w_in, m_ssd_conv_w, m_ssd_conv_b, m_ssd_dt_bias, m_ssd_a_log, m_ssd_d, m_ssd_norm, m_ssd_w_out, m_sc_w_in, m_sc_conv_w, m_sc_w_out, m_final_norm, v_ffn1_norm, v_ffn1_w_gu, v_ffn1_w_down, v_mix_norm, v_ffn2_norm, v_ffn2_w_gu, v_ffn2_w_down, v_sb_w_qkv, v_sb_w_o, v_ssd_w_in, v_ssd_conv_w, v_ssd_conv_b, v_ssd_dt_bias, v_ssd_a_log, v_ssd_d, v_ssd_norm, v_ssd_w_out, v_sc_w_in, v_sc_conv_w, v_sc_w_out, v_final_norm):
    args = dict(locals())
    return train_step(args, Cfg())
```
